```python
import jax
import jax.numpy as jnp
from jax import lax
import numpy as np

D_MODEL = 2048
BATCH = 8
SEQ = 8192
DEPTH = 2

GRID_W = 64
CTX_LEN = 256
EPS = 1e-6
N_MOD = 6
MLA_HEADS = 8
QK_NOPE = 128
QK_ROPE = 64
V_HEAD = 128
Q_LORA = 512
KV_LORA = 256
ROPE_THETA = 10000.0
AXIS_ROPE = QK_ROPE // 2
Q_BLOCK = 128
CONV_CH = D_MODEL // 2
GLA_HEADS = 4
GLA_DK = D_MODEL // 2 // GLA_HEADS
GLA_DV = D_MODEL // GLA_HEADS
GLA_KEY = GLA_HEADS * GLA_DK
GLA_VAL = GLA_HEADS * GLA_DV
GATE_RANK = 16
GATE_NORMALIZER = 16.0
CHUNK = 64
D_FF = 5632
AB_CTX_COLS = KV_LORA + QK_ROPE
AB_SPLITS = (KV_LORA, AB_CTX_COLS, AB_CTX_COLS + Q_LORA, AB_CTX_COLS + Q_LORA + CONV_CH, AB_CTX_COLS + Q_LORA + 2 * CONV_CH)
AB_COLS = AB_CTX_COLS + Q_LORA + 3 * CONV_CH
AB_OUT = CONV_CH + MLA_HEADS * V_HEAD
GLA_CTX_COLS = GLA_KEY + GLA_VAL + 2 * GATE_RANK
GLA_SPLITS = (GLA_KEY, GLA_KEY + GLA_VAL, GLA_KEY + GLA_VAL + GATE_RANK, GLA_CTX_COLS, GLA_CTX_COLS + GLA_KEY)
GLA_COLS = GLA_CTX_COLS + GLA_KEY + GLA_VAL

kernel_name = 'hybrid_shortconv_mla_gla_convffn_dit'


def rmsnorm(x, w):
    xf = x.astype(jnp.float32)
    y = xf * lax.rsqrt(jnp.mean(xf * xf, axis=-1, keepdims=True) + EPS)
    return (y * w.astype(jnp.float32)).astype(x.dtype)


def modulate(x, w, shift, scale):
    return rmsnorm(x, w) * (1.0 + scale) + shift


def dwconv3(x, w, b=None):
    xp = jnp.pad(x, ((0, 0), (1, 1), (0, 0)))
    y = xp[:, :-2] * w[0] + xp[:, 1:-1] * w[1] + xp[:, 2:] * w[2]
    return y if b is None else y + b


def axial_rope_tables(n_tokens):
    rows = n_tokens // GRID_W
    row = jnp.repeat(jnp.arange(rows, dtype=jnp.float32), GRID_W)
    col = jnp.tile(jnp.arange(GRID_W, dtype=jnp.float32), rows)
    inv = ROPE_THETA ** (-jnp.arange(0, AXIS_ROPE, 2, dtype=jnp.float32) / AXIS_ROPE)
    ar = row[:, None] * inv
    ac = col[:, None] * inv
    ang = jnp.concatenate([ar, ar, ac, ac], axis=-1)
    return jnp.cos(ang), jnp.sin(ang)


def apply_rope(x, cos, sin):
    extra = x.ndim - 3
    cos = cos.reshape(cos.shape[0], *([1] * extra), cos.shape[1])
    sin = sin.reshape(sin.shape[0], *([1] * extra), sin.shape[1])
    xf = x.astype(jnp.float32)
    a1, a2, b1, b2 = jnp.split(xf, 4, axis=-1)
    rot = jnp.concatenate([-a2, a1, -b2, b1], axis=-1)
    return (xf * cos + rot * sin).astype(x.dtype)


def attention(q, k, v):
    s = jnp.einsum('bqhd,bkhd->bhqk', q, k, preferred_element_type=jnp.float32) * (q.shape[-1] ** -0.5)
    p = jax.nn.softmax(s, axis=-1).astype(v.dtype)
    return jnp.einsum('bhqk,bkhd->bqhd', p, v)


def blocked_attention(q, k, v):
    bsz, t, nh, dk = q.shape
    nb = t // Q_BLOCK
    qb = q.reshape(bsz, nb, Q_BLOCK, nh, dk).transpose(1, 0, 2, 3, 4)
    o = lax.map(lambda qi: attention(qi, k, v), qb)
    return o.transpose(1, 0, 2, 3, 4).reshape(bsz, t, nh, v.shape[-1])


def mla_q(q_lat, q_norm, w_qb, cos_sin):
    q = (rmsnorm(q_lat, q_norm) @ w_qb).reshape(q_lat.shape[0], q_lat.shape[1], MLA_HEADS, QK_NOPE + QK_ROPE)
    if cos_sin is None:
        return q
    return jnp.concatenate([q[..., :QK_NOPE], apply_rope(q[..., QK_NOPE:], *cos_sin)], axis=-1)


def mla_kv(kv_lat, k_rope, kv_norm, w_kvb, cos_sin):
    kv = (rmsnorm(kv_lat, kv_norm) @ w_kvb).reshape(kv_lat.shape[0], kv_lat.shape[1], MLA_HEADS, QK_NOPE + V_HEAD)
    if cos_sin is not None:
        k_rope = apply_rope(k_rope, *cos_sin)
    k_rope = jnp.broadcast_to(k_rope[:, :, None, :], kv.shape[:3] + (QK_ROPE,))
    k = jnp.concatenate([kv[..., :QK_NOPE], k_rope], axis=-1)
    return k, kv[..., QK_NOPE:]


def mixer_ab(h, hc, w_in, conv_a, q_norm, w_qb, kv_norm, w_kvb, w_out, ctx_out):
    bsz, t, _ = h.shape
    cos_sin = axial_rope_tables(t)
    z = h @ w_in
    zc = hc @ (w_in if ctx_out else w_in[:, :AB_CTX_COLS])
    kv_lat, k_rope, q_lat, a_x, a_b, a_c = jnp.split(z, list(AB_SPLITS), axis=-1)
    k_l, v_l = mla_kv(kv_lat, k_rope, kv_norm, w_kvb, cos_sin)
    k_c, v_c = mla_kv(zc[..., :KV_LORA], zc[..., KV_LORA:AB_CTX_COLS], kv_norm, w_kvb, None)
    q_l = mla_q(q_lat, q_norm, w_qb, cos_sin)
    o_l = blocked_attention(q_l, jnp.concatenate([k_l, k_c], axis=1), jnp.concatenate([v_l, v_c], axis=1))
    y_a = a_b * dwconv3(a_c * a_x, conv_a)
    y = jnp.concatenate([y_a, o_l.reshape(bsz, t, MLA_HEADS * V_HEAD)], axis=-1) @ w_out
    if not ctx_out:
        return y, None
    _, _, qc_lat, c_x, c_b, c_c = jnp.split(zc, list(AB_SPLITS), axis=-1)
    o_c = attention(mla_q(qc_lat, q_norm, w_qb, None), k_c, v_c)
    yc_a = c_b * dwconv3(c_c * c_x, conv_a)
    yc = jnp.concatenate([yc_a, o_c.reshape(o_c.shape[0], o_c.shape[1], MLA_HEADS * V_HEAD)], axis=-1) @ w_out
    return y, yc


def gla_chunked(q, k, v, g, h0):
    bsz, nh, t, dk = k.shape
    n = t // CHUNK
    with_q = q is not None

    def chunks(a):
        return a.reshape(bsz, nh, n, CHUNK, a.shape[-1]).transpose(2, 0, 1, 3, 4)

    mask = jnp.tril(jnp.ones((CHUNK, CHUNK), dtype=bool))

    def step(state, xs):
        if with_q:
            qc, kc, vc, gc = xs
        else:
            kc, vc, gc = xs
        b = jnp.cumsum(gc, axis=-2)
        b_last = b[..., -1:, :]
        new_state = state * jnp.swapaxes(jnp.exp(b_last), -1, -2) + jnp.einsum('bhcd,bhcv->bhdv', kc * jnp.exp(b_last - b), vc)
        if not with_q:
            return new_state, None
        qe = qc * jnp.exp(b)
        ke = kc * jnp.exp(-b)
        att = jnp.where(mask, jnp.einsum('bhid,bhjd->bhij', qe, ke), 0.0)
        out = jnp.einsum('bhij,bhjv->bhiv', att, vc) + jnp.einsum('bhid,bhdv->bhiv', qe, state)
        return new_state, out

    xs = (chunks(k), chunks(v), chunks(g))
    if with_q:
        xs = (chunks(q),) + xs
    final, out = lax.scan(step, h0, xs)
    if with_q:
        out = out.transpose(1, 2, 0, 3, 4).reshape(bsz, nh, t, v.shape[-1])
    return out, final


def to_heads(a, d):
    bsz, t, _ = a.shape
    return a.reshape(bsz, t, GLA_HEADS, d).transpose(0, 2, 1, 3).astype(jnp.float32)


def gla_gate(lr, w, b):
    return jax.nn.log_sigmoid((lr @ w + b).astype(jnp.float32)) / GATE_NORMALIZER


def rev(a):
    return a[:, :, ::-1]


def gla_prep(z, with_q, gfw_w, gfw_b, gbw_w, gbw_b):
    parts = jnp.split(z, list(GLA_SPLITS if with_q else GLA_SPLITS[:3]), axis=-1)
    k = to_heads(parts[0], GLA_DK)
    v = to_heads(parts[1], GLA_DV)
    g_fw = to_heads(gla_gate(parts[2], gfw_w, gfw_b), GLA_DK)
    g_bw = to_heads(gla_gate(parts[3], gbw_w, gbw_b), GLA_DK)
    if not with_q:
        return None, k, v, g_fw, g_bw, None
    q = to_heads(parts[4], GLA_DK) * (GLA_DK ** -0.5)
    return q, k, v, g_fw, g_bw, parts[5]


def gla_out(o, og, o_norm, w_out):
    bsz, nh, t, _ = o.shape
    o = rmsnorm(o, o_norm).transpose(0, 2, 1, 3).reshape(bsz, t, GLA_VAL).astype(og.dtype)
    return (o * jax.nn.silu(og)) @ w_out


def mixer_gla(h, hc, w_in, gfw_w, gfw_b, gbw_w, gbw_b, o_norm, w_out, ctx_out):
    bsz = h.shape[0]
    h0 = jnp.zeros((bsz, GLA_HEADS, GLA_DK, GLA_DV), jnp.float32)
    zc = hc @ (w_in if ctx_out else w_in[:, :GLA_CTX_COLS])
    qc, kc, vc, gfc, gbc, ogc = gla_prep(zc, ctx_out, gfw_w, gfw_b, gbw_w, gbw_b)
    oc_f, sc_f = gla_chunked(qc, kc, vc, gfc, h0)
    oc_b, sc_b = gla_chunked(rev(qc) if ctx_out else None, rev(kc), rev(vc), rev(gbc), h0)
    q, k, v, gf, gb, og = gla_prep(h @ w_in, True, gfw_w, gfw_b, gbw_w, gbw_b)
    o_f, _ = gla_chunked(q, k, v, gf, sc_f)
    o_b, _ = gla_chunked(rev(q), rev(k), rev(v), rev(gb), sc_b)
    y = gla_out(o_f + rev(o_b), og, o_norm, w_out)
    if not ctx_out:
        return y, None
    return y, gla_out(oc_f + rev(oc_b), ogc, o_norm, w_out)


def conv_ffn(h, w_up, conv_w, conv_b, w_down):
    u = dwconv3(h @ w_up, conv_w, conv_b)
    gate, val = jnp.split(u, 2, axis=-1)
    return (jax.nn.silu(gate) * val) @ w_down


def block(x, xc, c_act, cc_act, ada_w, ada_b, norm1, norm2, mixer_fn, mix, ffn, ctx_out):
    m = jnp.split((c_act @ ada_w + ada_b)[:, None, :], N_MOD, axis=-1)
    mc = jnp.split((cc_act @ ada_w + ada_b)[:, None, :], N_MOD, axis=-1)
    y, yc = mixer_fn(modulate(x, norm1, m[0], m[1]), modulate(xc, norm1, mc[0], mc[1]), *mix, ctx_out)
    x = x + m[2] * y
    x = x + m[5] * conv_ffn(modulate(x, norm2, m[3], m[4]), *ffn)
    if not ctx_out:
        return x, None
    xc = xc + mc[2] * yc
    xc = xc + mc[5] * conv_ffn(modulate(xc, norm2, mc[3], mc[4]), *ffn)
    return x, xc


def _fwd_setup_inputs(seed: int = 0) -> dict:
    key = jax.random.key(seed)
    keys = iter(jax.random.split(key, 16 + 24 * DEPTH))

    def rnd(shape, scale):
        return jax.random.normal(next(keys), shape, jnp.float32) * scale

    def gain(n):
        return 1.0 + rnd((n,), 0.02)

    inp = {}
    inp['x'] = rnd((BATCH, SEQ, D_MODEL), 1.0)
    inp['c'] = rnd((BATCH, D_MODEL), 1.0)
    inp['ctx'] = rnd((BATCH, CTX_LEN, D_MODEL), 1.0)
    inp['c_ctx'] = rnd((D_MODEL,), 1.0)
    for i in range(DEPTH):
        p = 'l%d_' % i
        inp[p + 'ada_w'] = rnd((D_MODEL, N_MOD * D_MODEL), 0.5 * D_MODEL ** -0.5)
        inp[p + 'ada_b'] = rnd((N_MOD * D_MODEL,), 0.02)
        inp[p + 'norm1'] = gain(D_MODEL)
        if i % 2 == 0:
            inp[p + 'w_in'] = rnd((D_MODEL, AB_COLS), D_MODEL ** -0.5)
            inp[p + 'conv_a'] = rnd((3, CONV_CH), 3 ** -0.5)
            inp[p + 'q_norm'] = gain(Q_LORA)
            inp[p + 'w_qb'] = rnd((Q_LORA, MLA_HEADS * (QK_NOPE + QK_ROPE)), Q_LORA ** -0.5)
            inp[p + 'kv_norm'] = gain(KV_LORA)
            inp[p + 'w_kvb'] = rnd((KV_LORA, MLA_HEADS * (QK_NOPE + V_HEAD)), KV_LORA ** -0.5)
            inp[p + 'w_out'] = rnd((AB_OUT, D_MODEL), AB_OUT ** -0.5)
        else:
            inp[p + 'w_in'] = rnd((D_MODEL, GLA_COLS), D_MODEL ** -0.5)
            inp[p + 'gate_fw_w'] = rnd((GATE_RANK, GLA_KEY), GATE_RANK ** -0.5)
            inp[p + 'gate_fw_b'] = rnd((GLA_KEY,), 0.02)
            inp[p + 'gate_bw_w'] = rnd((GATE_RANK, GLA_KEY), GATE_RANK ** -0.5)
            inp[p + 'gate_bw_b'] = rnd((GLA_KEY,), 0.02)
            inp[p + 'o_norm'] = gain(GLA_DV)
            inp[p + 'w_out'] = rnd((GLA_VAL, D_MODEL), GLA_VAL ** -0.5)
        inp[p + 'norm2'] = gain(D_MODEL)
        inp[p + 'ffn_up'] = rnd((D_MODEL, 2 * D_FF), D_MODEL ** -0.5)
        inp[p + 'ffn_conv_w'] = rnd((3, 2 * D_FF), 3 ** -0.5)
        inp[p + 'ffn_conv_b'] = rnd((2 * D_FF,), 0.02)
        inp[p + 'ffn_down'] = rnd((D_FF, D_MODEL), D_FF ** -0.5)
    inp['final_norm'] = gain(D_MODEL)
    return inp


def _fwd_reference(x, c, ctx, c_ctx,
              l0_ada_w, l0_ada_b, l0_norm1, l0_w_in, l0_conv_a, l0_q_norm, l0_w_qb, l0_kv_norm, l0_w_kvb, l0_w_out,
              l0_norm2, l0_ffn_up, l0_ffn_conv_w, l0_ffn_conv_b, l0_ffn_down,
              l1_ada_w, l1_ada_b, l1_norm1, l1_w_in, l1_gate_fw_w, l1_gate_fw_b, l1_gate_bw_w, l1_gate_bw_b, l1_o_norm, l1_w_out,
              l1_norm2, l1_ffn_up, l1_ffn_conv_w, l1_ffn_conv_b, l1_ffn_down,
              final_norm):
    layers = (
        (l0_ada_w, l0_ada_b, l0_norm1, l0_norm2,
         (l0_w_in, l0_conv_a, l0_q_norm, l0_w_qb, l0_kv_norm, l0_w_kvb, l0_w_out),
         (l0_ffn_up, l0_ffn_conv_w, l0_ffn_conv_b, l0_ffn_down)),
        (l1_ada_w, l1_ada_b, l1_norm1, l1_norm2,
         (l1_w_in, l1_gate_fw_w, l1_gate_fw_b, l1_gate_bw_w, l1_gate_bw_b, l1_o_norm, l1_w_out),
         (l1_ffn_up, l1_ffn_conv_w, l1_ffn_conv_b, l1_ffn_down)),
    )
    c_act = jax.nn.silu(c)
    cc_act = jax.nn.silu(c_ctx)[None, :]
    xc = ctx
    for i in range(DEPTH):
        ada_w, ada_b, n1, n2, mix, ffn = layers[i]
        mixer_fn = mixer_ab if i % 2 == 0 else mixer_gla
        x, xc = block(x, xc, c_act, cc_act, ada_w, ada_b, n1, n2, mixer_fn, mix, ffn, i < DEPTH - 1)
    return rmsnorm(x, final_norm)


import jax as _jax
import jax.numpy as _jnp

TWIN_FORMAT = 'train_step'
FWD_PARAMS = ['x', 'c', 'ctx', 'c_ctx', 'l0_ada_w', 'l0_ada_b', 'l0_norm1', 'l0_w_in', 'l0_conv_a', 'l0_q_norm', 'l0_w_qb', 'l0_kv_norm', 'l0_w_kvb', 'l0_w_out', 'l0_norm2', 'l0_ffn_up', 'l0_ffn_conv_w', 'l0_ffn_conv_b', 'l0_ffn_down', 'l1_ada_w', 'l1_ada_b', 'l1_norm1', 'l1_w_in', 'l1_gate_fw_w', 'l1_gate_fw_b', 'l1_gate_bw_w', 'l1_gate_bw_b', 'l1_o_norm', 'l1_w_out', 'l1_norm2', 'l1_ffn_up', 'l1_ffn_conv_w', 'l1_ffn_conv_b', 'l1_ffn_down', 'final_norm']
TWIN_WEIGHTS = ['c_ctx', 'l0_ada_w', 'l0_ada_b', 'l0_norm1', 'l0_w_in', 'l0_conv_a', 'l0_q_norm', 'l0_w_qb', 'l0_kv_norm', 'l0_w_kvb', 'l0_w_out', 'l0_norm2', 'l0_ffn_up', 'l0_ffn_conv_w', 'l0_ffn_conv_b', 'l0_ffn_down', 'l1_ada_w', 'l1_ada_b', 'l1_norm1', 'l1_w_in', 'l1_gate_fw_w', 'l1_gate_fw_b', 'l1_gate_bw_w', 'l1_gate_bw_b', 'l1_o_norm', 'l1_w_out', 'l1_norm2', 'l1_ffn_up', 'l1_ffn_conv_w', 'l1_ffn_conv_b', 'l1_ffn_down', 'final_norm']
TWIN_DIFF_INPUT = 'x'
TWIN_INPUTS = ['x', 'c', 'ctx', 'c_ctx', 'l0_ada_w', 'l0_ada_b', 'l0_norm1', 'l0_w_in', 'l0_conv_a', 'l0_q_norm', 'l0_w_qb', 'l0_kv_norm', 'l0_w_kvb', 'l0_w_out', 'l0_norm2', 'l0_ffn_up', 'l0_ffn_conv_w', 'l0_ffn_conv_b', 'l0_ffn_down', 'l1_ada_w', 'l1_ada_b', 'l1_norm1', 'l1_w_in', 'l1_gate_fw_w', 'l1_gate_fw_b', 'l1_gate_bw_w', 'l1_gate_bw_b', 'l1_o_norm', 'l1_w_out', 'l1_norm2', 'l1_ffn_up', 'l1_ffn_conv_w', 'l1_ffn_conv_b', 'l1_ffn_down', 'final_norm', 'loss_target', 'm_c_ctx', 'm_l0_ada_w', 'm_l0_ada_b', 'm_l0_norm1', 'm_l0_w_in', 'm_l0_conv_a', 'm_l0_q_norm', 'm_l0_w_qb', 'm_l0_kv_norm', 'm_l0_w_kvb', 'm_l0_w_out', 'm_l0_norm2', 'm_l0_ffn_up', 'm_l0_ffn_conv_w', 'm_l0_ffn_conv_b', 'm_l0_ffn_down', 'm_l1_ada_w', 'm_l1_ada_b', 'm_l1_norm1', 'm_l1_w_in', 'm_l1_gate_fw_w', 'm_l1_gate_fw_b', 'm_l1_gate_bw_w', 'm_l1_gate_bw_b', 'm_l1_o_norm', 'm_l1_w_out', 'm_l1_norm2', 'm_l1_ffn_up', 'm_l1_ffn_conv_w', 'm_l1_ffn_conv_b', 'm_l1_ffn_down', 'm_final_norm', 'v_c_ctx', 'v_l0_ada_w', 'v_l0_ada_b', 'v_l0_norm1', 'v_l0_w_in', 'v_l0_conv_a', 'v_l0_q_norm', 'v_l0_w_qb', 'v_l0_kv_norm', 'v_l0_w_kvb', 'v_l0_w_out', 'v_l0_norm2', 'v_l0_ffn_up', 'v_l0_ffn_conv_w', 'v_l0_ffn_conv_b', 'v_l0_ffn_down', 'v_l1_ada_w', 'v_l1_ada_b', 'v_l1_norm1', 'v_l1_w_in', 'v_l1_gate_fw_w', 'v_l1_gate_fw_b', 'v_l1_gate_bw_w', 'v_l1_gate_bw_b', 'v_l1_o_norm', 'v_l1_w_out', 'v_l1_norm2', 'v_l1_ffn_up', 'v_l1_ffn_conv_w', 'v_l1_ffn_conv_b', 'v_l1_ffn_down', 'v_final_norm']
TWIN_OUTPUTS = ['loss', 'grad_x', 'grad_c_ctx', 'grad_l0_ada_w', 'grad_l0_ada_b', 'grad_l0_norm1', 'grad_l0_w_in', 'grad_l0_conv_a', 'grad_l0_q_norm', 'grad_l0_w_qb', 'grad_l0_kv_norm', 'grad_l0_w_kvb', 'grad_l0_w_out', 'grad_l0_norm2', 'grad_l0_ffn_up', 'grad_l0_ffn_conv_w', 'grad_l0_ffn_conv_b', 'grad_l0_ffn_down', 'grad_l1_ada_w', 'grad_l1_ada_b', 'grad_l1_norm1', 'grad_l1_w_in', 'grad_l1_gate_fw_w', 'grad_l1_gate_fw_b', 'grad_l1_gate_bw_w', 'grad_l1_gate_bw_b', 'grad_l1_o_norm', 'grad_l1_w_out', 'grad_l1_norm2', 'grad_l1_ffn_up', 'grad_l1_ffn_conv_w', 'grad_l1_ffn_conv_b', 'grad_l1_ffn_down', 'grad_final_norm', 'delta_c_ctx', 'delta_l0_ada_w', 'delta_l0_ada_b', 'delta_l0_norm1', 'delta_l0_w_in', 'delta_l0_conv_a', 'delta_l0_q_norm', 'delta_l0_w_qb', 'delta_l0_kv_norm', 'delta_l0_w_kvb', 'delta_l0_w_out', 'delta_l0_norm2', 'delta_l0_ffn_up', 'delta_l0_ffn_conv_w', 'delta_l0_ffn_conv_b', 'delta_l0_ffn_down', 'delta_l1_ada_w', 'delta_l1_ada_b', 'delta_l1_norm1', 'delta_l1_w_in', 'delta_l1_gate_fw_w', 'delta_l1_gate_fw_b', 'delta_l1_gate_bw_w', 'delta_l1_gate_bw_b', 'delta_l1_o_norm', 'delta_l1_w_out', 'delta_l1_norm2', 'delta_l1_ffn_up', 'delta_l1_ffn_conv_w', 'delta_l1_ffn_conv_b', 'delta_l1_ffn_down', 'delta_final_norm', 'new_m_c_ctx', 'new_m_l0_ada_w', 'new_m_l0_ada_b', 'new_m_l0_norm1', 'new_m_l0_w_in', 'new_m_l0_conv_a', 'new_m_l0_q_norm', 'new_m_l0_w_qb', 'new_m_l0_kv_norm', 'new_m_l0_w_kvb', 'new_m_l0_w_out', 'new_m_l0_norm2', 'new_m_l0_ffn_up', 'new_m_l0_ffn_conv_w', 'new_m_l0_ffn_conv_b', 'new_m_l0_ffn_down', 'new_m_l1_ada_w', 'new_m_l1_ada_b', 'new_m_l1_norm1', 'new_m_l1_w_in', 'new_m_l1_gate_fw_w', 'new_m_l1_gate_fw_b', 'new_m_l1_gate_bw_w', 'new_m_l1_gate_bw_b', 'new_m_l1_o_norm', 'new_m_l1_w_out', 'new_m_l1_norm2', 'new_m_l1_ffn_up', 'new_m_l1_ffn_conv_w', 'new_m_l1_ffn_conv_b', 'new_m_l1_ffn_down', 'new_m_final_norm', 'new_v_c_ctx', 'new_v_l0_ada_w', 'new_v_l0_ada_b', 'new_v_l0_norm1', 'new_v_l0_w_in', 'new_v_l0_conv_a', 'new_v_l0_q_norm', 'new_v_l0_w_qb', 'new_v_l0_kv_norm', 'new_v_l0_w_kvb', 'new_v_l0_w_out', 'new_v_l0_norm2', 'new_v_l0_ffn_up', 'new_v_l0_ffn_conv_w', 'new_v_l0_ffn_conv_b', 'new_v_l0_ffn_down', 'new_v_l1_ada_w', 'new_v_l1_ada_b', 'new_v_l1_norm1', 'new_v_l1_w_in', 'new_v_l1_gate_fw_w', 'new_v_l1_gate_fw_b', 'new_v_l1_gate_bw_w', 'new_v_l1_gate_bw_b', 'new_v_l1_o_norm', 'new_v_l1_w_out', 'new_v_l1_norm2', 'new_v_l1_ffn_up', 'new_v_l1_ffn_conv_w', 'new_v_l1_ffn_conv_b', 'new_v_l1_ffn_down', 'new_v_final_norm']
TWIN_LEAF_KINDS = {'loss': 'loss', 'grad_x': 'grad_x', 'grad_c_ctx': 'grad_w', 'grad_l0_ada_w': 'grad_w', 'grad_l0_ada_b': 'grad_w', 'grad_l0_norm1': 'grad_w', 'grad_l0_w_in': 'grad_w', 'grad_l0_conv_a': 'grad_w', 'grad_l0_q_norm': 'grad_w', 'grad_l0_w_qb': 'grad_w', 'grad_l0_kv_norm': 'grad_w', 'grad_l0_w_kvb': 'grad_w', 'grad_l0_w_out': 'grad_w', 'grad_l0_norm2': 'grad_w', 'grad_l0_ffn_up': 'grad_w', 'grad_l0_ffn_conv_w': 'grad_w', 'grad_l0_ffn_conv_b': 'grad_w', 'grad_l0_ffn_down': 'grad_w', 'grad_l1_ada_w': 'grad_w', 'grad_l1_ada_b': 'grad_w', 'grad_l1_norm1': 'grad_w', 'grad_l1_w_in': 'grad_w', 'grad_l1_gate_fw_w': 'grad_w', 'grad_l1_gate_fw_b': 'grad_w', 'grad_l1_gate_bw_w': 'grad_w', 'grad_l1_gate_bw_b': 'grad_w', 'grad_l1_o_norm': 'grad_w', 'grad_l1_w_out': 'grad_w', 'grad_l1_norm2': 'grad_w', 'grad_l1_ffn_up': 'grad_w', 'grad_l1_ffn_conv_w': 'grad_w', 'grad_l1_ffn_conv_b': 'grad_w', 'grad_l1_ffn_down': 'grad_w', 'grad_final_norm': 'grad_w', 'delta_c_ctx': 'delta_w', 'delta_l0_ada_w': 'delta_w', 'delta_l0_ada_b': 'delta_w', 'delta_l0_norm1': 'delta_w', 'delta_l0_w_in': 'delta_w', 'delta_l0_conv_a': 'delta_w', 'delta_l0_q_norm': 'delta_w', 'delta_l0_w_qb': 'delta_w', 'delta_l0_kv_norm': 'delta_w', 'delta_l0_w_kvb': 'delta_w', 'delta_l0_w_out': 'delta_w', 'delta_l0_norm2': 'delta_w', 'delta_l0_ffn_up': 'delta_w', 'delta_l0_ffn_conv_w': 'delta_w', 'delta_l0_ffn_conv_b': 'delta_w', 'delta_l0_ffn_down': 'delta_w', 'delta_l1_ada_w': 'delta_w', 'delta_l1_ada_b': 'delta_w', 'delta_l1_norm1': 'delta_w', 'delta_l1_w_in': 'delta_w', 'delta_l1_gate_fw_w': 'delta_w', 'delta_l1_gate_fw_b': 'delta_w', 'delta_l1_gate_bw_w': 'delta_w', 'delta_l1_gate_bw_b': 'delta_w', 'delta_l1_o_norm': 'delta_w', 'delta_l1_w_out': 'delta_w', 'delta_l1_norm2': 'delta_w', 'delta_l1_ffn_up': 'delta_w', 'delta_l1_ffn_conv_w': 'delta_w', 'delta_l1_ffn_conv_b': 'delta_w', 'delta_l1_ffn_down': 'delta_w', 'delta_final_norm': 'delta_w', 'new_m_c_ctx': 'new_m', 'new_m_l0_ada_w': 'new_m', 'new_m_l0_ada_b': 'new_m', 'new_m_l0_norm1': 'new_m', 'new_m_l0_w_in': 'new_m', 'new_m_l0_conv_a': 'new_m', 'new_m_l0_q_norm': 'new_m', 'new_m_l0_w_qb': 'new_m', 'new_m_l0_kv_norm': 'new_m', 'new_m_l0_w_kvb': 'new_m', 'new_m_l0_w_out': 'new_m', 'new_m_l0_norm2': 'new_m', 'new_m_l0_ffn_up': 'new_m', 'new_m_l0_ffn_conv_w': 'new_m', 'new_m_l0_ffn_conv_b': 'new_m', 'new_m_l0_ffn_down': 'new_m', 'new_m_l1_ada_w': 'new_m', 'new_m_l1_ada_b': 'new_m', 'new_m_l1_norm1': 'new_m', 'new_m_l1_w_in': 'new_m', 'new_m_l1_gate_fw_w': 'new_m', 'new_m_l1_gate_fw_b': 'new_m', 'new_m_l1_gate_bw_w': 'new_m', 'new_m_l1_gate_bw_b': 'new_m', 'new_m_l1_o_norm': 'new_m', 'new_m_l1_w_out': 'new_m', 'new_m_l1_norm2': 'new_m', 'new_m_l1_ffn_up': 'new_m', 'new_m_l1_ffn_conv_w': 'new_m', 'new_m_l1_ffn_conv_b': 'new_m', 'new_m_l1_ffn_down': 'new_m', 'new_m_final_norm': 'new_m', 'new_v_c_ctx': 'new_v', 'new_v_l0_ada_w': 'new_v', 'new_v_l0_ada_b': 'new_v', 'new_v_l0_norm1': 'new_v', 'new_v_l0_w_in': 'new_v', 'new_v_l0_conv_a': 'new_v', 'new_v_l0_q_norm': 'new_v', 'new_v_l0_w_qb': 'new_v', 'new_v_l0_kv_norm': 'new_v', 'new_v_l0_w_kvb': 'new_v', 'new_v_l0_w_out': 'new_v', 'new_v_l0_norm2': 'new_v', 'new_v_l0_ffn_up': 'new_v', 'new_v_l0_ffn_conv_w': 'new_v', 'new_v_l0_ffn_conv_b': 'new_v', 'new_v_l0_ffn_down': 'new_v', 'new_v_l1_ada_w': 'new_v', 'new_v_l1_ada_b': 'new_v', 'new_v_l1_norm1': 'new_v', 'new_v_l1_w_in': 'new_v', 'new_v_l1_gate_fw_w': 'new_v', 'new_v_l1_gate_fw_b': 'new_v', 'new_v_l1_gate_bw_w': 'new_v', 'new_v_l1_gate_bw_b': 'new_v', 'new_v_l1_o_norm': 'new_v', 'new_v_l1_w_out': 'new_v', 'new_v_l1_norm2': 'new_v', 'new_v_l1_ffn_up': 'new_v', 'new_v_l1_ffn_conv_w': 'new_v', 'new_v_l1_ffn_conv_b': 'new_v', 'new_v_l1_ffn_down': 'new_v', 'new_v_final_norm': 'new_v'}


def _forward(args):
    return _fwd_reference(*[args[k] for k in FWD_PARAMS])


def _output_shape():
    def fwd():
        inp = _fwd_setup_inputs(0)
        return _fwd_reference(*[inp[k] for k in FWD_PARAMS])
    out = _jax.eval_shape(fwd)
    return out.shape, out.dtype

N_MICROBATCH = 1
ADAM_LR = 0.001
ADAM_B1 = 0.9
ADAM_B2 = 0.999
ADAM_EPS = 1e-08
ADAM_WD = 0.01
ADAM_STEP = 10
PER_EXAMPLE_BATCH_AXIS = {'x': 0, 'c': 0, 'ctx': 0, 'loss_target': 0}
SHARED_INPUTS = []
_WEIGHT_DTYPES = {'c_ctx': _jnp.float32, 'l0_ada_w': _jnp.float32, 'l0_ada_b': _jnp.float32, 'l0_norm1': _jnp.float32, 'l0_w_in': _jnp.float32, 'l0_conv_a': _jnp.float32, 'l0_q_norm': _jnp.float32, 'l0_w_qb': _jnp.float32, 'l0_kv_norm': _jnp.float32, 'l0_w_kvb': _jnp.float32, 'l0_w_out': _jnp.float32, 'l0_norm2': _jnp.float32, 'l0_ffn_up': _jnp.float32, 'l0_ffn_conv_w': _jnp.float32, 'l0_ffn_conv_b': _jnp.float32, 'l0_ffn_down': _jnp.float32, 'l1_ada_w': _jnp.float32, 'l1_ada_b': _jnp.float32, 'l1_norm1': _jnp.float32, 'l1_w_in': _jnp.float32, 'l1_gate_fw_w': _jnp.float32, 'l1_gate_fw_b': _jnp.float32, 'l1_gate_bw_w': _jnp.float32, 'l1_gate_bw_b': _jnp.float32, 'l1_o_norm': _jnp.float32, 'l1_w_out': _jnp.float32, 'l1_norm2': _jnp.float32, 'l1_ffn_up': _jnp.float32, 'l1_ffn_conv_w': _jnp.float32, 'l1_ffn_conv_b': _jnp.float32, 'l1_ffn_down': _jnp.float32, 'final_norm': _jnp.float32}
MOMENT_SCALE = {'c_ctx': 6.179930e-03, 'l0_ada_w': 4.579581e-02, 'l0_ada_b': 8.042019e-02, 'l0_norm1': 5.937676e-02, 'l0_w_in': 4.367796e-02, 'l0_conv_a': 4.965989e-02, 'l0_q_norm': 5.198359e-03, 'l0_w_qb': 2.988878e-03, 'l0_kv_norm': 2.042912e-02, 'l0_w_kvb': 7.733356e-03, 'l0_w_out': 3.556309e-02, 'l0_norm2': 3.898080e-02, 'l0_ffn_up': 1.692529e-02, 'l0_ffn_conv_w': 1.685931e-02, 'l0_ffn_conv_b': 1.549947e-02, 'l0_ffn_down': 2.759087e-02, 'l1_ada_w': 3.775317e-02, 'l1_ada_b': 6.475297e-02, 'l1_norm1': 4.674581e-02, 'l1_w_in': 2.805631e-02, 'l1_gate_fw_w': 4.271125e-03, 'l1_gate_fw_b': 1.102901e-02, 'l1_gate_bw_w': 4.081729e-03, 'l1_gate_bw_b': 1.153161e-02, 'l1_o_norm': 5.015716e-02, 'l1_w_out': 2.315907e-02, 'l1_norm2': 3.354948e-02, 'l1_ffn_up': 1.529397e-02, 'l1_ffn_conv_w': 1.540647e-02, 'l1_ffn_conv_b': 1.370329e-02, 'l1_ffn_down': 2.508120e-02, 'final_norm': 3.197716e+01}


def _to_microbatches(a, axis):
    t = _jnp.moveaxis(a, axis, 0)
    t = t.reshape((N_MICROBATCH, t.shape[0] // N_MICROBATCH) + t.shape[1:])
    return _jnp.moveaxis(t, 1, axis + 1)


def setup_inputs(seed: int = 0) -> dict:
    inp = _fwd_setup_inputs(seed)
    key = _jax.random.fold_in(_jax.random.key(seed), 7919)
    shape, _ = _output_shape()
    out = dict(inp)
    out["loss_target"] = _jax.random.normal(_jax.random.fold_in(key, 0), shape, _jnp.float32)
    for i, name in enumerate(TWIN_WEIGHTS):
        w = inp[name].astype(_jnp.float32)
        if MOMENT_SCALE is None:
            s = _jnp.sqrt(_jnp.mean(_jnp.square(w)) + 1e-30)
        else:
            s = MOMENT_SCALE[name]
        km, kv = _jax.random.split(_jax.random.fold_in(key, i + 1))
        out[name] = w
        out["m_" + name] = s * _jax.random.normal(km, w.shape, _jnp.float32)
        out["v_" + name] = (s * s) * _jax.random.uniform(kv, w.shape, _jnp.float32, 0.5, 1.5)
    if N_MICROBATCH > 1:
        for name, axis in PER_EXAMPLE_BATCH_AXIS.items():
            out[name] = _to_microbatches(out[name], axis)
    return {'x': out['x'], 'c': out['c'], 'ctx': out['ctx'], 'c_ctx': out['c_ctx'], 'l0_ada_w': out['l0_ada_w'], 'l0_ada_b': out['l0_ada_b'], 'l0_norm1': out['l0_norm1'], 'l0_w_in': out['l0_w_in'], 'l0_conv_a': out['l0_conv_a'], 'l0_q_norm': out['l0_q_norm'], 'l0_w_qb': out['l0_w_qb'], 'l0_kv_norm': out['l0_kv_norm'], 'l0_w_kvb': out['l0_w_kvb'], 'l0_w_out': out['l0_w_out'], 'l0_norm2': out['l0_norm2'], 'l0_ffn_up': out['l0_ffn_up'], 'l0_ffn_conv_w': out['l0_ffn_conv_w'], 'l0_ffn_conv_b': out['l0_ffn_conv_b'], 'l0_ffn_down': out['l0_ffn_down'], 'l1_ada_w': out['l1_ada_w'], 'l1_ada_b': out['l1_ada_b'], 'l1_norm1': out['l1_norm1'], 'l1_w_in': out['l1_w_in'], 'l1_gate_fw_w': out['l1_gate_fw_w'], 'l1_gate_fw_b': out['l1_gate_fw_b'], 'l1_gate_bw_w': out['l1_gate_bw_w'], 'l1_gate_bw_b': out['l1_gate_bw_b'], 'l1_o_norm': out['l1_o_norm'], 'l1_w_out': out['l1_w_out'], 'l1_norm2': out['l1_norm2'], 'l1_ffn_up': out['l1_ffn_up'], 'l1_ffn_conv_w': out['l1_ffn_conv_w'], 'l1_ffn_conv_b': out['l1_ffn_conv_b'], 'l1_ffn_down': out['l1_ffn_down'], 'final_norm': out['final_norm'], 'loss_target': out['loss_target'], 'm_c_ctx': out['m_c_ctx'], 'm_l0_ada_w': out['m_l0_ada_w'], 'm_l0_ada_b': out['m_l0_ada_b'], 'm_l0_norm1': out['m_l0_norm1'], 'm_l0_w_in': out['m_l0_w_in'], 'm_l0_conv_a': out['m_l0_conv_a'], 'm_l0_q_norm': out['m_l0_q_norm'], 'm_l0_w_qb': out['m_l0_w_qb'], 'm_l0_kv_norm': out['m_l0_kv_norm'], 'm_l0_w_kvb': out['m_l0_w_kvb'], 'm_l0_w_out': out['m_l0_w_out'], 'm_l0_norm2': out['m_l0_norm2'], 'm_l0_ffn_up': out['m_l0_ffn_up'], 'm_l0_ffn_conv_w': out['m_l0_ffn_conv_w'], 'm_l0_ffn_conv_b': out['m_l0_ffn_conv_b'], 'm_l0_ffn_down': out['m_l0_ffn_down'], 'm_l1_ada_w': out['m_l1_ada_w'], 'm_l1_ada_b': out['m_l1_ada_b'], 'm_l1_norm1': out['m_l1_norm1'], 'm_l1_w_in': out['m_l1_w_in'], 'm_l1_gate_fw_w': out['m_l1_gate_fw_w'], 'm_l1_gate_fw_b': out['m_l1_gate_fw_b'], 'm_l1_gate_bw_w': out['m_l1_gate_bw_w'], 'm_l1_gate_bw_b': out['m_l1_gate_bw_b'], 'm_l1_o_norm': out['m_l1_o_norm'], 'm_l1_w_out': out['m_l1_w_out'], 'm_l1_norm2': out['m_l1_norm2'], 'm_l1_ffn_up': out['m_l1_ffn_up'], 'm_l1_ffn_conv_w': out['m_l1_ffn_conv_w'], 'm_l1_ffn_conv_b': out['m_l1_ffn_conv_b'], 'm_l1_ffn_down': out['m_l1_ffn_down'], 'm_final_norm': out['m_final_norm'], 'v_c_ctx': out['v_c_ctx'], 'v_l0_ada_w': out['v_l0_ada_w'], 'v_l0_ada_b': out['v_l0_ada_b'], 'v_l0_norm1': out['v_l0_norm1'], 'v_l0_w_in': out['v_l0_w_in'], 'v_l0_conv_a': out['v_l0_conv_a'], 'v_l0_q_norm': out['v_l0_q_norm'], 'v_l0_w_qb': out['v_l0_w_qb'], 'v_l0_kv_norm': out['v_l0_kv_norm'], 'v_l0_w_kvb': out['v_l0_w_kvb'], 'v_l0_w_out': out['v_l0_w_out'], 'v_l0_norm2': out['v_l0_norm2'], 'v_l0_ffn_up': out['v_l0_ffn_up'], 'v_l0_ffn_conv_w': out['v_l0_ffn_conv_w'], 'v_l0_ffn_conv_b': out['v_l0_ffn_conv_b'], 'v_l0_ffn_down': out['v_l0_ffn_down'], 'v_l1_ada_w': out['v_l1_ada_w'], 'v_l1_ada_b': out['v_l1_ada_b'], 'v_l1_norm1': out['v_l1_norm1'], 'v_l1_w_in': out['v_l1_w_in'], 'v_l1_gate_fw_w': out['v_l1_gate_fw_w'], 'v_l1_gate_fw_b': out['v_l1_gate_fw_b'], 'v_l1_gate_bw_w': out['v_l1_gate_bw_w'], 'v_l1_gate_bw_b': out['v_l1_gate_bw_b'], 'v_l1_o_norm': out['v_l1_o_norm'], 'v_l1_w_out': out['v_l1_w_out'], 'v_l1_norm2': out['v_l1_norm2'], 'v_l1_ffn_up': out['v_l1_ffn_up'], 'v_l1_ffn_conv_w': out['v_l1_ffn_conv_w'], 'v_l1_ffn_conv_b': out['v_l1_ffn_conv_b'], 'v_l1_ffn_down': out['v_l1_ffn_down'], 'v_final_norm': out['v_final_norm']}


def _loss(weights, diff, rest, loss_target):
    with _jax.named_scope("forward"):
        args = {**rest, TWIN_DIFF_INPUT: diff, **{k: w.astype(_WEIGHT_DTYPES[k]) for k, w in weights.items()}}
        y = _forward(args)
    with _jax.named_scope("loss_head"):
        err = _jnp.square(y.astype(_jnp.float32) - loss_target)
        return 0.5 * _jnp.sum(_jnp.mean(err, axis=-1)) if err.ndim else 0.5 * err


def _adamw(w, g, m, v):
    m = ADAM_B1 * m + (1.0 - ADAM_B1) * g
    v = ADAM_B2 * v + (1.0 - ADAM_B2) * _jnp.square(g)
    m_hat = m / (1.0 - ADAM_B1 ** ADAM_STEP)
    v_hat = v / (1.0 - ADAM_B2 ** ADAM_STEP)
    delta = -ADAM_LR * (m_hat / (_jnp.sqrt(v_hat) + ADAM_EPS) + ADAM_WD * w)
    return delta, m, v


def reference(x, c, ctx, c_ctx, l0_ada_w, l0_ada_b, l0_norm1, l0_w_in, l0_conv_a, l0_q_norm, l0_w_qb, l0_kv_norm, l0_w_kvb, l0_w_out, l0_norm2, l0_ffn_up, l0_ffn_conv_w, l0_ffn_conv_b, l0_ffn_down, l1_ada_w, l1_ada_b, l1_norm1, l1_w_in, l1_gate_fw_w, l1_gate_fw_b, l1_gate_bw_w, l1_gate_bw_b, l1_o_norm, l1_w_out, l1_norm2, l1_ffn_up, l1_ffn_conv_w, l1_ffn_conv_b, l1_ffn_down, final_norm, loss_target, m_c_ctx, m_l0_ada_w, m_l0_ada_b, m_l0_norm1, m_l0_w_in, m_l0_conv_a, m_l0_q_norm, m_l0_w_qb, m_l0_kv_norm, m_l0_w_kvb, m_l0_w_out, m_l0_norm2, m_l0_ffn_up, m_l0_ffn_conv_w, m_l0_ffn_conv_b, m_l0_ffn_down, m_l1_ada_w, m_l1_ada_b, m_l1_norm1, m_l1_w_in, m_l1_gate_fw_w, m_l1_gate_fw_b, m_l1_gate_bw_w, m_l1_gate_bw_b, m_l1_o_norm, m_l1_w_out, m_l1_norm2, m_l1_ffn_up, m_l1_ffn_conv_w, m_l1_ffn_conv_b, m_l1_ffn_down, m_final_norm, v_c_ctx, v_l0_ada_w, v_l0_ada_b, v_l0_norm1, v_l0_w_in, v_l0_conv_a, v_l0_q_norm, v_l0_w_qb, v_l0_kv_norm, v_l0_w_kvb, v_l0_w_out, v_l0_norm2, v_l0_ffn_up, v_l0_ffn_conv_w, v_l0_ffn_conv_b, v_l0_ffn_down, v_l1_ada_w, v_l1_ada_b, v_l1_norm1, v_l1_w_in, v_l1_gate_fw_w, v_l1_gate_fw_b, v_l1_gate_bw_w, v_l1_gate_bw_b, v_l1_o_norm, v_l1_w_out, v_l1_norm2, v_l1_ffn_up, v_l1_ffn_conv_w, v_l1_ffn_conv_b, v_l1_ffn_down, v_final_norm):
    given = dict(x=x, c=c, ctx=ctx, c_ctx=c_ctx, l0_ada_w=l0_ada_w, l0_ada_b=l0_ada_b, l0_norm1=l0_norm1, l0_w_in=l0_w_in, l0_conv_a=l0_conv_a, l0_q_norm=l0_q_norm, l0_w_qb=l0_w_qb, l0_kv_norm=l0_kv_norm, l0_w_kvb=l0_w_kvb, l0_w_out=l0_w_out, l0_norm2=l0_norm2, l0_ffn_up=l0_ffn_up, l0_ffn_conv_w=l0_ffn_conv_w, l0_ffn_conv_b=l0_ffn_conv_b, l0_ffn_down=l0_ffn_down, l1_ada_w=l1_ada_w, l1_ada_b=l1_ada_b, l1_norm1=l1_norm1, l1_w_in=l1_w_in, l1_gate_fw_w=l1_gate_fw_w, l1_gate_fw_b=l1_gate_fw_b, l1_gate_bw_w=l1_gate_bw_w, l1_gate_bw_b=l1_gate_bw_b, l1_o_norm=l1_o_norm, l1_w_out=l1_w_out, l1_norm2=l1_norm2, l1_ffn_up=l1_ffn_up, l1_ffn_conv_w=l1_ffn_conv_w, l1_ffn_conv_b=l1_ffn_conv_b, l1_ffn_down=l1_ffn_down, final_norm=final_norm, loss_target=loss_target, m_c_ctx=m_c_ctx, m_l0_ada_w=m_l0_ada_w, m_l0_ada_b=m_l0_ada_b, m_l0_norm1=m_l0_norm1, m_l0_w_in=m_l0_w_in, m_l0_conv_a=m_l0_conv_a, m_l0_q_norm=m_l0_q_norm, m_l0_w_qb=m_l0_w_qb, m_l0_kv_norm=m_l0_kv_norm, m_l0_w_kvb=m_l0_w_kvb, m_l0_w_out=m_l0_w_out, m_l0_norm2=m_l0_norm2, m_l0_ffn_up=m_l0_ffn_up, m_l0_ffn_conv_w=m_l0_ffn_conv_w, m_l0_ffn_conv_b=m_l0_ffn_conv_b, m_l0_ffn_down=m_l0_ffn_down, m_l1_ada_w=m_l1_ada_w, m_l1_ada_b=m_l1_ada_b, m_l1_norm1=m_l1_norm1, m_l1_w_in=m_l1_w_in, m_l1_gate_fw_w=m_l1_gate_fw_w, m_l1_gate_fw_b=m_l1_gate_fw_b, m_l1_gate_bw_w=m_l1_gate_bw_w, m_l1_gate_bw_b=m_l1_gate_bw_b, m_l1_o_norm=m_l1_o_norm, m_l1_w_out=m_l1_w_out, m_l1_norm2=m_l1_norm2, m_l1_ffn_up=m_l1_ffn_up, m_l1_ffn_conv_w=m_l1_ffn_conv_w, m_l1_ffn_conv_b=m_l1_ffn_conv_b, m_l1_ffn_down=m_l1_ffn_down, m_final_norm=m_final_norm, v_c_ctx=v_c_ctx, v_l0_ada_w=v_l0_ada_w, v_l0_ada_b=v_l0_ada_b, v_l0_norm1=v_l0_norm1, v_l0_w_in=v_l0_w_in, v_l0_conv_a=v_l0_conv_a, v_l0_q_norm=v_l0_q_norm, v_l0_w_qb=v_l0_w_qb, v_l0_kv_norm=v_l0_kv_norm, v_l0_w_kvb=v_l0_w_kvb, v_l0_w_out=v_l0_w_out, v_l0_norm2=v_l0_norm2, v_l0_ffn_up=v_l0_ffn_up, v_l0_ffn_conv_w=v_l0_ffn_conv_w, v_l0_ffn_conv_b=v_l0_ffn_conv_b, v_l0_ffn_down=v_l0_ffn_down, v_l1_ada_w=v_l1_ada_w, v_l1_ada_b=v_l1_ada_b, v_l1_norm1=v_l1_norm1, v_l1_w_in=v_l1_w_in, v_l1_gate_fw_w=v_l1_gate_fw_w, v_l1_gate_fw_b=v_l1_gate_fw_b, v_l1_gate_bw_w=v_l1_gate_bw_w, v_l1_gate_bw_b=v_l1_gate_bw_b, v_l1_o_norm=v_l1_o_norm, v_l1_w_out=v_l1_w_out, v_l1_norm2=v_l1_norm2, v_l1_ffn_up=v_l1_ffn_up, v_l1_ffn_conv_w=v_l1_ffn_conv_w, v_l1_ffn_conv_b=v_l1_ffn_conv_b, v_l1_ffn_down=v_l1_ffn_down, v_final_norm=v_final_norm)
    weights = {n: given[n] for n in TWIN_WEIGHTS}
    shared = {n: given[n] for n in SHARED_INPUTS}
    per_example = {n: given[n] for n in ['x', 'c', 'ctx']}
    grad_fn = _jax.value_and_grad(_loss, argnums=(0, 1))

    def one_microbatch(ex, loss_target):
        ex = dict(ex)
        diff = ex.pop(TWIN_DIFF_INPUT)
        return grad_fn(weights, diff, {**shared, **ex}, loss_target)

    if N_MICROBATCH == 1:
        loss, (grad_w, grad_x) = one_microbatch(per_example, given["loss_target"])
    else:
        def body(carry, xs):
            loss_sum, grad_sum = carry
            l_k, (gw_k, gx_k) = one_microbatch(xs[0], xs[1])
            with _jax.named_scope("update"):
                return (loss_sum + l_k, _jax.tree.map(_jnp.add, grad_sum, gw_k)), gx_k

        init = (_jnp.zeros((), _jnp.float32), _jax.tree.map(_jnp.zeros_like, weights))
        (loss, grad_w), grad_x = _jax.lax.scan(body, init, (per_example, given["loss_target"]))
    with _jax.named_scope("update"):
        delta_w, new_m, new_v = {}, {}, {}
        for n in TWIN_WEIGHTS:
            delta_w[n], new_m[n], new_v[n] = _adamw(weights[n], grad_w[n], given["m_" + n], given["v_" + n])
    return (loss, grad_x, *[grad_w[n] for n in TWIN_WEIGHTS], *[delta_w[n] for n in TWIN_WEIGHTS],
            *[new_m[n] for n in TWIN_WEIGHTS], *[new_v[n] for n in TWIN_WEIGHTS])
```

```python
import numpy as np

import jax
import jax.numpy as jnp
from jax import lax
from jax.experimental import pallas as pl
from jax.experimental.pallas import tpu as pltpu

F32, BF16 = jnp.float32, jnp.bfloat16
HIGHEST = lax.Precision.HIGHEST
MESH_ID = pl.DeviceIdType.MESH

EPS = 1e-6
N_MOD = 6
MLA_HEADS, QK_NOPE, QK_ROPE, V_HEAD, Q_LORA, KV_LORA = 8, 128, 64, 128, 512, 256
QK_PAD = 2 * QK_NOPE
ROPE_THETA, GRID_W = 10000.0, 64
GLA_HEADS, GATE_RANK, GATE_NORMALIZER, CHUNK = 4, 16, 16.0, 64
ADAM_LR, ADAM_B1, ADAM_B2, ADAM_EPS, ADAM_WD, ADAM_STEP = 0.001, 0.9, 0.999, 1e-08, 0.01, 10

LANES = 128
TR = 256
V7X_VMEM_BYTES = 64 * 2 ** 20
VMEM_LIMIT = V7X_VMEM_BYTES - 8 * 2 ** 20

NT_DIMS = (((1,), (1,)), ((), ()))
TN_DIMS = (((0,), (0,)), ((), ()))
NN_DIMS = (((1,), (0,)), ((), ()))

ROT_IDX = np.concatenate([np.arange(16, 32), np.arange(0, 16), np.arange(48, 64), np.arange(32, 48)])
ROT_SIGN = np.concatenate([-np.ones(16), np.ones(16), -np.ones(16), np.ones(16)]).astype(np.float32)


def _cparams(sem=None):
    return pltpu.CompilerParams(dimension_semantics=sem, vmem_limit_bytes=VMEM_LIMIT)


def _tile(dim, cap, quantum=LANES):
    if dim <= cap:
        return dim
    t = (cap // quantum) * quantum
    while t >= quantum:
        if dim % t == 0:
            return t
        t -= quantum
    return dim


_REL_XY = ((1, 0, 0), (0, 1, 0), (1, 1, 0))
_REL_ALL = tuple((a, b, c) for a in (0, 1) for b in (0, 1) for c in (0, 1))[1:]
_REL_C = ((0, 0, 1),)


def _exchange(name, arrs, mode):
    rels = {"ag_xy": _REL_XY, "a2a_xy": _REL_XY, "ag_all": _REL_ALL, "swap_c": _REL_C}[mode]
    n, nr = len(arrs), len(rels)
    lead = {"ag_xy": (4,), "ag_all": (8,), "a2a_xy": (), "swap_c": ()}[mode]
    out_shape = tuple(jax.ShapeDtypeStruct(lead + a.shape, a.dtype) for a in arrs)

    def body(*refs):
        ins, outs = refs[:n], refs[n:2 * n]
        send, recv, loc = refs[2 * n:]
        x, y, c = lax.axis_index("x"), lax.axis_index("y"), lax.axis_index("c")
        chip, dev = 2 * x + y, 4 * x + 2 * y + c
        started, local = [], []
        for a in range(n):
            if mode != "swap_c":
                slot = dev if mode == "ag_all" else chip
                src = ins[a].at[chip] if mode == "a2a_xy" else ins[a]
                lc = pltpu.make_async_copy(src, outs[a].at[slot], loc.at[a])
                lc.start()
                local.append(lc)
            for r, (bx, by, bc) in enumerate(rels):
                px = 1 - x if bx else x
                py = 1 - y if by else y
                pc = 1 - c if bc else c
                pchip, pdev = 2 * px + py, 4 * px + 2 * py + pc
                if mode == "ag_xy":
                    src, dst, mine = ins[a], outs[a].at[chip], outs[a].at[pchip]
                elif mode == "ag_all":
                    src, dst, mine = ins[a], outs[a].at[dev], outs[a].at[pdev]
                elif mode == "a2a_xy":
                    src, dst, mine = ins[a].at[pchip], outs[a].at[chip], outs[a].at[pchip]
                else:
                    src, dst, mine = ins[a], outs[a], outs[a]
                k = a * nr + r
                cp = pltpu.make_async_remote_copy(src_ref=src, dst_ref=dst, send_sem=send.at[k], recv_sem=recv.at[k],
                                                  device_id=(px, py, pc), device_id_type=MESH_ID)
                cp.start()
                landing = pltpu.make_async_remote_copy(src_ref=src, dst_ref=mine, send_sem=send.at[k],
                                                       recv_sem=recv.at[k], device_id=(px, py, pc),
                                                       device_id_type=MESH_ID)
                started.append((cp, landing))
        for cp, landing in started:
            cp.wait_send()
            landing.wait_recv()
        for lc in local:
            lc.wait()

    hbm = pl.BlockSpec(memory_space=pl.ANY)
    res = pl.pallas_call(
        body, name=name, out_shape=out_shape, in_specs=[hbm] * n, out_specs=tuple([hbm] * n),
        scratch_shapes=[pltpu.SemaphoreType.DMA((n * nr,)), pltpu.SemaphoreType.DMA((n * nr,)),
                        pltpu.SemaphoreType.DMA((max(n, 1),))],
    )(*arrs)
    return list(res)


def _matmul(name, a, b, mode, M, N, K, out_dtype=F32, a_off=(0, 0), b_off=(0, 0), tm_cap=1024, tn_cap=1024, tk_cap=2048):
    tm, tn = _tile(M, tm_cap, LANES if M % LANES == 0 else 8), _tile(N, tn_cap)
    tk = K if K <= 4096 else _tile(K, tk_cap)
    nk = K // tk
    assert M % tm == 0 and N % tn == 0 and K % tk == 0, (name, M, N, K, tm, tn, tk)
    if mode == "nn":
        ab, bb, dims = (tm, tk), (tk, tn), NN_DIMS
        ai = lambda i, j, k: (i + a_off[0] // tm, k + a_off[1] // tk)
        bi = lambda i, j, k: (k + b_off[0] // tk, j + b_off[1] // tn)
        chk = (a_off[0] % tm, a_off[1] % tk, b_off[0] % tk, b_off[1] % tn)
    elif mode == "nt":
        ab, bb, dims = (tm, tk), (tn, tk), NT_DIMS
        ai = lambda i, j, k: (i + a_off[0] // tm, k + a_off[1] // tk)
        bi = lambda i, j, k: (j + b_off[0] // tn, k + b_off[1] // tk)
        chk = (a_off[0] % tm, a_off[1] % tk, b_off[0] % tn, b_off[1] % tk)
    else:
        ab, bb, dims = (tk, tm), (tk, tn), TN_DIMS
        ai = lambda i, j, k: (k + a_off[0] // tk, i + a_off[1] // tm)
        bi = lambda i, j, k: (k + b_off[0] // tk, j + b_off[1] // tn)
        chk = (a_off[0] % tk, a_off[1] % tm, b_off[0] % tk, b_off[1] % tn)
    assert not any(chk), (name, chk)

    def body(a_ref, b_ref, o_ref, *acc):
        p = lax.dot_general(a_ref[...].astype(BF16), b_ref[...].astype(BF16), dims, preferred_element_type=F32)
        if nk == 1:
            o_ref[...] = p.astype(out_dtype)
        else:
            k = pl.program_id(2)

            @pl.when(k == 0)
            def _():
                acc[0][...] = p

            @pl.when(k > 0)
            def _():
                acc[0][...] += p

            @pl.when(k == nk - 1)
            def _():
                o_ref[...] = acc[0][...].astype(out_dtype)

    return pl.pallas_call(
        body, name=name, out_shape=jax.ShapeDtypeStruct((M, N), out_dtype), grid=(M // tm, N // tn, nk),
        in_specs=[pl.BlockSpec(ab, ai), pl.BlockSpec(bb, bi)], out_specs=pl.BlockSpec((tm, tn), lambda i, j, k: (i, j)),
        scratch_shapes=[pltpu.VMEM((tm, tn), F32)] if nk > 1 else [],
        compiler_params=_cparams(("parallel", "parallel", "arbitrary")),
    )(a, b)


def _rowwise(name, fn, grid_n, tr, ins, outs, nlat=None):
    nlat = grid_n if nlat is None else nlat
    grp = lambda i: jnp.minimum(i // nlat, 1)
    in_specs, args = [], []
    for spec in ins:
        kind, arr = spec[0], spec[1]
        if kind == "row":
            in_specs.append(pl.BlockSpec((tr, spec[3]), lambda i, cb=spec[2]: (i, cb)))
        elif kind == "rowc":
            in_specs.append(pl.BlockSpec((tr, spec[3]), lambda i, cb=spec[2], mx=spec[4]: (jnp.minimum(i, mx), cb)))
        elif kind == "prev":
            in_specs.append(pl.BlockSpec((8, spec[3]), lambda i, cb=spec[2]: (jnp.maximum(i * (tr // 8) - 1, 0), cb)))
        elif kind == "next":
            nb = arr.shape[0] // 8
            in_specs.append(pl.BlockSpec((8, spec[3]), lambda i, cb=spec[2], nb=nb: (jnp.minimum((i + 1) * (tr // 8), nb - 1), cb)))
        elif kind == "grp":
            in_specs.append(pl.BlockSpec((None,) + arr.shape[1:], lambda i: (grp(i), 0, 0)))
        else:
            in_specs.append(pl.BlockSpec(arr.shape, lambda i: (0, 0)))
        args.append(arr)
    out_shape, out_specs = [], []
    for spec in outs:
        if spec[0] == "row":
            out_shape.append(jax.ShapeDtypeStruct((spec[1], spec[2]), spec[3]))
            out_specs.append(pl.BlockSpec((tr, spec[2]), lambda i: (i, 0)))
        elif spec[0] == "acc":
            out_shape.append(jax.ShapeDtypeStruct((spec[1], spec[2]), F32))
            out_specs.append(pl.BlockSpec((spec[1], spec[2]), lambda i: (0, 0)))
        else:
            out_shape.append(jax.ShapeDtypeStruct((2, spec[1], spec[2]), F32))
            out_specs.append(pl.BlockSpec((None, spec[1], spec[2]), lambda i: (grp(i), 0, 0)))
    n_in = len(ins)
    has_acc = any(s[0] != "row" for s in outs)

    def body(*refs):
        i = pl.program_id(0)
        res = fn(i, *[r[...] for r in refs[:n_in]])
        for spec, ref, val in zip(outs, refs[n_in:], res):
            if spec[0] == "row":
                ref[...] = val.astype(ref.dtype)
            else:
                first = (i == 0) if spec[0] == "acc" else jnp.logical_or(i == 0, i == nlat)

                @pl.when(first)
                def _(ref=ref, val=val):
                    ref[...] = val

                @pl.when(jnp.logical_not(first))
                def _(ref=ref, val=val):
                    ref[...] += val

    return pl.pallas_call(
        body, name=name, out_shape=tuple(out_shape), grid=(grid_n,), in_specs=in_specs, out_specs=tuple(out_specs),
        compiler_params=_cparams(("arbitrary",) if has_acc else ("parallel",)),
    )(*args)


def _vjp_of(fwd, n_in, wrt):
    def bwd(i, *args):
        _, pull = jax.vjp(lambda *a: fwd(i, *a), *args[:n_in])
        g = pull(tuple(args[n_in:]))
        return tuple(g[k] for k in wrt)
    return bwd


def _rms(x, w):
    return x * lax.rsqrt(jnp.mean(x * x, axis=-1, keepdims=True) + EPS) * w


def _modulate(x, w, shift, scale):
    return _rms(x, w) * (1.0 + scale) + shift


def _silu(x):
    return x * jax.nn.sigmoid(x)


def _log_sigmoid(x):
    return jnp.minimum(x, 0.0) - jnp.log(1.0 + jnp.exp(-jnp.abs(x)))


def _f_mod(i, x, w, sh, sc):
    return (_modulate(x, w, sh, sc),)


def _f_resmod(i, x, y, gate, w, sh, sc):
    x1 = x + gate * y
    return x1, _modulate(x1, w, sh, sc)


def _f_norms(i, kvl, ql, kvw, qw):
    return _rms(kvl, kvw), _rms(ql, qw)


def _f_silu(i, x):
    return (_silu(x),)


def _rope128(x2, cos, sin):
    return x2 * cos + pltpu.roll(x2, QK_ROPE, 1) * sin


def _rope128_t(d2, cos, sin):
    return d2 * cos + pltpu.roll(d2 * sin, QK_ROPE, 1)


def _shifters(i, tr, T, TA):
    rows = i * tr + lax.broadcasted_iota(jnp.int32, (tr, 1), 0)
    loc = lax.broadcasted_iota(jnp.int32, (tr, 1), 0)
    has_prev = jnp.logical_and(rows != 0, rows != T)
    has_next = jnp.logical_and(rows != T - 1, rows != TA - 1)

    def prev(x, halo):
        r = jnp.where(loc == 0, halo[7:8, :], pltpu.roll(x, 1, 0))
        return jnp.where(has_prev, r, 0.0)

    def nxt(x, halo):
        r = jnp.where(loc == tr - 1, halo[0:1, :], pltpu.roll(x, tr - 1, 0))
        return jnp.where(has_next, r, 0.0)

    return prev, nxt


def _colsum(x):
    return jnp.sum(x, axis=0, keepdims=True)


def _attn_dims(T, TA):
    tq = TR
    tk = _tile(TA, 1408)
    return tq, TA // tq, T // tq, tk, TA // tk


def _attn_scores(q_ref, kv_ref, kr_ref, qi, ki, nql, tk, T):
    q, kvb = q_ref[...], kv_ref[...]
    s = lax.dot_general(q[:, :QK_NOPE], kvb[:, :QK_NOPE], NT_DIMS, preferred_element_type=F32)
    s = s + lax.dot_general(q[:, QK_NOPE:], kr_ref[...], NT_DIMS, preferred_element_type=F32)
    s = s * ((QK_NOPE + QK_ROPE) ** -0.5)
    col = ki * tk + lax.broadcasted_iota(jnp.int32, (1, tk), 1)
    return jnp.where(jnp.logical_and(qi >= nql, col < T), -1e30, s)


def _attn_fwd(q, kv, kr, T, TA):
    H = MLA_HEADS
    tq, nq, nql, tk, nk = _attn_dims(T, TA)

    def body(q_ref, kv_ref, kr_ref, o_ref, lse_ref, m_sc, l_sc, acc_sc):
        qi, ki = pl.program_id(1), pl.program_id(2)

        @pl.when(ki == 0)
        def _():
            m_sc[...] = jnp.full_like(m_sc, -jnp.inf)
            l_sc[...] = jnp.zeros_like(l_sc)
            acc_sc[...] = jnp.zeros_like(acc_sc)

        @pl.when(jnp.logical_or(qi < nql, (ki + 1) * tk > T))
        def _():
            s = _attn_scores(q_ref, kv_ref, kr_ref, qi, ki, nql, tk, T)
            m_new = jnp.maximum(m_sc[...], jnp.max(s, axis=-1, keepdims=True))
            alpha = jnp.exp(m_sc[...] - m_new)
            p = jnp.exp(s - m_new)
            l_sc[...] = alpha * l_sc[...] + jnp.sum(p, axis=-1, keepdims=True)
            acc_sc[...] = alpha * acc_sc[...] + jnp.dot(p.astype(BF16), kv_ref[:, QK_NOPE:], preferred_element_type=F32)
            m_sc[...] = m_new

        @pl.when(ki == nk - 1)
        def _():
            o_ref[...] = (acc_sc[...] / l_sc[...]).astype(o_ref.dtype)
            lse_ref[...] = m_sc[...] + jnp.log(l_sc[...])

    return pl.pallas_call(
        body, name="mla_fwd", grid=(H, nq, nk),
        out_shape=(jax.ShapeDtypeStruct((TA, H * V_HEAD), BF16), jax.ShapeDtypeStruct((H, TA, 1), F32)),
        in_specs=[pl.BlockSpec((tq, QK_PAD), lambda h, i, k: (i, h)), pl.BlockSpec((tk, QK_PAD), lambda h, i, k: (k, h)),
                  pl.BlockSpec((tk, LANES), lambda h, i, k: (k, 0))],
        out_specs=(pl.BlockSpec((tq, V_HEAD), lambda h, i, k: (i, h)), pl.BlockSpec((None, tq, 1), lambda h, i, k: (h, i, 0))),
        scratch_shapes=[pltpu.VMEM((tq, 1), F32), pltpu.VMEM((tq, 1), F32), pltpu.VMEM((tq, V_HEAD), F32)],
        compiler_params=_cparams(("parallel", "parallel", "arbitrary")),
    )(q, kv, kr)


def _attn_probs(q_ref, kv_ref, kr_ref, o_ref, do_ref, lse_ref, qi, ki, nql, tk, T):
    s = _attn_scores(q_ref, kv_ref, kr_ref, qi, ki, nql, tk, T)
    p = jnp.exp(s - lse_ref[...])
    do = do_ref[...]
    delta = jnp.sum(do * o_ref[...].astype(F32), axis=-1, keepdims=True)
    dp = lax.dot_general(do.astype(BF16), kv_ref[:, QK_NOPE:], NT_DIMS, preferred_element_type=F32)
    ds = p * (dp - delta) * ((QK_NOPE + QK_ROPE) ** -0.5)
    return p.astype(BF16), ds.astype(BF16), do.astype(BF16)


def _attn_bwd_specs(tq, tk, do_cb0, q_major):
    qm = (lambda h, a, b: a) if q_major else (lambda h, a, b: b)
    km = (lambda h, a, b: b) if q_major else (lambda h, a, b: a)
    return [pl.BlockSpec((tq, QK_PAD), lambda h, a, b: (qm(h, a, b), h)),
            pl.BlockSpec((tk, QK_PAD), lambda h, a, b: (km(h, a, b), h)),
            pl.BlockSpec((tk, LANES), lambda h, a, b: (km(h, a, b), 0)),
            pl.BlockSpec((tq, V_HEAD), lambda h, a, b: (qm(h, a, b), h)),
            pl.BlockSpec((tq, V_HEAD), lambda h, a, b: (qm(h, a, b), do_cb0 + h)),
            pl.BlockSpec((None, tq, 1), lambda h, a, b: (h, qm(h, a, b), 0))]


def _attn_bwd_dq(q, kv, kr, o, dcat, lse, do_cb0, T, TA):
    H = MLA_HEADS
    tq, nq, nql, tk, nk = _attn_dims(T, TA)

    def body(q_ref, kv_ref, kr_ref, o_ref, do_ref, lse_ref, dq_ref, acc):
        qi, ki = pl.program_id(1), pl.program_id(2)

        @pl.when(ki == 0)
        def _():
            acc[...] = jnp.zeros_like(acc)

        @pl.when(jnp.logical_or(qi < nql, (ki + 1) * tk > T))
        def _():
            _, ds, _ = _attn_probs(q_ref, kv_ref, kr_ref, o_ref, do_ref, lse_ref, qi, ki, nql, tk, T)
            acc[:, :QK_NOPE] += jnp.dot(ds, kv_ref[:, :QK_NOPE], preferred_element_type=F32)
            acc[:, QK_NOPE:] += jnp.dot(ds, kr_ref[...], preferred_element_type=F32)

        @pl.when(ki == nk - 1)
        def _():
            dq_ref[...] = acc[...]

    return pl.pallas_call(
        body, name="mla_bwd_dq", grid=(H, nq, nk), out_shape=jax.ShapeDtypeStruct((TA, H * QK_PAD), F32),
        in_specs=_attn_bwd_specs(tq, tk, do_cb0, True), out_specs=pl.BlockSpec((tq, QK_PAD), lambda h, i, k: (i, h)),
        scratch_shapes=[pltpu.VMEM((tq, QK_PAD), F32)], compiler_params=_cparams(("parallel", "parallel", "arbitrary")),
    )(q, kv, kr, o, dcat, lse)


def _attn_bwd_dkv(q, kv, kr, o, dcat, lse, do_cb0, T, TA):
    H = MLA_HEADS
    tq, nq, nql, tk, nk = _attn_dims(T, TA)

    def body(q_ref, kv_ref, kr_ref, o_ref, do_ref, lse_ref, dkv_ref, dkr_ref, acc, acc_r):
        ki, qi = pl.program_id(1), pl.program_id(2)

        @pl.when(qi == 0)
        def _():
            acc[...] = jnp.zeros_like(acc)
            acc_r[...] = jnp.zeros_like(acc_r)

        @pl.when(jnp.logical_or(qi < nql, (ki + 1) * tk > T))
        def _():
            p, ds, do = _attn_probs(q_ref, kv_ref, kr_ref, o_ref, do_ref, lse_ref, qi, ki, nql, tk, T)
            acc[:, QK_NOPE:] += lax.dot_general(p, do, TN_DIMS, preferred_element_type=F32)
            acc[:, :QK_NOPE] += lax.dot_general(ds, q_ref[:, :QK_NOPE], TN_DIMS, preferred_element_type=F32)
            acc_r[...] += lax.dot_general(ds, q_ref[:, QK_NOPE:], TN_DIMS, preferred_element_type=F32)

        @pl.when(qi == nq - 1)
        def _():
            dkv_ref[...] = acc[...].astype(dkv_ref.dtype)
            dkr_ref[...] = acc_r[...]

    return pl.pallas_call(
        body, name="mla_bwd_dkv", grid=(H, nk, nq),
        out_shape=(jax.ShapeDtypeStruct((TA, H * QK_PAD), BF16), jax.ShapeDtypeStruct((H, TA, LANES), F32)),
        in_specs=_attn_bwd_specs(tq, tk, do_cb0, False),
        out_specs=(pl.BlockSpec((tk, QK_PAD), lambda h, k, i: (k, h)), pl.BlockSpec((None, tk, LANES), lambda h, k, i: (h, k, 0))),
        scratch_shapes=[pltpu.VMEM((tk, QK_PAD), F32), pltpu.VMEM((tk, LANES), F32)],
        compiler_params=_cparams(("parallel", "parallel", "arbitrary")),
    )(q, kv, kr, o, dcat, lse)


def _gla_tile(st, q, k, v, g, rev, q_scale):
    nc = q.shape[0] // CHUNK
    ii = lax.broadcasted_iota(jnp.int32, (CHUNK, CHUNK), 0)
    jj = lax.broadcasted_iota(jnp.int32, (CHUNK, CHUNK), 1)
    tri = (jj >= ii) if rev else (jj <= ii)
    ones = tri.astype(F32)
    outs = [None] * nc
    for ci in (range(nc - 1, -1, -1) if rev else range(nc)):
        sl = slice(ci * CHUNK, (ci + 1) * CHUNK)
        qc, kc, vc, gc = q[sl] * q_scale, k[sl], v[sl].astype(BF16), g[sl]
        b = jnp.dot(ones, gc, precision=HIGHEST, preferred_element_type=F32)
        bl = b[0:1] if rev else b[CHUNK - 1:CHUNK]
        kd = (kc * jnp.exp(bl - b)).astype(BF16)
        qe = (qc * jnp.exp(b)).astype(BF16)
        ke = (kc * jnp.exp(-b)).astype(BF16)
        att = jnp.where(tri, lax.dot_general(qe, ke, NT_DIMS, preferred_element_type=F32), 0.0)
        outs[ci] = (jnp.dot(att.astype(BF16), vc, preferred_element_type=F32)
                    + lax.dot_general(qe, st.astype(BF16), NT_DIMS, preferred_element_type=F32))
        st = st * jnp.exp(bl) + lax.dot_general(vc, kd, TN_DIMS, preferred_element_type=F32)
    return st, jnp.concatenate(outs, axis=0)


def _gla_specs(dims, rev, nt, nlat):
    DK, DV, q_off, k_off, g_off = dims
    tile = (lambda p: nt - 1 - p) if rev else (lambda p: (p + nlat) % nt)
    return tile, [pl.BlockSpec((TR, DK), lambda h, p: (tile(p), q_off // DK + h)),
                  pl.BlockSpec((TR, DK), lambda h, p: (tile(p), k_off // DK + h)),
                  pl.BlockSpec((TR, DV), lambda h, p: (tile(p), h)),
                  pl.BlockSpec((TR, DK), lambda h, p: (tile(p), g_off // DK + h))]


def _gla_fwd(name, z1, g, dims, rev, nt, nlat):
    DK, DV = dims[0], dims[1]
    TA = nt * TR
    _, in_specs = _gla_specs(dims, rev, nt, nlat)
    tile = (lambda p: nt - 1 - p) if rev else (lambda p: (p + nlat) % nt)

    def body(q_ref, k_ref, v_ref, g_ref, o_ref, st_ref, st):
        @pl.when(pl.program_id(1) == 0)
        def _():
            st[...] = jnp.zeros_like(st)

        st_ref[...] = st[...]
        new, out = _gla_tile(st[...], q_ref[...], k_ref[...], v_ref[...], g_ref[...], rev, DK ** -0.5)
        st[...] = new
        o_ref[...] = out

    return pl.pallas_call(
        body, name=name, grid=(GLA_HEADS, nt),
        out_shape=(jax.ShapeDtypeStruct((TA, GLA_HEADS * DV), F32), jax.ShapeDtypeStruct((GLA_HEADS, nt, DV, DK), F32)),
        in_specs=in_specs,
        out_specs=(pl.BlockSpec((TR, DV), lambda h, p: (tile(p), h)), pl.BlockSpec((None, None, DV, DK), lambda h, p: (h, p, 0, 0))),
        scratch_shapes=[pltpu.VMEM((DV, DK), F32)], compiler_params=_cparams(("parallel", "arbitrary")),
    )(z1, z1, z1, g)


def _gla_bwd(name, z1, g, states, do, dims, rev, nt, nlat):
    DK, DV = dims[0], dims[1]
    TA = nt * TR
    fwd_tile = (lambda p: nt - 1 - p) if rev else (lambda p: (p + nlat) % nt)
    pos = lambda s: nt - 1 - s
    DKq, DVq, q_off, k_off, g_off = dims
    in_specs = [pl.BlockSpec((TR, DK), lambda h, s: (fwd_tile(pos(s)), q_off // DK + h)),
                pl.BlockSpec((TR, DK), lambda h, s: (fwd_tile(pos(s)), k_off // DK + h)),
                pl.BlockSpec((TR, DV), lambda h, s: (fwd_tile(pos(s)), h)),
                pl.BlockSpec((TR, DK), lambda h, s: (fwd_tile(pos(s)), g_off // DK + h)),
                pl.BlockSpec((None, None, DV, DK), lambda h, s: (h, pos(s), 0, 0)),
                pl.BlockSpec((TR, DV), lambda h, s: (fwd_tile(pos(s)), h))]
    row_k = pl.BlockSpec((TR, DK), lambda h, s: (fwd_tile(pos(s)), h))
    row_v = pl.BlockSpec((TR, DV), lambda h, s: (fwd_tile(pos(s)), h))

    def body(q_ref, k_ref, v_ref, g_ref, st_ref, do_ref, dq_ref, dk_ref, dv_ref, dg_ref, dst):
        @pl.when(pl.program_id(1) == 0)
        def _():
            dst[...] = jnp.zeros_like(dst)

        _, pull = jax.vjp(lambda st, q, k, v, gg: _gla_tile(st, q, k, v, gg, rev, DK ** -0.5),
                          st_ref[...], q_ref[...], k_ref[...], v_ref[...], g_ref[...])
        d_st, dq, dk, dv, dg = pull((dst[...], do_ref[...]))
        dst[...] = d_st
        dq_ref[...] = dq
        dk_ref[...] = dk
        dv_ref[...] = dv
        dg_ref[...] = dg

    KEY, VAL = GLA_HEADS * DK, GLA_HEADS * DV
    return pl.pallas_call(
        body, name=name, grid=(GLA_HEADS, nt),
        out_shape=(jax.ShapeDtypeStruct((TA, KEY), F32), jax.ShapeDtypeStruct((TA, KEY), F32),
                   jax.ShapeDtypeStruct((TA, VAL), F32), jax.ShapeDtypeStruct((TA, KEY), F32)),
        in_specs=in_specs, out_specs=(row_k, row_k, row_v, row_k),
        scratch_shapes=[pltpu.VMEM((DV, DK), F32)], compiler_params=_cparams(("parallel", "arbitrary")),
    )(z1, z1, z1, g, states, do)


def _sum_slots(name, arr):
    S, R, C = arr.shape
    tr = _tile(R, max(8, (2 ** 20) // max(C, 1) // 8 * 8), 8)

    def body(a_ref, o_ref):
        acc = a_ref[0]
        for s in range(1, S):
            acc = acc + a_ref[s]
        o_ref[...] = acc

    return pl.pallas_call(
        body, name=name, grid=(R // tr,), out_shape=jax.ShapeDtypeStruct((R, C), F32),
        in_specs=[pl.BlockSpec((S, tr, C), lambda i: (0, i, 0))], out_specs=pl.BlockSpec((tr, C), lambda i: (i, 0)),
        compiler_params=_cparams(("parallel",)),
    )(arr)


def _adam_math(w, g, m, v):
    m2 = ADAM_B1 * m + (1.0 - ADAM_B1) * g
    v2 = ADAM_B2 * v + (1.0 - ADAM_B2) * (g * g)
    m_hat = m2 / (1.0 - ADAM_B1 ** ADAM_STEP)
    v_hat = v2 / (1.0 - ADAM_B2 ** ADAM_STEP)
    return -ADAM_LR * (m_hat / (jnp.sqrt(v_hat) + ADAM_EPS) + ADAM_WD * w), m2, v2


def _adam(name, w, gs, m, v):
    R, C = w.shape
    tr = _tile(R, max(8, (2 ** 19) // max(C, 1) // 8 * 8), 8)
    ng = len(gs)

    def fn(i, w_, *rest):
        g = rest[0] if ng == 1 else rest[0] + rest[1]
        d, m2, v2 = _adam_math(w_, g, rest[ng], rest[ng + 1])
        return g, d, m2, v2

    ins = [("row", a, 0, C) for a in (w, *gs, m, v)]
    return _rowwise(name, fn, R // tr, tr, ins, [("row", R, C, F32)] * 4)


def _ffn_fwd(tag, h, w_up, cw, cb, w_down, n_rows, T, TA, F):
    tr = 128
    u = _matmul(tag + "_up", h, w_up, "nn", n_rows, 2 * F, h.shape[1])
    cwg, cwv, cbg, cbv = cw[:, :F], cw[:, F:], cb[:, :F], cb[:, F:]

    def fn(i, ug, uv, pg, pv, ng, nv, wg, wv, bg, bv):
        prev, nxt = _shifters(i, tr, T, TA)
        cg = wg[0:1] * prev(ug, pg) + wg[1:2] * ug + wg[2:3] * nxt(ug, ng) + bg
        cv = wv[0:1] * prev(uv, pv) + wv[1:2] * uv + wv[2:3] * nxt(uv, nv) + bv
        return (_silu(cg) * cv,)

    ins = [("row", u, 0, F), ("row", u, 1, F), ("prev", u, 0, F), ("prev", u, 1, F), ("next", u, 0, F), ("next", u, 1, F),
           ("full", cwg), ("full", cwv), ("full", cbg), ("full", cbv)]
    (act,) = _rowwise(tag + "_conv", fn, n_rows // tr, tr, ins, [("row", n_rows, F, BF16)])
    f = _matmul(tag + "_down", act, w_down, "nn", n_rows, w_down.shape[1], F)
    return u, act, f


def _ffn_bwd(tag, df, h, u, act, w_up, cw, cb, w_down, n_rows, T, TA, F):
    tr = 128
    D = w_down.shape[1]
    dact = _matmul(tag + "_ddown_x", df, w_down, "nt", n_rows, F, D)
    g_down = _matmul(tag + "_ddown_w", act, df, "tn", F, D, n_rows)
    cwg, cwv, cbg, cbv = cw[:, :F], cw[:, F:], cb[:, :F], cb[:, F:]

    def fn_a(i, ug, uv, pg, pv, ng, nv, da, wg, wv, bg, bv):
        prev, nxt = _shifters(i, tr, T, TA)
        ugp, ugn, uvp, uvn = prev(ug, pg), nxt(ug, ng), prev(uv, pv), nxt(uv, nv)
        cg = wg[0:1] * ugp + wg[1:2] * ug + wg[2:3] * ugn + bg
        cv = wv[0:1] * uvp + wv[1:2] * uv + wv[2:3] * uvn + bv
        sg = jax.nn.sigmoid(cg)
        dcv = da * (cg * sg)
        dcg = da * cv * (sg * (1.0 + cg * (1.0 - sg)))
        return (jnp.concatenate([dcg, dcv], axis=1),
                _colsum(ugp * dcg), _colsum(ug * dcg), _colsum(ugn * dcg),
                _colsum(uvp * dcv), _colsum(uv * dcv), _colsum(uvn * dcv), _colsum(dcg), _colsum(dcv))

    ins = [("row", u, 0, F), ("row", u, 1, F), ("prev", u, 0, F), ("prev", u, 1, F), ("next", u, 0, F), ("next", u, 1, F),
           ("row", dact, 0, F), ("full", cwg), ("full", cwv), ("full", cbg), ("full", cbv)]
    res = _rowwise(tag + "_dconv_a", fn_a, n_rows // tr, tr, ins, [("row", n_rows, 2 * F, F32)] + [("acc", 1, F)] * 8)
    duc = res[0]
    g_cw = jnp.concatenate([jnp.concatenate(res[1:4], axis=0), jnp.concatenate(res[4:7], axis=0)], axis=1)
    g_cb = jnp.concatenate([res[7], res[8]], axis=1)

    def fn_b(i, dg, dv, pg, pv, ng, nv, wg, wv):
        prev, nxt = _shifters(i, tr, T, TA)
        dug = wg[0:1] * nxt(dg, ng) + wg[1:2] * dg + wg[2:3] * prev(dg, pg)
        duv = wv[0:1] * nxt(dv, nv) + wv[1:2] * dv + wv[2:3] * prev(dv, pv)
        return (jnp.concatenate([dug, duv], axis=1),)

    ins = [("row", duc, 0, F), ("row", duc, 1, F), ("prev", duc, 0, F), ("prev", duc, 1, F), ("next", duc, 0, F),
           ("next", duc, 1, F), ("full", cwg), ("full", cwv)]
    (du,) = _rowwise(tag + "_dconv_b", fn_b, n_rows // tr, tr, ins, [("row", n_rows, 2 * F, BF16)])
    dh = _matmul(tag + "_dup_x", du, w_up, "nt", n_rows, D, 2 * F)
    g_up = _matmul(tag + "_dup_w", h, du, "tn", D, 2 * F, n_rows)
    return dh, g_up, g_cw, g_cb, g_down


def _cols_to_shards(g):
    r, c = g.shape
    return g.reshape(r, 4, c // 4).transpose(1, 0, 2)


def _shards_to_cols(s):
    return s.transpose(1, 0, 2).reshape(s.shape[1], 4 * s.shape[2])


def kernel(x, c, ctx, c_ctx, l0_ada_w, l0_ada_b, l0_norm1, l0_w_in, l0_conv_a, l0_q_norm, l0_w_qb, l0_kv_norm, l0_w_kvb, l0_w_out, l0_norm2, l0_ffn_up, l0_ffn_conv_w, l0_ffn_conv_b, l0_ffn_down, l1_ada_w, l1_ada_b, l1_norm1, l1_w_in, l1_gate_fw_w, l1_gate_fw_b, l1_gate_bw_w, l1_gate_bw_b, l1_o_norm, l1_w_out, l1_norm2, l1_ffn_up, l1_ffn_conv_w, l1_ffn_conv_b, l1_ffn_down, final_norm, loss_target, m_c_ctx, m_l0_ada_w, m_l0_ada_b, m_l0_norm1, m_l0_w_in, m_l0_conv_a, m_l0_q_norm, m_l0_w_qb, m_l0_kv_norm, m_l0_w_kvb, m_l0_w_out, m_l0_norm2, m_l0_ffn_up, m_l0_ffn_conv_w, m_l0_ffn_conv_b, m_l0_ffn_down, m_l1_ada_w, m_l1_ada_b, m_l1_norm1, m_l1_w_in, m_l1_gate_fw_w, m_l1_gate_fw_b, m_l1_gate_bw_w, m_l1_gate_bw_b, m_l1_o_norm, m_l1_w_out, m_l1_norm2, m_l1_ffn_up, m_l1_ffn_conv_w, m_l1_ffn_conv_b, m_l1_ffn_down, m_final_norm, v_c_ctx, v_l0_ada_w, v_l0_ada_b, v_l0_norm1, v_l0_w_in, v_l0_conv_a, v_l0_q_norm, v_l0_w_qb, v_l0_kv_norm, v_l0_w_kvb, v_l0_w_out, v_l0_norm2, v_l0_ffn_up, v_l0_ffn_conv_w, v_l0_ffn_conv_b, v_l0_ffn_down, v_l1_ada_w, v_l1_ada_b, v_l1_norm1, v_l1_w_in, v_l1_gate_fw_w, v_l1_gate_fw_b, v_l1_gate_bw_w, v_l1_gate_bw_b, v_l1_o_norm, v_l1_w_out, v_l1_norm2, v_l1_ffn_up, v_l1_ffn_conv_w, v_l1_ffn_conv_b, v_l1_ffn_down, v_final_norm):
    W = dict(c_ctx=c_ctx, l0_ada_w=l0_ada_w, l0_ada_b=l0_ada_b, l0_norm1=l0_norm1, l0_w_in=l0_w_in, l0_conv_a=l0_conv_a, l0_q_norm=l0_q_norm, l0_w_qb=l0_w_qb, l0_kv_norm=l0_kv_norm, l0_w_kvb=l0_w_kvb, l0_w_out=l0_w_out, l0_norm2=l0_norm2, l0_ffn_up=l0_ffn_up, l0_ffn_conv_w=l0_ffn_conv_w, l0_ffn_conv_b=l0_ffn_conv_b, l0_ffn_down=l0_ffn_down, l1_ada_w=l1_ada_w, l1_ada_b=l1_ada_b, l1_norm1=l1_norm1, l1_w_in=l1_w_in, l1_gate_fw_w=l1_gate_fw_w, l1_gate_fw_b=l1_gate_fw_b, l1_gate_bw_w=l1_gate_bw_w, l1_gate_bw_b=l1_gate_bw_b, l1_o_norm=l1_o_norm, l1_w_out=l1_w_out, l1_norm2=l1_norm2, l1_ffn_up=l1_ffn_up, l1_ffn_conv_w=l1_ffn_conv_w, l1_ffn_conv_b=l1_ffn_conv_b, l1_ffn_down=l1_ffn_down, final_norm=final_norm)
    MO = dict(c_ctx=m_c_ctx, l0_ada_w=m_l0_ada_w, l0_ada_b=m_l0_ada_b, l0_norm1=m_l0_norm1, l0_w_in=m_l0_w_in, l0_conv_a=m_l0_conv_a, l0_q_norm=m_l0_q_norm, l0_w_qb=m_l0_w_qb, l0_kv_norm=m_l0_kv_norm, l0_w_kvb=m_l0_w_kvb, l0_w_out=m_l0_w_out, l0_norm2=m_l0_norm2, l0_ffn_up=m_l0_ffn_up, l0_ffn_conv_w=m_l0_ffn_conv_w, l0_ffn_conv_b=m_l0_ffn_conv_b, l0_ffn_down=m_l0_ffn_down, l1_ada_w=m_l1_ada_w, l1_ada_b=m_l1_ada_b, l1_norm1=m_l1_norm1, l1_w_in=m_l1_w_in, l1_gate_fw_w=m_l1_gate_fw_w, l1_gate_fw_b=m_l1_gate_fw_b, l1_gate_bw_w=m_l1_gate_bw_w, l1_gate_bw_b=m_l1_gate_bw_b, l1_o_norm=m_l1_o_norm, l1_w_out=m_l1_w_out, l1_norm2=m_l1_norm2, l1_ffn_up=m_l1_ffn_up, l1_ffn_conv_w=m_l1_ffn_conv_w, l1_ffn_conv_b=m_l1_ffn_conv_b, l1_ffn_down=m_l1_ffn_down, final_norm=m_final_norm)
    VO = dict(c_ctx=v_c_ctx, l0_ada_w=v_l0_ada_w, l0_ada_b=v_l0_ada_b, l0_norm1=v_l0_norm1, l0_w_in=v_l0_w_in, l0_conv_a=v_l0_conv_a, l0_q_norm=v_l0_q_norm, l0_w_qb=v_l0_w_qb, l0_kv_norm=v_l0_kv_norm, l0_w_kvb=v_l0_w_kvb, l0_w_out=v_l0_w_out, l0_norm2=v_l0_norm2, l0_ffn_up=v_l0_ffn_up, l0_ffn_conv_w=v_l0_ffn_conv_w, l0_ffn_conv_b=v_l0_ffn_conv_b, l0_ffn_down=v_l0_ffn_down, l1_ada_w=v_l1_ada_w, l1_ada_b=v_l1_ada_b, l1_norm1=v_l1_norm1, l1_w_in=v_l1_w_in, l1_gate_fw_w=v_l1_gate_fw_w, l1_gate_fw_b=v_l1_gate_fw_b, l1_gate_bw_w=v_l1_gate_bw_w, l1_gate_bw_b=v_l1_gate_bw_b, l1_o_norm=v_l1_o_norm, l1_w_out=v_l1_w_out, l1_norm2=v_l1_norm2, l1_ffn_up=v_l1_ffn_up, l1_ffn_conv_w=v_l1_ffn_conv_w, l1_ffn_conv_b=v_l1_ffn_conv_b, l1_ffn_down=v_l1_ffn_down, final_norm=v_final_norm)
    names = list(W)

    T, D = x.shape[1], x.shape[2]
    TC = ctx.shape[1]
    assert TC == TR and T % TR == 0
    TA = T + TC
    nt, nlat = TA // TR, T // TR
    CC = D // 2
    F = l0_ffn_down.shape[0] * 4
    DK, DV = D // 2 // GLA_HEADS, D // GLA_HEADS
    KEY, VAL = GLA_HEADS * DK, GLA_HEADS * DV
    ADA_S = l0_ada_w.shape[1]
    H = MLA_HEADS
    my_x, my_y, my_c = lax.axis_index("x"), lax.axis_index("y"), lax.axis_index("c")
    chip, dev = 2 * my_x + my_y, 4 * my_x + 2 * my_y + my_c
    row = lambda a: a.reshape(1, -1)

    big = ["l0_w_in", "l0_w_qb", "l0_w_kvb", "l0_w_out", "l0_ffn_up", "l0_ffn_down",
           "l1_w_in", "l1_w_out", "l1_ffn_up", "l1_ffn_down"]
    small_sh = ["l0_conv_a", "l0_ffn_conv_w", "l1_gate_fw_w", "l1_gate_bw_w", "l1_ffn_conv_w"]
    gathered = _exchange("gather_weights", [W[n].astype(BF16) for n in big] + [W[n] for n in small_sh], "ag_xy")
    G = dict(zip(big + small_sh, gathered))
    (c_all,) = _exchange("gather_c", [c], "ag_all")
    col_sh = {"l0_w_in", "l0_w_qb", "l0_w_kvb", "l0_ffn_up", "l1_w_in", "l1_ffn_up", "l0_conv_a", "l0_ffn_conv_w",
              "l1_gate_fw_w", "l1_gate_bw_w", "l1_ffn_conv_w"}
    full = {n: (_shards_to_cols(G[n]) if n in col_sh else G[n].reshape(-1, G[n].shape[-1])) for n in G}

    o_kv, o_kr, o_ql, o_ax = 0, KV_LORA, KV_LORA + QK_ROPE, KV_LORA + QK_ROPE + Q_LORA
    sel0 = np.concatenate([np.arange(o_ax, o_ax + 3 * CC), np.arange(o_ql, o_ql + Q_LORA), np.arange(o_kv, o_kv + KV_LORA),
                           np.arange(o_kr, o_kr + QK_ROPE)])
    W0 = -(-(len(sel0) + QK_ROPE) // 256) * 256
    QOFF0, KVOFF0, KROFF0 = 3 * CC, 3 * CC + Q_LORA, 3 * CC + Q_LORA + KV_LORA
    w_in0 = full["l0_w_in"]
    w_kr = w_in0[:, o_kr:o_kr + QK_ROPE]
    w_in0p = jnp.concatenate([w_in0[:, sel0], w_kr[:, ROT_IDX] * jnp.asarray(ROT_SIGN, BF16),
                              jnp.zeros((D, W0 - len(sel0) - QK_ROPE), BF16)], axis=1)
    wq3 = full["l0_w_qb"].reshape(Q_LORA, H, QK_NOPE + QK_ROPE)
    w_qbp = jnp.concatenate([wq3, wq3[:, :, QK_NOPE:][:, :, ROT_IDX] * jnp.asarray(ROT_SIGN, BF16)], axis=2).reshape(Q_LORA, H * QK_PAD)
    sel1 = np.concatenate([np.arange(KEY, KEY + VAL), np.arange(2 * KEY + VAL + 2 * GATE_RANK, 2 * KEY + 2 * VAL + 2 * GATE_RANK),
                           np.arange(0, KEY), np.arange(KEY + VAL + 2 * GATE_RANK, 2 * KEY + VAL + 2 * GATE_RANK),
                           np.arange(KEY + VAL, KEY + VAL + 2 * GATE_RANK)])
    W1 = 2 * VAL + 2 * KEY + LANES
    OGOFF, KOFF1, QOFF1, LROFF = VAL, 2 * VAL, 2 * VAL + KEY, 2 * VAL + 2 * KEY
    w_in1p = jnp.concatenate([full["l1_w_in"][:, sel1], jnp.zeros((D, W1 - len(sel1)), BF16)], axis=1)
    w_gate = jnp.zeros((LANES, 2 * KEY), F32)
    w_gate = w_gate.at[:GATE_RANK, :KEY].set(full["l1_gate_fw_w"]).at[GATE_RANK:2 * GATE_RANK, KEY:].set(full["l1_gate_bw_w"])
    b_gate = jnp.concatenate([l1_gate_fw_b, l1_gate_bw_b]).reshape(1, -1)

    pos = np.arange(T)
    inv = ROPE_THETA ** (-np.arange(0, QK_ROPE // 2, 2, dtype=np.float32) / (QK_ROPE // 2))
    ar, ac = (pos // GRID_W).astype(np.float32)[:, None] * inv, (pos % GRID_W).astype(np.float32)[:, None] * inv
    ang = jnp.asarray(np.concatenate([ar, ar, ac, ac], axis=-1).astype(np.float32))
    zpad = jnp.zeros((TA, LANES - QK_ROPE), F32)
    cos_t = jnp.concatenate([jnp.concatenate([jnp.cos(ang), jnp.ones((TC, QK_ROPE), F32)], axis=0), zpad], axis=1)
    sin_t = jnp.concatenate([jnp.concatenate([jnp.sin(ang), jnp.zeros((TC, QK_ROPE), F32)], axis=0), zpad], axis=1)

    c16 = jnp.concatenate([c_all.reshape(8, D), c_ctx.reshape(1, D), jnp.zeros((7, D), F32)], axis=0)
    (act16,) = _rowwise("silu_c", _f_silu, 1, 16, [("row", c16, 0, D)], [("row", 16, D, F32)])
    mod_parts = [_matmul("ada_fwd", act16, W["l%d_ada_w" % l], "nn", 16, ADA_S, D, tn_cap=512) for l in (0, 1)]
    mod_g = _exchange("gather_mod", mod_parts, "ag_xy")
    mods = []
    for l in (0, 1):
        mfull = _shards_to_cols(mod_g[l]) + W["l%d_ada_b" % l][None, :]
        mine = lax.dynamic_slice_in_dim(mfull, dev, 1, axis=0)
        mods.append(jnp.concatenate([mine, mfull[8:9]], axis=0).reshape(2, N_MOD, D))
    P = lambda l, k: mods[l][:, k:k + 1, :]

    X = jnp.concatenate([x[0], ctx[0]], axis=0)
    n1_0, n2_0, n1_1, n2_1 = row(l0_norm1), row(l0_norm2), row(l1_norm1), row(l1_norm2)
    (h0,) = _rowwise("l0_mod1", _f_mod, nt, TR, [("row", X, 0, D), ("full", n1_0), ("grp", P(0, 0)), ("grp", P(0, 1))],
                     [("row", TA, D, BF16)], nlat)
    z0 = _matmul("l0_in", h0, w_in0p, "nn", TA, W0, D)
    kvn_w, qn_w = row(l0_kv_norm), row(l0_q_norm)
    norm_ins = [("row", z0, KVOFF0 // KV_LORA, KV_LORA), ("row", z0, QOFF0 // Q_LORA, Q_LORA), ("full", kvn_w), ("full", qn_w)]
    kvn, qn = _rowwise("l0_latnorm", _f_norms, nt, TR, norm_ins, [("row", TA, KV_LORA, BF16), ("row", TA, Q_LORA, BF16)])
    kv = _matmul("l0_kvb", kvn, full["l0_w_kvb"], "nn", TA, H * QK_PAD, KV_LORA, out_dtype=BF16)
    qraw = _matmul("l0_qb", qn, w_qbp, "nn", TA, H * QK_PAD, Q_LORA)

    def f_rope(i, qr, krr, cs, sn):
        pieces = []
        for h in range(H):
            pieces += [qr[:, h * QK_PAD:h * QK_PAD + QK_NOPE], _rope128(qr[:, h * QK_PAD + QK_NOPE:(h + 1) * QK_PAD], cs, sn)]
        return jnp.concatenate(pieces, axis=1), _rope128(krr, cs, sn)

    rope_ins = [("row", qraw, 0, H * QK_PAD), ("row", z0, KROFF0 // LANES, LANES), ("row", cos_t, 0, LANES), ("row", sin_t, 0, LANES)]
    q, kr = _rowwise("l0_rope", f_rope, nt, TR, rope_ins, [("row", TA, H * QK_PAD, BF16), ("row", TA, LANES, BF16)])
    o_att, lse = _attn_fwd(q, kv, kr, T, TA)

    conv_a = full["l0_conv_a"]

    def f_conva(i, ax, ab, ac_, pax, pac, nax, nac, w):
        prev, nxt = _shifters(i, TR, T, TA)
        p = ac_ * ax
        return (ab * (w[0:1] * prev(p, pac * pax) + w[1:2] * p + w[2:3] * nxt(p, nac * nax)),)

    conva_ins = [("row", z0, 0, CC), ("row", z0, 1, CC), ("row", z0, 2, CC), ("prev", z0, 0, CC), ("prev", z0, 2, CC),
                 ("next", z0, 0, CC), ("next", z0, 2, CC), ("full", conv_a)]
    (y_a,) = _rowwise("l0_conva", f_conva, nt, TR, conva_ins, [("row", TA, CC, BF16)])
    cat0 = jnp.concatenate([y_a, o_att], axis=1)
    y0 = _matmul("l0_out", cat0, full["l0_w_out"], "nn", TA, D, CC + H * V_HEAD)
    resmod_outs = [("row", TA, D, F32), ("row", TA, D, BF16)]
    X1, h2_0 = _rowwise("l0_resmod2", _f_resmod, nt, TR,
                        [("row", X, 0, D), ("row", y0, 0, D), ("grp", P(0, 2)), ("full", n2_0), ("grp", P(0, 3)), ("grp", P(0, 4))],
                        resmod_outs, nlat)
    cw0, cb0 = full["l0_ffn_conv_w"], row(l0_ffn_conv_b)
    u0, act0, f0 = _ffn_fwd("l0_ffn", h2_0, full["l0_ffn_up"], cw0, cb0, full["l0_ffn_down"], TA, T, TA, F)
    X2, h1 = _rowwise("l1_resmod1", _f_resmod, nt, TR,
                      [("row", X1, 0, D), ("row", f0, 0, D), ("grp", P(0, 5)), ("full", n1_1), ("grp", P(1, 0)), ("grp", P(1, 1))],
                      resmod_outs, nlat)

    z1 = _matmul("l1_in", h1, w_in1p, "nn", TA, W1, D)

    def f_gates(i, lr, wg, bg):
        pre = jnp.dot(lr.astype(BF16), wg.astype(BF16), preferred_element_type=F32) + bg
        return (_log_sigmoid(pre) / GATE_NORMALIZER,)

    gate_ins = [("row", z1, LROFF // LANES, LANES), ("full", w_gate), ("full", b_gate)]
    (gates,) = _rowwise("l1_gates", f_gates, nt, TR, gate_ins, [("row", TA, 2 * KEY, F32)])
    dims_f, dims_b = (DK, DV, QOFF1, KOFF1, 0), (DK, DV, QOFF1, KOFF1, KEY)
    o_f, st_f = _gla_fwd("gla_fwd_f", z1, gates, dims_f, False, nt, nlat)
    o_b, st_b = _gla_fwd("gla_fwd_b", z1, gates, dims_b, True, nt, nlat)
    onw = row(l1_o_norm)

    def f_glaout(i, of, ob, og, w):
        o = of + ob
        on = jnp.concatenate([_rms(o[:, h * DV:(h + 1) * DV], w) for h in range(GLA_HEADS)], axis=1)
        return (on * _silu(og),)

    glaout_ins = [("row", o_f, 0, VAL), ("row", o_b, 0, VAL), ("row", z1, OGOFF // VAL, VAL), ("full", onw)]
    (go,) = _rowwise("l1_glaout", f_glaout, nlat, TR, glaout_ins, [("row", T, VAL, BF16)])
    y1 = _matmul("l1_out", go, full["l1_w_out"], "nn", T, D, VAL)
    X3, h2_1 = _rowwise("l1_resmod2", _f_resmod, nlat, TR,
                        [("row", X2, 0, D), ("row", y1, 0, D), ("grp", P(1, 2)), ("full", n2_1), ("grp", P(1, 3)), ("grp", P(1, 4))],
                        [("row", T, D, F32), ("row", T, D, BF16)])
    cw1, cb1 = full["l1_ffn_conv_w"], row(l1_ffn_conv_b)
    u1, act1, f1 = _ffn_fwd("l1_ffn", h2_1, full["l1_ffn_up"], cw1, cb1, full["l1_ffn_down"], T, T, T, F)

    fnw = row(final_norm)

    def f_head(i, x3, ff, g5, w, tgt):
        fin = lambda a, b, g_, w_: _rms(a + g_ * b, w_)
        y, pull = jax.vjp(fin, x3, ff, g5, w)
        err = y - tgt
        dx3, dff, dg5, dw = pull(err / D)
        loss = 0.5 * jnp.sum(jnp.mean(err * err, axis=-1, keepdims=True), axis=0, keepdims=True)
        return jnp.broadcast_to(loss, (1, LANES)), dx3, dff, dg5, dw

    head_ins = [("row", X3, 0, D), ("row", f1, 0, D), ("grp", P(1, 5)), ("full", fnw), ("row", loss_target[0], 0, D)]
    loss_acc, dX3, df1, dm5_1, g_final = _rowwise(
        "loss_head", f_head, nlat, TR, head_ins,
        [("acc", 1, LANES), ("row", T, D, F32), ("row", T, D, BF16), ("acc", 1, D), ("acc", 1, D)])
    loss = lax.psum(loss_acc[0, 0], ("x", "y", "c"))

    grads = {"final_norm": g_final}
    dh2_1, grads["l1_ffn_up"], grads["l1_ffn_conv_w"], grads["l1_ffn_conv_b"], grads["l1_ffn_down"] = _ffn_bwd(
        "l1_dffn", df1, h2_1, u1, act1, full["l1_ffn_up"], cw1, cb1, full["l1_ffn_down"], T, T, T, F)
    resmod_bwd = _vjp_of(_f_resmod, 6, (0, 1, 2, 3, 4, 5))
    dX2l, dy1, dm2_1, grads["l1_norm2"], dm3_1, dm4_1 = _rowwise(
        "l1_dresmod2", resmod_bwd, nlat, TR,
        [("row", X2, 0, D), ("row", y1, 0, D), ("grp", P(1, 2)), ("full", n2_1), ("grp", P(1, 3)), ("grp", P(1, 4)),
         ("row", dX3, 0, D), ("row", dh2_1, 0, D)],
        [("row", T, D, F32), ("row", T, D, BF16)] + [("acc", 1, D)] * 4)
    dgo = _matmul("l1_dout_x", dy1, full["l1_w_out"], "nt", T, VAL, D)
    grads["l1_w_out"] = _matmul("l1_dout_w", go, dy1, "tn", VAL, D, T)

    def f_glaout_bwd(i, of, ob, og, w, d):
        is_ctx = i >= nlat
        _, pull = jax.vjp(lambda a, b, c_, w_: f_glaout(i, a, b, c_, w_)[0], of, ob, og, w)
        dof, _, dog, dw = pull(jnp.where(is_ctx, 0.0, d))
        return dof, dog, dw

    glaout_b_ins = glaout_ins + [("rowc", dgo, 0, VAL, nlat - 1)]
    do_gla, dog, g_onorm = _rowwise("l1_dglaout", f_glaout_bwd, nt, TR, glaout_b_ins,
                                    [("row", TA, VAL, F32), ("row", TA, VAL, F32), ("acc", 1, DV)])
    grads["l1_o_norm"] = g_onorm
    dq_f, dk_f, dv_f, dg_f = _gla_bwd("gla_bwd_f", z1, gates, st_f, do_gla, dims_f, False, nt, nlat)
    dq_b, dk_b, dv_b, dg_b = _gla_bwd("gla_bwd_b", z1, gates, st_b, do_gla, dims_b, True, nt, nlat)

    def f_dz1(i, dvf, dvb, dog_, dkf, dkb, dqf, dqb, dgf, dgb, lr, wg, bg):
        _, pull = jax.vjp(lambda a, b, c_: f_gates(i, a, b, c_)[0], lr, wg, bg)
        dlr, dwg, dbg = pull(jnp.concatenate([dgf, dgb], axis=1))
        return jnp.concatenate([dvf + dvb, dog_, dkf + dkb, dqf + dqb, dlr], axis=1), dwg, dbg

    dz1_ins = [("row", dv_f, 0, VAL), ("row", dv_b, 0, VAL), ("row", dog, 0, VAL), ("row", dk_f, 0, KEY), ("row", dk_b, 0, KEY),
               ("row", dq_f, 0, KEY), ("row", dq_b, 0, KEY), ("row", dg_f, 0, KEY), ("row", dg_b, 0, KEY)] + gate_ins
    dz1, g_wgate, g_bgate = _rowwise("l1_dz", f_dz1, nt, TR, dz1_ins, [("row", TA, W1, BF16), ("acc", LANES, 2 * KEY), ("acc", 1, 2 * KEY)])
    grads["l1_gate_fw_w"], grads["l1_gate_bw_w"] = g_wgate[:GATE_RANK, :KEY], g_wgate[GATE_RANK:2 * GATE_RANK, KEY:]
    grads["l1_gate_fw_b"], grads["l1_gate_bw_b"] = g_bgate[:, :KEY], g_bgate[:, KEY:]
    dh1 = _matmul("l1_din_x", dz1, w_in1p, "nt", TA, D, W1)
    g_win1p = _matmul("l1_din_w", h1, dz1, "tn", D, W1, TA)
    grads["l1_w_in"] = g_win1p[:, np.argsort(sel1)]

    def f_resmod1_bwd(i, x_, y_, g_, w_, sh, sc, dx2, dh):
        return resmod_bwd(i, x_, y_, g_, w_, sh, sc, jnp.where(i >= nlat, 0.0, dx2), dh)

    dX1, df0, dm5_0, grads["l1_norm1"], dm0_1, dm1_1 = _rowwise(
        "l1_dresmod1", f_resmod1_bwd, nt, TR,
        [("row", X1, 0, D), ("row", f0, 0, D), ("grp", P(0, 5)), ("full", n1_1), ("grp", P(1, 0)), ("grp", P(1, 1)),
         ("rowc", dX2l, 0, D, nlat - 1), ("row", dh1, 0, D)],
        [("row", TA, D, F32), ("row", TA, D, BF16), ("accg", 1, D), ("acc", 1, D), ("accg", 1, D), ("accg", 1, D)], nlat)

    dh2_0, grads["l0_ffn_up"], grads["l0_ffn_conv_w"], grads["l0_ffn_conv_b"], grads["l0_ffn_down"] = _ffn_bwd(
        "l0_dffn", df0, h2_0, u0, act0, full["l0_ffn_up"], cw0, cb0, full["l0_ffn_down"], TA, T, TA, F)
    dXd, dy0, dm2_0, grads["l0_norm2"], dm3_0, dm4_0 = _rowwise(
        "l0_dresmod2", resmod_bwd, nt, TR,
        [("row", X, 0, D), ("row", y0, 0, D), ("grp", P(0, 2)), ("full", n2_0), ("grp", P(0, 3)), ("grp", P(0, 4)),
         ("row", dX1, 0, D), ("row", dh2_0, 0, D)],
        [("row", TA, D, F32), ("row", TA, D, BF16), ("accg", 1, D), ("acc", 1, D), ("accg", 1, D), ("accg", 1, D)], nlat)
    dcat = _matmul("l0_dout_x", dy0, full["l0_w_out"], "nt", TA, CC + H * V_HEAD, D)
    grads["l0_w_out"] = _matmul("l0_dout_w", cat0, dy0, "tn", CC + H * V_HEAD, D, TA)

    def f_conva_bwd(i, ax, ab, ac_, pax, pab, pac, nax, nab, nac, dy, pdy, ndy, w):
        prev, nxt = _shifters(i, TR, T, TA)
        p = ac_ * ax
        pp, pn = prev(p, pac * pax), nxt(p, nac * nax)
        cv = w[0:1] * pp + w[1:2] * p + w[2:3] * pn
        dcv = dy * ab
        dp = w[0:1] * nxt(dcv, ndy * nab) + w[1:2] * dcv + w[2:3] * prev(dcv, pdy * pab)
        return (jnp.concatenate([dp * ac_, dy * cv, dp * ax], axis=1), _colsum(pp * dcv), _colsum(p * dcv), _colsum(pn * dcv))

    conva_b_ins = [("row", z0, 0, CC), ("row", z0, 1, CC), ("row", z0, 2, CC),
                   ("prev", z0, 0, CC), ("prev", z0, 1, CC), ("prev", z0, 2, CC),
                   ("next", z0, 0, CC), ("next", z0, 1, CC), ("next", z0, 2, CC),
                   ("row", dcat, 0, CC), ("prev", dcat, 0, CC), ("next", dcat, 0, CC), ("full", conv_a)]
    dz_a, ga0, ga1, ga2 = _rowwise("l0_dconva", f_conva_bwd, nt, TR, conva_b_ins, [("row", TA, 3 * CC, BF16)] + [("acc", 1, CC)] * 3)
    grads["l0_conv_a"] = jnp.concatenate([ga0, ga1, ga2], axis=0)
    do_cb0 = CC // V_HEAD
    dq = _attn_bwd_dq(q, kv, kr, o_att, dcat, lse, do_cb0, T, TA)
    dkv, dkr_h = _attn_bwd_dkv(q, kv, kr, o_att, dcat, lse, do_cb0, T, TA)

    def f_rope_bwd(i, dq_, dkr_, cs, sn):
        pieces = []
        for h in range(H):
            pieces += [dq_[:, h * QK_PAD:h * QK_PAD + QK_NOPE], _rope128_t(dq_[:, h * QK_PAD + QK_NOPE:(h + 1) * QK_PAD], cs, sn)]
        dk = dkr_[0]
        for h in range(1, H):
            dk = dk + dkr_[h]
        return jnp.concatenate(pieces, axis=1), _rope128_t(dk, cs, sn)

    def rope_bwd_call():
        def body(dq_ref, dkr_ref, cs_ref, sn_ref, dqr_ref, dkk_ref):
            a, b = f_rope_bwd(0, dq_ref[...], dkr_ref[...], cs_ref[...], sn_ref[...])
            dqr_ref[...] = a.astype(BF16)
            dkk_ref[...] = b.astype(BF16)
        return pl.pallas_call(
            body, name="l0_drope", grid=(nt,),
            out_shape=(jax.ShapeDtypeStruct((TA, H * QK_PAD), BF16), jax.ShapeDtypeStruct((TA, LANES), BF16)),
            in_specs=[pl.BlockSpec((TR, H * QK_PAD), lambda i: (i, 0)), pl.BlockSpec((H, TR, LANES), lambda i: (0, i, 0)),
                      pl.BlockSpec((TR, LANES), lambda i: (i, 0)), pl.BlockSpec((TR, LANES), lambda i: (i, 0))],
            out_specs=(pl.BlockSpec((TR, H * QK_PAD), lambda i: (i, 0)), pl.BlockSpec((TR, LANES), lambda i: (i, 0))),
            compiler_params=_cparams(("parallel",)),
        )(dq, dkr_h, cos_t, sin_t)

    dqraw, dz_kr = rope_bwd_call()
    dqn = _matmul("l0_dqb_x", dqraw, w_qbp, "nt", TA, Q_LORA, H * QK_PAD)
    g_wqbp = _matmul("l0_dqb_w", qn, dqraw, "tn", Q_LORA, H * QK_PAD, TA).reshape(Q_LORA, H, QK_PAD)
    g_rope = g_wqbp[:, :, QK_NOPE:QK_NOPE + QK_ROPE] - g_wqbp[:, :, QK_NOPE + QK_ROPE:][:, :, ROT_IDX] * jnp.asarray(ROT_SIGN)
    grads["l0_w_qb"] = jnp.concatenate([g_wqbp[:, :, :QK_NOPE], g_rope], axis=2).reshape(Q_LORA, H * (QK_NOPE + QK_ROPE))
    dkvn = _matmul("l0_dkvb_x", dkv, full["l0_w_kvb"], "nt", TA, KV_LORA, H * QK_PAD)
    grads["l0_w_kvb"] = _matmul("l0_dkvb_w", kvn, dkv, "tn", KV_LORA, H * QK_PAD, TA)
    norms_bwd = _vjp_of(_f_norms, 4, (0, 1, 2, 3))
    dz_kv, dz_q, grads["l0_kv_norm"], grads["l0_q_norm"] = _rowwise(
        "l0_dlatnorm", norms_bwd, nt, TR, norm_ins + [("row", dkvn, 0, KV_LORA), ("row", dqn, 0, Q_LORA)],
        [("row", TA, KV_LORA, BF16), ("row", TA, Q_LORA, BF16), ("acc", 1, KV_LORA), ("acc", 1, Q_LORA)])
    dz0 = jnp.concatenate([dz_a, dz_q, dz_kv, dz_kr, jnp.zeros((TA, W0 - KROFF0 - LANES), BF16)], axis=1)
    dh0 = _matmul("l0_din_x", dz0, w_in0p, "nt", TA, D, W0)
    g_win0p = _matmul("l0_din_w", h0, dz0, "tn", D, W0, TA)
    n_sel = len(sel0)
    g_win0 = g_win0p[:, :n_sel][:, np.argsort(sel0)]
    fold = g_win0p[:, n_sel:n_sel + QK_ROPE][:, ROT_IDX] * jnp.asarray(-ROT_SIGN)
    grads["l0_w_in"] = g_win0.at[:, o_kr:o_kr + QK_ROPE].add(fold)

    def f_mod_bwd(i, x_, w_, sh, sc, dh, dxd):
        _, pull = jax.vjp(lambda a, b, c_, d_: _modulate(a, b, c_, d_), x_, w_, sh, sc)
        dx, dw, dsh, dsc = pull(dh)
        return dx + dxd, dw, dsh, dsc

    dXf, grads["l0_norm1"], dm0_0, dm1_0 = _rowwise(
        "l0_dmod1", f_mod_bwd, nt, TR,
        [("row", X, 0, D), ("full", n1_0), ("grp", P(0, 0)), ("grp", P(0, 1)), ("row", dh0, 0, D), ("row", dXd, 0, D)],
        [("row", TA, D, F32), ("acc", 1, D), ("accg", 1, D), ("accg", 1, D)], nlat)
    grad_x = dXf[:T][None]

    zD = jnp.zeros((1, D), F32)
    lat = lambda a: a[0] if a.ndim == 3 else a
    cxt = lambda a: a[1] if a.ndim == 3 else zD
    dmods = []
    for parts in ((dm0_0, dm1_0, dm2_0, dm3_0, dm4_0, dm5_0), (dm0_1, dm1_1, dm2_1, dm3_1, dm4_1, dm5_1)):
        dmods.append((jnp.concatenate([lat(a) for a in parts], axis=1), jnp.concatenate([cxt(a) for a in parts], axis=1)))
    small_names = ["l0_norm1", "l0_norm2", "l0_kv_norm", "l0_q_norm", "l0_conv_a", "l0_ffn_conv_w", "l0_ffn_conv_b",
                   "l1_norm1", "l1_norm2", "l1_o_norm", "l1_gate_fw_w", "l1_gate_fw_b", "l1_gate_bw_w", "l1_gate_bw_b",
                   "l1_ffn_conv_w", "l1_ffn_conv_b", "final_norm"]
    pieces = [("dm0", dmods[0][0]), ("dmc0", dmods[0][1]), ("dm1", dmods[1][0]), ("dmc1", dmods[1][1])]
    pieces += [("l0_ada_b", dmods[0][0] + dmods[0][1]), ("l1_ada_b", dmods[1][0] + dmods[1][1])]
    pieces += [(n, grads[n]) for n in small_names]
    offs, cur = {}, 0
    for n, a in pieces:
        offs[n] = (cur, a.size, a.shape)
        cur += -(-a.size // LANES) * LANES
    n_pad = -(-cur // 1024) * 1024
    flat = jnp.concatenate([jnp.pad(a.reshape(-1), (0, -a.size % LANES)) for _, a in pieces] + [jnp.zeros((n_pad - cur,), F32)])
    (small_g,) = _exchange("gather_small", [flat.reshape(n_pad // LANES, LANES)], "ag_all")
    small_sum = _sum_slots("sum_small", small_g).reshape(-1)
    small_all = small_g.reshape(8, -1)
    take = lambda n: small_sum[offs[n][0]:offs[n][0] + offs[n][1]].reshape(offs[n][2])

    cc_parts = []
    for l in (0, 1):
        o_m, sz, _ = offs["dm%d" % l]
        d16 = jnp.concatenate([small_all[:, o_m:o_m + sz], take("dmc%d" % l).reshape(1, -1), jnp.zeros((7, sz), F32)], axis=0)
        d16 = lax.dynamic_slice_in_dim(d16.reshape(16, 4, ADA_S), chip, 1, axis=1).reshape(16, ADA_S)
        grads["l%d_ada_w" % l] = _matmul("ada_dw", act16, d16, "tn", D, ADA_S, 16, tn_cap=512)
        cc_parts.append(_matmul("ada_dx", d16, W["l%d_ada_w" % l], "nt", 16, D, ADA_S, tn_cap=512))
        grads["l%d_ada_b" % l] = take("l%d_ada_b" % l).reshape(-1)
    cc_g = _exchange("gather_cc", cc_parts, "ag_xy")

    def cc_call():
        def body(a_ref, b_ref, c_ref, o_ref):
            tot = a_ref[0, 8:9, :] + b_ref[0, 8:9, :]
            for j in range(1, 4):
                tot = tot + (a_ref[j, 8:9, :] + b_ref[j, 8:9, :])
            _, pull = jax.vjp(_silu, c_ref[...])
            o_ref[...] = pull(tot)[0]
        return pl.pallas_call(body, name="c_ctx_grad", out_shape=jax.ShapeDtypeStruct((1, D), F32))(cc_g[0], cc_g[1], row(c_ctx))

    grads["c_ctx"] = cc_call().reshape(-1)

    row_sh = {"l0_w_out", "l1_w_out", "l0_ffn_down", "l1_ffn_down"}
    send = [grads[n].reshape((4, -1) + grads[n].shape[1:]) if n in row_sh else _cols_to_shards(grads[n]) for n in big]
    landed = _exchange("scatter_grads", send, "a2a_xy")
    halves = [_sum_slots("sum_" + n[3:], a) for n, a in zip(big, landed)]
    others = _exchange("swap_grads", halves, "swap_c")

    out_g, out_d, out_m, out_v = {}, {}, {}, {}
    for n, mine, other in zip(big, halves, others):
        out_g[n], out_d[n], out_m[n], out_v[n] = _adam("adam_" + n[3:], W[n], [mine, other], MO[n], VO[n])
    for l in (0, 1):
        n = "l%d_ada_w" % l
        out_g[n], out_d[n], out_m[n], out_v[n] = _adam("adam_ada_w", W[n], [grads[n]], MO[n], VO[n])
    rest = [n for n in names if n not in out_g]
    g_rest = {}
    for n in rest:
        if n == "c_ctx" or n.endswith("ada_b"):
            g_rest[n] = grads[n]
        elif n in small_sh:
            gfull = take(n)
            cs = gfull.shape[1] // 4
            g_rest[n] = lax.dynamic_slice_in_dim(gfull, chip * cs, cs, axis=1)
        else:
            g_rest[n] = take(n).reshape(W[n].shape)
    sizes = [W[n].size for n in rest]
    tot = sum(-(-s // LANES) * LANES for s in sizes)
    tot_pad = -(-tot // 1024) * 1024

    def pack(d):
        parts = [jnp.pad(d[n].reshape(-1), (0, -d[n].size % LANES)) for n in rest]
        return jnp.concatenate(parts + [jnp.zeros((tot_pad - tot,), F32)]).reshape(tot_pad // LANES, LANES)

    packed = _adam("adam_small", pack(W), [pack(g_rest)], pack(MO), pack(VO))
    cur = 0
    for n, s in zip(rest, sizes):
        for dst, arr in zip((out_g, out_d, out_m, out_v), packed):
            dst[n] = arr.reshape(-1)[cur:cur + s].reshape(W[n].shape)
        cur += -(-s // LANES) * LANES

    return (loss, grad_x, *[out_g[n] for n in names], *[out_d[n] for n in names],
            *[out_m[n] for n in names], *[out_v[n] for n in names])
```

```python
import numpy as np

import jax
import jax.numpy as jnp
from jax import lax
from jax.experimental import pallas as pl
from jax.experimental.pallas import tpu as pltpu

F32, BF16 = jnp.float32, jnp.bfloat16
HIGHEST = lax.Precision.HIGHEST
MESH_ID = pl.DeviceIdType.MESH

EPS = 1e-6
N_MOD = 6
MLA_HEADS, QK_NOPE, QK_ROPE, V_HEAD, Q_LORA, KV_LORA = 8, 128, 64, 128, 512, 256
QK_PAD = 2 * QK_NOPE
ROPE_THETA, GRID_W = 10000.0, 64
GLA_HEADS, GATE_RANK, GATE_NORMALIZER, CHUNK = 4, 16, 16.0, 64
ADAM_LR, ADAM_B1, ADAM_B2, ADAM_EPS, ADAM_WD, ADAM_STEP = 0.001, 0.9, 0.999, 1e-08, 0.01, 10

LANES = 128
TR = 256
V7X_VMEM_BYTES = 64 * 2 ** 20
VMEM_LIMIT = V7X_VMEM_BYTES - 8 * 2 ** 20

NT_DIMS = (((1,), (1,)), ((), ()))
TN_DIMS = (((0,), (0,)), ((), ()))
NN_DIMS = (((1,), (0,)), ((), ()))

def _rot_cols(w):
    q = QK_ROPE // 4
    return jnp.concatenate([-w[..., q:2 * q], w[..., :q], -w[..., 3 * q:], w[..., 2 * q:3 * q]], axis=-1)


def _rot_cols_t(g):
    q = QK_ROPE // 4
    return jnp.concatenate([g[..., q:2 * q], -g[..., :q], g[..., 3 * q:], -g[..., 2 * q:3 * q]], axis=-1)


def _cparams(sem=None):
    return pltpu.CompilerParams(dimension_semantics=sem, vmem_limit_bytes=VMEM_LIMIT)


def _tile(dim, cap, quantum=LANES):
    if dim <= cap:
        return dim
    t = (cap // quantum) * quantum
    while t >= quantum:
        if dim % t == 0:
            return t
        t -= quantum
    return dim


_REL_XY = ((1, 0, 0), (0, 1, 0), (1, 1, 0))
_REL_ALL = tuple((a, b, c) for a in (0, 1) for b in (0, 1) for c in (0, 1))[1:]
_REL_C = ((0, 0, 1),)


def _exchange(name, arrs, mode):
    rels = {"ag_xy": _REL_XY, "a2a_xy": _REL_XY, "ag_all": _REL_ALL, "swap_c": _REL_C, "half_ag_xy": _REL_XY,
            "pair_gather": _REL_C, "pair_split": _REL_C}[mode]
    n, nr = len(arrs), len(rels)
    n_out = 2 if mode == "pair_split" else 1

    def shapes(a):
        if mode == "ag_xy":
            return [(4,) + a.shape]
        if mode == "ag_all":
            return [(8,) + a.shape]
        if mode == "half_ag_xy":
            return [(4,) + a.shape[1:]]
        if mode == "pair_gather":
            return [(2,) + a.shape]
        if mode == "pair_split":
            return [a.shape[1:], a.shape[1:]]
        return [a.shape]

    out_shape = tuple(jax.ShapeDtypeStruct(sh, a.dtype) for a in arrs for sh in shapes(a))

    def body(*refs):
        ins, outs = refs[:n], refs[n:n + n * n_out]
        send, recv, loc = refs[n + n * n_out:]
        x, y, c = lax.axis_index("x"), lax.axis_index("y"), lax.axis_index("c")
        chip, dev = 2 * x + y, 4 * x + 2 * y + c
        started, local = [], []
        for a in range(n):
            i_ref, o = ins[a], outs[a * n_out]
            if mode == "ag_xy":
                lsrc, ldst = i_ref, o.at[chip]
            elif mode == "ag_all":
                lsrc, ldst = i_ref, o.at[dev]
            elif mode == "a2a_xy":
                lsrc, ldst = i_ref.at[chip], o.at[chip]
            elif mode == "half_ag_xy":
                lsrc, ldst = i_ref.at[c], o.at[chip]
            elif mode == "pair_gather":
                lsrc, ldst = i_ref, o.at[c]
            elif mode == "pair_split":
                lsrc, ldst = i_ref.at[c], o
            else:
                lsrc = ldst = None
            if lsrc is not None:
                lc = pltpu.make_async_copy(lsrc, ldst, loc.at[a])
                lc.start()
                local.append(lc)
            for r, (bx, by, bc) in enumerate(rels):
                px = 1 - x if bx else x
                py = 1 - y if by else y
                pc = 1 - c if bc else c
                pchip, pdev = 2 * px + py, 4 * px + 2 * py + pc
                if mode == "ag_xy":
                    src, dst, mine = i_ref, o.at[chip], o.at[pchip]
                elif mode == "ag_all":
                    src, dst, mine = i_ref, o.at[dev], o.at[pdev]
                elif mode == "a2a_xy":
                    src, dst, mine = i_ref.at[pchip], o.at[chip], o.at[pchip]
                elif mode == "half_ag_xy":
                    src, dst, mine = i_ref.at[c], o.at[chip], o.at[pchip]
                elif mode == "pair_gather":
                    src, dst, mine = i_ref, o.at[c], o.at[pc]
                elif mode == "pair_split":
                    src, dst, mine = i_ref.at[pc], outs[a * n_out + 1], outs[a * n_out + 1]
                else:
                    src, dst, mine = i_ref, o, o
                k = a * nr + r
                cp = pltpu.make_async_remote_copy(src_ref=src, dst_ref=dst, send_sem=send.at[k], recv_sem=recv.at[k],
                                                  device_id=(px, py, pc), device_id_type=MESH_ID)
                cp.start()
                landing = pltpu.make_async_remote_copy(src_ref=src, dst_ref=mine, send_sem=send.at[k],
                                                       recv_sem=recv.at[k], device_id=(px, py, pc),
                                                       device_id_type=MESH_ID)
                started.append((cp, landing))
        for cp, landing in started:
            cp.wait_send()
            landing.wait_recv()
        for lc in local:
            lc.wait()

    hbm = pl.BlockSpec(memory_space=pl.ANY)
    res = pl.pallas_call(
        body, name=name, out_shape=out_shape, in_specs=[hbm] * n, out_specs=tuple([hbm] * (n * n_out)),
        scratch_shapes=[pltpu.SemaphoreType.DMA((n * nr,)), pltpu.SemaphoreType.DMA((n * nr,)),
                        pltpu.SemaphoreType.DMA((max(n, 1),))],
    )(*arrs)
    return list(res)


def _matmul(name, a, b, mode, M, N, K, out_dtype=F32, a_off=(0, 0), b_off=(0, 0), tm_cap=1024, tn_cap=1024, tk_cap=2048):
    tm, tn = _tile(M, tm_cap, LANES if M % LANES == 0 else 8), _tile(N, tn_cap)
    tk = K if K <= 4096 else _tile(K, tk_cap)
    nk = K // tk
    assert M % tm == 0 and N % tn == 0 and K % tk == 0, (name, M, N, K, tm, tn, tk)
    if mode == "nn":
        ab, bb, dims = (tm, tk), (tk, tn), NN_DIMS
        ai = lambda i, j, k: (i + a_off[0] // tm, k + a_off[1] // tk)
        bi = lambda i, j, k: (k + b_off[0] // tk, j + b_off[1] // tn)
        chk = (a_off[0] % tm, a_off[1] % tk, b_off[0] % tk, b_off[1] % tn)
    elif mode == "nt":
        ab, bb, dims = (tm, tk), (tn, tk), NT_DIMS
        ai = lambda i, j, k: (i + a_off[0] // tm, k + a_off[1] // tk)
        bi = lambda i, j, k: (j + b_off[0] // tn, k + b_off[1] // tk)
        chk = (a_off[0] % tm, a_off[1] % tk, b_off[0] % tn, b_off[1] % tk)
    else:
        ab, bb, dims = (tk, tm), (tk, tn), TN_DIMS
        ai = lambda i, j, k: (k + a_off[0] // tk, i + a_off[1] // tm)
        bi = lambda i, j, k: (k + b_off[0] // tk, j + b_off[1] // tn)
        chk = (a_off[0] % tk, a_off[1] % tm, b_off[0] % tk, b_off[1] % tn)
    assert not any(chk), (name, chk)

    def body(a_ref, b_ref, o_ref, *acc):
        p = lax.dot_general(a_ref[...].astype(BF16), b_ref[...].astype(BF16), dims, preferred_element_type=F32)
        if nk == 1:
            o_ref[...] = p.astype(out_dtype)
        else:
            k = pl.program_id(2)

            @pl.when(k == 0)
            def _():
                acc[0][...] = p

            @pl.when(k > 0)
            def _():
                acc[0][...] += p

            @pl.when(k == nk - 1)
            def _():
                o_ref[...] = acc[0][...].astype(out_dtype)

    return pl.pallas_call(
        body, name=name, out_shape=jax.ShapeDtypeStruct((M, N), out_dtype), grid=(M // tm, N // tn, nk),
        in_specs=[pl.BlockSpec(ab, ai), pl.BlockSpec(bb, bi)], out_specs=pl.BlockSpec((tm, tn), lambda i, j, k: (i, j)),
        scratch_shapes=[pltpu.VMEM((tm, tn), F32)] if nk > 1 else [],
        compiler_params=_cparams(("parallel", "parallel", "arbitrary")),
    )(a, b)


def _rowwise(name, fn, grid_n, tr, ins, outs, nlat=None):
    nlat = grid_n if nlat is None else nlat
    grp = lambda i: jnp.minimum(i // nlat, 1)
    in_specs, args = [], []
    for spec in ins:
        kind, arr = spec[0], spec[1]
        if kind == "row":
            in_specs.append(pl.BlockSpec((tr, spec[3]), lambda i, cb=spec[2]: (i, cb)))
        elif kind == "rowc":
            in_specs.append(pl.BlockSpec((tr, spec[3]), lambda i, cb=spec[2], mx=spec[4]: (jnp.minimum(i, mx), cb)))
        elif kind == "prev":
            in_specs.append(pl.BlockSpec((8, spec[3]), lambda i, cb=spec[2]: (jnp.maximum(i * (tr // 8) - 1, 0), cb)))
        elif kind == "next":
            nb = arr.shape[0] // 8
            in_specs.append(pl.BlockSpec((8, spec[3]), lambda i, cb=spec[2], nb=nb: (jnp.minimum((i + 1) * (tr // 8), nb - 1), cb)))
        elif kind == "grp":
            in_specs.append(pl.BlockSpec((None,) + arr.shape[1:], lambda i: (grp(i), 0, 0)))
        else:
            in_specs.append(pl.BlockSpec(arr.shape, lambda i: (0, 0)))
        args.append(arr)
    out_shape, out_specs = [], []
    for spec in outs:
        if spec[0] == "row":
            out_shape.append(jax.ShapeDtypeStruct((spec[1], spec[2]), spec[3]))
            out_specs.append(pl.BlockSpec((tr, spec[2]), lambda i: (i, 0)))
        elif spec[0] == "acc":
            out_shape.append(jax.ShapeDtypeStruct((spec[1], spec[2]), F32))
            out_specs.append(pl.BlockSpec((spec[1], spec[2]), lambda i: (0, 0)))
        else:
            out_shape.append(jax.ShapeDtypeStruct((2, spec[1], spec[2]), F32))
            out_specs.append(pl.BlockSpec((None, spec[1], spec[2]), lambda i: (grp(i), 0, 0)))
    n_in = len(ins)
    has_acc = any(s[0] != "row" for s in outs)

    def body(*refs):
        i = pl.program_id(0)
        res = fn(i, *[r[...] for r in refs[:n_in]])
        for spec, ref, val in zip(outs, refs[n_in:], res):
            if spec[0] == "row":
                ref[...] = val.astype(ref.dtype)
            else:
                first = (i == 0) if spec[0] == "acc" else jnp.logical_or(i == 0, i == nlat)

                @pl.when(first)
                def _(ref=ref, val=val):
                    ref[...] = val

                @pl.when(jnp.logical_not(first))
                def _(ref=ref, val=val):
                    ref[...] += val

    return pl.pallas_call(
        body, name=name, out_shape=tuple(out_shape), grid=(grid_n,), in_specs=in_specs, out_specs=tuple(out_specs),
        compiler_params=_cparams(("arbitrary",) if has_acc else ("parallel",)),
    )(*args)


def _vjp_of(fwd, n_in, wrt):
    def bwd(i, *args):
        _, pull = jax.vjp(lambda *a: fwd(i, *a), *args[:n_in])
        g = pull(tuple(args[n_in:]))
        return tuple(g[k] for k in wrt)
    return bwd


def _rms(x, w):
    return x * lax.rsqrt(jnp.mean(x * x, axis=-1, keepdims=True) + EPS) * w


def _modulate(x, w, shift, scale):
    return _rms(x, w) * (1.0 + scale) + shift


def _silu(x):
    return x * jax.nn.sigmoid(x)


def _log_sigmoid(x):
    return jnp.minimum(x, 0.0) - jnp.log(1.0 + jnp.exp(-jnp.abs(x)))


def _f_mod(i, x, w, sh, sc):
    return (_modulate(x, w, sh, sc),)


def _f_resmod(i, x, y, gate, w, sh, sc):
    x1 = x + gate * y
    return x1, _modulate(x1, w, sh, sc)


def _f_norms(i, kvl, ql, kvw, qw):
    return _rms(kvl, kvw), _rms(ql, qw)


def _f_silu(i, x):
    return (_silu(x),)


def _rope128(x2, cos, sin):
    return x2 * cos + pltpu.roll(x2, QK_ROPE, 1) * sin


def _rope128_t(d2, cos, sin):
    return d2 * cos + pltpu.roll(d2 * sin, QK_ROPE, 1)


def _shifters(i, tr, T, TA):
    rows = i * tr + lax.broadcasted_iota(jnp.int32, (tr, 1), 0)
    loc = lax.broadcasted_iota(jnp.int32, (tr, 1), 0)
    has_prev = jnp.logical_and(rows != 0, rows != T)
    has_next = jnp.logical_and(rows != T - 1, rows != TA - 1)

    def prev(x, halo):
        r = jnp.where(loc == 0, halo[7:8, :], pltpu.roll(x, 1, 0))
        return jnp.where(has_prev, r, 0.0)

    def nxt(x, halo):
        r = jnp.where(loc == tr - 1, halo[0:1, :], pltpu.roll(x, tr - 1, 0))
        return jnp.where(has_next, r, 0.0)

    return prev, nxt


def _colsum(x):
    return jnp.sum(x, axis=0, keepdims=True)


def _attn_dims(T, TA):
    tq = TR
    tk = _tile(TA, 1408)
    return tq, TA // tq, T // tq, tk, TA // tk


def _attn_scores(q_ref, kv_ref, kr_ref, qi, ki, nql, tk, T):
    q, kvb = q_ref[...], kv_ref[...]
    s = lax.dot_general(q[:, :QK_NOPE], kvb[:, :QK_NOPE], NT_DIMS, preferred_element_type=F32)
    s = s + lax.dot_general(q[:, QK_NOPE:], kr_ref[...], NT_DIMS, preferred_element_type=F32)
    s = s * ((QK_NOPE + QK_ROPE) ** -0.5)
    col = ki * tk + lax.broadcasted_iota(jnp.int32, (1, tk), 1)
    return jnp.where(jnp.logical_and(qi >= nql, col < T), -1e30, s)


def _attn_fwd(q, kv, kr, T, TA):
    H = MLA_HEADS
    tq, nq, nql, tk, nk = _attn_dims(T, TA)

    def body(q_ref, kv_ref, kr_ref, o_ref, lse_ref, m_sc, l_sc, acc_sc):
        qi, ki = pl.program_id(1), pl.program_id(2)

        @pl.when(ki == 0)
        def _():
            m_sc[...] = jnp.full_like(m_sc, -jnp.inf)
            l_sc[...] = jnp.zeros_like(l_sc)
            acc_sc[...] = jnp.zeros_like(acc_sc)

        @pl.when(jnp.logical_or(qi < nql, (ki + 1) * tk > T))
        def _():
            s = _attn_scores(q_ref, kv_ref, kr_ref, qi, ki, nql, tk, T)
            m_new = jnp.maximum(m_sc[...], jnp.max(s, axis=-1, keepdims=True))
            alpha = jnp.exp(m_sc[...] - m_new)
            p = jnp.exp(s - m_new)
            l_sc[...] = alpha * l_sc[...] + jnp.sum(p, axis=-1, keepdims=True)
            acc_sc[...] = alpha * acc_sc[...] + jnp.dot(p.astype(BF16), kv_ref[:, QK_NOPE:], preferred_element_type=F32)
            m_sc[...] = m_new

        @pl.when(ki == nk - 1)
        def _():
            o_ref[...] = (acc_sc[...] / l_sc[...]).astype(o_ref.dtype)
            lse_ref[...] = m_sc[...] + jnp.log(l_sc[...])

    return pl.pallas_call(
        body, name="mla_fwd", grid=(H, nq, nk),
        out_shape=(jax.ShapeDtypeStruct((TA, H * V_HEAD), BF16), jax.ShapeDtypeStruct((H, TA, 1), F32)),
        in_specs=[pl.BlockSpec((tq, QK_PAD), lambda h, i, k: (i, h)), pl.BlockSpec((tk, QK_PAD), lambda h, i, k: (k, h)),
                  pl.BlockSpec((tk, LANES), lambda h, i, k: (k, 0))],
        out_specs=(pl.BlockSpec((tq, V_HEAD), lambda h, i, k: (i, h)), pl.BlockSpec((None, tq, 1), lambda h, i, k: (h, i, 0))),
        scratch_shapes=[pltpu.VMEM((tq, 1), F32), pltpu.VMEM((tq, 1), F32), pltpu.VMEM((tq, V_HEAD), F32)],
        compiler_params=_cparams(("parallel", "parallel", "arbitrary")),
    )(q, kv, kr)


def _attn_probs(q_ref, kv_ref, kr_ref, o_ref, do_ref, lse_ref, qi, ki, nql, tk, T):
    s = _attn_scores(q_ref, kv_ref, kr_ref, qi, ki, nql, tk, T)
    p = jnp.exp(s - lse_ref[...])
    do = do_ref[...]
    delta = jnp.sum(do * o_ref[...].astype(F32), axis=-1, keepdims=True)
    dp = lax.dot_general(do.astype(BF16), kv_ref[:, QK_NOPE:], NT_DIMS, preferred_element_type=F32)
    ds = p * (dp - delta) * ((QK_NOPE + QK_ROPE) ** -0.5)
    return p.astype(BF16), ds.astype(BF16), do.astype(BF16)


def _attn_bwd_specs(tq, tk, do_cb0, q_major):
    qm = (lambda h, a, b: a) if q_major else (lambda h, a, b: b)
    km = (lambda h, a, b: b) if q_major else (lambda h, a, b: a)
    return [pl.BlockSpec((tq, QK_PAD), lambda h, a, b: (qm(h, a, b), h)),
            pl.BlockSpec((tk, QK_PAD), lambda h, a, b: (km(h, a, b), h)),
            pl.BlockSpec((tk, LANES), lambda h, a, b: (km(h, a, b), 0)),
            pl.BlockSpec((tq, V_HEAD), lambda h, a, b: (qm(h, a, b), h)),
            pl.BlockSpec((tq, V_HEAD), lambda h, a, b: (qm(h, a, b), do_cb0 + h)),
            pl.BlockSpec((None, tq, 1), lambda h, a, b: (h, qm(h, a, b), 0))]


def _attn_bwd_dq(q, kv, kr, o, dcat, lse, do_cb0, T, TA):
    H = MLA_HEADS
    tq, nq, nql, tk, nk = _attn_dims(T, TA)

    def body(q_ref, kv_ref, kr_ref, o_ref, do_ref, lse_ref, dq_ref, acc):
        qi, ki = pl.program_id(1), pl.program_id(2)

        @pl.when(ki == 0)
        def _():
            acc[...] = jnp.zeros_like(acc)

        @pl.when(jnp.logical_or(qi < nql, (ki + 1) * tk > T))
        def _():
            _, ds, _ = _attn_probs(q_ref, kv_ref, kr_ref, o_ref, do_ref, lse_ref, qi, ki, nql, tk, T)
            acc[:, :QK_NOPE] += jnp.dot(ds, kv_ref[:, :QK_NOPE], preferred_element_type=F32)
            acc[:, QK_NOPE:] += jnp.dot(ds, kr_ref[...], preferred_element_type=F32)

        @pl.when(ki == nk - 1)
        def _():
            dq_ref[...] = acc[...]

    return pl.pallas_call(
        body, name="mla_bwd_dq", grid=(H, nq, nk), out_shape=jax.ShapeDtypeStruct((TA, H * QK_PAD), F32),
        in_specs=_attn_bwd_specs(tq, tk, do_cb0, True), out_specs=pl.BlockSpec((tq, QK_PAD), lambda h, i, k: (i, h)),
        scratch_shapes=[pltpu.VMEM((tq, QK_PAD), F32)], compiler_params=_cparams(("parallel", "parallel", "arbitrary")),
    )(q, kv, kr, o, dcat, lse)


def _attn_bwd_dkv(q, kv, kr, o, dcat, lse, do_cb0, T, TA):
    H = MLA_HEADS
    tq, nq, nql, tk, nk = _attn_dims(T, TA)

    def body(q_ref, kv_ref, kr_ref, o_ref, do_ref, lse_ref, dkv_ref, dkr_ref, acc, acc_r):
        ki, qi = pl.program_id(1), pl.program_id(2)

        @pl.when(qi == 0)
        def _():
            acc[...] = jnp.zeros_like(acc)
            acc_r[...] = jnp.zeros_like(acc_r)

        @pl.when(jnp.logical_or(qi < nql, (ki + 1) * tk > T))
        def _():
            p, ds, do = _attn_probs(q_ref, kv_ref, kr_ref, o_ref, do_ref, lse_ref, qi, ki, nql, tk, T)
            acc[:, QK_NOPE:] += lax.dot_general(p, do, TN_DIMS, preferred_element_type=F32)
            acc[:, :QK_NOPE] += lax.dot_general(ds, q_ref[:, :QK_NOPE], TN_DIMS, preferred_element_type=F32)
            acc_r[...] += lax.dot_general(ds, q_ref[:, QK_NOPE:], TN_DIMS, preferred_element_type=F32)

        @pl.when(qi == nq - 1)
        def _():
            dkv_ref[...] = acc[...].astype(dkv_ref.dtype)
            dkr_ref[...] = acc_r[...]

    return pl.pallas_call(
        body, name="mla_bwd_dkv", grid=(H, nk, nq),
        out_shape=(jax.ShapeDtypeStruct((TA, H * QK_PAD), BF16), jax.ShapeDtypeStruct((H, TA, LANES), F32)),
        in_specs=_attn_bwd_specs(tq, tk, do_cb0, False),
        out_specs=(pl.BlockSpec((tk, QK_PAD), lambda h, k, i: (k, h)), pl.BlockSpec((None, tk, LANES), lambda h, k, i: (h, k, 0))),
        scratch_shapes=[pltpu.VMEM((tk, QK_PAD), F32), pltpu.VMEM((tk, LANES), F32)],
        compiler_params=_cparams(("parallel", "parallel", "arbitrary")),
    )(q, kv, kr, o, dcat, lse)


def _gla_tile(st, q, k, v, g, rev, q_scale):
    nc = q.shape[0] // CHUNK
    ii = lax.broadcasted_iota(jnp.int32, (CHUNK, CHUNK), 0)
    jj = lax.broadcasted_iota(jnp.int32, (CHUNK, CHUNK), 1)
    tri = (jj >= ii) if rev else (jj <= ii)
    ones = tri.astype(F32)
    outs = [None] * nc
    for ci in (range(nc - 1, -1, -1) if rev else range(nc)):
        sl = slice(ci * CHUNK, (ci + 1) * CHUNK)
        qc, kc, vc, gc = q[sl] * q_scale, k[sl], v[sl].astype(BF16), g[sl]
        b = jnp.dot(ones, gc, precision=HIGHEST, preferred_element_type=F32)
        bl = b[0:1] if rev else b[CHUNK - 1:CHUNK]
        kd = (kc * jnp.exp(bl - b)).astype(BF16)
        qe = (qc * jnp.exp(b)).astype(BF16)
        ke = (kc * jnp.exp(-b)).astype(BF16)
        att = jnp.where(tri, lax.dot_general(qe, ke, NT_DIMS, preferred_element_type=F32), 0.0)
        outs[ci] = (jnp.dot(att.astype(BF16), vc, preferred_element_type=F32)
                    + lax.dot_general(qe, st.astype(BF16), NT_DIMS, preferred_element_type=F32))
        st = st * jnp.exp(bl) + lax.dot_general(vc, kd, TN_DIMS, preferred_element_type=F32)
    return st, jnp.concatenate(outs, axis=0)


def _gla_specs(dims, rev, nt, nlat):
    DK, DV, q_off, k_off, g_off = dims
    tile = (lambda p: nt - 1 - p) if rev else (lambda p: (p + nlat) % nt)
    return tile, [pl.BlockSpec((TR, DK), lambda h, p: (tile(p), q_off // DK + h)),
                  pl.BlockSpec((TR, DK), lambda h, p: (tile(p), k_off // DK + h)),
                  pl.BlockSpec((TR, DV), lambda h, p: (tile(p), h)),
                  pl.BlockSpec((TR, DK), lambda h, p: (tile(p), g_off // DK + h))]


def _gla_fwd(name, z1, g, dims, rev, nt, nlat):
    DK, DV = dims[0], dims[1]
    TA = nt * TR
    _, in_specs = _gla_specs(dims, rev, nt, nlat)
    tile = (lambda p: nt - 1 - p) if rev else (lambda p: (p + nlat) % nt)

    def body(q_ref, k_ref, v_ref, g_ref, o_ref, st_ref, st):
        @pl.when(pl.program_id(1) == 0)
        def _():
            st[...] = jnp.zeros_like(st)

        st_ref[...] = st[...]
        new, out = _gla_tile(st[...], q_ref[...], k_ref[...], v_ref[...], g_ref[...], rev, DK ** -0.5)
        st[...] = new
        o_ref[...] = out

    return pl.pallas_call(
        body, name=name, grid=(GLA_HEADS, nt),
        out_shape=(jax.ShapeDtypeStruct((TA, GLA_HEADS * DV), F32), jax.ShapeDtypeStruct((GLA_HEADS, nt, DV, DK), F32)),
        in_specs=in_specs,
        out_specs=(pl.BlockSpec((TR, DV), lambda h, p: (tile(p), h)), pl.BlockSpec((None, None, DV, DK), lambda h, p: (h, p, 0, 0))),
        scratch_shapes=[pltpu.VMEM((DV, DK), F32)], compiler_params=_cparams(("parallel", "arbitrary")),
    )(z1, z1, z1, g)


def _gla_bwd(name, z1, g, states, do, dims, rev, nt, nlat):
    DK, DV = dims[0], dims[1]
    TA = nt * TR
    fwd_tile = (lambda p: nt - 1 - p) if rev else (lambda p: (p + nlat) % nt)
    pos = lambda s: nt - 1 - s
    DKq, DVq, q_off, k_off, g_off = dims
    in_specs = [pl.BlockSpec((TR, DK), lambda h, s: (fwd_tile(pos(s)), q_off // DK + h)),
                pl.BlockSpec((TR, DK), lambda h, s: (fwd_tile(pos(s)), k_off // DK + h)),
                pl.BlockSpec((TR, DV), lambda h, s: (fwd_tile(pos(s)), h)),
                pl.BlockSpec((TR, DK), lambda h, s: (fwd_tile(pos(s)), g_off // DK + h)),
                pl.BlockSpec((None, None, DV, DK), lambda h, s: (h, pos(s), 0, 0)),
                pl.BlockSpec((TR, DV), lambda h, s: (fwd_tile(pos(s)), h))]
    row_k = pl.BlockSpec((TR, DK), lambda h, s: (fwd_tile(pos(s)), h))
    row_v = pl.BlockSpec((TR, DV), lambda h, s: (fwd_tile(pos(s)), h))

    def body(q_ref, k_ref, v_ref, g_ref, st_ref, do_ref, dq_ref, dk_ref, dv_ref, dg_ref, dst):
        @pl.when(pl.program_id(1) == 0)
        def _():
            dst[...] = jnp.zeros_like(dst)

        _, pull = jax.vjp(lambda st, q, k, v, gg: _gla_tile(st, q, k, v, gg, rev, DK ** -0.5),
                          st_ref[...], q_ref[...], k_ref[...], v_ref[...], g_ref[...])
        d_st, dq, dk, dv, dg = pull((dst[...], do_ref[...]))
        dst[...] = d_st
        dq_ref[...] = dq
        dk_ref[...] = dk
        dv_ref[...] = dv
        dg_ref[...] = dg

    KEY, VAL = GLA_HEADS * DK, GLA_HEADS * DV
    return pl.pallas_call(
        body, name=name, grid=(GLA_HEADS, nt),
        out_shape=(jax.ShapeDtypeStruct((TA, KEY), F32), jax.ShapeDtypeStruct((TA, KEY), F32),
                   jax.ShapeDtypeStruct((TA, VAL), F32), jax.ShapeDtypeStruct((TA, KEY), F32)),
        in_specs=in_specs, out_specs=(row_k, row_k, row_v, row_k),
        scratch_shapes=[pltpu.VMEM((DV, DK), F32)], compiler_params=_cparams(("parallel", "arbitrary")),
    )(z1, z1, z1, g, states, do)


def _sum_slots(name, arr):
    S, R, C = arr.shape
    tr = _tile(R, max(16, (2 ** 19) // max(C, 1) // 16 * 16), 16)

    def body(a_ref, o_ref):
        acc = a_ref[0].astype(F32)
        for s in range(1, S):
            acc = acc + a_ref[s].astype(F32)
        o_ref[...] = acc

    return pl.pallas_call(
        body, name=name, grid=(R // tr,), out_shape=jax.ShapeDtypeStruct((R, C), F32),
        in_specs=[pl.BlockSpec((S, tr, C), lambda i: (0, i, 0))], out_specs=pl.BlockSpec((tr, C), lambda i: (i, 0)),
        compiler_params=_cparams(("parallel",)),
    )(arr)


def _sum_pair(name, a, b, out_dtype):
    shape, C = a.shape, a.shape[-1]
    a2, b2 = a.reshape(-1, C), b.reshape(-1, C)
    R = a2.shape[0]
    tr = _tile(R, max(16, (2 ** 19) // max(C, 1) // 16 * 16), 16)
    fn = lambda i, p, q: (p.astype(F32) + q.astype(F32),)
    (res,) = _rowwise(name, fn, R // tr, tr, [("row", a2, 0, C), ("row", b2, 0, C)], [("row", R, C, out_dtype)])
    return res.reshape(shape)


def _adam_math(w, g, m, v):
    m2 = ADAM_B1 * m + (1.0 - ADAM_B1) * g
    v2 = ADAM_B2 * v + (1.0 - ADAM_B2) * (g * g)
    m_hat = m2 / (1.0 - ADAM_B1 ** ADAM_STEP)
    v_hat = v2 / (1.0 - ADAM_B2 ** ADAM_STEP)
    return -ADAM_LR * (m_hat / (jnp.sqrt(v_hat) + ADAM_EPS) + ADAM_WD * w), m2, v2


def _adam(name, w, gs, m, v):
    R, C = w.shape
    tr = _tile(R, max(8, (2 ** 19) // max(C, 1) // 8 * 8), 8)
    ng = len(gs)

    def fn(i, w_, *rest):
        g = rest[0] if ng == 1 else rest[0] + rest[1]
        d, m2, v2 = _adam_math(w_, g, rest[ng], rest[ng + 1])
        return g, d, m2, v2

    ins = [("row", a, 0, C) for a in (w, *gs, m, v)]
    return _rowwise(name, fn, R // tr, tr, ins, [("row", R, C, F32)] * 4)


def _ffn_fwd(tag, h, w_up, cw, cb, w_down, n_rows, T, TA, F):
    tr = 128
    u = _matmul(tag + "_up", h, w_up, "nn", n_rows, 2 * F, h.shape[1])
    cwg, cwv, cbg, cbv = cw[:, :F], cw[:, F:], cb[:, :F], cb[:, F:]

    def fn(i, ug, uv, pg, pv, ng, nv, wg, wv, bg, bv):
        prev, nxt = _shifters(i, tr, T, TA)
        cg = wg[0:1] * prev(ug, pg) + wg[1:2] * ug + wg[2:3] * nxt(ug, ng) + bg
        cv = wv[0:1] * prev(uv, pv) + wv[1:2] * uv + wv[2:3] * nxt(uv, nv) + bv
        return (_silu(cg) * cv,)

    ins = [("row", u, 0, F), ("row", u, 1, F), ("prev", u, 0, F), ("prev", u, 1, F), ("next", u, 0, F), ("next", u, 1, F),
           ("full", cwg), ("full", cwv), ("full", cbg), ("full", cbv)]
    (act,) = _rowwise(tag + "_conv", fn, n_rows // tr, tr, ins, [("row", n_rows, F, BF16)])
    f = _matmul(tag + "_down", act, w_down, "nn", n_rows, w_down.shape[1], F)
    return u, act, f


def _ffn_bwd(tag, df, h, u, act, w_up, cw, cb, w_down, n_rows, T, TA, F):
    tr = 128
    D = w_down.shape[1]
    dact = _matmul(tag + "_ddown_x", df, w_down, "nt", n_rows, F, D)
    g_down = _matmul(tag + "_ddown_w", act, df, "tn", F, D, n_rows, out_dtype=BF16)
    cwg, cwv, cbg, cbv = cw[:, :F], cw[:, F:], cb[:, :F], cb[:, F:]

    def fn_a(i, ug, uv, pg, pv, ng, nv, da, wg, wv, bg, bv):
        prev, nxt = _shifters(i, tr, T, TA)
        ugp, ugn, uvp, uvn = prev(ug, pg), nxt(ug, ng), prev(uv, pv), nxt(uv, nv)
        cg = wg[0:1] * ugp + wg[1:2] * ug + wg[2:3] * ugn + bg
        cv = wv[0:1] * uvp + wv[1:2] * uv + wv[2:3] * uvn + bv
        sg = jax.nn.sigmoid(cg)
        dcv = da * (cg * sg)
        dcg = da * cv * (sg * (1.0 + cg * (1.0 - sg)))
        return (jnp.concatenate([dcg, dcv], axis=1),
                _colsum(ugp * dcg), _colsum(ug * dcg), _colsum(ugn * dcg),
                _colsum(uvp * dcv), _colsum(uv * dcv), _colsum(uvn * dcv), _colsum(dcg), _colsum(dcv))

    ins = [("row", u, 0, F), ("row", u, 1, F), ("prev", u, 0, F), ("prev", u, 1, F), ("next", u, 0, F), ("next", u, 1, F),
           ("row", dact, 0, F), ("full", cwg), ("full", cwv), ("full", cbg), ("full", cbv)]
    res = _rowwise(tag + "_dconv_a", fn_a, n_rows // tr, tr, ins, [("row", n_rows, 2 * F, F32)] + [("acc", 1, F)] * 8)
    duc = res[0]
    g_cw = jnp.concatenate([jnp.concatenate(res[1:4], axis=0), jnp.concatenate(res[4:7], axis=0)], axis=1)
    g_cb = jnp.concatenate([res[7], res[8]], axis=1)

    def fn_b(i, dg, dv, pg, pv, ng, nv, wg, wv):
        prev, nxt = _shifters(i, tr, T, TA)
        dug = wg[0:1] * nxt(dg, ng) + wg[1:2] * dg + wg[2:3] * prev(dg, pg)
        duv = wv[0:1] * nxt(dv, nv) + wv[1:2] * dv + wv[2:3] * prev(dv, pv)
        return (jnp.concatenate([dug, duv], axis=1),)

    ins = [("row", duc, 0, F), ("row", duc, 1, F), ("prev", duc, 0, F), ("prev", duc, 1, F), ("next", duc, 0, F),
           ("next", duc, 1, F), ("full", cwg), ("full", cwv)]
    (du,) = _rowwise(tag + "_dconv_b", fn_b, n_rows // tr, tr, ins, [("row", n_rows, 2 * F, BF16)])
    dh = _matmul(tag + "_dup_x", du, w_up, "nt", n_rows, D, 2 * F)
    g_up = _matmul(tag + "_dup_w", h, du, "tn", D, 2 * F, n_rows, out_dtype=BF16)
    return dh, g_up, g_cw, g_cb, g_down


def _cols_to_shards(g):
    r, c = g.shape
    return g.reshape(r, 4, c // 4).transpose(1, 0, 2)


def _shards_to_cols(s):
    return s.transpose(1, 0, 2).reshape(s.shape[1], 4 * s.shape[2])


def kernel(x, c, ctx, c_ctx, l0_ada_w, l0_ada_b, l0_norm1, l0_w_in, l0_conv_a, l0_q_norm, l0_w_qb, l0_kv_norm, l0_w_kvb, l0_w_out, l0_norm2, l0_ffn_up, l0_ffn_conv_w, l0_ffn_conv_b, l0_ffn_down, l1_ada_w, l1_ada_b, l1_norm1, l1_w_in, l1_gate_fw_w, l1_gate_fw_b, l1_gate_bw_w, l1_gate_bw_b, l1_o_norm, l1_w_out, l1_norm2, l1_ffn_up, l1_ffn_conv_w, l1_ffn_conv_b, l1_ffn_down, final_norm, loss_target, m_c_ctx, m_l0_ada_w, m_l0_ada_b, m_l0_norm1, m_l0_w_in, m_l0_conv_a, m_l0_q_norm, m_l0_w_qb, m_l0_kv_norm, m_l0_w_kvb, m_l0_w_out, m_l0_norm2, m_l0_ffn_up, m_l0_ffn_conv_w, m_l0_ffn_conv_b, m_l0_ffn_down, m_l1_ada_w, m_l1_ada_b, m_l1_norm1, m_l1_w_in, m_l1_gate_fw_w, m_l1_gate_fw_b, m_l1_gate_bw_w, m_l1_gate_bw_b, m_l1_o_norm, m_l1_w_out, m_l1_norm2, m_l1_ffn_up, m_l1_ffn_conv_w, m_l1_ffn_conv_b, m_l1_ffn_down, m_final_norm, v_c_ctx, v_l0_ada_w, v_l0_ada_b, v_l0_norm1, v_l0_w_in, v_l0_conv_a, v_l0_q_norm, v_l0_w_qb, v_l0_kv_norm, v_l0_w_kvb, v_l0_w_out, v_l0_norm2, v_l0_ffn_up, v_l0_ffn_conv_w, v_l0_ffn_conv_b, v_l0_ffn_down, v_l1_ada_w, v_l1_ada_b, v_l1_norm1, v_l1_w_in, v_l1_gate_fw_w, v_l1_gate_fw_b, v_l1_gate_bw_w, v_l1_gate_bw_b, v_l1_o_norm, v_l1_w_out, v_l1_norm2, v_l1_ffn_up, v_l1_ffn_conv_w, v_l1_ffn_conv_b, v_l1_ffn_down, v_final_norm):
    W = dict(c_ctx=c_ctx, l0_ada_w=l0_ada_w, l0_ada_b=l0_ada_b, l0_norm1=l0_norm1, l0_w_in=l0_w_in, l0_conv_a=l0_conv_a, l0_q_norm=l0_q_norm, l0_w_qb=l0_w_qb, l0_kv_norm=l0_kv_norm, l0_w_kvb=l0_w_kvb, l0_w_out=l0_w_out, l0_norm2=l0_norm2, l0_ffn_up=l0_ffn_up, l0_ffn_conv_w=l0_ffn_conv_w, l0_ffn_conv_b=l0_ffn_conv_b, l0_ffn_down=l0_ffn_down, l1_ada_w=l1_ada_w, l1_ada_b=l1_ada_b, l1_norm1=l1_norm1, l1_w_in=l1_w_in, l1_gate_fw_w=l1_gate_fw_w, l1_gate_fw_b=l1_gate_fw_b, l1_gate_bw_w=l1_gate_bw_w, l1_gate_bw_b=l1_gate_bw_b, l1_o_norm=l1_o_norm, l1_w_out=l1_w_out, l1_norm2=l1_norm2, l1_ffn_up=l1_ffn_up, l1_ffn_conv_w=l1_ffn_conv_w, l1_ffn_conv_b=l1_ffn_conv_b, l1_ffn_down=l1_ffn_down, final_norm=final_norm)
    MO = dict(c_ctx=m_c_ctx, l0_ada_w=m_l0_ada_w, l0_ada_b=m_l0_ada_b, l0_norm1=m_l0_norm1, l0_w_in=m_l0_w_in, l0_conv_a=m_l0_conv_a, l0_q_norm=m_l0_q_norm, l0_w_qb=m_l0_w_qb, l0_kv_norm=m_l0_kv_norm, l0_w_kvb=m_l0_w_kvb, l0_w_out=m_l0_w_out, l0_norm2=m_l0_norm2, l0_ffn_up=m_l0_ffn_up, l0_ffn_conv_w=m_l0_ffn_conv_w, l0_ffn_conv_b=m_l0_ffn_conv_b, l0_ffn_down=m_l0_ffn_down, l1_ada_w=m_l1_ada_w, l1_ada_b=m_l1_ada_b, l1_norm1=m_l1_norm1, l1_w_in=m_l1_w_in, l1_gate_fw_w=m_l1_gate_fw_w, l1_gate_fw_b=m_l1_gate_fw_b, l1_gate_bw_w=m_l1_gate_bw_w, l1_gate_bw_b=m_l1_gate_bw_b, l1_o_norm=m_l1_o_norm, l1_w_out=m_l1_w_out, l1_norm2=m_l1_norm2, l1_ffn_up=m_l1_ffn_up, l1_ffn_conv_w=m_l1_ffn_conv_w, l1_ffn_conv_b=m_l1_ffn_conv_b, l1_ffn_down=m_l1_ffn_down, final_norm=m_final_norm)
    VO = dict(c_ctx=v_c_ctx, l0_ada_w=v_l0_ada_w, l0_ada_b=v_l0_ada_b, l0_norm1=v_l0_norm1, l0_w_in=v_l0_w_in, l0_conv_a=v_l0_conv_a, l0_q_norm=v_l0_q_norm, l0_w_qb=v_l0_w_qb, l0_kv_norm=v_l0_kv_norm, l0_w_kvb=v_l0_w_kvb, l0_w_out=v_l0_w_out, l0_norm2=v_l0_norm2, l0_ffn_up=v_l0_ffn_up, l0_ffn_conv_w=v_l0_ffn_conv_w, l0_ffn_conv_b=v_l0_ffn_conv_b, l0_ffn_down=v_l0_ffn_down, l1_ada_w=v_l1_ada_w, l1_ada_b=v_l1_ada_b, l1_norm1=v_l1_norm1, l1_w_in=v_l1_w_in, l1_gate_fw_w=v_l1_gate_fw_w, l1_gate_fw_b=v_l1_gate_fw_b, l1_gate_bw_w=v_l1_gate_bw_w, l1_gate_bw_b=v_l1_gate_bw_b, l1_o_norm=v_l1_o_norm, l1_w_out=v_l1_w_out, l1_norm2=v_l1_norm2, l1_ffn_up=v_l1_ffn_up, l1_ffn_conv_w=v_l1_ffn_conv_w, l1_ffn_conv_b=v_l1_ffn_conv_b, l1_ffn_down=v_l1_ffn_down, final_norm=v_final_norm)
    names = list(W)

    T, D = x.shape[1], x.shape[2]
    TC = ctx.shape[1]
    assert TC == TR and T % TR == 0
    TA = T + TC
    nt, nlat = TA // TR, T // TR
    CC = D // 2
    F = l0_ffn_down.shape[0] * 4
    DK, DV = D // 2 // GLA_HEADS, D // GLA_HEADS
    KEY, VAL = GLA_HEADS * DK, GLA_HEADS * DV
    ADA_S = l0_ada_w.shape[1]
    H = MLA_HEADS
    my_x, my_y, my_c = lax.axis_index("x"), lax.axis_index("y"), lax.axis_index("c")
    chip, dev = 2 * my_x + my_y, 4 * my_x + 2 * my_y + my_c
    row = lambda a: a.reshape(1, -1)

    big = ["l0_w_in", "l0_w_qb", "l0_w_kvb", "l0_w_out", "l0_ffn_up", "l0_ffn_down",
           "l1_w_in", "l1_w_out", "l1_ffn_up", "l1_ffn_down"]
    small_sh = ["l0_conv_a", "l0_ffn_conv_w", "l1_gate_fw_w", "l1_gate_bw_w", "l1_ffn_conv_w"]
    halves_w = _exchange("gather_weights", [W[n].astype(BF16).reshape(2, W[n].shape[0] // 2, -1) for n in big], "half_ag_xy")
    paired_w = _exchange("pair_weights", halves_w, "pair_gather")
    G = {n: p.transpose(1, 0, 2, 3).reshape(4, 2 * p.shape[2], p.shape[3]) for n, p in zip(big, paired_w)}
    G.update(zip(small_sh, _exchange("gather_small_weights", [W[n] for n in small_sh], "ag_xy")))
    (c_all,) = _exchange("gather_c", [c], "ag_all")
    col_sh = {"l0_w_in", "l0_w_qb", "l0_w_kvb", "l0_ffn_up", "l1_w_in", "l1_ffn_up", "l0_conv_a", "l0_ffn_conv_w",
              "l1_gate_fw_w", "l1_gate_bw_w", "l1_ffn_conv_w"}
    full = {n: (_shards_to_cols(G[n]) if n in col_sh else G[n].reshape(-1, G[n].shape[-1])) for n in G}

    o_kv, o_kr, o_ql, o_ax = 0, KV_LORA, KV_LORA + QK_ROPE, KV_LORA + QK_ROPE + Q_LORA
    n_sel = 3 * CC + Q_LORA + KV_LORA + QK_ROPE
    W0 = -(-(n_sel + QK_ROPE) // 256) * 256
    QOFF0, KVOFF0, KROFF0 = 3 * CC, 3 * CC + Q_LORA, 3 * CC + Q_LORA + KV_LORA
    w_in0 = full["l0_w_in"]
    w_kr = w_in0[:, o_kr:o_kr + QK_ROPE]
    w_in0p = jnp.concatenate([w_in0[:, o_ax:], w_in0[:, o_ql:o_ax], w_in0[:, :o_kr], w_kr, _rot_cols(w_kr),
                              jnp.zeros((D, W0 - n_sel - QK_ROPE), BF16)], axis=1)
    wq3 = full["l0_w_qb"].reshape(Q_LORA, H, QK_NOPE + QK_ROPE)
    w_qbp = jnp.concatenate([wq3, _rot_cols(wq3[:, :, QK_NOPE:])], axis=2).reshape(Q_LORA, H * QK_PAD)
    o_v1, o_lr1, o_q1, o_og1 = KEY, KEY + VAL, KEY + VAL + 2 * GATE_RANK, 2 * KEY + VAL + 2 * GATE_RANK
    W1 = 2 * VAL + 2 * KEY + LANES
    OGOFF, KOFF1, QOFF1, LROFF = VAL, 2 * VAL, 2 * VAL + KEY, 2 * VAL + 2 * KEY
    w_in1 = full["l1_w_in"]
    w_in1p = jnp.concatenate([w_in1[:, o_v1:o_lr1], w_in1[:, o_og1:], w_in1[:, :o_v1], w_in1[:, o_q1:o_og1],
                              w_in1[:, o_lr1:o_q1], jnp.zeros((D, LANES - 2 * GATE_RANK), BF16)], axis=1)
    w_gate = jnp.zeros((LANES, 2 * KEY), F32)
    w_gate = w_gate.at[:GATE_RANK, :KEY].set(full["l1_gate_fw_w"]).at[GATE_RANK:2 * GATE_RANK, KEY:].set(full["l1_gate_bw_w"])
    b_gate = jnp.concatenate([l1_gate_fw_b, l1_gate_bw_b]).reshape(1, -1)

    pos = np.arange(T)
    inv = ROPE_THETA ** (-np.arange(0, QK_ROPE // 2, 2, dtype=np.float32) / (QK_ROPE // 2))
    ar, ac = (pos // GRID_W).astype(np.float32)[:, None] * inv, (pos % GRID_W).astype(np.float32)[:, None] * inv
    ang = jnp.asarray(np.concatenate([ar, ar, ac, ac], axis=-1).astype(np.float32))
    zpad = jnp.zeros((TA, LANES - QK_ROPE), F32)
    cos_t = jnp.concatenate([jnp.concatenate([jnp.cos(ang), jnp.ones((TC, QK_ROPE), F32)], axis=0), zpad], axis=1)
    sin_t = jnp.concatenate([jnp.concatenate([jnp.sin(ang), jnp.zeros((TC, QK_ROPE), F32)], axis=0), zpad], axis=1)

    c16 = jnp.concatenate([c_all.reshape(8, D), c_ctx.reshape(1, D), jnp.zeros((7, D), F32)], axis=0)
    (act16,) = _rowwise("silu_c", _f_silu, 1, 16, [("row", c16, 0, D)], [("row", 16, D, F32)])
    mod_parts = [_matmul("ada_fwd", act16, W["l%d_ada_w" % l], "nn", 16, ADA_S, D, tn_cap=512) for l in (0, 1)]
    mod_g = _exchange("gather_mod", mod_parts, "ag_xy")
    mods = []
    for l in (0, 1):
        mfull = _shards_to_cols(mod_g[l]) + W["l%d_ada_b" % l][None, :]
        mine = lax.dynamic_slice_in_dim(mfull, dev, 1, axis=0)
        mods.append(jnp.concatenate([mine, mfull[8:9]], axis=0).reshape(2, N_MOD, D))
    P = lambda l, k: mods[l][:, k:k + 1, :]

    X = jnp.concatenate([x[0], ctx[0]], axis=0)
    n1_0, n2_0, n1_1, n2_1 = row(l0_norm1), row(l0_norm2), row(l1_norm1), row(l1_norm2)
    (h0,) = _rowwise("l0_mod1", _f_mod, nt, TR, [("row", X, 0, D), ("full", n1_0), ("grp", P(0, 0)), ("grp", P(0, 1))],
                     [("row", TA, D, BF16)], nlat)
    z0 = _matmul("l0_in", h0, w_in0p, "nn", TA, W0, D)
    kvn_w, qn_w = row(l0_kv_norm), row(l0_q_norm)
    norm_ins = [("row", z0, KVOFF0 // KV_LORA, KV_LORA), ("row", z0, QOFF0 // Q_LORA, Q_LORA), ("full", kvn_w), ("full", qn_w)]
    kvn, qn = _rowwise("l0_latnorm", _f_norms, nt, TR, norm_ins, [("row", TA, KV_LORA, BF16), ("row", TA, Q_LORA, BF16)])
    kv = _matmul("l0_kvb", kvn, full["l0_w_kvb"], "nn", TA, H * QK_PAD, KV_LORA, out_dtype=BF16)
    qraw = _matmul("l0_qb", qn, w_qbp, "nn", TA, H * QK_PAD, Q_LORA)

    def f_rope(i, qr, krr, cs, sn):
        pieces = []
        for h in range(H):
            pieces += [qr[:, h * QK_PAD:h * QK_PAD + QK_NOPE], _rope128(qr[:, h * QK_PAD + QK_NOPE:(h + 1) * QK_PAD], cs, sn)]
        return jnp.concatenate(pieces, axis=1), _rope128(krr, cs, sn)

    rope_ins = [("row", qraw, 0, H * QK_PAD), ("row", z0, KROFF0 // LANES, LANES), ("row", cos_t, 0, LANES), ("row", sin_t, 0, LANES)]
    q, kr = _rowwise("l0_rope", f_rope, nt, TR, rope_ins, [("row", TA, H * QK_PAD, BF16), ("row", TA, LANES, BF16)])
    o_att, lse = _attn_fwd(q, kv, kr, T, TA)

    conv_a = full["l0_conv_a"]

    def f_conva(i, ax, ab, ac_, pax, pac, nax, nac, w):
        prev, nxt = _shifters(i, TR, T, TA)
        p = ac_ * ax
        return (ab * (w[0:1] * prev(p, pac * pax) + w[1:2] * p + w[2:3] * nxt(p, nac * nax)),)

    conva_ins = [("row", z0, 0, CC), ("row", z0, 1, CC), ("row", z0, 2, CC), ("prev", z0, 0, CC), ("prev", z0, 2, CC),
                 ("next", z0, 0, CC), ("next", z0, 2, CC), ("full", conv_a)]
    (y_a,) = _rowwise("l0_conva", f_conva, nt, TR, conva_ins, [("row", TA, CC, BF16)])
    cat0 = jnp.concatenate([y_a, o_att], axis=1)
    y0 = _matmul("l0_out", cat0, full["l0_w_out"], "nn", TA, D, CC + H * V_HEAD)
    resmod_outs = [("row", TA, D, F32), ("row", TA, D, BF16)]
    X1, h2_0 = _rowwise("l0_resmod2", _f_resmod, nt, TR,
                        [("row", X, 0, D), ("row", y0, 0, D), ("grp", P(0, 2)), ("full", n2_0), ("grp", P(0, 3)), ("grp", P(0, 4))],
                        resmod_outs, nlat)
    cw0, cb0 = full["l0_ffn_conv_w"], row(l0_ffn_conv_b)
    u0, act0, f0 = _ffn_fwd("l0_ffn", h2_0, full["l0_ffn_up"], cw0, cb0, full["l0_ffn_down"], TA, T, TA, F)
    X2, h1 = _rowwise("l1_resmod1", _f_resmod, nt, TR,
                      [("row", X1, 0, D), ("row", f0, 0, D), ("grp", P(0, 5)), ("full", n1_1), ("grp", P(1, 0)), ("grp", P(1, 1))],
                      resmod_outs, nlat)

    z1 = _matmul("l1_in", h1, w_in1p, "nn", TA, W1, D)

    def f_gates(i, lr, wg, bg):
        pre = jnp.dot(lr.astype(BF16), wg.astype(BF16), preferred_element_type=F32) + bg
        return (_log_sigmoid(pre) / GATE_NORMALIZER,)

    gate_ins = [("row", z1, LROFF // LANES, LANES), ("full", w_gate), ("full", b_gate)]
    (gates,) = _rowwise("l1_gates", f_gates, nt, TR, gate_ins, [("row", TA, 2 * KEY, F32)])
    dims_f, dims_b = (DK, DV, QOFF1, KOFF1, 0), (DK, DV, QOFF1, KOFF1, KEY)
    o_f, st_f = _gla_fwd("gla_fwd_f", z1, gates, dims_f, False, nt, nlat)
    o_b, st_b = _gla_fwd("gla_fwd_b", z1, gates, dims_b, True, nt, nlat)
    onw = row(l1_o_norm)

    def f_glaout(i, of, ob, og, w):
        o = of + ob
        on = jnp.concatenate([_rms(o[:, h * DV:(h + 1) * DV], w) for h in range(GLA_HEADS)], axis=1)
        return (on * _silu(og),)

    glaout_ins = [("row", o_f, 0, VAL), ("row", o_b, 0, VAL), ("row", z1, OGOFF // VAL, VAL), ("full", onw)]
    (go,) = _rowwise("l1_glaout", f_glaout, nlat, TR, glaout_ins, [("row", T, VAL, BF16)])
    y1 = _matmul("l1_out", go, full["l1_w_out"], "nn", T, D, VAL)
    X3, h2_1 = _rowwise("l1_resmod2", _f_resmod, nlat, TR,
                        [("row", X2, 0, D), ("row", y1, 0, D), ("grp", P(1, 2)), ("full", n2_1), ("grp", P(1, 3)), ("grp", P(1, 4))],
                        [("row", T, D, F32), ("row", T, D, BF16)])
    cw1, cb1 = full["l1_ffn_conv_w"], row(l1_ffn_conv_b)
    u1, act1, f1 = _ffn_fwd("l1_ffn", h2_1, full["l1_ffn_up"], cw1, cb1, full["l1_ffn_down"], T, T, T, F)

    fnw = row(final_norm)

    def f_head(i, x3, ff, g5, w, tgt):
        fin = lambda a, b, g_, w_: _rms(a + g_ * b, w_)
        y, pull = jax.vjp(fin, x3, ff, g5, w)
        err = y - tgt
        dx3, dff, dg5, dw = pull(err / D)
        loss = 0.5 * jnp.sum(jnp.mean(err * err, axis=-1, keepdims=True), axis=0, keepdims=True)
        return jnp.broadcast_to(loss, (1, LANES)), dx3, dff, dg5, dw

    head_ins = [("row", X3, 0, D), ("row", f1, 0, D), ("grp", P(1, 5)), ("full", fnw), ("row", loss_target[0], 0, D)]
    loss_acc, dX3, df1, dm5_1, g_final = _rowwise(
        "loss_head", f_head, nlat, TR, head_ins,
        [("acc", 1, LANES), ("row", T, D, F32), ("row", T, D, BF16), ("acc", 1, D), ("acc", 1, D)])
    loss = lax.psum(loss_acc[0, 0], ("x", "y", "c"))

    grads = {"final_norm": g_final}
    dh2_1, grads["l1_ffn_up"], grads["l1_ffn_conv_w"], grads["l1_ffn_conv_b"], grads["l1_ffn_down"] = _ffn_bwd(
        "l1_dffn", df1, h2_1, u1, act1, full["l1_ffn_up"], cw1, cb1, full["l1_ffn_down"], T, T, T, F)
    resmod_bwd = _vjp_of(_f_resmod, 6, (0, 1, 2, 3, 4, 5))
    dX2l, dy1, dm2_1, grads["l1_norm2"], dm3_1, dm4_1 = _rowwise(
        "l1_dresmod2", resmod_bwd, nlat, TR,
        [("row", X2, 0, D), ("row", y1, 0, D), ("grp", P(1, 2)), ("full", n2_1), ("grp", P(1, 3)), ("grp", P(1, 4)),
         ("row", dX3, 0, D), ("row", dh2_1, 0, D)],
        [("row", T, D, F32), ("row", T, D, BF16)] + [("acc", 1, D)] * 4)
    dgo = _matmul("l1_dout_x", dy1, full["l1_w_out"], "nt", T, VAL, D)
    grads["l1_w_out"] = _matmul("l1_dout_w", go, dy1, "tn", VAL, D, T, out_dtype=BF16)

    def f_glaout_bwd(i, of, ob, og, w, d):
        is_ctx = i >= nlat
        _, pull = jax.vjp(lambda a, b, c_, w_: f_glaout(i, a, b, c_, w_)[0], of, ob, og, w)
        dof, _, dog, dw = pull(jnp.where(is_ctx, 0.0, d))
        return dof, dog, dw

    glaout_b_ins = glaout_ins + [("rowc", dgo, 0, VAL, nlat - 1)]
    do_gla, dog, g_onorm = _rowwise("l1_dglaout", f_glaout_bwd, nt, TR, glaout_b_ins,
                                    [("row", TA, VAL, F32), ("row", TA, VAL, F32), ("acc", 1, DV)])
    grads["l1_o_norm"] = g_onorm
    dq_f, dk_f, dv_f, dg_f = _gla_bwd("gla_bwd_f", z1, gates, st_f, do_gla, dims_f, False, nt, nlat)
    dq_b, dk_b, dv_b, dg_b = _gla_bwd("gla_bwd_b", z1, gates, st_b, do_gla, dims_b, True, nt, nlat)

    def f_dz1(i, dvf, dvb, dog_, dkf, dkb, dqf, dqb, dgf, dgb, lr, wg, bg):
        _, pull = jax.vjp(lambda a, b, c_: f_gates(i, a, b, c_)[0], lr, wg, bg)
        dlr, dwg, dbg = pull(jnp.concatenate([dgf, dgb], axis=1))
        return jnp.concatenate([dvf + dvb, dog_, dkf + dkb, dqf + dqb, dlr], axis=1), dwg, dbg

    dz1_ins = [("row", dv_f, 0, VAL), ("row", dv_b, 0, VAL), ("row", dog, 0, VAL), ("row", dk_f, 0, KEY), ("row", dk_b, 0, KEY),
               ("row", dq_f, 0, KEY), ("row", dq_b, 0, KEY), ("row", dg_f, 0, KEY), ("row", dg_b, 0, KEY)] + gate_ins
    dz1, g_wgate, g_bgate = _rowwise("l1_dz", f_dz1, nt, TR, dz1_ins, [("row", TA, W1, BF16), ("acc", LANES, 2 * KEY), ("acc", 1, 2 * KEY)])
    grads["l1_gate_fw_w"], grads["l1_gate_bw_w"] = g_wgate[:GATE_RANK, :KEY], g_wgate[GATE_RANK:2 * GATE_RANK, KEY:]
    grads["l1_gate_fw_b"], grads["l1_gate_bw_b"] = g_bgate[:, :KEY], g_bgate[:, KEY:]
    dh1 = _matmul("l1_din_x", dz1, w_in1p, "nt", TA, D, W1)
    g1p = _matmul("l1_din_w", h1, dz1, "tn", D, W1, TA, out_dtype=BF16)
    grads["l1_w_in"] = jnp.concatenate([g1p[:, KOFF1:QOFF1], g1p[:, :OGOFF], g1p[:, LROFF:LROFF + 2 * GATE_RANK],
                                        g1p[:, QOFF1:LROFF], g1p[:, OGOFF:KOFF1]], axis=1)

    def f_resmod1_bwd(i, x_, y_, g_, w_, sh, sc, dx2, dh):
        return resmod_bwd(i, x_, y_, g_, w_, sh, sc, jnp.where(i >= nlat, 0.0, dx2), dh)

    dX1, df0, dm5_0, grads["l1_norm1"], dm0_1, dm1_1 = _rowwise(
        "l1_dresmod1", f_resmod1_bwd, nt, TR,
        [("row", X1, 0, D), ("row", f0, 0, D), ("grp", P(0, 5)), ("full", n1_1), ("grp", P(1, 0)), ("grp", P(1, 1)),
         ("rowc", dX2l, 0, D, nlat - 1), ("row", dh1, 0, D)],
        [("row", TA, D, F32), ("row", TA, D, BF16), ("accg", 1, D), ("acc", 1, D), ("accg", 1, D), ("accg", 1, D)], nlat)

    dh2_0, grads["l0_ffn_up"], grads["l0_ffn_conv_w"], grads["l0_ffn_conv_b"], grads["l0_ffn_down"] = _ffn_bwd(
        "l0_dffn", df0, h2_0, u0, act0, full["l0_ffn_up"], cw0, cb0, full["l0_ffn_down"], TA, T, TA, F)
    dXd, dy0, dm2_0, grads["l0_norm2"], dm3_0, dm4_0 = _rowwise(
        "l0_dresmod2", resmod_bwd, nt, TR,
        [("row", X, 0, D), ("row", y0, 0, D), ("grp", P(0, 2)), ("full", n2_0), ("grp", P(0, 3)), ("grp", P(0, 4)),
         ("row", dX1, 0, D), ("row", dh2_0, 0, D)],
        [("row", TA, D, F32), ("row", TA, D, BF16), ("accg", 1, D), ("acc", 1, D), ("accg", 1, D), ("accg", 1, D)], nlat)
    dcat = _matmul("l0_dout_x", dy0, full["l0_w_out"], "nt", TA, CC + H * V_HEAD, D)
    grads["l0_w_out"] = _matmul("l0_dout_w", cat0, dy0, "tn", CC + H * V_HEAD, D, TA, out_dtype=BF16)

    def f_conva_bwd(i, ax, ab, ac_, pax, pab, pac, nax, nab, nac, dy, pdy, ndy, w):
        prev, nxt = _shifters(i, TR, T, TA)
        p = ac_ * ax
        pp, pn = prev(p, pac * pax), nxt(p, nac * nax)
        cv = w[0:1] * pp + w[1:2] * p + w[2:3] * pn
        dcv = dy * ab
        dp = w[0:1] * nxt(dcv, ndy * nab) + w[1:2] * dcv + w[2:3] * prev(dcv, pdy * pab)
        return (jnp.concatenate([dp * ac_, dy * cv, dp * ax], axis=1), _colsum(pp * dcv), _colsum(p * dcv), _colsum(pn * dcv))

    conva_b_ins = [("row", z0, 0, CC), ("row", z0, 1, CC), ("row", z0, 2, CC),
                   ("prev", z0, 0, CC), ("prev", z0, 1, CC), ("prev", z0, 2, CC),
                   ("next", z0, 0, CC), ("next", z0, 1, CC), ("next", z0, 2, CC),
                   ("row", dcat, 0, CC), ("prev", dcat, 0, CC), ("next", dcat, 0, CC), ("full", conv_a)]
    dz_a, ga0, ga1, ga2 = _rowwise("l0_dconva", f_conva_bwd, nt, TR, conva_b_ins, [("row", TA, 3 * CC, BF16)] + [("acc", 1, CC)] * 3)
    grads["l0_conv_a"] = jnp.concatenate([ga0, ga1, ga2], axis=0)
    do_cb0 = CC // V_HEAD
    dq = _attn_bwd_dq(q, kv, kr, o_att, dcat, lse, do_cb0, T, TA)
    dkv, dkr_h = _attn_bwd_dkv(q, kv, kr, o_att, dcat, lse, do_cb0, T, TA)

    def f_rope_bwd(i, dq_, dkr_, cs, sn):
        pieces = []
        for h in range(H):
            pieces += [dq_[:, h * QK_PAD:h * QK_PAD + QK_NOPE], _rope128_t(dq_[:, h * QK_PAD + QK_NOPE:(h + 1) * QK_PAD], cs, sn)]
        dk = dkr_[0]
        for h in range(1, H):
            dk = dk + dkr_[h]
        return jnp.concatenate(pieces, axis=1), _rope128_t(dk, cs, sn)

    def rope_bwd_call():
        def body(dq_ref, dkr_ref, cs_ref, sn_ref, dqr_ref, dkk_ref):
            a, b = f_rope_bwd(0, dq_ref[...], dkr_ref[...], cs_ref[...], sn_ref[...])
            dqr_ref[...] = a.astype(BF16)
            dkk_ref[...] = b.astype(BF16)
        return pl.pallas_call(
            body, name="l0_drope", grid=(nt,),
            out_shape=(jax.ShapeDtypeStruct((TA, H * QK_PAD), BF16), jax.ShapeDtypeStruct((TA, LANES), BF16)),
            in_specs=[pl.BlockSpec((TR, H * QK_PAD), lambda i: (i, 0)), pl.BlockSpec((H, TR, LANES), lambda i: (0, i, 0)),
                      pl.BlockSpec((TR, LANES), lambda i: (i, 0)), pl.BlockSpec((TR, LANES), lambda i: (i, 0))],
            out_specs=(pl.BlockSpec((TR, H * QK_PAD), lambda i: (i, 0)), pl.BlockSpec((TR, LANES), lambda i: (i, 0))),
            compiler_params=_cparams(("parallel",)),
        )(dq, dkr_h, cos_t, sin_t)

    dqraw, dz_kr = rope_bwd_call()
    dqn = _matmul("l0_dqb_x", dqraw, w_qbp, "nt", TA, Q_LORA, H * QK_PAD)
    g_wqbp = _matmul("l0_dqb_w", qn, dqraw, "tn", Q_LORA, H * QK_PAD, TA).reshape(Q_LORA, H, QK_PAD)
    g_rope = g_wqbp[:, :, QK_NOPE:QK_NOPE + QK_ROPE] + _rot_cols_t(g_wqbp[:, :, QK_NOPE + QK_ROPE:])
    grads["l0_w_qb"] = jnp.concatenate([g_wqbp[:, :, :QK_NOPE], g_rope], axis=2).reshape(Q_LORA, H * (QK_NOPE + QK_ROPE)).astype(BF16)
    dkvn = _matmul("l0_dkvb_x", dkv, full["l0_w_kvb"], "nt", TA, KV_LORA, H * QK_PAD)
    grads["l0_w_kvb"] = _matmul("l0_dkvb_w", kvn, dkv, "tn", KV_LORA, H * QK_PAD, TA, out_dtype=BF16)
    norms_bwd = _vjp_of(_f_norms, 4, (0, 1, 2, 3))
    dz_kv, dz_q, grads["l0_kv_norm"], grads["l0_q_norm"] = _rowwise(
        "l0_dlatnorm", norms_bwd, nt, TR, norm_ins + [("row", dkvn, 0, KV_LORA), ("row", dqn, 0, Q_LORA)],
        [("row", TA, KV_LORA, BF16), ("row", TA, Q_LORA, BF16), ("acc", 1, KV_LORA), ("acc", 1, Q_LORA)])
    dz0 = jnp.concatenate([dz_a, dz_q, dz_kv, dz_kr, jnp.zeros((TA, W0 - KROFF0 - LANES), BF16)], axis=1)
    dh0 = _matmul("l0_din_x", dz0, w_in0p, "nt", TA, D, W0)
    g0p = _matmul("l0_din_w", h0, dz0, "tn", D, W0, TA)
    g_kr = g0p[:, KROFF0:KROFF0 + QK_ROPE] + _rot_cols_t(g0p[:, KROFF0 + QK_ROPE:KROFF0 + 2 * QK_ROPE])
    grads["l0_w_in"] = jnp.concatenate([g0p[:, KVOFF0:KROFF0], g_kr, g0p[:, QOFF0:KVOFF0], g0p[:, :QOFF0]], axis=1).astype(BF16)

    def f_mod_bwd(i, x_, w_, sh, sc, dh, dxd):
        _, pull = jax.vjp(lambda a, b, c_, d_: _modulate(a, b, c_, d_), x_, w_, sh, sc)
        dx, dw, dsh, dsc = pull(dh)
        return dx + dxd, dw, dsh, dsc

    dXf, grads["l0_norm1"], dm0_0, dm1_0 = _rowwise(
        "l0_dmod1", f_mod_bwd, nt, TR,
        [("row", X, 0, D), ("full", n1_0), ("grp", P(0, 0)), ("grp", P(0, 1)), ("row", dh0, 0, D), ("row", dXd, 0, D)],
        [("row", TA, D, F32), ("acc", 1, D), ("accg", 1, D), ("accg", 1, D)], nlat)
    grad_x = dXf[:T][None]

    zD = jnp.zeros((1, D), F32)
    lat = lambda a: a[0] if a.ndim == 3 else a
    cxt = lambda a: a[1] if a.ndim == 3 else zD
    dmods = []
    for parts in ((dm0_0, dm1_0, dm2_0, dm3_0, dm4_0, dm5_0), (dm0_1, dm1_1, dm2_1, dm3_1, dm4_1, dm5_1)):
        dmods.append((jnp.concatenate([lat(a) for a in parts], axis=1), jnp.concatenate([cxt(a) for a in parts], axis=1)))
    small_names = ["l0_norm1", "l0_norm2", "l0_kv_norm", "l0_q_norm", "l0_conv_a", "l0_ffn_conv_w", "l0_ffn_conv_b",
                   "l1_norm1", "l1_norm2", "l1_o_norm", "l1_gate_fw_w", "l1_gate_fw_b", "l1_gate_bw_w", "l1_gate_bw_b",
                   "l1_ffn_conv_w", "l1_ffn_conv_b", "final_norm"]
    pieces = [("dm0", dmods[0][0]), ("dmc0", dmods[0][1]), ("dm1", dmods[1][0]), ("dmc1", dmods[1][1])]
    pieces += [("l0_ada_b", dmods[0][0] + dmods[0][1]), ("l1_ada_b", dmods[1][0] + dmods[1][1])]
    pieces += [(n, grads[n]) for n in small_names]
    offs, cur = {}, 0
    for n, a in pieces:
        offs[n] = (cur, a.size, a.shape)
        cur += -(-a.size // LANES) * LANES
    n_pad = -(-cur // 1024) * 1024
    flat = jnp.concatenate([jnp.pad(a.reshape(-1), (0, -a.size % LANES)) for _, a in pieces] + [jnp.zeros((n_pad - cur,), F32)])
    (small_g,) = _exchange("gather_small", [flat.reshape(n_pad // LANES, LANES)], "ag_all")
    small_sum = _sum_slots("sum_small", small_g).reshape(-1)
    small_all = small_g.reshape(8, -1)
    take = lambda n: small_sum[offs[n][0]:offs[n][0] + offs[n][1]].reshape(offs[n][2])

    cc_parts = []
    for l in (0, 1):
        o_m, sz, _ = offs["dm%d" % l]
        d16 = jnp.concatenate([small_all[:, o_m:o_m + sz], take("dmc%d" % l).reshape(1, -1), jnp.zeros((7, sz), F32)], axis=0)
        d16 = lax.dynamic_slice_in_dim(d16.reshape(16, 4, ADA_S), chip, 1, axis=1).reshape(16, ADA_S)
        grads["l%d_ada_w" % l] = _matmul("ada_dw", act16, d16, "tn", D, ADA_S, 16, tn_cap=512)
        cc_parts.append(_matmul("ada_dx", d16, W["l%d_ada_w" % l], "nt", 16, D, ADA_S, tn_cap=512))
        grads["l%d_ada_b" % l] = take("l%d_ada_b" % l).reshape(-1)
    cc_g = _exchange("gather_cc", cc_parts, "ag_xy")

    def cc_call():
        def body(a_ref, b_ref, c_ref, o_ref):
            tot = a_ref[0, 8:9, :] + b_ref[0, 8:9, :]
            for j in range(1, 4):
                tot = tot + (a_ref[j, 8:9, :] + b_ref[j, 8:9, :])
            _, pull = jax.vjp(_silu, c_ref[...])
            o_ref[...] = pull(tot)[0]
        return pl.pallas_call(body, name="c_ctx_grad", out_shape=jax.ShapeDtypeStruct((1, D), F32))(cc_g[0], cc_g[1], row(c_ctx))

    grads["c_ctx"] = cc_call().reshape(-1)

    row_sh = {"l0_w_out", "l1_w_out", "l0_ffn_down", "l1_ffn_down"}
    def by_half(n):
        s = grads[n].reshape((4, -1) + grads[n].shape[1:]) if n in row_sh else _cols_to_shards(grads[n])
        return s.reshape(4, 2, s.shape[1] // 2, s.shape[2]).transpose(1, 0, 2, 3)

    split = _exchange("pair_split_grads", [by_half(n) for n in big], "pair_split")
    pair = [_sum_pair("pairsum_" + n[3:], split[2 * k], split[2 * k + 1], BF16) for k, n in enumerate(big)]
    landed = _exchange("scatter_grads", pair, "a2a_xy")
    halves = [_sum_slots("sum_" + n[3:], a) for n, a in zip(big, landed)]
    whole = _exchange("pair_grads", halves, "pair_gather")

    out_g, out_d, out_m, out_v = {}, {}, {}, {}
    for n, g2 in zip(big, whole):
        out_g[n], out_d[n], out_m[n], out_v[n] = _adam("adam_" + n[3:], W[n], [g2.reshape(W[n].shape)], MO[n], VO[n])
    for l in (0, 1):
        n = "l%d_ada_w" % l
        out_g[n], out_d[n], out_m[n], out_v[n] = _adam("adam_ada_w", W[n], [grads[n]], MO[n], VO[n])
    rest = [n for n in names if n not in out_g]
    g_rest = {}
    for n in rest:
        if n == "c_ctx" or n.endswith("ada_b"):
            g_rest[n] = grads[n]
        elif n in small_sh:
            gfull = take(n)
            cs = gfull.shape[1] // 4
            g_rest[n] = lax.dynamic_slice_in_dim(gfull, chip * cs, cs, axis=1)
        else:
            g_rest[n] = take(n).reshape(W[n].shape)
    sizes = [W[n].size for n in rest]
    tot = sum(-(-s // LANES) * LANES for s in sizes)
    tot_pad = -(-tot // 1024) * 1024

    def pack(d):
        parts = [jnp.pad(d[n].reshape(-1), (0, -d[n].size % LANES)) for n in rest]
        return jnp.concatenate(parts + [jnp.zeros((tot_pad - tot,), F32)]).reshape(tot_pad // LANES, LANES)

    packed = _adam("adam_small", pack(W), [pack(g_rest)], pack(MO), pack(VO))
    cur = 0
    for n, s in zip(rest, sizes):
        for dst, arr in zip((out_g, out_d, out_m, out_v), packed):
            dst[n] = arr.reshape(-1)[cur:cur + s].reshape(W[n].shape)
        cur += -(-s // LANES) * LANES

    return (loss, grad_x, *[out_g[n] for n in names], *[out_d[n] for n in names],
            *[out_m[n] for n in names], *[out_v[n] for n in names])
```

```python
import numpy as np

import jax
import jax.numpy as jnp
from jax import lax
from jax.experimental import pallas as pl
from jax.experimental.pallas import tpu as pltpu

F32, BF16 = jnp.float32, jnp.bfloat16
HIGHEST = lax.Precision.HIGHEST
MESH_ID = pl.DeviceIdType.MESH

EPS = 1e-6
N_MOD = 6
MLA_HEADS, QK_NOPE, QK_ROPE, V_HEAD, Q_LORA, KV_LORA = 8, 128, 64, 128, 512, 256
QK_PAD = 2 * QK_NOPE
ROPE_THETA, GRID_W = 10000.0, 64
GLA_HEADS, GATE_RANK, GATE_NORMALIZER, CHUNK = 4, 16, 16.0, 64
ADAM_LR, ADAM_B1, ADAM_B2, ADAM_EPS, ADAM_WD, ADAM_STEP = 0.001, 0.9, 0.999, 1e-08, 0.01, 10

LANES = 128
TR = 256
V7X_VMEM_BYTES = 64 * 2 ** 20
VMEM_LIMIT = V7X_VMEM_BYTES - 8 * 2 ** 20

NT_DIMS = (((1,), (1,)), ((), ()))
TN_DIMS = (((0,), (0,)), ((), ()))
NN_DIMS = (((1,), (0,)), ((), ()))

def _rot_cols(w):
    q = QK_ROPE // 4
    return jnp.concatenate([-w[..., q:2 * q], w[..., :q], -w[..., 3 * q:], w[..., 2 * q:3 * q]], axis=-1)


def _rot_cols_t(g):
    q = QK_ROPE // 4
    return jnp.concatenate([g[..., q:2 * q], -g[..., :q], g[..., 3 * q:], -g[..., 2 * q:3 * q]], axis=-1)


def _cparams(sem=None):
    return pltpu.CompilerParams(dimension_semantics=sem, vmem_limit_bytes=VMEM_LIMIT)


def _tile(dim, cap, quantum=LANES):
    if dim <= cap:
        return dim
    t = (cap // quantum) * quantum
    while t >= quantum:
        if dim % t == 0:
            return t
        t -= quantum
    return dim


_REL_XY = ((1, 0, 0), (0, 1, 0), (1, 1, 0))
_REL_ALL = tuple((a, b, c) for a in (0, 1) for b in (0, 1) for c in (0, 1))[1:]
_REL_C = ((0, 0, 1),)


def _exchange(name, arrs, mode, fill_own=True):
    rels = {"ag_xy": _REL_XY, "a2a_xy": _REL_XY, "ag_all": _REL_ALL, "swap_c": _REL_C}[mode]
    n, nr = len(arrs), len(rels)
    lead = {"ag_xy": (4,), "ag_all": (8,), "a2a_xy": (), "swap_c": ()}[mode]
    out_shape = tuple(jax.ShapeDtypeStruct(lead + a.shape, a.dtype) for a in arrs)

    def body(*refs):
        ins, outs = refs[:n], refs[n:2 * n]
        send, recv, loc = refs[2 * n:]
        x, y, c = lax.axis_index("x"), lax.axis_index("y"), lax.axis_index("c")
        chip, dev = 2 * x + y, 4 * x + 2 * y + c
        started, local = [], []
        for a in range(n):
            i_ref, o = ins[a], outs[a]
            if fill_own and mode != "swap_c":
                slot = dev if mode == "ag_all" else chip
                lc = pltpu.make_async_copy(i_ref.at[chip] if mode == "a2a_xy" else i_ref, o.at[slot], loc.at[a])
                lc.start()
                local.append(lc)
            for r, (bx, by, bc) in enumerate(rels):
                px = 1 - x if bx else x
                py = 1 - y if by else y
                pc = 1 - c if bc else c
                pchip, pdev = 2 * px + py, 4 * px + 2 * py + pc
                if mode == "ag_xy":
                    src, dst, mine = i_ref, o.at[chip], o.at[pchip]
                elif mode == "ag_all":
                    src, dst, mine = i_ref, o.at[dev], o.at[pdev]
                elif mode == "a2a_xy":
                    src, dst, mine = i_ref.at[pchip], o.at[chip], o.at[pchip]
                else:
                    src, dst, mine = i_ref, o, o
                k = a * nr + r
                cp = pltpu.make_async_remote_copy(src_ref=src, dst_ref=dst, send_sem=send.at[k], recv_sem=recv.at[k],
                                                  device_id=(px, py, pc), device_id_type=MESH_ID)
                cp.start()
                landing = pltpu.make_async_remote_copy(src_ref=src, dst_ref=mine, send_sem=send.at[k],
                                                       recv_sem=recv.at[k], device_id=(px, py, pc),
                                                       device_id_type=MESH_ID)
                started.append((cp, landing))
        for cp, landing in started:
            cp.wait_send()
            landing.wait_recv()
        for lc in local:
            lc.wait()

    hbm = pl.BlockSpec(memory_space=pl.ANY)
    res = pl.pallas_call(
        body, name=name, out_shape=out_shape, in_specs=[hbm] * n, out_specs=tuple([hbm] * n),
        scratch_shapes=[pltpu.SemaphoreType.DMA((n * nr,)), pltpu.SemaphoreType.DMA((n * nr,)),
                        pltpu.SemaphoreType.DMA((max(n, 1),))],
    )(*arrs)
    return list(res)


def _matmul(name, a, b, mode, M, N, K, out_dtype=F32, a_off=(0, 0), b_off=(0, 0), tm_cap=1024, tn_cap=1024, tk_cap=2048):
    tm, tn = _tile(M, tm_cap, LANES if M % LANES == 0 else 8), _tile(N, tn_cap)
    tk = K if K <= 4096 else _tile(K, tk_cap)
    nk = K // tk
    assert M % tm == 0 and N % tn == 0 and K % tk == 0, (name, M, N, K, tm, tn, tk)
    if mode == "nn":
        ab, bb, dims = (tm, tk), (tk, tn), NN_DIMS
        ai = lambda i, j, k: (i + a_off[0] // tm, k + a_off[1] // tk)
        bi = lambda i, j, k: (k + b_off[0] // tk, j + b_off[1] // tn)
        chk = (a_off[0] % tm, a_off[1] % tk, b_off[0] % tk, b_off[1] % tn)
    elif mode == "nt":
        ab, bb, dims = (tm, tk), (tn, tk), NT_DIMS
        ai = lambda i, j, k: (i + a_off[0] // tm, k + a_off[1] // tk)
        bi = lambda i, j, k: (j + b_off[0] // tn, k + b_off[1] // tk)
        chk = (a_off[0] % tm, a_off[1] % tk, b_off[0] % tn, b_off[1] % tk)
    else:
        ab, bb, dims = (tk, tm), (tk, tn), TN_DIMS
        ai = lambda i, j, k: (k + a_off[0] // tk, i + a_off[1] // tm)
        bi = lambda i, j, k: (k + b_off[0] // tk, j + b_off[1] // tn)
        chk = (a_off[0] % tk, a_off[1] % tm, b_off[0] % tk, b_off[1] % tn)
    assert not any(chk), (name, chk)

    def body(a_ref, b_ref, o_ref, *acc):
        p = lax.dot_general(a_ref[...].astype(BF16), b_ref[...].astype(BF16), dims, preferred_element_type=F32)
        if nk == 1:
            o_ref[...] = p.astype(out_dtype)
        else:
            k = pl.program_id(2)

            @pl.when(k == 0)
            def _():
                acc[0][...] = p

            @pl.when(k > 0)
            def _():
                acc[0][...] += p

            @pl.when(k == nk - 1)
            def _():
                o_ref[...] = acc[0][...].astype(out_dtype)

    return pl.pallas_call(
        body, name=name, out_shape=jax.ShapeDtypeStruct((M, N), out_dtype), grid=(M // tm, N // tn, nk),
        in_specs=[pl.BlockSpec(ab, ai), pl.BlockSpec(bb, bi)], out_specs=pl.BlockSpec((tm, tn), lambda i, j, k: (i, j)),
        scratch_shapes=[pltpu.VMEM((tm, tn), F32)] if nk > 1 else [],
        compiler_params=_cparams(("parallel", "parallel", "arbitrary")),
    )(a, b)


def _rowwise(name, fn, grid_n, tr, ins, outs, nlat=None):
    nlat = grid_n if nlat is None else nlat
    grp = lambda i: jnp.minimum(i // nlat, 1)
    in_specs, args = [], []
    for spec in ins:
        kind, arr = spec[0], spec[1]
        if kind == "row":
            in_specs.append(pl.BlockSpec((tr, spec[3]), lambda i, cb=spec[2]: (i, cb)))
        elif kind == "rowc":
            in_specs.append(pl.BlockSpec((tr, spec[3]), lambda i, cb=spec[2], mx=spec[4]: (jnp.minimum(i, mx), cb)))
        elif kind == "prev":
            in_specs.append(pl.BlockSpec((8, spec[3]), lambda i, cb=spec[2]: (jnp.maximum(i * (tr // 8) - 1, 0), cb)))
        elif kind == "next":
            nb = arr.shape[0] // 8
            in_specs.append(pl.BlockSpec((8, spec[3]), lambda i, cb=spec[2], nb=nb: (jnp.minimum((i + 1) * (tr // 8), nb - 1), cb)))
        elif kind == "grp":
            in_specs.append(pl.BlockSpec((None,) + arr.shape[1:], lambda i: (grp(i), 0, 0)))
        else:
            in_specs.append(pl.BlockSpec(arr.shape, lambda i: (0, 0)))
        args.append(arr)
    out_shape, out_specs = [], []
    for spec in outs:
        if spec[0] == "row":
            out_shape.append(jax.ShapeDtypeStruct((spec[1], spec[2]), spec[3]))
            out_specs.append(pl.BlockSpec((tr, spec[2]), lambda i: (i, 0)))
        elif spec[0] == "acc":
            out_shape.append(jax.ShapeDtypeStruct((spec[1], spec[2]), F32))
            out_specs.append(pl.BlockSpec((spec[1], spec[2]), lambda i: (0, 0)))
        else:
            out_shape.append(jax.ShapeDtypeStruct((2, spec[1], spec[2]), F32))
            out_specs.append(pl.BlockSpec((None, spec[1], spec[2]), lambda i: (grp(i), 0, 0)))
    n_in = len(ins)
    has_acc = any(s[0] != "row" for s in outs)

    def body(*refs):
        i = pl.program_id(0)
        res = fn(i, *[r[...] for r in refs[:n_in]])
        for spec, ref, val in zip(outs, refs[n_in:], res):
            if spec[0] == "row":
                ref[...] = val.astype(ref.dtype)
            else:
                first = (i == 0) if spec[0] == "acc" else jnp.logical_or(i == 0, i == nlat)

                @pl.when(first)
                def _(ref=ref, val=val):
                    ref[...] = val

                @pl.when(jnp.logical_not(first))
                def _(ref=ref, val=val):
                    ref[...] += val

    return pl.pallas_call(
        body, name=name, out_shape=tuple(out_shape), grid=(grid_n,), in_specs=in_specs, out_specs=tuple(out_specs),
        compiler_params=_cparams(("arbitrary",) if has_acc else ("parallel",)),
    )(*args)


def _vjp_of(fwd, n_in, wrt):
    def bwd(i, *args):
        _, pull = jax.vjp(lambda *a: fwd(i, *a), *args[:n_in])
        g = pull(tuple(args[n_in:]))
        return tuple(g[k] for k in wrt)
    return bwd


def _rms(x, w):
    return x * lax.rsqrt(jnp.mean(x * x, axis=-1, keepdims=True) + EPS) * w


def _modulate(x, w, shift, scale):
    return _rms(x, w) * (1.0 + scale) + shift


def _silu(x):
    return x * jax.nn.sigmoid(x)


def _log_sigmoid(x):
    return jnp.minimum(x, 0.0) - jnp.log(1.0 + jnp.exp(-jnp.abs(x)))


def _f_mod(i, x, w, sh, sc):
    return (_modulate(x, w, sh, sc),)


def _f_resmod(i, x, y, gate, w, sh, sc):
    x1 = x + gate * y
    return x1, _modulate(x1, w, sh, sc)


def _f_norms(i, kvl, ql, kvw, qw):
    return _rms(kvl, kvw), _rms(ql, qw)


def _f_silu(i, x):
    return (_silu(x),)


def _rope128(x2, cos, sin):
    return x2 * cos + pltpu.roll(x2, QK_ROPE, 1) * sin


def _rope128_t(d2, cos, sin):
    return d2 * cos + pltpu.roll(d2 * sin, QK_ROPE, 1)


def _shifters(i, tr, T, TA):
    rows = i * tr + lax.broadcasted_iota(jnp.int32, (tr, 1), 0)
    loc = lax.broadcasted_iota(jnp.int32, (tr, 1), 0)
    has_prev = jnp.logical_and(rows != 0, rows != T)
    has_next = jnp.logical_and(rows != T - 1, rows != TA - 1)

    def prev(x, halo):
        r = jnp.where(loc == 0, halo[7:8, :], pltpu.roll(x, 1, 0))
        return jnp.where(has_prev, r, 0.0)

    def nxt(x, halo):
        r = jnp.where(loc == tr - 1, halo[0:1, :], pltpu.roll(x, tr - 1, 0))
        return jnp.where(has_next, r, 0.0)

    return prev, nxt


def _colsum(x):
    return jnp.sum(x, axis=0, keepdims=True)


MLA_SCALE = (QK_NOPE + QK_ROPE) ** -0.5


def _mla_fwd_lat(q, kf, kv, T, TA):
    H = MLA_HEADS
    tq, tk = _tile(T, 1024, TR), _tile(TA, 1408)
    nk = TA // tk

    def body(q_ref, k_ref, v_ref, o_ref, lse_ref, m_sc, l_sc, acc_sc):
        ki = pl.program_id(2)

        @pl.when(ki == 0)
        def _():
            m_sc[...] = jnp.full_like(m_sc, -jnp.inf)
            l_sc[...] = jnp.zeros_like(l_sc)
            acc_sc[...] = jnp.zeros_like(acc_sc)

        s = lax.dot_general(q_ref[...], k_ref[...], NT_DIMS, preferred_element_type=F32) * MLA_SCALE
        m_new = jnp.maximum(m_sc[...], jnp.max(s, axis=-1, keepdims=True))
        alpha = jnp.exp(m_sc[...] - m_new)
        p = jnp.exp(s - m_new)
        l_sc[...] = alpha * l_sc[...] + jnp.sum(p, axis=-1, keepdims=True)
        acc_sc[...] = alpha * acc_sc[...] + jnp.dot(p.astype(BF16), v_ref[...], preferred_element_type=F32)
        m_sc[...] = m_new

        @pl.when(ki == nk - 1)
        def _():
            o_ref[...] = (acc_sc[...] / l_sc[...]).astype(o_ref.dtype)
            lse_ref[...] = m_sc[...] + jnp.log(l_sc[...])

    return pl.pallas_call(
        body, name="mla_fwd_lat", grid=(H, T // tq, nk),
        out_shape=(jax.ShapeDtypeStruct((T, H * V_HEAD), BF16), jax.ShapeDtypeStruct((H, T, 1), F32)),
        in_specs=[pl.BlockSpec((tq, QK_PAD), lambda h, i, k: (i, h)), pl.BlockSpec((tk, QK_PAD), lambda h, i, k: (k, h)),
                  pl.BlockSpec((tk, V_HEAD), lambda h, i, k: (k, 2 * h + 1))],
        out_specs=(pl.BlockSpec((tq, V_HEAD), lambda h, i, k: (i, h)), pl.BlockSpec((None, tq, 1), lambda h, i, k: (h, i, 0))),
        scratch_shapes=[pltpu.VMEM((tq, 1), F32), pltpu.VMEM((tq, 1), F32), pltpu.VMEM((tq, V_HEAD), F32)],
        compiler_params=_cparams(("parallel", "parallel", "arbitrary")),
    )(q, kf, kv)


def _mla_ctx_probs(q_ref, k_ref):
    s = lax.dot_general(q_ref[...], k_ref[...], NT_DIMS, preferred_element_type=F32) * MLA_SCALE
    p = jnp.exp(s - jnp.max(s, axis=-1, keepdims=True))
    return p, jnp.sum(p, axis=-1, keepdims=True)


def _mla_fwd_ctx(q, kf, kv, T):
    H, cb = MLA_HEADS, T // TR

    def body(q_ref, k_ref, v_ref, o_ref):
        p, l = _mla_ctx_probs(q_ref, k_ref)
        o_ref[...] = (jnp.dot(p.astype(BF16), v_ref[...], preferred_element_type=F32) / l).astype(o_ref.dtype)

    return pl.pallas_call(
        body, name="mla_fwd_ctx", grid=(H,), out_shape=jax.ShapeDtypeStruct((TR, H * V_HEAD), BF16),
        in_specs=[pl.BlockSpec((TR, QK_PAD), lambda h: (cb, h)), pl.BlockSpec((TR, QK_PAD), lambda h: (cb, h)),
                  pl.BlockSpec((TR, V_HEAD), lambda h: (cb, 2 * h + 1))],
        out_specs=pl.BlockSpec((TR, V_HEAD), lambda h: (0, h)), compiler_params=_cparams(("parallel",)),
    )(q, kf, kv)


def _mla_grads(p, q_ref, k_ref, v_ref, do, delta):
    dob = do.astype(BF16)
    dp = lax.dot_general(dob, v_ref[...], NT_DIMS, preferred_element_type=F32)
    ds = (p * (dp - delta) * MLA_SCALE).astype(BF16)
    return (jnp.dot(ds, k_ref[...], preferred_element_type=F32),
            lax.dot_general(ds, q_ref[...], TN_DIMS, preferred_element_type=F32),
            lax.dot_general(p.astype(BF16), dob, TN_DIMS, preferred_element_type=F32))


def _mla_bwd_lat(q, kf, kv, o, dcat, lse, do_cb0, T, TA):
    H = MLA_HEADS
    tq, tk = _tile(T, 512, TR), _tile(TA, 1408)
    nq, nk = T // tq, TA // tk

    def body(q_ref, k_ref, v_ref, o_ref, do_ref, lse_ref, dq_ref, dk_ref, dv_ref, dk_acc, dv_acc):
        ki, qi = pl.program_id(1), pl.program_id(2)

        @pl.when(jnp.logical_and(ki == 0, qi == 0))
        def _():
            dq_ref[...] = jnp.zeros_like(dq_ref)

        @pl.when(qi == 0)
        def _():
            dk_acc[...] = jnp.zeros_like(dk_acc)
            dv_acc[...] = jnp.zeros_like(dv_acc)

        s = lax.dot_general(q_ref[...], k_ref[...], NT_DIMS, preferred_element_type=F32) * MLA_SCALE
        do = do_ref[...]
        delta = jnp.sum(do * o_ref[...].astype(F32), axis=-1, keepdims=True)
        dq, dk, dv = _mla_grads(jnp.exp(s - lse_ref[...]), q_ref, k_ref, v_ref, do, delta)
        rows = pl.ds(pl.multiple_of(qi * tq, tq), tq)
        dq_ref[rows, :] += dq
        dk_acc[...] += dk
        dv_acc[...] += dv

        @pl.when(qi == nq - 1)
        def _():
            dk_ref[...] = dk_acc[...]
            dv_ref[...] = dv_acc[...].astype(dv_ref.dtype)

    return pl.pallas_call(
        body, name="mla_bwd_lat", grid=(H, nk, nq),
        out_shape=(jax.ShapeDtypeStruct((T, H * QK_PAD), F32), jax.ShapeDtypeStruct((TA, H * QK_PAD), F32),
                   jax.ShapeDtypeStruct((TA, H * V_HEAD), BF16)),
        in_specs=[pl.BlockSpec((tq, QK_PAD), lambda h, k, i: (i, h)), pl.BlockSpec((tk, QK_PAD), lambda h, k, i: (k, h)),
                  pl.BlockSpec((tk, V_HEAD), lambda h, k, i: (k, 2 * h + 1)), pl.BlockSpec((tq, V_HEAD), lambda h, k, i: (i, h)),
                  pl.BlockSpec((tq, V_HEAD), lambda h, k, i: (i, do_cb0 + h)), pl.BlockSpec((None, tq, 1), lambda h, k, i: (h, i, 0))],
        out_specs=(pl.BlockSpec((T, QK_PAD), lambda h, k, i: (0, h)), pl.BlockSpec((tk, QK_PAD), lambda h, k, i: (k, h)),
                   pl.BlockSpec((tk, V_HEAD), lambda h, k, i: (k, h))),
        scratch_shapes=[pltpu.VMEM((tk, QK_PAD), F32), pltpu.VMEM((tk, V_HEAD), F32)],
        compiler_params=_cparams(("parallel", "arbitrary", "arbitrary")),
    )(q, kf, kv, o, dcat, lse)


def _mla_bwd_ctx(q, kf, kv, dcat, dk_lat, dv_lat, do_cb0, T):
    H, cb = MLA_HEADS, T // TR

    def body(q_ref, k_ref, v_ref, do_ref, dkl_ref, dvl_ref, dq_ref, dk_ref, dv_ref):
        p, l = _mla_ctx_probs(q_ref, k_ref)
        do = do_ref[...]
        o = jnp.dot(p.astype(BF16), v_ref[...], preferred_element_type=F32) / l
        dq, dk, dv = _mla_grads(p / l, q_ref, k_ref, v_ref, do, jnp.sum(do * o, axis=-1, keepdims=True))
        dq_ref[...] = dq
        dk_ref[...] = dkl_ref[...] + dk
        dv_ref[...] = (dvl_ref[...].astype(F32) + dv).astype(dv_ref.dtype)

    at_ctx = lambda w, f: pl.BlockSpec((TR, w), lambda h: (cb, f(h)))
    at_0 = lambda w: pl.BlockSpec((TR, w), lambda h: (0, h))
    return pl.pallas_call(
        body, name="mla_bwd_ctx", grid=(H,),
        out_shape=(jax.ShapeDtypeStruct((TR, H * QK_PAD), F32), jax.ShapeDtypeStruct((TR, H * QK_PAD), F32),
                   jax.ShapeDtypeStruct((TR, H * V_HEAD), BF16)),
        in_specs=[at_ctx(QK_PAD, lambda h: h), at_ctx(QK_PAD, lambda h: h), at_ctx(V_HEAD, lambda h: 2 * h + 1),
                  at_ctx(V_HEAD, lambda h: do_cb0 + h), at_ctx(QK_PAD, lambda h: h), at_ctx(V_HEAD, lambda h: h)],
        out_specs=(at_0(QK_PAD), at_0(QK_PAD), at_0(V_HEAD)), compiler_params=_cparams(("parallel",)),
    )(q, kf, kv, dcat, dk_lat, dv_lat)


def _gla_tile(st, q, k, v, g, rev, q_scale):
    nc = q.shape[0] // CHUNK
    ii = lax.broadcasted_iota(jnp.int32, (CHUNK, CHUNK), 0)
    jj = lax.broadcasted_iota(jnp.int32, (CHUNK, CHUNK), 1)
    tri = (jj >= ii) if rev else (jj <= ii)
    ones = tri.astype(F32)
    outs = [None] * nc
    for ci in (range(nc - 1, -1, -1) if rev else range(nc)):
        sl = slice(ci * CHUNK, (ci + 1) * CHUNK)
        qc, kc, vc, gc = q[sl] * q_scale, k[sl], v[sl].astype(BF16), g[sl]
        b = jnp.dot(ones, gc, precision=HIGHEST, preferred_element_type=F32)
        bl = b[0:1] if rev else b[CHUNK - 1:CHUNK]
        kd = (kc * jnp.exp(bl - b)).astype(BF16)
        qe = (qc * jnp.exp(b)).astype(BF16)
        ke = (kc * jnp.exp(-b)).astype(BF16)
        att = jnp.where(tri, lax.dot_general(qe, ke, NT_DIMS, preferred_element_type=F32), 0.0)
        outs[ci] = (jnp.dot(att.astype(BF16), vc, preferred_element_type=F32)
                    + lax.dot_general(qe, st.astype(BF16), NT_DIMS, preferred_element_type=F32))
        st = st * jnp.exp(bl) + lax.dot_general(vc, kd, TN_DIMS, preferred_element_type=F32)
    return st, jnp.concatenate(outs, axis=0)


def _gla_specs(dims, rev, nt, nlat):
    DK, DV, q_off, k_off, g_off = dims
    tile = (lambda p: nt - 1 - p) if rev else (lambda p: (p + nlat) % nt)
    return tile, [pl.BlockSpec((TR, DK), lambda h, p: (tile(p), q_off // DK + h)),
                  pl.BlockSpec((TR, DK), lambda h, p: (tile(p), k_off // DK + h)),
                  pl.BlockSpec((TR, DV), lambda h, p: (tile(p), h)),
                  pl.BlockSpec((TR, DK), lambda h, p: (tile(p), g_off // DK + h))]


def _gla_fwd(name, z1, g, dims, rev, nt, nlat):
    DK, DV = dims[0], dims[1]
    TA = nt * TR
    _, in_specs = _gla_specs(dims, rev, nt, nlat)
    tile = (lambda p: nt - 1 - p) if rev else (lambda p: (p + nlat) % nt)

    def body(q_ref, k_ref, v_ref, g_ref, o_ref, st_ref, st):
        @pl.when(pl.program_id(1) == 0)
        def _():
            st[...] = jnp.zeros_like(st)

        st_ref[...] = st[...]
        new, out = _gla_tile(st[...], q_ref[...], k_ref[...], v_ref[...], g_ref[...], rev, DK ** -0.5)
        st[...] = new
        o_ref[...] = out

    return pl.pallas_call(
        body, name=name, grid=(GLA_HEADS, nt),
        out_shape=(jax.ShapeDtypeStruct((TA, GLA_HEADS * DV), F32), jax.ShapeDtypeStruct((GLA_HEADS, nt, DV, DK), F32)),
        in_specs=in_specs,
        out_specs=(pl.BlockSpec((TR, DV), lambda h, p: (tile(p), h)), pl.BlockSpec((None, None, DV, DK), lambda h, p: (h, p, 0, 0))),
        scratch_shapes=[pltpu.VMEM((DV, DK), F32)], compiler_params=_cparams(("parallel", "arbitrary")),
    )(z1, z1, z1, g)


def _gla_bwd(name, z1, g, states, do, dims, rev, nt, nlat):
    DK, DV = dims[0], dims[1]
    TA = nt * TR
    fwd_tile = (lambda p: nt - 1 - p) if rev else (lambda p: (p + nlat) % nt)
    pos = lambda s: nt - 1 - s
    DKq, DVq, q_off, k_off, g_off = dims
    in_specs = [pl.BlockSpec((TR, DK), lambda h, s: (fwd_tile(pos(s)), q_off // DK + h)),
                pl.BlockSpec((TR, DK), lambda h, s: (fwd_tile(pos(s)), k_off // DK + h)),
                pl.BlockSpec((TR, DV), lambda h, s: (fwd_tile(pos(s)), h)),
                pl.BlockSpec((TR, DK), lambda h, s: (fwd_tile(pos(s)), g_off // DK + h)),
                pl.BlockSpec((None, None, DV, DK), lambda h, s: (h, pos(s), 0, 0)),
                pl.BlockSpec((TR, DV), lambda h, s: (fwd_tile(pos(s)), h))]
    row_k = pl.BlockSpec((TR, DK), lambda h, s: (fwd_tile(pos(s)), h))
    row_v = pl.BlockSpec((TR, DV), lambda h, s: (fwd_tile(pos(s)), h))

    def body(q_ref, k_ref, v_ref, g_ref, st_ref, do_ref, dq_ref, dk_ref, dv_ref, dg_ref, dst):
        @pl.when(pl.program_id(1) == 0)
        def _():
            dst[...] = jnp.zeros_like(dst)

        _, pull = jax.vjp(lambda st, q, k, v, gg: _gla_tile(st, q, k, v, gg, rev, DK ** -0.5),
                          st_ref[...], q_ref[...], k_ref[...], v_ref[...], g_ref[...])
        d_st, dq, dk, dv, dg = pull((dst[...], do_ref[...]))
        dst[...] = d_st
        dq_ref[...] = dq
        dk_ref[...] = dk
        dv_ref[...] = dv
        dg_ref[...] = dg

    KEY, VAL = GLA_HEADS * DK, GLA_HEADS * DV
    return pl.pallas_call(
        body, name=name, grid=(GLA_HEADS, nt),
        out_shape=(jax.ShapeDtypeStruct((TA, KEY), F32), jax.ShapeDtypeStruct((TA, KEY), F32),
                   jax.ShapeDtypeStruct((TA, VAL), F32), jax.ShapeDtypeStruct((TA, KEY), F32)),
        in_specs=in_specs, out_specs=(row_k, row_k, row_v, row_k),
        scratch_shapes=[pltpu.VMEM((DV, DK), F32)], compiler_params=_cparams(("parallel", "arbitrary")),
    )(z1, z1, z1, g, states, do)


def _sum_slots(name, arr):
    S, R, C = arr.shape
    tr = _tile(R, max(16, (2 ** 19) // max(C, 1) // 16 * 16), 16)

    def body(a_ref, o_ref):
        acc = a_ref[0].astype(F32)
        for s in range(1, S):
            acc = acc + a_ref[s].astype(F32)
        o_ref[...] = acc

    return pl.pallas_call(
        body, name=name, grid=(R // tr,), out_shape=jax.ShapeDtypeStruct((R, C), F32),
        in_specs=[pl.BlockSpec((S, tr, C), lambda i: (0, i, 0))], out_specs=pl.BlockSpec((tr, C), lambda i: (i, 0)),
        compiler_params=_cparams(("parallel",)),
    )(arr)


def _sum_pair(name, a, b, out_dtype):
    shape, C = a.shape, a.shape[-1]
    a2, b2 = a.reshape(-1, C), b.reshape(-1, C)
    R = a2.shape[0]
    tr = _tile(R, max(16, (2 ** 19) // max(C, 1) // 16 * 16), 16)
    fn = lambda i, p, q: (p.astype(F32) + q.astype(F32),)
    (res,) = _rowwise(name, fn, R // tr, tr, [("row", a2, 0, C), ("row", b2, 0, C)], [("row", R, C, out_dtype)])
    return res.reshape(shape)


def _adam_math(w, g, m, v):
    m2 = ADAM_B1 * m + (1.0 - ADAM_B1) * g
    v2 = ADAM_B2 * v + (1.0 - ADAM_B2) * (g * g)
    m_hat = m2 / (1.0 - ADAM_B1 ** ADAM_STEP)
    v_hat = v2 / (1.0 - ADAM_B2 ** ADAM_STEP)
    return -ADAM_LR * (m_hat / (jnp.sqrt(v_hat) + ADAM_EPS) + ADAM_WD * w), m2, v2


def _adam(name, w, gs, m, v):
    R, C = w.shape
    tr = _tile(R, max(8, (2 ** 19) // max(C, 1) // 8 * 8), 8)
    ng = len(gs)

    def fn(i, w_, *rest):
        g = rest[0] if ng == 1 else rest[0] + rest[1]
        d, m2, v2 = _adam_math(w_, g, rest[ng], rest[ng + 1])
        return g, d, m2, v2

    ins = [("row", a, 0, C) for a in (w, *gs, m, v)]
    return _rowwise(name, fn, R // tr, tr, ins, [("row", R, C, F32)] * 4)


def _ffn_fwd(tag, h, w_up, cw, cb, w_down, n_rows, T, TA, F):
    tr = 128
    u = _matmul(tag + "_up", h, w_up, "nn", n_rows, 2 * F, h.shape[1])
    cwg, cwv, cbg, cbv = cw[:, :F], cw[:, F:], cb[:, :F], cb[:, F:]

    def fn(i, ug, uv, pg, pv, ng, nv, wg, wv, bg, bv):
        prev, nxt = _shifters(i, tr, T, TA)
        cg = wg[0:1] * prev(ug, pg) + wg[1:2] * ug + wg[2:3] * nxt(ug, ng) + bg
        cv = wv[0:1] * prev(uv, pv) + wv[1:2] * uv + wv[2:3] * nxt(uv, nv) + bv
        return (_silu(cg) * cv,)

    ins = [("row", u, 0, F), ("row", u, 1, F), ("prev", u, 0, F), ("prev", u, 1, F), ("next", u, 0, F), ("next", u, 1, F),
           ("full", cwg), ("full", cwv), ("full", cbg), ("full", cbv)]
    (act,) = _rowwise(tag + "_conv", fn, n_rows // tr, tr, ins, [("row", n_rows, F, BF16)])
    f = _matmul(tag + "_down", act, w_down, "nn", n_rows, w_down.shape[1], F)
    return u, act, f


def _ffn_bwd(tag, df, h, u, act, w_up, cw, cb, w_down, n_rows, T, TA, F):
    tr = 128
    D = w_down.shape[1]
    dact = _matmul(tag + "_ddown_x", df, w_down, "nt", n_rows, F, D)
    g_down = _matmul(tag + "_ddown_w", act, df, "tn", F, D, n_rows, out_dtype=BF16)
    cwg, cwv, cbg, cbv = cw[:, :F], cw[:, F:], cb[:, :F], cb[:, F:]

    def fn_a(i, ug, uv, pg, pv, ng, nv, da, wg, wv, bg, bv):
        prev, nxt = _shifters(i, tr, T, TA)
        ugp, ugn, uvp, uvn = prev(ug, pg), nxt(ug, ng), prev(uv, pv), nxt(uv, nv)
        cg = wg[0:1] * ugp + wg[1:2] * ug + wg[2:3] * ugn + bg
        cv = wv[0:1] * uvp + wv[1:2] * uv + wv[2:3] * uvn + bv
        sg = jax.nn.sigmoid(cg)
        dcv = da * (cg * sg)
        dcg = da * cv * (sg * (1.0 + cg * (1.0 - sg)))
        return (jnp.concatenate([dcg, dcv], axis=1),
                _colsum(ugp * dcg), _colsum(ug * dcg), _colsum(ugn * dcg),
                _colsum(uvp * dcv), _colsum(uv * dcv), _colsum(uvn * dcv), _colsum(dcg), _colsum(dcv))

    ins = [("row", u, 0, F), ("row", u, 1, F), ("prev", u, 0, F), ("prev", u, 1, F), ("next", u, 0, F), ("next", u, 1, F),
           ("row", dact, 0, F), ("full", cwg), ("full", cwv), ("full", cbg), ("full", cbv)]
    res = _rowwise(tag + "_dconv_a", fn_a, n_rows // tr, tr, ins, [("row", n_rows, 2 * F, F32)] + [("acc", 1, F)] * 8)
    duc = res[0]
    g_cw = jnp.concatenate([jnp.concatenate(res[1:4], axis=0), jnp.concatenate(res[4:7], axis=0)], axis=1)
    g_cb = jnp.concatenate([res[7], res[8]], axis=1)

    def fn_b(i, dg, dv, pg, pv, ng, nv, wg, wv):
        prev, nxt = _shifters(i, tr, T, TA)
        dug = wg[0:1] * nxt(dg, ng) + wg[1:2] * dg + wg[2:3] * prev(dg, pg)
        duv = wv[0:1] * nxt(dv, nv) + wv[1:2] * dv + wv[2:3] * prev(dv, pv)
        return (jnp.concatenate([dug, duv], axis=1),)

    ins = [("row", duc, 0, F), ("row", duc, 1, F), ("prev", duc, 0, F), ("prev", duc, 1, F), ("next", duc, 0, F),
           ("next", duc, 1, F), ("full", cwg), ("full", cwv)]
    (du,) = _rowwise(tag + "_dconv_b", fn_b, n_rows // tr, tr, ins, [("row", n_rows, 2 * F, BF16)])
    dh = _matmul(tag + "_dup_x", du, w_up, "nt", n_rows, D, 2 * F)
    g_up = _matmul(tag + "_dup_w", h, du, "tn", D, 2 * F, n_rows, out_dtype=BF16)
    return dh, g_up, g_cw, g_cb, g_down


def _cols_to_shards(g):
    r, c = g.shape
    return g.reshape(r, 4, c // 4).transpose(1, 0, 2)


def _shards_to_cols(s):
    return s.transpose(1, 0, 2).reshape(s.shape[1], 4 * s.shape[2])


def kernel(x, c, ctx, c_ctx, l0_ada_w, l0_ada_b, l0_norm1, l0_w_in, l0_conv_a, l0_q_norm, l0_w_qb, l0_kv_norm, l0_w_kvb, l0_w_out, l0_norm2, l0_ffn_up, l0_ffn_conv_w, l0_ffn_conv_b, l0_ffn_down, l1_ada_w, l1_ada_b, l1_norm1, l1_w_in, l1_gate_fw_w, l1_gate_fw_b, l1_gate_bw_w, l1_gate_bw_b, l1_o_norm, l1_w_out, l1_norm2, l1_ffn_up, l1_ffn_conv_w, l1_ffn_conv_b, l1_ffn_down, final_norm, loss_target, m_c_ctx, m_l0_ada_w, m_l0_ada_b, m_l0_norm1, m_l0_w_in, m_l0_conv_a, m_l0_q_norm, m_l0_w_qb, m_l0_kv_norm, m_l0_w_kvb, m_l0_w_out, m_l0_norm2, m_l0_ffn_up, m_l0_ffn_conv_w, m_l0_ffn_conv_b, m_l0_ffn_down, m_l1_ada_w, m_l1_ada_b, m_l1_norm1, m_l1_w_in, m_l1_gate_fw_w, m_l1_gate_fw_b, m_l1_gate_bw_w, m_l1_gate_bw_b, m_l1_o_norm, m_l1_w_out, m_l1_norm2, m_l1_ffn_up, m_l1_ffn_conv_w, m_l1_ffn_conv_b, m_l1_ffn_down, m_final_norm, v_c_ctx, v_l0_ada_w, v_l0_ada_b, v_l0_norm1, v_l0_w_in, v_l0_conv_a, v_l0_q_norm, v_l0_w_qb, v_l0_kv_norm, v_l0_w_kvb, v_l0_w_out, v_l0_norm2, v_l0_ffn_up, v_l0_ffn_conv_w, v_l0_ffn_conv_b, v_l0_ffn_down, v_l1_ada_w, v_l1_ada_b, v_l1_norm1, v_l1_w_in, v_l1_gate_fw_w, v_l1_gate_fw_b, v_l1_gate_bw_w, v_l1_gate_bw_b, v_l1_o_norm, v_l1_w_out, v_l1_norm2, v_l1_ffn_up, v_l1_ffn_conv_w, v_l1_ffn_conv_b, v_l1_ffn_down, v_final_norm):
    W = dict(c_ctx=c_ctx, l0_ada_w=l0_ada_w, l0_ada_b=l0_ada_b, l0_norm1=l0_norm1, l0_w_in=l0_w_in, l0_conv_a=l0_conv_a, l0_q_norm=l0_q_norm, l0_w_qb=l0_w_qb, l0_kv_norm=l0_kv_norm, l0_w_kvb=l0_w_kvb, l0_w_out=l0_w_out, l0_norm2=l0_norm2, l0_ffn_up=l0_ffn_up, l0_ffn_conv_w=l0_ffn_conv_w, l0_ffn_conv_b=l0_ffn_conv_b, l0_ffn_down=l0_ffn_down, l1_ada_w=l1_ada_w, l1_ada_b=l1_ada_b, l1_norm1=l1_norm1, l1_w_in=l1_w_in, l1_gate_fw_w=l1_gate_fw_w, l1_gate_fw_b=l1_gate_fw_b, l1_gate_bw_w=l1_gate_bw_w, l1_gate_bw_b=l1_gate_bw_b, l1_o_norm=l1_o_norm, l1_w_out=l1_w_out, l1_norm2=l1_norm2, l1_ffn_up=l1_ffn_up, l1_ffn_conv_w=l1_ffn_conv_w, l1_ffn_conv_b=l1_ffn_conv_b, l1_ffn_down=l1_ffn_down, final_norm=final_norm)
    MO = dict(c_ctx=m_c_ctx, l0_ada_w=m_l0_ada_w, l0_ada_b=m_l0_ada_b, l0_norm1=m_l0_norm1, l0_w_in=m_l0_w_in, l0_conv_a=m_l0_conv_a, l0_q_norm=m_l0_q_norm, l0_w_qb=m_l0_w_qb, l0_kv_norm=m_l0_kv_norm, l0_w_kvb=m_l0_w_kvb, l0_w_out=m_l0_w_out, l0_norm2=m_l0_norm2, l0_ffn_up=m_l0_ffn_up, l0_ffn_conv_w=m_l0_ffn_conv_w, l0_ffn_conv_b=m_l0_ffn_conv_b, l0_ffn_down=m_l0_ffn_down, l1_ada_w=m_l1_ada_w, l1_ada_b=m_l1_ada_b, l1_norm1=m_l1_norm1, l1_w_in=m_l1_w_in, l1_gate_fw_w=m_l1_gate_fw_w, l1_gate_fw_b=m_l1_gate_fw_b, l1_gate_bw_w=m_l1_gate_bw_w, l1_gate_bw_b=m_l1_gate_bw_b, l1_o_norm=m_l1_o_norm, l1_w_out=m_l1_w_out, l1_norm2=m_l1_norm2, l1_ffn_up=m_l1_ffn_up, l1_ffn_conv_w=m_l1_ffn_conv_w, l1_ffn_conv_b=m_l1_ffn_conv_b, l1_ffn_down=m_l1_ffn_down, final_norm=m_final_norm)
    VO = dict(c_ctx=v_c_ctx, l0_ada_w=v_l0_ada_w, l0_ada_b=v_l0_ada_b, l0_norm1=v_l0_norm1, l0_w_in=v_l0_w_in, l0_conv_a=v_l0_conv_a, l0_q_norm=v_l0_q_norm, l0_w_qb=v_l0_w_qb, l0_kv_norm=v_l0_kv_norm, l0_w_kvb=v_l0_w_kvb, l0_w_out=v_l0_w_out, l0_norm2=v_l0_norm2, l0_ffn_up=v_l0_ffn_up, l0_ffn_conv_w=v_l0_ffn_conv_w, l0_ffn_conv_b=v_l0_ffn_conv_b, l0_ffn_down=v_l0_ffn_down, l1_ada_w=v_l1_ada_w, l1_ada_b=v_l1_ada_b, l1_norm1=v_l1_norm1, l1_w_in=v_l1_w_in, l1_gate_fw_w=v_l1_gate_fw_w, l1_gate_fw_b=v_l1_gate_fw_b, l1_gate_bw_w=v_l1_gate_bw_w, l1_gate_bw_b=v_l1_gate_bw_b, l1_o_norm=v_l1_o_norm, l1_w_out=v_l1_w_out, l1_norm2=v_l1_norm2, l1_ffn_up=v_l1_ffn_up, l1_ffn_conv_w=v_l1_ffn_conv_w, l1_ffn_conv_b=v_l1_ffn_conv_b, l1_ffn_down=v_l1_ffn_down, final_norm=v_final_norm)
    names = list(W)

    T, D = x.shape[1], x.shape[2]
    TC = ctx.shape[1]
    assert TC == TR and T % TR == 0
    TA = T + TC
    nt, nlat = TA // TR, T // TR
    CC = D // 2
    F = l0_ffn_down.shape[0] * 4
    DK, DV = D // 2 // GLA_HEADS, D // GLA_HEADS
    KEY, VAL = GLA_HEADS * DK, GLA_HEADS * DV
    ADA_S = l0_ada_w.shape[1]
    H = MLA_HEADS
    my_x, my_y, my_c = lax.axis_index("x"), lax.axis_index("y"), lax.axis_index("c")
    chip, dev = 2 * my_x + my_y, 4 * my_x + 2 * my_y + my_c
    row = lambda a: a.reshape(1, -1)

    big = ["l0_w_in", "l0_w_qb", "l0_w_kvb", "l0_w_out", "l0_ffn_up", "l0_ffn_down",
           "l1_w_in", "l1_w_out", "l1_ffn_up", "l1_ffn_down"]
    small_sh = ["l0_conv_a", "l0_ffn_conv_w", "l1_gate_fw_w", "l1_gate_bw_w", "l1_ffn_conv_w"]
    own_half = lambda a, k: lax.dynamic_index_in_dim(a, k, 0, keepdims=False)
    set_own = lambda arr, val, k: lax.dynamic_update_index_in_dim(arr, val[None], k, 0)
    in_core_order = lambda mine, other: jnp.where(my_c == 0, jnp.stack([mine, other], 1), jnp.stack([other, mine], 1))
    w_half = [own_half(W[n].astype(BF16).reshape(2, W[n].shape[0] // 2, -1), my_c) for n in big]
    got = _exchange("gather_weights", w_half, "ag_xy", fill_own=False)
    got = [set_own(g, h, chip) for g, h in zip(got, w_half)]
    other = _exchange("pair_weights", got, "swap_c")
    G = {n: in_core_order(g, o).reshape(4, 2 * g.shape[1], g.shape[2]) for n, g, o in zip(big, got, other)}
    G.update(zip(small_sh, _exchange("gather_small_weights", [W[n] for n in small_sh], "ag_xy")))
    (c_all,) = _exchange("gather_c", [c], "ag_all")
    col_sh = {"l0_w_in", "l0_w_qb", "l0_w_kvb", "l0_ffn_up", "l1_w_in", "l1_ffn_up", "l0_conv_a", "l0_ffn_conv_w",
              "l1_gate_fw_w", "l1_gate_bw_w", "l1_ffn_conv_w"}
    full = {n: (_shards_to_cols(G[n]) if n in col_sh else G[n].reshape(-1, G[n].shape[-1])) for n in G}

    o_kv, o_kr, o_ql, o_ax = 0, KV_LORA, KV_LORA + QK_ROPE, KV_LORA + QK_ROPE + Q_LORA
    n_sel = 3 * CC + Q_LORA + KV_LORA + QK_ROPE
    W0 = -(-(n_sel + QK_ROPE) // 256) * 256
    QOFF0, KVOFF0, KROFF0 = 3 * CC, 3 * CC + Q_LORA, 3 * CC + Q_LORA + KV_LORA
    w_in0 = full["l0_w_in"]
    w_kr = w_in0[:, o_kr:o_kr + QK_ROPE]
    w_in0p = jnp.concatenate([w_in0[:, o_ax:], w_in0[:, o_ql:o_ax], w_in0[:, :o_kr], w_kr, _rot_cols(w_kr),
                              jnp.zeros((D, W0 - n_sel - QK_ROPE), BF16)], axis=1)
    wq3 = full["l0_w_qb"].reshape(Q_LORA, H, QK_NOPE + QK_ROPE)
    w_qbp = jnp.concatenate([wq3, _rot_cols(wq3[:, :, QK_NOPE:])], axis=2).reshape(Q_LORA, H * QK_PAD)
    o_v1, o_lr1, o_q1, o_og1 = KEY, KEY + VAL, KEY + VAL + 2 * GATE_RANK, 2 * KEY + VAL + 2 * GATE_RANK
    W1 = 2 * VAL + 2 * KEY + LANES
    OGOFF, KOFF1, QOFF1, LROFF = VAL, 2 * VAL, 2 * VAL + KEY, 2 * VAL + 2 * KEY
    w_in1 = full["l1_w_in"]
    w_in1p = jnp.concatenate([w_in1[:, o_v1:o_lr1], w_in1[:, o_og1:], w_in1[:, :o_v1], w_in1[:, o_q1:o_og1],
                              w_in1[:, o_lr1:o_q1], jnp.zeros((D, LANES - 2 * GATE_RANK), BF16)], axis=1)
    w_gate = jnp.zeros((LANES, 2 * KEY), F32)
    w_gate = w_gate.at[:GATE_RANK, :KEY].set(full["l1_gate_fw_w"]).at[GATE_RANK:2 * GATE_RANK, KEY:].set(full["l1_gate_bw_w"])
    b_gate = jnp.concatenate([l1_gate_fw_b, l1_gate_bw_b]).reshape(1, -1)

    pos = np.arange(T)
    inv = ROPE_THETA ** (-np.arange(0, QK_ROPE // 2, 2, dtype=np.float32) / (QK_ROPE // 2))
    ar, ac = (pos // GRID_W).astype(np.float32)[:, None] * inv, (pos % GRID_W).astype(np.float32)[:, None] * inv
    ang = jnp.asarray(np.concatenate([ar, ar, ac, ac], axis=-1).astype(np.float32))
    zpad = jnp.zeros((TA, LANES - QK_ROPE), F32)
    cos_t = jnp.concatenate([jnp.concatenate([jnp.cos(ang), jnp.ones((TC, QK_ROPE), F32)], axis=0), zpad], axis=1)
    sin_t = jnp.concatenate([jnp.concatenate([jnp.sin(ang), jnp.zeros((TC, QK_ROPE), F32)], axis=0), zpad], axis=1)

    c16 = jnp.concatenate([c_all.reshape(8, D), c_ctx.reshape(1, D), jnp.zeros((7, D), F32)], axis=0)
    (act16,) = _rowwise("silu_c", _f_silu, 1, 16, [("row", c16, 0, D)], [("row", 16, D, F32)])
    mod_parts = [_matmul("ada_fwd", act16, W["l%d_ada_w" % l], "nn", 16, ADA_S, D, tn_cap=512) for l in (0, 1)]
    mod_g = _exchange("gather_mod", mod_parts, "ag_xy")
    mods = []
    for l in (0, 1):
        mfull = _shards_to_cols(mod_g[l]) + W["l%d_ada_b" % l][None, :]
        mine = lax.dynamic_slice_in_dim(mfull, dev, 1, axis=0)
        mods.append(jnp.concatenate([mine, mfull[8:9]], axis=0).reshape(2, N_MOD, D))
    P = lambda l, k: mods[l][:, k:k + 1, :]

    X = jnp.concatenate([x[0], ctx[0]], axis=0)
    n1_0, n2_0, n1_1, n2_1 = row(l0_norm1), row(l0_norm2), row(l1_norm1), row(l1_norm2)
    (h0,) = _rowwise("l0_mod1", _f_mod, nt, TR, [("row", X, 0, D), ("full", n1_0), ("grp", P(0, 0)), ("grp", P(0, 1))],
                     [("row", TA, D, BF16)], nlat)
    z0 = _matmul("l0_in", h0, w_in0p, "nn", TA, W0, D)
    kvn_w, qn_w = row(l0_kv_norm), row(l0_q_norm)
    norm_ins = [("row", z0, KVOFF0 // KV_LORA, KV_LORA), ("row", z0, QOFF0 // Q_LORA, Q_LORA), ("full", kvn_w), ("full", qn_w)]
    kvn, qn = _rowwise("l0_latnorm", _f_norms, nt, TR, norm_ins, [("row", TA, KV_LORA, BF16), ("row", TA, Q_LORA, BF16)])
    kv = _matmul("l0_kvb", kvn, full["l0_w_kvb"], "nn", TA, H * QK_PAD, KV_LORA, out_dtype=BF16)
    qraw = _matmul("l0_qb", qn, w_qbp, "nn", TA, H * QK_PAD, Q_LORA)

    def f_rope(i, qr, kvv, krr, cs, sn):
        kr = _rope128(krr, cs, sn).astype(BF16)
        qs, ks = [], []
        for h in range(H):
            qs += [qr[:, h * QK_PAD:h * QK_PAD + QK_NOPE], _rope128(qr[:, h * QK_PAD + QK_NOPE:(h + 1) * QK_PAD], cs, sn)]
            ks += [kvv[:, h * QK_PAD:h * QK_PAD + QK_NOPE], kr]
        return jnp.concatenate(qs, axis=1), jnp.concatenate(ks, axis=1)

    rope_ins = [("row", qraw, 0, H * QK_PAD), ("row", kv, 0, H * QK_PAD), ("row", z0, KROFF0 // LANES, LANES),
                ("row", cos_t, 0, LANES), ("row", sin_t, 0, LANES)]
    q, kf = _rowwise("l0_rope", f_rope, nt, TR, rope_ins, [("row", TA, H * QK_PAD, BF16), ("row", TA, H * QK_PAD, BF16)])
    o_lat, lse = _mla_fwd_lat(q, kf, kv, T, TA)
    o_ctx = _mla_fwd_ctx(q, kf, kv, T)

    conv_a = full["l0_conv_a"]

    def f_conva(i, ax, ab, ac_, pax, pac, nax, nac, w):
        prev, nxt = _shifters(i, TR, T, TA)
        p = ac_ * ax
        return (ab * (w[0:1] * prev(p, pac * pax) + w[1:2] * p + w[2:3] * nxt(p, nac * nax)),)

    conva_ins = [("row", z0, 0, CC), ("row", z0, 1, CC), ("row", z0, 2, CC), ("prev", z0, 0, CC), ("prev", z0, 2, CC),
                 ("next", z0, 0, CC), ("next", z0, 2, CC), ("full", conv_a)]
    (y_a,) = _rowwise("l0_conva", f_conva, nt, TR, conva_ins, [("row", TA, CC, BF16)])
    cat0 = jnp.concatenate([y_a, jnp.concatenate([o_lat, o_ctx], axis=0)], axis=1)
    y0 = _matmul("l0_out", cat0, full["l0_w_out"], "nn", TA, D, CC + H * V_HEAD)
    resmod_outs = [("row", TA, D, F32), ("row", TA, D, BF16)]
    X1, h2_0 = _rowwise("l0_resmod2", _f_resmod, nt, TR,
                        [("row", X, 0, D), ("row", y0, 0, D), ("grp", P(0, 2)), ("full", n2_0), ("grp", P(0, 3)), ("grp", P(0, 4))],
                        resmod_outs, nlat)
    cw0, cb0 = full["l0_ffn_conv_w"], row(l0_ffn_conv_b)
    u0, act0, f0 = _ffn_fwd("l0_ffn", h2_0, full["l0_ffn_up"], cw0, cb0, full["l0_ffn_down"], TA, T, TA, F)
    X2, h1 = _rowwise("l1_resmod1", _f_resmod, nt, TR,
                      [("row", X1, 0, D), ("row", f0, 0, D), ("grp", P(0, 5)), ("full", n1_1), ("grp", P(1, 0)), ("grp", P(1, 1))],
                      resmod_outs, nlat)

    z1 = _matmul("l1_in", h1, w_in1p, "nn", TA, W1, D)

    def f_gates(i, lr, wg, bg):
        pre = jnp.dot(lr.astype(BF16), wg.astype(BF16), preferred_element_type=F32) + bg
        return (_log_sigmoid(pre) / GATE_NORMALIZER,)

    gate_ins = [("row", z1, LROFF // LANES, LANES), ("full", w_gate), ("full", b_gate)]
    (gates,) = _rowwise("l1_gates", f_gates, nt, TR, gate_ins, [("row", TA, 2 * KEY, F32)])
    dims_f, dims_b = (DK, DV, QOFF1, KOFF1, 0), (DK, DV, QOFF1, KOFF1, KEY)
    o_f, st_f = _gla_fwd("gla_fwd_f", z1, gates, dims_f, False, nt, nlat)
    o_b, st_b = _gla_fwd("gla_fwd_b", z1, gates, dims_b, True, nt, nlat)
    onw = row(l1_o_norm)

    def f_glaout(i, of, ob, og, w):
        o = of + ob
        on = jnp.concatenate([_rms(o[:, h * DV:(h + 1) * DV], w) for h in range(GLA_HEADS)], axis=1)
        return (on * _silu(og),)

    glaout_ins = [("row", o_f, 0, VAL), ("row", o_b, 0, VAL), ("row", z1, OGOFF // VAL, VAL), ("full", onw)]
    (go,) = _rowwise("l1_glaout", f_glaout, nlat, TR, glaout_ins, [("row", T, VAL, BF16)])
    y1 = _matmul("l1_out", go, full["l1_w_out"], "nn", T, D, VAL)
    X3, h2_1 = _rowwise("l1_resmod2", _f_resmod, nlat, TR,
                        [("row", X2, 0, D), ("row", y1, 0, D), ("grp", P(1, 2)), ("full", n2_1), ("grp", P(1, 3)), ("grp", P(1, 4))],
                        [("row", T, D, F32), ("row", T, D, BF16)])
    cw1, cb1 = full["l1_ffn_conv_w"], row(l1_ffn_conv_b)
    u1, act1, f1 = _ffn_fwd("l1_ffn", h2_1, full["l1_ffn_up"], cw1, cb1, full["l1_ffn_down"], T, T, T, F)

    fnw = row(final_norm)

    def f_head(i, x3, ff, g5, w, tgt):
        fin = lambda a, b, g_, w_: _rms(a + g_ * b, w_)
        y, pull = jax.vjp(fin, x3, ff, g5, w)
        err = y - tgt
        dx3, dff, dg5, dw = pull(err / D)
        loss = 0.5 * jnp.sum(jnp.mean(err * err, axis=-1, keepdims=True), axis=0, keepdims=True)
        return jnp.broadcast_to(loss, (1, LANES)), dx3, dff, dg5, dw

    head_ins = [("row", X3, 0, D), ("row", f1, 0, D), ("grp", P(1, 5)), ("full", fnw), ("row", loss_target[0], 0, D)]
    loss_acc, dX3, df1, dm5_1, g_final = _rowwise(
        "loss_head", f_head, nlat, TR, head_ins,
        [("acc", 1, LANES), ("row", T, D, F32), ("row", T, D, BF16), ("acc", 1, D), ("acc", 1, D)])
    loss = lax.psum(loss_acc[0, 0], ("x", "y", "c"))

    grads = {"final_norm": g_final}
    dh2_1, grads["l1_ffn_up"], grads["l1_ffn_conv_w"], grads["l1_ffn_conv_b"], grads["l1_ffn_down"] = _ffn_bwd(
        "l1_dffn", df1, h2_1, u1, act1, full["l1_ffn_up"], cw1, cb1, full["l1_ffn_down"], T, T, T, F)
    resmod_bwd = _vjp_of(_f_resmod, 6, (0, 1, 2, 3, 4, 5))
    dX2l, dy1, dm2_1, grads["l1_norm2"], dm3_1, dm4_1 = _rowwise(
        "l1_dresmod2", resmod_bwd, nlat, TR,
        [("row", X2, 0, D), ("row", y1, 0, D), ("grp", P(1, 2)), ("full", n2_1), ("grp", P(1, 3)), ("grp", P(1, 4)),
         ("row", dX3, 0, D), ("row", dh2_1, 0, D)],
        [("row", T, D, F32), ("row", T, D, BF16)] + [("acc", 1, D)] * 4)
    dgo = _matmul("l1_dout_x", dy1, full["l1_w_out"], "nt", T, VAL, D)
    grads["l1_w_out"] = _matmul("l1_dout_w", go, dy1, "tn", VAL, D, T, out_dtype=BF16)

    def f_glaout_bwd(i, of, ob, og, w, d):
        is_ctx = i >= nlat
        _, pull = jax.vjp(lambda a, b, c_, w_: f_glaout(i, a, b, c_, w_)[0], of, ob, og, w)
        dof, _, dog, dw = pull(jnp.where(is_ctx, 0.0, d))
        return dof, dog, dw

    glaout_b_ins = glaout_ins + [("rowc", dgo, 0, VAL, nlat - 1)]
    do_gla, dog, g_onorm = _rowwise("l1_dglaout", f_glaout_bwd, nt, TR, glaout_b_ins,
                                    [("row", TA, VAL, F32), ("row", TA, VAL, F32), ("acc", 1, DV)])
    grads["l1_o_norm"] = g_onorm
    dq_f, dk_f, dv_f, dg_f = _gla_bwd("gla_bwd_f", z1, gates, st_f, do_gla, dims_f, False, nt, nlat)
    dq_b, dk_b, dv_b, dg_b = _gla_bwd("gla_bwd_b", z1, gates, st_b, do_gla, dims_b, True, nt, nlat)

    def f_dz1(i, dvf, dvb, dog_, dkf, dkb, dqf, dqb, dgf, dgb, lr, wg, bg):
        _, pull = jax.vjp(lambda a, b, c_: f_gates(i, a, b, c_)[0], lr, wg, bg)
        dlr, dwg, dbg = pull(jnp.concatenate([dgf, dgb], axis=1))
        return jnp.concatenate([dvf + dvb, dog_, dkf + dkb, dqf + dqb, dlr], axis=1), dwg, dbg

    dz1_ins = [("row", dv_f, 0, VAL), ("row", dv_b, 0, VAL), ("row", dog, 0, VAL), ("row", dk_f, 0, KEY), ("row", dk_b, 0, KEY),
               ("row", dq_f, 0, KEY), ("row", dq_b, 0, KEY), ("row", dg_f, 0, KEY), ("row", dg_b, 0, KEY)] + gate_ins
    dz1, g_wgate, g_bgate = _rowwise("l1_dz", f_dz1, nt, TR, dz1_ins, [("row", TA, W1, BF16), ("acc", LANES, 2 * KEY), ("acc", 1, 2 * KEY)])
    grads["l1_gate_fw_w"], grads["l1_gate_bw_w"] = g_wgate[:GATE_RANK, :KEY], g_wgate[GATE_RANK:2 * GATE_RANK, KEY:]
    grads["l1_gate_fw_b"], grads["l1_gate_bw_b"] = g_bgate[:, :KEY], g_bgate[:, KEY:]
    dh1 = _matmul("l1_din_x", dz1, w_in1p, "nt", TA, D, W1)
    g1p = _matmul("l1_din_w", h1, dz1, "tn", D, W1, TA, out_dtype=BF16)
    grads["l1_w_in"] = jnp.concatenate([g1p[:, KOFF1:QOFF1], g1p[:, :OGOFF], g1p[:, LROFF:LROFF + 2 * GATE_RANK],
                                        g1p[:, QOFF1:LROFF], g1p[:, OGOFF:KOFF1]], axis=1)

    def f_resmod1_bwd(i, x_, y_, g_, w_, sh, sc, dx2, dh):
        return resmod_bwd(i, x_, y_, g_, w_, sh, sc, jnp.where(i >= nlat, 0.0, dx2), dh)

    dX1, df0, dm5_0, grads["l1_norm1"], dm0_1, dm1_1 = _rowwise(
        "l1_dresmod1", f_resmod1_bwd, nt, TR,
        [("row", X1, 0, D), ("row", f0, 0, D), ("grp", P(0, 5)), ("full", n1_1), ("grp", P(1, 0)), ("grp", P(1, 1)),
         ("rowc", dX2l, 0, D, nlat - 1), ("row", dh1, 0, D)],
        [("row", TA, D, F32), ("row", TA, D, BF16), ("accg", 1, D), ("acc", 1, D), ("accg", 1, D), ("accg", 1, D)], nlat)

    dh2_0, grads["l0_ffn_up"], grads["l0_ffn_conv_w"], grads["l0_ffn_conv_b"], grads["l0_ffn_down"] = _ffn_bwd(
        "l0_dffn", df0, h2_0, u0, act0, full["l0_ffn_up"], cw0, cb0, full["l0_ffn_down"], TA, T, TA, F)
    dXd, dy0, dm2_0, grads["l0_norm2"], dm3_0, dm4_0 = _rowwise(
        "l0_dresmod2", resmod_bwd, nt, TR,
        [("row", X, 0, D), ("row", y0, 0, D), ("grp", P(0, 2)), ("full", n2_0), ("grp", P(0, 3)), ("grp", P(0, 4)),
         ("row", dX1, 0, D), ("row", dh2_0, 0, D)],
        [("row", TA, D, F32), ("row", TA, D, BF16), ("accg", 1, D), ("acc", 1, D), ("accg", 1, D), ("accg", 1, D)], nlat)
    dcat = _matmul("l0_dout_x", dy0, full["l0_w_out"], "nt", TA, CC + H * V_HEAD, D)
    grads["l0_w_out"] = _matmul("l0_dout_w", cat0, dy0, "tn", CC + H * V_HEAD, D, TA, out_dtype=BF16)

    def f_conva_bwd(i, ax, ab, ac_, pax, pab, pac, nax, nab, nac, dy, pdy, ndy, w):
        prev, nxt = _shifters(i, TR, T, TA)
        p = ac_ * ax
        pp, pn = prev(p, pac * pax), nxt(p, nac * nax)
        cv = w[0:1] * pp + w[1:2] * p + w[2:3] * pn
        dcv = dy * ab
        dp = w[0:1] * nxt(dcv, ndy * nab) + w[1:2] * dcv + w[2:3] * prev(dcv, pdy * pab)
        return (jnp.concatenate([dp * ac_, dy * cv, dp * ax], axis=1), _colsum(pp * dcv), _colsum(p * dcv), _colsum(pn * dcv))

    conva_b_ins = [("row", z0, 0, CC), ("row", z0, 1, CC), ("row", z0, 2, CC),
                   ("prev", z0, 0, CC), ("prev", z0, 1, CC), ("prev", z0, 2, CC),
                   ("next", z0, 0, CC), ("next", z0, 1, CC), ("next", z0, 2, CC),
                   ("row", dcat, 0, CC), ("prev", dcat, 0, CC), ("next", dcat, 0, CC), ("full", conv_a)]
    dz_a, ga0, ga1, ga2 = _rowwise("l0_dconva", f_conva_bwd, nt, TR, conva_b_ins, [("row", TA, 3 * CC, BF16)] + [("acc", 1, CC)] * 3)
    grads["l0_conv_a"] = jnp.concatenate([ga0, ga1, ga2], axis=0)
    do_cb0 = CC // V_HEAD
    dq_lat, dk_lat, dv_lat = _mla_bwd_lat(q, kf, kv, o_lat, dcat, lse, do_cb0, T, TA)
    dq_ctx, dk_ctx, dv_ctx = _mla_bwd_ctx(q, kf, kv, dcat, dk_lat, dv_lat, do_cb0, T)

    def f_rope_bwd(i, dql, dqc, dkl, dkc, dvl, dvc, cs, sn):
        is_ctx = i >= nlat
        dq_, dk_, dv_ = jnp.where(is_ctx, dqc, dql), jnp.where(is_ctx, dkc, dkl), jnp.where(is_ctx, dvc, dvl)
        dqs, dkvs, dkr = [], [], None
        for h in range(H):
            dqs += [dq_[:, h * QK_PAD:h * QK_PAD + QK_NOPE], _rope128_t(dq_[:, h * QK_PAD + QK_NOPE:(h + 1) * QK_PAD], cs, sn)]
            dkvs += [dk_[:, h * QK_PAD:h * QK_PAD + QK_NOPE].astype(BF16), dv_[:, h * V_HEAD:(h + 1) * V_HEAD]]
            part = dk_[:, h * QK_PAD + QK_NOPE:(h + 1) * QK_PAD]
            dkr = part if dkr is None else dkr + part
        return jnp.concatenate(dqs, axis=1), jnp.concatenate(dkvs, axis=1), _rope128_t(dkr, cs, sn)

    drope_ins = [("rowc", dq_lat, 0, H * QK_PAD, nlat - 1), ("full", dq_ctx), ("row", dk_lat, 0, H * QK_PAD), ("full", dk_ctx),
                 ("row", dv_lat, 0, H * V_HEAD), ("full", dv_ctx), ("row", cos_t, 0, LANES), ("row", sin_t, 0, LANES)]
    dqraw, dkv, dz_kr = _rowwise("l0_drope", f_rope_bwd, nt, TR, drope_ins,
                                 [("row", TA, H * QK_PAD, BF16), ("row", TA, H * QK_PAD, BF16), ("row", TA, LANES, BF16)])
    dqn = _matmul("l0_dqb_x", dqraw, w_qbp, "nt", TA, Q_LORA, H * QK_PAD)
    g_wqbp = _matmul("l0_dqb_w", qn, dqraw, "tn", Q_LORA, H * QK_PAD, TA).reshape(Q_LORA, H, QK_PAD)
    g_rope = g_wqbp[:, :, QK_NOPE:QK_NOPE + QK_ROPE] + _rot_cols_t(g_wqbp[:, :, QK_NOPE + QK_ROPE:])
    grads["l0_w_qb"] = jnp.concatenate([g_wqbp[:, :, :QK_NOPE], g_rope], axis=2).reshape(Q_LORA, H * (QK_NOPE + QK_ROPE)).astype(BF16)
    dkvn = _matmul("l0_dkvb_x", dkv, full["l0_w_kvb"], "nt", TA, KV_LORA, H * QK_PAD)
    grads["l0_w_kvb"] = _matmul("l0_dkvb_w", kvn, dkv, "tn", KV_LORA, H * QK_PAD, TA, out_dtype=BF16)
    norms_bwd = _vjp_of(_f_norms, 4, (0, 1, 2, 3))
    dz_kv, dz_q, grads["l0_kv_norm"], grads["l0_q_norm"] = _rowwise(
        "l0_dlatnorm", norms_bwd, nt, TR, norm_ins + [("row", dkvn, 0, KV_LORA), ("row", dqn, 0, Q_LORA)],
        [("row", TA, KV_LORA, BF16), ("row", TA, Q_LORA, BF16), ("acc", 1, KV_LORA), ("acc", 1, Q_LORA)])
    dz0 = jnp.concatenate([dz_a, dz_q, dz_kv, dz_kr, jnp.zeros((TA, W0 - KROFF0 - LANES), BF16)], axis=1)
    dh0 = _matmul("l0_din_x", dz0, w_in0p, "nt", TA, D, W0)
    g0p = _matmul("l0_din_w", h0, dz0, "tn", D, W0, TA)
    g_kr = g0p[:, KROFF0:KROFF0 + QK_ROPE] + _rot_cols_t(g0p[:, KROFF0 + QK_ROPE:KROFF0 + 2 * QK_ROPE])
    grads["l0_w_in"] = jnp.concatenate([g0p[:, KVOFF0:KROFF0], g_kr, g0p[:, QOFF0:KVOFF0], g0p[:, :QOFF0]], axis=1).astype(BF16)

    def f_mod_bwd(i, x_, w_, sh, sc, dh, dxd):
        _, pull = jax.vjp(lambda a, b, c_, d_: _modulate(a, b, c_, d_), x_, w_, sh, sc)
        dx, dw, dsh, dsc = pull(dh)
        return dx + dxd, dw, dsh, dsc

    dXf, grads["l0_norm1"], dm0_0, dm1_0 = _rowwise(
        "l0_dmod1", f_mod_bwd, nt, TR,
        [("row", X, 0, D), ("full", n1_0), ("grp", P(0, 0)), ("grp", P(0, 1)), ("row", dh0, 0, D), ("row", dXd, 0, D)],
        [("row", TA, D, F32), ("acc", 1, D), ("accg", 1, D), ("accg", 1, D)], nlat)
    grad_x = dXf[:T][None]

    zD = jnp.zeros((1, D), F32)
    lat = lambda a: a[0] if a.ndim == 3 else a
    cxt = lambda a: a[1] if a.ndim == 3 else zD
    dmods = []
    for parts in ((dm0_0, dm1_0, dm2_0, dm3_0, dm4_0, dm5_0), (dm0_1, dm1_1, dm2_1, dm3_1, dm4_1, dm5_1)):
        dmods.append((jnp.concatenate([lat(a) for a in parts], axis=1), jnp.concatenate([cxt(a) for a in parts], axis=1)))
    small_names = ["l0_norm1", "l0_norm2", "l0_kv_norm", "l0_q_norm", "l0_conv_a", "l0_ffn_conv_w", "l0_ffn_conv_b",
                   "l1_norm1", "l1_norm2", "l1_o_norm", "l1_gate_fw_w", "l1_gate_fw_b", "l1_gate_bw_w", "l1_gate_bw_b",
                   "l1_ffn_conv_w", "l1_ffn_conv_b", "final_norm"]
    pieces = [("dm0", dmods[0][0]), ("dmc0", dmods[0][1]), ("dm1", dmods[1][0]), ("dmc1", dmods[1][1])]
    pieces += [("l0_ada_b", dmods[0][0] + dmods[0][1]), ("l1_ada_b", dmods[1][0] + dmods[1][1])]
    pieces += [(n, grads[n]) for n in small_names]
    offs, cur = {}, 0
    for n, a in pieces:
        offs[n] = (cur, a.size, a.shape)
        cur += -(-a.size // LANES) * LANES
    n_pad = -(-cur // 1024) * 1024
    flat = jnp.concatenate([jnp.pad(a.reshape(-1), (0, -a.size % LANES)) for _, a in pieces] + [jnp.zeros((n_pad - cur,), F32)])
    (small_g,) = _exchange("gather_small", [flat.reshape(n_pad // LANES, LANES)], "ag_all")
    small_sum = _sum_slots("sum_small", small_g).reshape(-1)
    small_all = small_g.reshape(8, -1)
    take = lambda n: small_sum[offs[n][0]:offs[n][0] + offs[n][1]].reshape(offs[n][2])

    cc_parts = []
    for l in (0, 1):
        o_m, sz, _ = offs["dm%d" % l]
        d16 = jnp.concatenate([small_all[:, o_m:o_m + sz], take("dmc%d" % l).reshape(1, -1), jnp.zeros((7, sz), F32)], axis=0)
        d16 = lax.dynamic_slice_in_dim(d16.reshape(16, 4, ADA_S), chip, 1, axis=1).reshape(16, ADA_S)
        grads["l%d_ada_w" % l] = _matmul("ada_dw", act16, d16, "tn", D, ADA_S, 16, tn_cap=512)
        cc_parts.append(_matmul("ada_dx", d16, W["l%d_ada_w" % l], "nt", 16, D, ADA_S, tn_cap=512))
        grads["l%d_ada_b" % l] = take("l%d_ada_b" % l).reshape(-1)
    cc_g = _exchange("gather_cc", cc_parts, "ag_xy")

    def cc_call():
        def body(a_ref, b_ref, c_ref, o_ref):
            tot = a_ref[0, 8:9, :] + b_ref[0, 8:9, :]
            for j in range(1, 4):
                tot = tot + (a_ref[j, 8:9, :] + b_ref[j, 8:9, :])
            _, pull = jax.vjp(_silu, c_ref[...])
            o_ref[...] = pull(tot)[0]
        return pl.pallas_call(body, name="c_ctx_grad", out_shape=jax.ShapeDtypeStruct((1, D), F32))(cc_g[0], cc_g[1], row(c_ctx))

    grads["c_ctx"] = cc_call().reshape(-1)

    row_sh = {"l0_w_out", "l1_w_out", "l0_ffn_down", "l1_ffn_down"}
    def by_half(n):
        s = grads[n].reshape((4, -1) + grads[n].shape[1:]) if n in row_sh else _cols_to_shards(grads[n])
        return s.reshape(4, 2, s.shape[1] // 2, s.shape[2]).transpose(1, 0, 2, 3)

    g_half = [by_half(n) for n in big]
    keep, give = [own_half(g, my_c) for g in g_half], [own_half(g, 1 - my_c) for g in g_half]
    taken = _exchange("pair_split_grads", give, "swap_c")
    pair = [_sum_pair("pairsum_" + n[3:], a, b, BF16) for n, a, b in zip(big, keep, taken)]
    landed = _exchange("scatter_grads", pair, "a2a_xy", fill_own=False)
    landed = [set_own(l, own_half(p, chip), chip) for l, p in zip(landed, pair)]
    halves = [_sum_slots("sum_" + n[3:], a) for n, a in zip(big, landed)]
    others = _exchange("pair_grads", halves, "swap_c")

    out_g, out_d, out_m, out_v = {}, {}, {}, {}
    for n, mine, other in zip(big, halves, others):
        g2 = jnp.where(my_c == 0, jnp.concatenate([mine, other], 0), jnp.concatenate([other, mine], 0))
        out_g[n], out_d[n], out_m[n], out_v[n] = _adam("adam_" + n[3:], W[n], [g2], MO[n], VO[n])
    for l in (0, 1):
        n = "l%d_ada_w" % l
        out_g[n], out_d[n], out_m[n], out_v[n] = _adam("adam_ada_w", W[n], [grads[n]], MO[n], VO[n])
    rest = [n for n in names if n not in out_g]
    g_rest = {}
    for n in rest:
        if n == "c_ctx" or n.endswith("ada_b"):
            g_rest[n] = grads[n]
        elif n in small_sh:
            gfull = take(n)
            cs = gfull.shape[1] // 4
            g_rest[n] = lax.dynamic_slice_in_dim(gfull, chip * cs, cs, axis=1)
        else:
            g_rest[n] = take(n).reshape(W[n].shape)
    sizes = [W[n].size for n in rest]
    tot = sum(-(-s // LANES) * LANES for s in sizes)
    tot_pad = -(-tot // 1024) * 1024

    def pack(d):
        parts = [jnp.pad(d[n].reshape(-1), (0, -d[n].size % LANES)) for n in rest]
        return jnp.concatenate(parts + [jnp.zeros((tot_pad - tot,), F32)]).reshape(tot_pad // LANES, LANES)

    packed = _adam("adam_small", pack(W), [pack(g_rest)], pack(MO), pack(VO))
    cur = 0
    for n, s in zip(rest, sizes):
        for dst, arr in zip((out_g, out_d, out_m, out_v), packed):
            dst[n] = arr.reshape(-1)[cur:cur + s].reshape(W[n].shape)
        cur += -(-s // LANES) * LANES

    return (loss, grad_x, *[out_g[n] for n in names], *[out_d[n] for n in names],
            *[out_m[n] for n in names], *[out_v[n] for n in names])
```

```python
import numpy as np

import jax
import jax.numpy as jnp
from jax import lax
from jax.experimental import pallas as pl
from jax.experimental.pallas import tpu as pltpu

F32, BF16 = jnp.float32, jnp.bfloat16
HIGHEST = lax.Precision.HIGHEST
MESH_ID = pl.DeviceIdType.MESH

EPS = 1e-6
N_MOD = 6
MLA_HEADS, QK_NOPE, QK_ROPE, V_HEAD, Q_LORA, KV_LORA = 8, 128, 64, 128, 512, 256
QK_PAD = 2 * QK_NOPE
ROPE_THETA, GRID_W = 10000.0, 64
GLA_HEADS, GATE_RANK, GATE_NORMALIZER, CHUNK = 4, 16, 16.0, 64
ADAM_LR, ADAM_B1, ADAM_B2, ADAM_EPS, ADAM_WD, ADAM_STEP = 0.001, 0.9, 0.999, 1e-08, 0.01, 10

LANES = 128
TR = 256
V7X_VMEM_BYTES = 64 * 2 ** 20
VMEM_LIMIT = V7X_VMEM_BYTES - 8 * 2 ** 20

NT_DIMS = (((1,), (1,)), ((), ()))
TN_DIMS = (((0,), (0,)), ((), ()))
NN_DIMS = (((1,), (0,)), ((), ()))

def _rot_cols(w):
    q = QK_ROPE // 4
    return jnp.concatenate([-w[..., q:2 * q], w[..., :q], -w[..., 3 * q:], w[..., 2 * q:3 * q]], axis=-1)


def _rot_cols_t(g):
    q = QK_ROPE // 4
    return jnp.concatenate([g[..., q:2 * q], -g[..., :q], g[..., 3 * q:], -g[..., 2 * q:3 * q]], axis=-1)


def _cparams(sem=None):
    return pltpu.CompilerParams(dimension_semantics=sem, vmem_limit_bytes=VMEM_LIMIT)


def _tile(dim, cap, quantum=LANES):
    if dim <= cap:
        return dim
    t = (cap // quantum) * quantum
    while t >= quantum:
        if dim % t == 0:
            return t
        t -= quantum
    return dim


_REL_XY = ((1, 0, 0), (0, 1, 0), (1, 1, 0))
_REL_ALL = tuple((a, b, c) for a in (0, 1) for b in (0, 1) for c in (0, 1))[1:]
_REL_C = ((0, 0, 1),)


_RELS = {"ag_xy": _REL_XY, "a2a_xy": _REL_XY, "ag_all": _REL_ALL, "swap_c": _REL_C}
_LEAD = {"ag_xy": (4,), "ag_all": (8,), "a2a_xy": (), "swap_c": ()}


def _remote_copies(mode, ins, outs, send, recv):
    x, y, c = lax.axis_index("x"), lax.axis_index("y"), lax.axis_index("c")
    chip, dev = 2 * x + y, 4 * x + 2 * y + c
    rels, plan = _RELS[mode], []
    for a, (i_ref, o) in enumerate(zip(ins, outs)):
        for r, (bx, by, bc) in enumerate(rels):
            px = 1 - x if bx else x
            py = 1 - y if by else y
            pc = 1 - c if bc else c
            pchip, pdev = 2 * px + py, 4 * px + 2 * py + pc
            if mode == "ag_xy":
                src, dst, mine = i_ref, o.at[chip], o.at[pchip]
            elif mode == "ag_all":
                src, dst, mine = i_ref, o.at[dev], o.at[pdev]
            elif mode == "a2a_xy":
                src, dst, mine = i_ref.at[pchip], o.at[chip], o.at[pchip]
            else:
                src, dst, mine = i_ref, o, o
            k = a * len(rels) + r
            mk = lambda d: pltpu.make_async_remote_copy(src_ref=src, dst_ref=d, send_sem=send.at[k], recv_sem=recv.at[k],
                                                        device_id=(px, py, pc), device_id_type=MESH_ID)
            plan.append((mk(dst), mk(mine)))
    return plan


def _exchange(name, arrs, mode, fill_own=True):
    n, nr = len(arrs), len(_RELS[mode])
    out_shape = tuple(jax.ShapeDtypeStruct(_LEAD[mode] + a.shape, a.dtype) for a in arrs)

    def body(*refs):
        ins, outs = refs[:n], refs[n:2 * n]
        send, recv, loc = refs[2 * n:]
        chip = 2 * lax.axis_index("x") + lax.axis_index("y")
        local = []
        if fill_own and mode != "swap_c":
            slot = 2 * chip + lax.axis_index("c") if mode == "ag_all" else chip
            for a in range(n):
                lc = pltpu.make_async_copy(ins[a].at[chip] if mode == "a2a_xy" else ins[a], outs[a].at[slot], loc.at[a])
                lc.start()
                local.append(lc)
        plan = _remote_copies(mode, ins, outs, send, recv)
        for cp, _ in plan:
            cp.start()
        for cp, landing in plan:
            cp.wait_send()
            landing.wait_recv()
        for lc in local:
            lc.wait()

    hbm = pl.BlockSpec(memory_space=pl.ANY)
    res = pl.pallas_call(
        body, name=name, out_shape=out_shape, in_specs=[hbm] * n, out_specs=tuple([hbm] * n),
        scratch_shapes=[pltpu.SemaphoreType.DMA((n * nr,)), pltpu.SemaphoreType.DMA((n * nr,)),
                        pltpu.SemaphoreType.DMA((max(n, 1),))],
    )(*arrs)
    return list(res)


class _Hosted:
    def __init__(self, arrs, mode):
        self.arrs, self.mode, self.n = list(arrs), mode, len(arrs)
        self.out_shape = [jax.ShapeDtypeStruct(_LEAD[mode] + a.shape, a.dtype) for a in arrs]
        self.specs = [pl.BlockSpec(memory_space=pl.ANY)] * self.n
        nsem = self.n * len(_RELS[mode])
        self.scratch = [pltpu.SemaphoreType.DMA((nsem,)), pltpu.SemaphoreType.DMA((nsem,))]

    def run(self, first, last, ins, outs, sems):
        @pl.when(first)
        def _():
            for cp, _ in _remote_copies(self.mode, ins, outs, *sems):
                cp.start()

        @pl.when(last)
        def _():
            for cp, landing in _remote_copies(self.mode, ins, outs, *sems):
                cp.wait_send()
                landing.wait_recv()


def _matmul(name, a, b, mode, M, N, K, out_dtype=F32, a_off=(0, 0), b_off=(0, 0), tm_cap=1024, tn_cap=1024, tk_cap=2048):
    tm, tn = _tile(M, tm_cap, LANES if M % LANES == 0 else 8), _tile(N, tn_cap)
    tk = K if K <= 4096 else _tile(K, tk_cap)
    nk = K // tk
    assert M % tm == 0 and N % tn == 0 and K % tk == 0, (name, M, N, K, tm, tn, tk)
    if mode == "nn":
        ab, bb, dims = (tm, tk), (tk, tn), NN_DIMS
        ai = lambda i, j, k: (i + a_off[0] // tm, k + a_off[1] // tk)
        bi = lambda i, j, k: (k + b_off[0] // tk, j + b_off[1] // tn)
        chk = (a_off[0] % tm, a_off[1] % tk, b_off[0] % tk, b_off[1] % tn)
    elif mode == "nt":
        ab, bb, dims = (tm, tk), (tn, tk), NT_DIMS
        ai = lambda i, j, k: (i + a_off[0] // tm, k + a_off[1] // tk)
        bi = lambda i, j, k: (j + b_off[0] // tn, k + b_off[1] // tk)
        chk = (a_off[0] % tm, a_off[1] % tk, b_off[0] % tn, b_off[1] % tk)
    else:
        ab, bb, dims = (tk, tm), (tk, tn), TN_DIMS
        ai = lambda i, j, k: (k + a_off[0] // tk, i + a_off[1] // tm)
        bi = lambda i, j, k: (k + b_off[0] // tk, j + b_off[1] // tn)
        chk = (a_off[0] % tk, a_off[1] % tm, b_off[0] % tk, b_off[1] % tn)
    assert not any(chk), (name, chk)

    def body(a_ref, b_ref, o_ref, *acc):
        p = lax.dot_general(a_ref[...].astype(BF16), b_ref[...].astype(BF16), dims, preferred_element_type=F32)
        if nk == 1:
            o_ref[...] = p.astype(out_dtype)
        else:
            k = pl.program_id(2)

            @pl.when(k == 0)
            def _():
                acc[0][...] = p

            @pl.when(k > 0)
            def _():
                acc[0][...] += p

            @pl.when(k == nk - 1)
            def _():
                o_ref[...] = acc[0][...].astype(out_dtype)

    return pl.pallas_call(
        body, name=name, out_shape=jax.ShapeDtypeStruct((M, N), out_dtype), grid=(M // tm, N // tn, nk),
        in_specs=[pl.BlockSpec(ab, ai), pl.BlockSpec(bb, bi)], out_specs=pl.BlockSpec((tm, tn), lambda i, j, k: (i, j)),
        scratch_shapes=[pltpu.VMEM((tm, tn), F32)] if nk > 1 else [],
        compiler_params=_cparams(("parallel", "parallel", "arbitrary")),
    )(a, b)


def _rowwise(name, fn, grid_n, tr, ins, outs, nlat=None):
    nlat = grid_n if nlat is None else nlat
    grp = lambda i: jnp.minimum(i // nlat, 1)
    in_specs, args = [], []
    for spec in ins:
        kind, arr = spec[0], spec[1]
        if kind == "row":
            in_specs.append(pl.BlockSpec((tr, spec[3]), lambda i, cb=spec[2]: (i, cb)))
        elif kind == "rowc":
            in_specs.append(pl.BlockSpec((tr, spec[3]), lambda i, cb=spec[2], mx=spec[4]: (jnp.minimum(i, mx), cb)))
        elif kind == "prev":
            in_specs.append(pl.BlockSpec((8, spec[3]), lambda i, cb=spec[2]: (jnp.maximum(i * (tr // 8) - 1, 0), cb)))
        elif kind == "next":
            nb = arr.shape[0] // 8
            in_specs.append(pl.BlockSpec((8, spec[3]), lambda i, cb=spec[2], nb=nb: (jnp.minimum((i + 1) * (tr // 8), nb - 1), cb)))
        elif kind == "grp":
            in_specs.append(pl.BlockSpec((None,) + arr.shape[1:], lambda i: (grp(i), 0, 0)))
        else:
            in_specs.append(pl.BlockSpec(arr.shape, lambda i: (0, 0)))
        args.append(arr)
    out_shape, out_specs = [], []
    for spec in outs:
        if spec[0] == "row":
            out_shape.append(jax.ShapeDtypeStruct((spec[1], spec[2]), spec[3]))
            out_specs.append(pl.BlockSpec((tr, spec[2]), lambda i: (i, 0)))
        elif spec[0] == "acc":
            out_shape.append(jax.ShapeDtypeStruct((spec[1], spec[2]), F32))
            out_specs.append(pl.BlockSpec((spec[1], spec[2]), lambda i: (0, 0)))
        else:
            out_shape.append(jax.ShapeDtypeStruct((2, spec[1], spec[2]), F32))
            out_specs.append(pl.BlockSpec((None, spec[1], spec[2]), lambda i: (grp(i), 0, 0)))
    n_in = len(ins)
    has_acc = any(s[0] != "row" for s in outs)

    def body(*refs):
        i = pl.program_id(0)
        res = fn(i, *[r[...] for r in refs[:n_in]])
        for spec, ref, val in zip(outs, refs[n_in:], res):
            if spec[0] == "row":
                ref[...] = val.astype(ref.dtype)
            else:
                first = (i == 0) if spec[0] == "acc" else jnp.logical_or(i == 0, i == nlat)

                @pl.when(first)
                def _(ref=ref, val=val):
                    ref[...] = val

                @pl.when(jnp.logical_not(first))
                def _(ref=ref, val=val):
                    ref[...] += val

    return pl.pallas_call(
        body, name=name, out_shape=tuple(out_shape), grid=(grid_n,), in_specs=in_specs, out_specs=tuple(out_specs),
        compiler_params=_cparams(("arbitrary",) if has_acc else ("parallel",)),
    )(*args)


def _vjp_of(fwd, n_in, wrt):
    def bwd(i, *args):
        _, pull = jax.vjp(lambda *a: fwd(i, *a), *args[:n_in])
        g = pull(tuple(args[n_in:]))
        return tuple(g[k] for k in wrt)
    return bwd


def _rms(x, w):
    return x * lax.rsqrt(jnp.mean(x * x, axis=-1, keepdims=True) + EPS) * w


def _modulate(x, w, shift, scale):
    return _rms(x, w) * (1.0 + scale) + shift


def _silu(x):
    return x * jax.nn.sigmoid(x)


def _log_sigmoid(x):
    return jnp.minimum(x, 0.0) - jnp.log(1.0 + jnp.exp(-jnp.abs(x)))


def _f_mod(i, x, w, sh, sc):
    return (_modulate(x, w, sh, sc),)


def _f_resmod(i, x, y, gate, w, sh, sc):
    x1 = x + gate * y
    return x1, _modulate(x1, w, sh, sc)


def _f_norms(i, kvl, ql, kvw, qw):
    return _rms(kvl, kvw), _rms(ql, qw)


def _f_silu(i, x):
    return (_silu(x),)


def _rope128(x2, cos, sin):
    return x2 * cos + pltpu.roll(x2, QK_ROPE, 1) * sin


def _rope128_t(d2, cos, sin):
    return d2 * cos + pltpu.roll(d2 * sin, QK_ROPE, 1)


def _shifters(i, tr, T, TA):
    rows = i * tr + lax.broadcasted_iota(jnp.int32, (tr, 1), 0)
    loc = lax.broadcasted_iota(jnp.int32, (tr, 1), 0)
    has_prev = jnp.logical_and(rows != 0, rows != T)
    has_next = jnp.logical_and(rows != T - 1, rows != TA - 1)

    def prev(x, halo):
        r = jnp.where(loc == 0, halo[7:8, :], pltpu.roll(x, 1, 0))
        return jnp.where(has_prev, r, 0.0)

    def nxt(x, halo):
        r = jnp.where(loc == tr - 1, halo[0:1, :], pltpu.roll(x, tr - 1, 0))
        return jnp.where(has_next, r, 0.0)

    return prev, nxt


def _colsum(x):
    return jnp.sum(x, axis=0, keepdims=True)


MLA_SCALE = (QK_NOPE + QK_ROPE) ** -0.5


def _mla_fwd_lat(q, kf, kv, T, TA, hosted):
    H = MLA_HEADS
    tq, tk = _tile(T, 1024, TR), _tile(TA, 1408)
    nq, nk = T // tq, TA // tk
    nh = hosted.n

    def body(q_ref, k_ref, v_ref, *rest):
        o_ref, lse_ref = rest[nh:nh + 2]
        m_sc, l_sc, acc_sc = rest[2 * nh + 2:2 * nh + 5]
        hi, qi, ki = pl.program_id(0), pl.program_id(1), pl.program_id(2)
        hosted.run(jnp.logical_and(hi == 0, jnp.logical_and(qi == 0, ki == 0)),
                   jnp.logical_and(hi == H - 1, jnp.logical_and(qi == nq - 1, ki == nk - 1)),
                   rest[:nh], rest[nh + 2:2 * nh + 2], rest[2 * nh + 5:])

        @pl.when(ki == 0)
        def _():
            m_sc[...] = jnp.full_like(m_sc, -jnp.inf)
            l_sc[...] = jnp.zeros_like(l_sc)
            acc_sc[...] = jnp.zeros_like(acc_sc)

        s = lax.dot_general(q_ref[...], k_ref[...], NT_DIMS, preferred_element_type=F32) * MLA_SCALE
        m_new = jnp.maximum(m_sc[...], jnp.max(s, axis=-1, keepdims=True))
        alpha = jnp.exp(m_sc[...] - m_new)
        p = jnp.exp(s - m_new)
        l_sc[...] = alpha * l_sc[...] + jnp.sum(p, axis=-1, keepdims=True)
        acc_sc[...] = alpha * acc_sc[...] + jnp.dot(p.astype(BF16), v_ref[...], preferred_element_type=F32)
        m_sc[...] = m_new

        @pl.when(ki == nk - 1)
        def _():
            o_ref[...] = (acc_sc[...] / l_sc[...]).astype(o_ref.dtype)
            lse_ref[...] = m_sc[...] + jnp.log(l_sc[...])

    res = pl.pallas_call(
        body, name="mla_fwd_lat", grid=(H, nq, nk),
        out_shape=[jax.ShapeDtypeStruct((T, H * V_HEAD), BF16), jax.ShapeDtypeStruct((H, T, 1), F32)] + hosted.out_shape,
        in_specs=[pl.BlockSpec((tq, QK_PAD), lambda h, i, k: (i, h)), pl.BlockSpec((tk, QK_PAD), lambda h, i, k: (k, h)),
                  pl.BlockSpec((tk, V_HEAD), lambda h, i, k: (k, 2 * h + 1))] + hosted.specs,
        out_specs=[pl.BlockSpec((tq, V_HEAD), lambda h, i, k: (i, h)),
                   pl.BlockSpec((None, tq, 1), lambda h, i, k: (h, i, 0))] + hosted.specs,
        scratch_shapes=[pltpu.VMEM((tq, 1), F32), pltpu.VMEM((tq, 1), F32), pltpu.VMEM((tq, V_HEAD), F32)] + hosted.scratch,
        compiler_params=_cparams(("arbitrary", "arbitrary", "arbitrary")),
    )(q, kf, kv, *hosted.arrs)
    return res[0], res[1], list(res[2:])


def _mla_ctx_probs(q_ref, k_ref):
    s = lax.dot_general(q_ref[...], k_ref[...], NT_DIMS, preferred_element_type=F32) * MLA_SCALE
    p = jnp.exp(s - jnp.max(s, axis=-1, keepdims=True))
    return p, jnp.sum(p, axis=-1, keepdims=True)


def _mla_fwd_ctx(q, kf, kv, T):
    H, cb = MLA_HEADS, T // TR

    def body(q_ref, k_ref, v_ref, o_ref):
        p, l = _mla_ctx_probs(q_ref, k_ref)
        o_ref[...] = (jnp.dot(p.astype(BF16), v_ref[...], preferred_element_type=F32) / l).astype(o_ref.dtype)

    return pl.pallas_call(
        body, name="mla_fwd_ctx", grid=(H,), out_shape=jax.ShapeDtypeStruct((TR, H * V_HEAD), BF16),
        in_specs=[pl.BlockSpec((TR, QK_PAD), lambda h: (cb, h)), pl.BlockSpec((TR, QK_PAD), lambda h: (cb, h)),
                  pl.BlockSpec((TR, V_HEAD), lambda h: (cb, 2 * h + 1))],
        out_specs=pl.BlockSpec((TR, V_HEAD), lambda h: (0, h)), compiler_params=_cparams(("parallel",)),
    )(q, kf, kv)


def _mla_grads(p, q_ref, k_ref, v_ref, do, delta):
    dob = do.astype(BF16)
    dp = lax.dot_general(dob, v_ref[...], NT_DIMS, preferred_element_type=F32)
    ds = (p * (dp - delta) * MLA_SCALE).astype(BF16)
    return (jnp.dot(ds, k_ref[...], preferred_element_type=F32),
            lax.dot_general(ds, q_ref[...], TN_DIMS, preferred_element_type=F32),
            lax.dot_general(p.astype(BF16), dob, TN_DIMS, preferred_element_type=F32))


def _mla_bwd_lat(q, kf, kv, o, dcat, lse, do_cb0, T, TA, hosted):
    H = MLA_HEADS
    tq, tk = _tile(T, 512, TR), _tile(TA, 1408)
    nq, nk = T // tq, TA // tk
    nh = hosted.n

    def body(q_ref, k_ref, v_ref, o_ref, do_ref, lse_ref, *rest):
        dq_ref, dk_ref, dv_ref = rest[nh:nh + 3]
        dk_acc, dv_acc = rest[2 * nh + 3:2 * nh + 5]
        hi, ki, qi = pl.program_id(0), pl.program_id(1), pl.program_id(2)
        hosted.run(jnp.logical_and(hi == 0, jnp.logical_and(qi == 0, ki == 0)),
                   jnp.logical_and(hi == H - 1, jnp.logical_and(qi == nq - 1, ki == nk - 1)),
                   rest[:nh], rest[nh + 3:2 * nh + 3], rest[2 * nh + 5:])

        @pl.when(jnp.logical_and(ki == 0, qi == 0))
        def _():
            dq_ref[...] = jnp.zeros_like(dq_ref)

        @pl.when(qi == 0)
        def _():
            dk_acc[...] = jnp.zeros_like(dk_acc)
            dv_acc[...] = jnp.zeros_like(dv_acc)

        s = lax.dot_general(q_ref[...], k_ref[...], NT_DIMS, preferred_element_type=F32) * MLA_SCALE
        do = do_ref[...]
        delta = jnp.sum(do * o_ref[...].astype(F32), axis=-1, keepdims=True)
        dq, dk, dv = _mla_grads(jnp.exp(s - lse_ref[...]), q_ref, k_ref, v_ref, do, delta)
        rows = pl.ds(pl.multiple_of(qi * tq, tq), tq)
        dq_ref[rows, :] += dq
        dk_acc[...] += dk
        dv_acc[...] += dv

        @pl.when(qi == nq - 1)
        def _():
            dk_ref[...] = dk_acc[...]
            dv_ref[...] = dv_acc[...].astype(dv_ref.dtype)

    res = pl.pallas_call(
        body, name="mla_bwd_lat", grid=(H, nk, nq),
        out_shape=[jax.ShapeDtypeStruct((T, H * QK_PAD), F32), jax.ShapeDtypeStruct((TA, H * QK_PAD), F32),
                   jax.ShapeDtypeStruct((TA, H * V_HEAD), BF16)] + hosted.out_shape,
        in_specs=[pl.BlockSpec((tq, QK_PAD), lambda h, k, i: (i, h)), pl.BlockSpec((tk, QK_PAD), lambda h, k, i: (k, h)),
                  pl.BlockSpec((tk, V_HEAD), lambda h, k, i: (k, 2 * h + 1)), pl.BlockSpec((tq, V_HEAD), lambda h, k, i: (i, h)),
                  pl.BlockSpec((tq, V_HEAD), lambda h, k, i: (i, do_cb0 + h)),
                  pl.BlockSpec((None, tq, 1), lambda h, k, i: (h, i, 0))] + hosted.specs,
        out_specs=[pl.BlockSpec((T, QK_PAD), lambda h, k, i: (0, h)), pl.BlockSpec((tk, QK_PAD), lambda h, k, i: (k, h)),
                   pl.BlockSpec((tk, V_HEAD), lambda h, k, i: (k, h))] + hosted.specs,
        scratch_shapes=[pltpu.VMEM((tk, QK_PAD), F32), pltpu.VMEM((tk, V_HEAD), F32)] + hosted.scratch,
        compiler_params=_cparams(("arbitrary", "arbitrary", "arbitrary")),
    )(q, kf, kv, o, dcat, lse, *hosted.arrs)
    return res[0], res[1], res[2], list(res[3:])


def _mla_bwd_ctx(q, kf, kv, dcat, dk_lat, dv_lat, do_cb0, T):
    H, cb = MLA_HEADS, T // TR

    def body(q_ref, k_ref, v_ref, do_ref, dkl_ref, dvl_ref, dq_ref, dk_ref, dv_ref):
        p, l = _mla_ctx_probs(q_ref, k_ref)
        do = do_ref[...]
        o = jnp.dot(p.astype(BF16), v_ref[...], preferred_element_type=F32) / l
        dq, dk, dv = _mla_grads(p / l, q_ref, k_ref, v_ref, do, jnp.sum(do * o, axis=-1, keepdims=True))
        dq_ref[...] = dq
        dk_ref[...] = dkl_ref[...] + dk
        dv_ref[...] = (dvl_ref[...].astype(F32) + dv).astype(dv_ref.dtype)

    at_ctx = lambda w, f: pl.BlockSpec((TR, w), lambda h: (cb, f(h)))
    at_0 = lambda w: pl.BlockSpec((TR, w), lambda h: (0, h))
    return pl.pallas_call(
        body, name="mla_bwd_ctx", grid=(H,),
        out_shape=(jax.ShapeDtypeStruct((TR, H * QK_PAD), F32), jax.ShapeDtypeStruct((TR, H * QK_PAD), F32),
                   jax.ShapeDtypeStruct((TR, H * V_HEAD), BF16)),
        in_specs=[at_ctx(QK_PAD, lambda h: h), at_ctx(QK_PAD, lambda h: h), at_ctx(V_HEAD, lambda h: 2 * h + 1),
                  at_ctx(V_HEAD, lambda h: do_cb0 + h), at_ctx(QK_PAD, lambda h: h), at_ctx(V_HEAD, lambda h: h)],
        out_specs=(at_0(QK_PAD), at_0(QK_PAD), at_0(V_HEAD)), compiler_params=_cparams(("parallel",)),
    )(q, kf, kv, dcat, dk_lat, dv_lat)


def _gla_tile(st, q, k, v, g, rev, q_scale):
    nc = q.shape[0] // CHUNK
    ii = lax.broadcasted_iota(jnp.int32, (CHUNK, CHUNK), 0)
    jj = lax.broadcasted_iota(jnp.int32, (CHUNK, CHUNK), 1)
    tri = (jj >= ii) if rev else (jj <= ii)
    ones = tri.astype(F32)
    outs = [None] * nc
    for ci in (range(nc - 1, -1, -1) if rev else range(nc)):
        sl = slice(ci * CHUNK, (ci + 1) * CHUNK)
        qc, kc, vc, gc = q[sl] * q_scale, k[sl], v[sl].astype(BF16), g[sl]
        b = jnp.dot(ones, gc, precision=HIGHEST, preferred_element_type=F32)
        bl = b[0:1] if rev else b[CHUNK - 1:CHUNK]
        kd = (kc * jnp.exp(bl - b)).astype(BF16)
        qe = (qc * jnp.exp(b)).astype(BF16)
        ke = (kc * jnp.exp(-b)).astype(BF16)
        att = jnp.where(tri, lax.dot_general(qe, ke, NT_DIMS, preferred_element_type=F32), 0.0)
        outs[ci] = (jnp.dot(att.astype(BF16), vc, preferred_element_type=F32)
                    + lax.dot_general(qe, st.astype(BF16), NT_DIMS, preferred_element_type=F32))
        st = st * jnp.exp(bl) + lax.dot_general(vc, kd, TN_DIMS, preferred_element_type=F32)
    return st, jnp.concatenate(outs, axis=0)


def _gla_tiles(nt, nlat):
    return (lambda p: (p + nlat) % nt), (lambda p: nt - 1 - p)


def _gla_fwd(z1, g, dims, nt, nlat):
    DK, DV, q_off, k_off = dims
    KEY, TA = GLA_HEADS * DK, nt * TR
    in_specs, out_o = [], []
    for d, tile in enumerate(_gla_tiles(nt, nlat)):
        in_specs += [pl.BlockSpec((TR, DK), lambda h, p, t=tile: (t(p), q_off // DK + h)),
                     pl.BlockSpec((TR, DK), lambda h, p, t=tile: (t(p), k_off // DK + h)),
                     pl.BlockSpec((TR, DV), lambda h, p, t=tile: (t(p), h)),
                     pl.BlockSpec((TR, DK), lambda h, p, t=tile, d=d: (t(p), d * GLA_HEADS + h))]
        out_o.append(pl.BlockSpec((TR, DV), lambda h, p, t=tile: (t(p), h)))
    st_spec = pl.BlockSpec((None, None, DV, DK), lambda h, p: (h, p, 0, 0))

    def body(qf, kf, vf, gf, qb, kb, vb, gb, of_ref, ob_ref, sf_ref, sb_ref, st_f, st_b):
        @pl.when(pl.program_id(1) == 0)
        def _():
            st_f[...] = jnp.zeros_like(st_f)
            st_b[...] = jnp.zeros_like(st_b)

        sf_ref[...] = st_f[...]
        sb_ref[...] = st_b[...]
        new_f, out_f = _gla_tile(st_f[...], qf[...], kf[...], vf[...], gf[...], False, DK ** -0.5)
        new_b, out_b = _gla_tile(st_b[...], qb[...], kb[...], vb[...], gb[...], True, DK ** -0.5)
        st_f[...] = new_f
        st_b[...] = new_b
        of_ref[...] = out_f
        ob_ref[...] = out_b

    o_shape = jax.ShapeDtypeStruct((TA, GLA_HEADS * DV), F32)
    s_shape = jax.ShapeDtypeStruct((GLA_HEADS, nt, DV, DK), F32)
    return pl.pallas_call(
        body, name="gla_fwd", grid=(GLA_HEADS, nt), out_shape=(o_shape, o_shape, s_shape, s_shape),
        in_specs=in_specs, out_specs=(out_o[0], out_o[1], st_spec, st_spec),
        scratch_shapes=[pltpu.VMEM((DV, DK), F32)] * 2, compiler_params=_cparams(("parallel", "arbitrary")),
    )(z1, z1, z1, g, z1, z1, z1, g)


def _gla_bwd(z1, g, st_f, st_b, do, dims, nt, nlat):
    DK, DV, q_off, k_off = dims
    KEY, VAL, TA = GLA_HEADS * DK, GLA_HEADS * DV, nt * TR
    pos = lambda s: nt - 1 - s
    in_specs, out_specs = [], []
    for d, tile in enumerate(_gla_tiles(nt, nlat)):
        at = lambda w, f, t=tile: pl.BlockSpec((TR, w), lambda h, s: (t(pos(s)), f(h)))
        in_specs += [at(DK, lambda h: q_off // DK + h), at(DK, lambda h: k_off // DK + h), at(DV, lambda h: h),
                     at(DK, lambda h, d=d: d * GLA_HEADS + h),
                     pl.BlockSpec((None, None, DV, DK), lambda h, s: (h, pos(s), 0, 0)), at(DV, lambda h: h)]
        out_specs += [at(DK, lambda h: h), at(DK, lambda h: h), at(DV, lambda h: h), at(DK, lambda h: h)]

    def one(refs, outs, dst, rev):
        q_ref, k_ref, v_ref, g_ref, st_ref, do_ref = refs
        _, pull = jax.vjp(lambda st, q, k, v, gg: _gla_tile(st, q, k, v, gg, rev, DK ** -0.5),
                          st_ref[...], q_ref[...], k_ref[...], v_ref[...], g_ref[...])
        grads = pull((dst[...], do_ref[...]))
        dst[...] = grads[0]
        for o_ref, val in zip(outs, grads[1:]):
            o_ref[...] = val

    def body(*refs):
        dst_f, dst_b = refs[20:]

        @pl.when(pl.program_id(1) == 0)
        def _():
            dst_f[...] = jnp.zeros_like(dst_f)
            dst_b[...] = jnp.zeros_like(dst_b)

        one(refs[0:6], refs[12:16], dst_f, False)
        one(refs[6:12], refs[16:20], dst_b, True)

    shapes = [jax.ShapeDtypeStruct((TA, KEY), F32), jax.ShapeDtypeStruct((TA, KEY), F32),
              jax.ShapeDtypeStruct((TA, VAL), F32), jax.ShapeDtypeStruct((TA, KEY), F32)]
    res = pl.pallas_call(
        body, name="gla_bwd", grid=(GLA_HEADS, nt), out_shape=shapes * 2, in_specs=in_specs, out_specs=out_specs,
        scratch_shapes=[pltpu.VMEM((DV, DK), F32)] * 2, compiler_params=_cparams(("parallel", "arbitrary")),
    )(z1, z1, z1, g, st_f, do, z1, z1, z1, g, st_b, do)
    return res[:4], res[4:]


def _sum_slots(name, arr):
    S, R, C = arr.shape
    tr = _tile(R, max(16, (2 ** 19) // max(C, 1) // 16 * 16), 16)

    def body(a_ref, o_ref):
        acc = a_ref[0].astype(F32)
        for s in range(1, S):
            acc = acc + a_ref[s].astype(F32)
        o_ref[...] = acc

    return pl.pallas_call(
        body, name=name, grid=(R // tr,), out_shape=jax.ShapeDtypeStruct((R, C), F32),
        in_specs=[pl.BlockSpec((S, tr, C), lambda i: (0, i, 0))], out_specs=pl.BlockSpec((tr, C), lambda i: (i, 0)),
        compiler_params=_cparams(("parallel",)),
    )(arr)


def _sum_pair(name, a, b, out_dtype):
    shape, C = a.shape, a.shape[-1]
    a2, b2 = a.reshape(-1, C), b.reshape(-1, C)
    R = a2.shape[0]
    tr = _tile(R, max(16, (2 ** 19) // max(C, 1) // 16 * 16), 16)
    fn = lambda i, p, q: (p.astype(F32) + q.astype(F32),)
    (res,) = _rowwise(name, fn, R // tr, tr, [("row", a2, 0, C), ("row", b2, 0, C)], [("row", R, C, out_dtype)])
    return res.reshape(shape)


def _adam_math(w, g, m, v):
    m2 = ADAM_B1 * m + (1.0 - ADAM_B1) * g
    v2 = ADAM_B2 * v + (1.0 - ADAM_B2) * (g * g)
    m_hat = m2 / (1.0 - ADAM_B1 ** ADAM_STEP)
    v_hat = v2 / (1.0 - ADAM_B2 ** ADAM_STEP)
    return -ADAM_LR * (m_hat / (jnp.sqrt(v_hat) + ADAM_EPS) + ADAM_WD * w), m2, v2


def _adam(name, w, gs, m, v):
    R, C = w.shape
    tr = _tile(R, max(8, (2 ** 19) // max(C, 1) // 8 * 8), 8)
    ng = len(gs)

    def fn(i, w_, *rest):
        g = rest[0] if ng == 1 else rest[0] + rest[1]
        d, m2, v2 = _adam_math(w_, g, rest[ng], rest[ng + 1])
        return g, d, m2, v2

    ins = [("row", a, 0, C) for a in (w, *gs, m, v)]
    return _rowwise(name, fn, R // tr, tr, ins, [("row", R, C, F32)] * 4)


def _ffn_fwd(tag, h, w_up, cw, cb, w_down, n_rows, T, TA, F):
    tr = 128
    u = _matmul(tag + "_up", h, w_up, "nn", n_rows, 2 * F, h.shape[1])
    cwg, cwv, cbg, cbv = cw[:, :F], cw[:, F:], cb[:, :F], cb[:, F:]

    def fn(i, ug, uv, pg, pv, ng, nv, wg, wv, bg, bv):
        prev, nxt = _shifters(i, tr, T, TA)
        cg = wg[0:1] * prev(ug, pg) + wg[1:2] * ug + wg[2:3] * nxt(ug, ng) + bg
        cv = wv[0:1] * prev(uv, pv) + wv[1:2] * uv + wv[2:3] * nxt(uv, nv) + bv
        return (_silu(cg) * cv,)

    ins = [("row", u, 0, F), ("row", u, 1, F), ("prev", u, 0, F), ("prev", u, 1, F), ("next", u, 0, F), ("next", u, 1, F),
           ("full", cwg), ("full", cwv), ("full", cbg), ("full", cbv)]
    (act,) = _rowwise(tag + "_conv", fn, n_rows // tr, tr, ins, [("row", n_rows, F, BF16)])
    f = _matmul(tag + "_down", act, w_down, "nn", n_rows, w_down.shape[1], F)
    return u, act, f


def _ffn_bwd(tag, df, h, u, act, w_up, cw, cb, w_down, n_rows, T, TA, F):
    tr = 128
    D = w_down.shape[1]
    dact = _matmul(tag + "_ddown_x", df, w_down, "nt", n_rows, F, D)
    g_down = _matmul(tag + "_ddown_w", act, df, "tn", F, D, n_rows, out_dtype=BF16)
    cwg, cwv, cbg, cbv = cw[:, :F], cw[:, F:], cb[:, :F], cb[:, F:]

    def fn_a(i, ug, uv, pg, pv, ng, nv, da, wg, wv, bg, bv):
        prev, nxt = _shifters(i, tr, T, TA)
        ugp, ugn, uvp, uvn = prev(ug, pg), nxt(ug, ng), prev(uv, pv), nxt(uv, nv)
        cg = wg[0:1] * ugp + wg[1:2] * ug + wg[2:3] * ugn + bg
        cv = wv[0:1] * uvp + wv[1:2] * uv + wv[2:3] * uvn + bv
        sg = jax.nn.sigmoid(cg)
        dcv = da * (cg * sg)
        dcg = da * cv * (sg * (1.0 + cg * (1.0 - sg)))
        return (jnp.concatenate([dcg, dcv], axis=1),
                _colsum(ugp * dcg), _colsum(ug * dcg), _colsum(ugn * dcg),
                _colsum(uvp * dcv), _colsum(uv * dcv), _colsum(uvn * dcv), _colsum(dcg), _colsum(dcv))

    ins = [("row", u, 0, F), ("row", u, 1, F), ("prev", u, 0, F), ("prev", u, 1, F), ("next", u, 0, F), ("next", u, 1, F),
           ("row", dact, 0, F), ("full", cwg), ("full", cwv), ("full", cbg), ("full", cbv)]
    res = _rowwise(tag + "_dconv_a", fn_a, n_rows // tr, tr, ins, [("row", n_rows, 2 * F, F32)] + [("acc", 1, F)] * 8)
    duc = res[0]
    g_cw = jnp.concatenate([jnp.concatenate(res[1:4], axis=0), jnp.concatenate(res[4:7], axis=0)], axis=1)
    g_cb = jnp.concatenate([res[7], res[8]], axis=1)

    def fn_b(i, dg, dv, pg, pv, ng, nv, wg, wv):
        prev, nxt = _shifters(i, tr, T, TA)
        dug = wg[0:1] * nxt(dg, ng) + wg[1:2] * dg + wg[2:3] * prev(dg, pg)
        duv = wv[0:1] * nxt(dv, nv) + wv[1:2] * dv + wv[2:3] * prev(dv, pv)
        return (jnp.concatenate([dug, duv], axis=1),)

    ins = [("row", duc, 0, F), ("row", duc, 1, F), ("prev", duc, 0, F), ("prev", duc, 1, F), ("next", duc, 0, F),
           ("next", duc, 1, F), ("full", cwg), ("full", cwv)]
    (du,) = _rowwise(tag + "_dconv_b", fn_b, n_rows // tr, tr, ins, [("row", n_rows, 2 * F, BF16)])
    dh = _matmul(tag + "_dup_x", du, w_up, "nt", n_rows, D, 2 * F)
    g_up = _matmul(tag + "_dup_w", h, du, "tn", D, 2 * F, n_rows, out_dtype=BF16)
    return dh, g_up, g_cw, g_cb, g_down


def _cols_to_shards(g):
    r, c = g.shape
    return g.reshape(r, 4, c // 4).transpose(1, 0, 2)


def _shards_to_cols(s):
    return s.transpose(1, 0, 2).reshape(s.shape[1], 4 * s.shape[2])


def kernel(x, c, ctx, c_ctx, l0_ada_w, l0_ada_b, l0_norm1, l0_w_in, l0_conv_a, l0_q_norm, l0_w_qb, l0_kv_norm, l0_w_kvb, l0_w_out, l0_norm2, l0_ffn_up, l0_ffn_conv_w, l0_ffn_conv_b, l0_ffn_down, l1_ada_w, l1_ada_b, l1_norm1, l1_w_in, l1_gate_fw_w, l1_gate_fw_b, l1_gate_bw_w, l1_gate_bw_b, l1_o_norm, l1_w_out, l1_norm2, l1_ffn_up, l1_ffn_conv_w, l1_ffn_conv_b, l1_ffn_down, final_norm, loss_target, m_c_ctx, m_l0_ada_w, m_l0_ada_b, m_l0_norm1, m_l0_w_in, m_l0_conv_a, m_l0_q_norm, m_l0_w_qb, m_l0_kv_norm, m_l0_w_kvb, m_l0_w_out, m_l0_norm2, m_l0_ffn_up, m_l0_ffn_conv_w, m_l0_ffn_conv_b, m_l0_ffn_down, m_l1_ada_w, m_l1_ada_b, m_l1_norm1, m_l1_w_in, m_l1_gate_fw_w, m_l1_gate_fw_b, m_l1_gate_bw_w, m_l1_gate_bw_b, m_l1_o_norm, m_l1_w_out, m_l1_norm2, m_l1_ffn_up, m_l1_ffn_conv_w, m_l1_ffn_conv_b, m_l1_ffn_down, m_final_norm, v_c_ctx, v_l0_ada_w, v_l0_ada_b, v_l0_norm1, v_l0_w_in, v_l0_conv_a, v_l0_q_norm, v_l0_w_qb, v_l0_kv_norm, v_l0_w_kvb, v_l0_w_out, v_l0_norm2, v_l0_ffn_up, v_l0_ffn_conv_w, v_l0_ffn_conv_b, v_l0_ffn_down, v_l1_ada_w, v_l1_ada_b, v_l1_norm1, v_l1_w_in, v_l1_gate_fw_w, v_l1_gate_fw_b, v_l1_gate_bw_w, v_l1_gate_bw_b, v_l1_o_norm, v_l1_w_out, v_l1_norm2, v_l1_ffn_up, v_l1_ffn_conv_w, v_l1_ffn_conv_b, v_l1_ffn_down, v_final_norm):
    W = dict(c_ctx=c_ctx, l0_ada_w=l0_ada_w, l0_ada_b=l0_ada_b, l0_norm1=l0_norm1, l0_w_in=l0_w_in, l0_conv_a=l0_conv_a, l0_q_norm=l0_q_norm, l0_w_qb=l0_w_qb, l0_kv_norm=l0_kv_norm, l0_w_kvb=l0_w_kvb, l0_w_out=l0_w_out, l0_norm2=l0_norm2, l0_ffn_up=l0_ffn_up, l0_ffn_conv_w=l0_ffn_conv_w, l0_ffn_conv_b=l0_ffn_conv_b, l0_ffn_down=l0_ffn_down, l1_ada_w=l1_ada_w, l1_ada_b=l1_ada_b, l1_norm1=l1_norm1, l1_w_in=l1_w_in, l1_gate_fw_w=l1_gate_fw_w, l1_gate_fw_b=l1_gate_fw_b, l1_gate_bw_w=l1_gate_bw_w, l1_gate_bw_b=l1_gate_bw_b, l1_o_norm=l1_o_norm, l1_w_out=l1_w_out, l1_norm2=l1_norm2, l1_ffn_up=l1_ffn_up, l1_ffn_conv_w=l1_ffn_conv_w, l1_ffn_conv_b=l1_ffn_conv_b, l1_ffn_down=l1_ffn_down, final_norm=final_norm)
    MO = dict(c_ctx=m_c_ctx, l0_ada_w=m_l0_ada_w, l0_ada_b=m_l0_ada_b, l0_norm1=m_l0_norm1, l0_w_in=m_l0_w_in, l0_conv_a=m_l0_conv_a, l0_q_norm=m_l0_q_norm, l0_w_qb=m_l0_w_qb, l0_kv_norm=m_l0_kv_norm, l0_w_kvb=m_l0_w_kvb, l0_w_out=m_l0_w_out, l0_norm2=m_l0_norm2, l0_ffn_up=m_l0_ffn_up, l0_ffn_conv_w=m_l0_ffn_conv_w, l0_ffn_conv_b=m_l0_ffn_conv_b, l0_ffn_down=m_l0_ffn_down, l1_ada_w=m_l1_ada_w, l1_ada_b=m_l1_ada_b, l1_norm1=m_l1_norm1, l1_w_in=m_l1_w_in, l1_gate_fw_w=m_l1_gate_fw_w, l1_gate_fw_b=m_l1_gate_fw_b, l1_gate_bw_w=m_l1_gate_bw_w, l1_gate_bw_b=m_l1_gate_bw_b, l1_o_norm=m_l1_o_norm, l1_w_out=m_l1_w_out, l1_norm2=m_l1_norm2, l1_ffn_up=m_l1_ffn_up, l1_ffn_conv_w=m_l1_ffn_conv_w, l1_ffn_conv_b=m_l1_ffn_conv_b, l1_ffn_down=m_l1_ffn_down, final_norm=m_final_norm)
    VO = dict(c_ctx=v_c_ctx, l0_ada_w=v_l0_ada_w, l0_ada_b=v_l0_ada_b, l0_norm1=v_l0_norm1, l0_w_in=v_l0_w_in, l0_conv_a=v_l0_conv_a, l0_q_norm=v_l0_q_norm, l0_w_qb=v_l0_w_qb, l0_kv_norm=v_l0_kv_norm, l0_w_kvb=v_l0_w_kvb, l0_w_out=v_l0_w_out, l0_norm2=v_l0_norm2, l0_ffn_up=v_l0_ffn_up, l0_ffn_conv_w=v_l0_ffn_conv_w, l0_ffn_conv_b=v_l0_ffn_conv_b, l0_ffn_down=v_l0_ffn_down, l1_ada_w=v_l1_ada_w, l1_ada_b=v_l1_ada_b, l1_norm1=v_l1_norm1, l1_w_in=v_l1_w_in, l1_gate_fw_w=v_l1_gate_fw_w, l1_gate_fw_b=v_l1_gate_fw_b, l1_gate_bw_w=v_l1_gate_bw_w, l1_gate_bw_b=v_l1_gate_bw_b, l1_o_norm=v_l1_o_norm, l1_w_out=v_l1_w_out, l1_norm2=v_l1_norm2, l1_ffn_up=v_l1_ffn_up, l1_ffn_conv_w=v_l1_ffn_conv_w, l1_ffn_conv_b=v_l1_ffn_conv_b, l1_ffn_down=v_l1_ffn_down, final_norm=v_final_norm)
    names = list(W)

    T, D = x.shape[1], x.shape[2]
    TC = ctx.shape[1]
    assert TC == TR and T % TR == 0
    TA = T + TC
    nt, nlat = TA // TR, T // TR
    CC = D // 2
    F = l0_ffn_down.shape[0] * 4
    DK, DV = D // 2 // GLA_HEADS, D // GLA_HEADS
    KEY, VAL = GLA_HEADS * DK, GLA_HEADS * DV
    ADA_S = l0_ada_w.shape[1]
    H = MLA_HEADS
    my_x, my_y, my_c = lax.axis_index("x"), lax.axis_index("y"), lax.axis_index("c")
    chip, dev = 2 * my_x + my_y, 4 * my_x + 2 * my_y + my_c
    row = lambda a: a.reshape(1, -1)

    big = ["l0_w_in", "l0_w_qb", "l0_w_kvb", "l0_w_out", "l0_ffn_up", "l0_ffn_down",
           "l1_w_in", "l1_w_out", "l1_ffn_up", "l1_ffn_down"]
    small_sh = ["l0_conv_a", "l0_ffn_conv_w", "l1_gate_fw_w", "l1_gate_bw_w", "l1_ffn_conv_w"]
    own_half = lambda a, k: lax.dynamic_index_in_dim(a, k, 0, keepdims=False)
    set_own = lambda arr, val, k: lax.dynamic_update_index_in_dim(arr, val[None], k, 0)
    in_core_order = lambda mine, other: jnp.where(my_c == 0, jnp.stack([mine, other], 1), jnp.stack([other, mine], 1))
    early, late = big[:3], big[3:]
    w_half = {n: own_half(W[n].astype(BF16).reshape(2, W[n].shape[0] // 2, -1), my_c) for n in big}
    col_sh = {"l0_w_in", "l0_w_qb", "l0_w_kvb", "l0_ffn_up", "l1_w_in", "l1_ffn_up", "l0_conv_a", "l0_ffn_conv_w",
              "l1_gate_fw_w", "l1_gate_bw_w", "l1_ffn_conv_w"}
    whole_w = lambda n, s: _shards_to_cols(s) if n in col_sh else s.reshape(-1, s.shape[-1])

    def pair_up(tag, group, got):
        got = [set_own(g, w_half[n], chip) for n, g in zip(group, got)]
        other = _exchange("pair_weights_" + tag, got, "swap_c")
        return {n: whole_w(n, in_core_order(g, o).reshape(4, 2 * g.shape[1], g.shape[2])) for n, g, o in zip(group, got, other)}

    full = pair_up("a", early, _exchange("gather_weights", [w_half[n] for n in early], "ag_xy", fill_own=False))
    for n, s in zip(small_sh, _exchange("gather_small_weights", [W[n] for n in small_sh], "ag_xy")):
        full[n] = whole_w(n, s)
    (c_all,) = _exchange("gather_c", [c], "ag_all")

    o_kv, o_kr, o_ql, o_ax = 0, KV_LORA, KV_LORA + QK_ROPE, KV_LORA + QK_ROPE + Q_LORA
    n_sel = 3 * CC + Q_LORA + KV_LORA + QK_ROPE
    W0 = -(-(n_sel + QK_ROPE) // 256) * 256
    QOFF0, KVOFF0, KROFF0 = 3 * CC, 3 * CC + Q_LORA, 3 * CC + Q_LORA + KV_LORA
    w_in0 = full["l0_w_in"]
    w_kr = w_in0[:, o_kr:o_kr + QK_ROPE]
    w_in0p = jnp.concatenate([w_in0[:, o_ax:], w_in0[:, o_ql:o_ax], w_in0[:, :o_kr], w_kr, _rot_cols(w_kr),
                              jnp.zeros((D, W0 - n_sel - QK_ROPE), BF16)], axis=1)
    wq3 = full["l0_w_qb"].reshape(Q_LORA, H, QK_NOPE + QK_ROPE)
    w_qbp = jnp.concatenate([wq3, _rot_cols(wq3[:, :, QK_NOPE:])], axis=2).reshape(Q_LORA, H * QK_PAD)
    o_v1, o_lr1, o_q1, o_og1 = KEY, KEY + VAL, KEY + VAL + 2 * GATE_RANK, 2 * KEY + VAL + 2 * GATE_RANK
    W1 = 2 * VAL + 2 * KEY + LANES
    OGOFF, KOFF1, QOFF1, LROFF = VAL, 2 * VAL, 2 * VAL + KEY, 2 * VAL + 2 * KEY
    w_gate = jnp.zeros((LANES, 2 * KEY), F32)
    w_gate = w_gate.at[:GATE_RANK, :KEY].set(full["l1_gate_fw_w"]).at[GATE_RANK:2 * GATE_RANK, KEY:].set(full["l1_gate_bw_w"])
    b_gate = jnp.concatenate([l1_gate_fw_b, l1_gate_bw_b]).reshape(1, -1)

    pos = np.arange(T)
    inv = ROPE_THETA ** (-np.arange(0, QK_ROPE // 2, 2, dtype=np.float32) / (QK_ROPE // 2))
    ar, ac = (pos // GRID_W).astype(np.float32)[:, None] * inv, (pos % GRID_W).astype(np.float32)[:, None] * inv
    ang = jnp.asarray(np.concatenate([ar, ar, ac, ac], axis=-1).astype(np.float32))
    zpad = jnp.zeros((TA, LANES - QK_ROPE), F32)
    cos_t = jnp.concatenate([jnp.concatenate([jnp.cos(ang), jnp.ones((TC, QK_ROPE), F32)], axis=0), zpad], axis=1)
    sin_t = jnp.concatenate([jnp.concatenate([jnp.sin(ang), jnp.zeros((TC, QK_ROPE), F32)], axis=0), zpad], axis=1)

    c16 = jnp.concatenate([c_all.reshape(8, D), c_ctx.reshape(1, D), jnp.zeros((7, D), F32)], axis=0)
    (act16,) = _rowwise("silu_c", _f_silu, 1, 16, [("row", c16, 0, D)], [("row", 16, D, F32)])
    mod_parts = [_matmul("ada_fwd", act16, W["l%d_ada_w" % l], "nn", 16, ADA_S, D, tn_cap=512) for l in (0, 1)]
    mod_g = _exchange("gather_mod", mod_parts, "ag_xy")
    mods = []
    for l in (0, 1):
        mfull = _shards_to_cols(mod_g[l]) + W["l%d_ada_b" % l][None, :]
        mine = lax.dynamic_slice_in_dim(mfull, dev, 1, axis=0)
        mods.append(jnp.concatenate([mine, mfull[8:9]], axis=0).reshape(2, N_MOD, D))
    P = lambda l, k: mods[l][:, k:k + 1, :]

    X = jnp.concatenate([x[0], ctx[0]], axis=0)
    n1_0, n2_0, n1_1, n2_1 = row(l0_norm1), row(l0_norm2), row(l1_norm1), row(l1_norm2)
    (h0,) = _rowwise("l0_mod1", _f_mod, nt, TR, [("row", X, 0, D), ("full", n1_0), ("grp", P(0, 0)), ("grp", P(0, 1))],
                     [("row", TA, D, BF16)], nlat)
    z0 = _matmul("l0_in", h0, w_in0p, "nn", TA, W0, D)
    kvn_w, qn_w = row(l0_kv_norm), row(l0_q_norm)
    norm_ins = [("row", z0, KVOFF0 // KV_LORA, KV_LORA), ("row", z0, QOFF0 // Q_LORA, Q_LORA), ("full", kvn_w), ("full", qn_w)]
    kvn, qn = _rowwise("l0_latnorm", _f_norms, nt, TR, norm_ins, [("row", TA, KV_LORA, BF16), ("row", TA, Q_LORA, BF16)])
    kv = _matmul("l0_kvb", kvn, full["l0_w_kvb"], "nn", TA, H * QK_PAD, KV_LORA, out_dtype=BF16)
    qraw = _matmul("l0_qb", qn, w_qbp, "nn", TA, H * QK_PAD, Q_LORA)

    def f_rope(i, qr, kvv, krr, cs, sn):
        kr = _rope128(krr, cs, sn).astype(BF16)
        qs, ks = [], []
        for h in range(H):
            qs += [qr[:, h * QK_PAD:h * QK_PAD + QK_NOPE], _rope128(qr[:, h * QK_PAD + QK_NOPE:(h + 1) * QK_PAD], cs, sn)]
            ks += [kvv[:, h * QK_PAD:h * QK_PAD + QK_NOPE], kr]
        return jnp.concatenate(qs, axis=1), jnp.concatenate(ks, axis=1)

    rope_ins = [("row", qraw, 0, H * QK_PAD), ("row", kv, 0, H * QK_PAD), ("row", z0, KROFF0 // LANES, LANES),
                ("row", cos_t, 0, LANES), ("row", sin_t, 0, LANES)]
    q, kf = _rowwise("l0_rope", f_rope, nt, TR, rope_ins, [("row", TA, H * QK_PAD, BF16), ("row", TA, H * QK_PAD, BF16)])
    o_lat, lse, got_late = _mla_fwd_lat(q, kf, kv, T, TA, _Hosted([w_half[n] for n in late], "ag_xy"))
    o_ctx = _mla_fwd_ctx(q, kf, kv, T)
    full.update(pair_up("b", late, got_late))
    w_in1 = full["l1_w_in"]
    w_in1p = jnp.concatenate([w_in1[:, o_v1:o_lr1], w_in1[:, o_og1:], w_in1[:, :o_v1], w_in1[:, o_q1:o_og1],
                              w_in1[:, o_lr1:o_q1], jnp.zeros((D, LANES - 2 * GATE_RANK), BF16)], axis=1)

    conv_a = full["l0_conv_a"]

    def f_conva(i, ax, ab, ac_, pax, pac, nax, nac, w):
        prev, nxt = _shifters(i, TR, T, TA)
        p = ac_ * ax
        return (ab * (w[0:1] * prev(p, pac * pax) + w[1:2] * p + w[2:3] * nxt(p, nac * nax)),)

    conva_ins = [("row", z0, 0, CC), ("row", z0, 1, CC), ("row", z0, 2, CC), ("prev", z0, 0, CC), ("prev", z0, 2, CC),
                 ("next", z0, 0, CC), ("next", z0, 2, CC), ("full", conv_a)]
    (y_a,) = _rowwise("l0_conva", f_conva, nt, TR, conva_ins, [("row", TA, CC, BF16)])
    cat0 = jnp.concatenate([y_a, jnp.concatenate([o_lat, o_ctx], axis=0)], axis=1)
    y0 = _matmul("l0_out", cat0, full["l0_w_out"], "nn", TA, D, CC + H * V_HEAD)
    resmod_outs = [("row", TA, D, F32), ("row", TA, D, BF16)]
    X1, h2_0 = _rowwise("l0_resmod2", _f_resmod, nt, TR,
                        [("row", X, 0, D), ("row", y0, 0, D), ("grp", P(0, 2)), ("full", n2_0), ("grp", P(0, 3)), ("grp", P(0, 4))],
                        resmod_outs, nlat)
    cw0, cb0 = full["l0_ffn_conv_w"], row(l0_ffn_conv_b)
    u0, act0, f0 = _ffn_fwd("l0_ffn", h2_0, full["l0_ffn_up"], cw0, cb0, full["l0_ffn_down"], TA, T, TA, F)
    X2, h1 = _rowwise("l1_resmod1", _f_resmod, nt, TR,
                      [("row", X1, 0, D), ("row", f0, 0, D), ("grp", P(0, 5)), ("full", n1_1), ("grp", P(1, 0)), ("grp", P(1, 1))],
                      resmod_outs, nlat)

    z1 = _matmul("l1_in", h1, w_in1p, "nn", TA, W1, D)

    def f_gates(i, lr, wg, bg):
        pre = jnp.dot(lr.astype(BF16), wg.astype(BF16), preferred_element_type=F32) + bg
        return (_log_sigmoid(pre) / GATE_NORMALIZER,)

    gate_ins = [("row", z1, LROFF // LANES, LANES), ("full", w_gate), ("full", b_gate)]
    (gates,) = _rowwise("l1_gates", f_gates, nt, TR, gate_ins, [("row", TA, 2 * KEY, F32)])
    gla_dims = (DK, DV, QOFF1, KOFF1)
    o_f, o_b, st_f, st_b = _gla_fwd(z1, gates, gla_dims, nt, nlat)
    onw = row(l1_o_norm)

    def f_glaout(i, of, ob, og, w):
        o = of + ob
        on = jnp.concatenate([_rms(o[:, h * DV:(h + 1) * DV], w) for h in range(GLA_HEADS)], axis=1)
        return (on * _silu(og),)

    glaout_ins = [("row", o_f, 0, VAL), ("row", o_b, 0, VAL), ("row", z1, OGOFF // VAL, VAL), ("full", onw)]
    (go,) = _rowwise("l1_glaout", f_glaout, nlat, TR, glaout_ins, [("row", T, VAL, BF16)])
    y1 = _matmul("l1_out", go, full["l1_w_out"], "nn", T, D, VAL)
    X3, h2_1 = _rowwise("l1_resmod2", _f_resmod, nlat, TR,
                        [("row", X2, 0, D), ("row", y1, 0, D), ("grp", P(1, 2)), ("full", n2_1), ("grp", P(1, 3)), ("grp", P(1, 4))],
                        [("row", T, D, F32), ("row", T, D, BF16)])
    cw1, cb1 = full["l1_ffn_conv_w"], row(l1_ffn_conv_b)
    u1, act1, f1 = _ffn_fwd("l1_ffn", h2_1, full["l1_ffn_up"], cw1, cb1, full["l1_ffn_down"], T, T, T, F)

    fnw = row(final_norm)

    def f_head(i, x3, ff, g5, w, tgt):
        fin = lambda a, b, g_, w_: _rms(a + g_ * b, w_)
        y, pull = jax.vjp(fin, x3, ff, g5, w)
        err = y - tgt
        dx3, dff, dg5, dw = pull(err / D)
        loss = 0.5 * jnp.sum(jnp.mean(err * err, axis=-1, keepdims=True), axis=0, keepdims=True)
        return jnp.broadcast_to(loss, (1, LANES)), dx3, dff, dg5, dw

    head_ins = [("row", X3, 0, D), ("row", f1, 0, D), ("grp", P(1, 5)), ("full", fnw), ("row", loss_target[0], 0, D)]
    loss_acc, dX3, df1, dm5_1, g_final = _rowwise(
        "loss_head", f_head, nlat, TR, head_ins,
        [("acc", 1, LANES), ("row", T, D, F32), ("row", T, D, BF16), ("acc", 1, D), ("acc", 1, D)])
    loss = lax.psum(loss_acc[0, 0], ("x", "y", "c"))

    grads = {"final_norm": g_final}
    dh2_1, grads["l1_ffn_up"], grads["l1_ffn_conv_w"], grads["l1_ffn_conv_b"], grads["l1_ffn_down"] = _ffn_bwd(
        "l1_dffn", df1, h2_1, u1, act1, full["l1_ffn_up"], cw1, cb1, full["l1_ffn_down"], T, T, T, F)
    resmod_bwd = _vjp_of(_f_resmod, 6, (0, 1, 2, 3, 4, 5))
    dX2l, dy1, dm2_1, grads["l1_norm2"], dm3_1, dm4_1 = _rowwise(
        "l1_dresmod2", resmod_bwd, nlat, TR,
        [("row", X2, 0, D), ("row", y1, 0, D), ("grp", P(1, 2)), ("full", n2_1), ("grp", P(1, 3)), ("grp", P(1, 4)),
         ("row", dX3, 0, D), ("row", dh2_1, 0, D)],
        [("row", T, D, F32), ("row", T, D, BF16)] + [("acc", 1, D)] * 4)
    dgo = _matmul("l1_dout_x", dy1, full["l1_w_out"], "nt", T, VAL, D)
    grads["l1_w_out"] = _matmul("l1_dout_w", go, dy1, "tn", VAL, D, T, out_dtype=BF16)

    def f_glaout_bwd(i, of, ob, og, w, d):
        is_ctx = i >= nlat
        _, pull = jax.vjp(lambda a, b, c_, w_: f_glaout(i, a, b, c_, w_)[0], of, ob, og, w)
        dof, _, dog, dw = pull(jnp.where(is_ctx, 0.0, d))
        return dof, dog, dw

    glaout_b_ins = glaout_ins + [("rowc", dgo, 0, VAL, nlat - 1)]
    do_gla, dog, g_onorm = _rowwise("l1_dglaout", f_glaout_bwd, nt, TR, glaout_b_ins,
                                    [("row", TA, VAL, F32), ("row", TA, VAL, F32), ("acc", 1, DV)])
    grads["l1_o_norm"] = g_onorm
    (dq_f, dk_f, dv_f, dg_f), (dq_b, dk_b, dv_b, dg_b) = _gla_bwd(z1, gates, st_f, st_b, do_gla, gla_dims, nt, nlat)

    def f_dz1(i, dvf, dvb, dog_, dkf, dkb, dqf, dqb, dgf, dgb, lr, wg, bg):
        _, pull = jax.vjp(lambda a, b, c_: f_gates(i, a, b, c_)[0], lr, wg, bg)
        dlr, dwg, dbg = pull(jnp.concatenate([dgf, dgb], axis=1))
        return jnp.concatenate([dvf + dvb, dog_, dkf + dkb, dqf + dqb, dlr], axis=1), dwg, dbg

    dz1_ins = [("row", dv_f, 0, VAL), ("row", dv_b, 0, VAL), ("row", dog, 0, VAL), ("row", dk_f, 0, KEY), ("row", dk_b, 0, KEY),
               ("row", dq_f, 0, KEY), ("row", dq_b, 0, KEY), ("row", dg_f, 0, KEY), ("row", dg_b, 0, KEY)] + gate_ins
    dz1, g_wgate, g_bgate = _rowwise("l1_dz", f_dz1, nt, TR, dz1_ins, [("row", TA, W1, BF16), ("acc", LANES, 2 * KEY), ("acc", 1, 2 * KEY)])
    grads["l1_gate_fw_w"], grads["l1_gate_bw_w"] = g_wgate[:GATE_RANK, :KEY], g_wgate[GATE_RANK:2 * GATE_RANK, KEY:]
    grads["l1_gate_fw_b"], grads["l1_gate_bw_b"] = g_bgate[:, :KEY], g_bgate[:, KEY:]
    dh1 = _matmul("l1_din_x", dz1, w_in1p, "nt", TA, D, W1)
    g1p = _matmul("l1_din_w", h1, dz1, "tn", D, W1, TA, out_dtype=BF16)
    grads["l1_w_in"] = jnp.concatenate([g1p[:, KOFF1:QOFF1], g1p[:, :OGOFF], g1p[:, LROFF:LROFF + 2 * GATE_RANK],
                                        g1p[:, QOFF1:LROFF], g1p[:, OGOFF:KOFF1]], axis=1)

    def f_resmod1_bwd(i, x_, y_, g_, w_, sh, sc, dx2, dh):
        return resmod_bwd(i, x_, y_, g_, w_, sh, sc, jnp.where(i >= nlat, 0.0, dx2), dh)

    dX1, df0, dm5_0, grads["l1_norm1"], dm0_1, dm1_1 = _rowwise(
        "l1_dresmod1", f_resmod1_bwd, nt, TR,
        [("row", X1, 0, D), ("row", f0, 0, D), ("grp", P(0, 5)), ("full", n1_1), ("grp", P(1, 0)), ("grp", P(1, 1)),
         ("rowc", dX2l, 0, D, nlat - 1), ("row", dh1, 0, D)],
        [("row", TA, D, F32), ("row", TA, D, BF16), ("accg", 1, D), ("acc", 1, D), ("accg", 1, D), ("accg", 1, D)], nlat)

    dh2_0, grads["l0_ffn_up"], grads["l0_ffn_conv_w"], grads["l0_ffn_conv_b"], grads["l0_ffn_down"] = _ffn_bwd(
        "l0_dffn", df0, h2_0, u0, act0, full["l0_ffn_up"], cw0, cb0, full["l0_ffn_down"], TA, T, TA, F)
    dXd, dy0, dm2_0, grads["l0_norm2"], dm3_0, dm4_0 = _rowwise(
        "l0_dresmod2", resmod_bwd, nt, TR,
        [("row", X, 0, D), ("row", y0, 0, D), ("grp", P(0, 2)), ("full", n2_0), ("grp", P(0, 3)), ("grp", P(0, 4)),
         ("row", dX1, 0, D), ("row", dh2_0, 0, D)],
        [("row", TA, D, F32), ("row", TA, D, BF16), ("accg", 1, D), ("acc", 1, D), ("accg", 1, D), ("accg", 1, D)], nlat)
    dcat = _matmul("l0_dout_x", dy0, full["l0_w_out"], "nt", TA, CC + H * V_HEAD, D)
    grads["l0_w_out"] = _matmul("l0_dout_w", cat0, dy0, "tn", CC + H * V_HEAD, D, TA, out_dtype=BF16)

    def f_conva_bwd(i, ax, ab, ac_, pax, pab, pac, nax, nab, nac, dy, pdy, ndy, w):
        prev, nxt = _shifters(i, TR, T, TA)
        p = ac_ * ax
        pp, pn = prev(p, pac * pax), nxt(p, nac * nax)
        cv = w[0:1] * pp + w[1:2] * p + w[2:3] * pn
        dcv = dy * ab
        dp = w[0:1] * nxt(dcv, ndy * nab) + w[1:2] * dcv + w[2:3] * prev(dcv, pdy * pab)
        return (jnp.concatenate([dp * ac_, dy * cv, dp * ax], axis=1), _colsum(pp * dcv), _colsum(p * dcv), _colsum(pn * dcv))

    conva_b_ins = [("row", z0, 0, CC), ("row", z0, 1, CC), ("row", z0, 2, CC),
                   ("prev", z0, 0, CC), ("prev", z0, 1, CC), ("prev", z0, 2, CC),
                   ("next", z0, 0, CC), ("next", z0, 1, CC), ("next", z0, 2, CC),
                   ("row", dcat, 0, CC), ("prev", dcat, 0, CC), ("next", dcat, 0, CC), ("full", conv_a)]
    dz_a, ga0, ga1, ga2 = _rowwise("l0_dconva", f_conva_bwd, nt, TR, conva_b_ins, [("row", TA, 3 * CC, BF16)] + [("acc", 1, CC)] * 3)
    grads["l0_conv_a"] = jnp.concatenate([ga0, ga1, ga2], axis=0)
    do_cb0 = CC // V_HEAD
    row_sh = {"l0_w_out", "l1_w_out", "l0_ffn_down", "l1_ffn_down"}

    def pair_sums(tag, group):
        res = []
        for n in group:
            s = grads[n].reshape((4, -1) + grads[n].shape[1:]) if n in row_sh else _cols_to_shards(grads[n])
            res.append(s.reshape(4, 2, s.shape[1] // 2, s.shape[2]).transpose(1, 0, 2, 3))
        taken = _exchange("pair_split_grads_" + tag, [own_half(g, 1 - my_c) for g in res], "swap_c")
        return [_sum_pair("pairsum_" + n[3:], own_half(g, my_c), t, BF16) for n, g, t in zip(group, res, taken)]

    pair_late = pair_sums("a", late)
    dq_lat, dk_lat, dv_lat, got = _mla_bwd_lat(q, kf, kv, o_lat, dcat, lse, do_cb0, T, TA, _Hosted(pair_late, "a2a_xy"))
    landed_late = {n: set_own(l, own_half(p, chip), chip) for n, l, p in zip(late, got, pair_late)}
    dq_ctx, dk_ctx, dv_ctx = _mla_bwd_ctx(q, kf, kv, dcat, dk_lat, dv_lat, do_cb0, T)

    def f_rope_bwd(i, dql, dqc, dkl, dkc, dvl, dvc, cs, sn):
        is_ctx = i >= nlat
        dq_, dk_, dv_ = jnp.where(is_ctx, dqc, dql), jnp.where(is_ctx, dkc, dkl), jnp.where(is_ctx, dvc, dvl)
        dqs, dkvs, dkr = [], [], None
        for h in range(H):
            dqs += [dq_[:, h * QK_PAD:h * QK_PAD + QK_NOPE], _rope128_t(dq_[:, h * QK_PAD + QK_NOPE:(h + 1) * QK_PAD], cs, sn)]
            dkvs += [dk_[:, h * QK_PAD:h * QK_PAD + QK_NOPE].astype(BF16), dv_[:, h * V_HEAD:(h + 1) * V_HEAD]]
            part = dk_[:, h * QK_PAD + QK_NOPE:(h + 1) * QK_PAD]
            dkr = part if dkr is None else dkr + part
        return jnp.concatenate(dqs, axis=1), jnp.concatenate(dkvs, axis=1), _rope128_t(dkr, cs, sn)

    drope_ins = [("rowc", dq_lat, 0, H * QK_PAD, nlat - 1), ("full", dq_ctx), ("row", dk_lat, 0, H * QK_PAD), ("full", dk_ctx),
                 ("row", dv_lat, 0, H * V_HEAD), ("full", dv_ctx), ("row", cos_t, 0, LANES), ("row", sin_t, 0, LANES)]
    dqraw, dkv, dz_kr = _rowwise("l0_drope", f_rope_bwd, nt, TR, drope_ins,
                                 [("row", TA, H * QK_PAD, BF16), ("row", TA, H * QK_PAD, BF16), ("row", TA, LANES, BF16)])
    dqn = _matmul("l0_dqb_x", dqraw, w_qbp, "nt", TA, Q_LORA, H * QK_PAD)
    g_wqbp = _matmul("l0_dqb_w", qn, dqraw, "tn", Q_LORA, H * QK_PAD, TA).reshape(Q_LORA, H, QK_PAD)
    g_rope = g_wqbp[:, :, QK_NOPE:QK_NOPE + QK_ROPE] + _rot_cols_t(g_wqbp[:, :, QK_NOPE + QK_ROPE:])
    grads["l0_w_qb"] = jnp.concatenate([g_wqbp[:, :, :QK_NOPE], g_rope], axis=2).reshape(Q_LORA, H * (QK_NOPE + QK_ROPE)).astype(BF16)
    dkvn = _matmul("l0_dkvb_x", dkv, full["l0_w_kvb"], "nt", TA, KV_LORA, H * QK_PAD)
    grads["l0_w_kvb"] = _matmul("l0_dkvb_w", kvn, dkv, "tn", KV_LORA, H * QK_PAD, TA, out_dtype=BF16)
    norms_bwd = _vjp_of(_f_norms, 4, (0, 1, 2, 3))
    dz_kv, dz_q, grads["l0_kv_norm"], grads["l0_q_norm"] = _rowwise(
        "l0_dlatnorm", norms_bwd, nt, TR, norm_ins + [("row", dkvn, 0, KV_LORA), ("row", dqn, 0, Q_LORA)],
        [("row", TA, KV_LORA, BF16), ("row", TA, Q_LORA, BF16), ("acc", 1, KV_LORA), ("acc", 1, Q_LORA)])
    dz0 = jnp.concatenate([dz_a, dz_q, dz_kv, dz_kr, jnp.zeros((TA, W0 - KROFF0 - LANES), BF16)], axis=1)
    dh0 = _matmul("l0_din_x", dz0, w_in0p, "nt", TA, D, W0)
    g0p = _matmul("l0_din_w", h0, dz0, "tn", D, W0, TA)
    g_kr = g0p[:, KROFF0:KROFF0 + QK_ROPE] + _rot_cols_t(g0p[:, KROFF0 + QK_ROPE:KROFF0 + 2 * QK_ROPE])
    grads["l0_w_in"] = jnp.concatenate([g0p[:, KVOFF0:KROFF0], g_kr, g0p[:, QOFF0:KVOFF0], g0p[:, :QOFF0]], axis=1).astype(BF16)

    def f_mod_bwd(i, x_, w_, sh, sc, dh, dxd):
        _, pull = jax.vjp(lambda a, b, c_, d_: _modulate(a, b, c_, d_), x_, w_, sh, sc)
        dx, dw, dsh, dsc = pull(dh)
        return dx + dxd, dw, dsh, dsc

    dXf, grads["l0_norm1"], dm0_0, dm1_0 = _rowwise(
        "l0_dmod1", f_mod_bwd, nt, TR,
        [("row", X, 0, D), ("full", n1_0), ("grp", P(0, 0)), ("grp", P(0, 1)), ("row", dh0, 0, D), ("row", dXd, 0, D)],
        [("row", TA, D, F32), ("acc", 1, D), ("accg", 1, D), ("accg", 1, D)], nlat)
    grad_x = dXf[:T][None]

    zD = jnp.zeros((1, D), F32)
    lat = lambda a: a[0] if a.ndim == 3 else a
    cxt = lambda a: a[1] if a.ndim == 3 else zD
    dmods = []
    for parts in ((dm0_0, dm1_0, dm2_0, dm3_0, dm4_0, dm5_0), (dm0_1, dm1_1, dm2_1, dm3_1, dm4_1, dm5_1)):
        dmods.append((jnp.concatenate([lat(a) for a in parts], axis=1), jnp.concatenate([cxt(a) for a in parts], axis=1)))
    small_names = ["l0_norm1", "l0_norm2", "l0_kv_norm", "l0_q_norm", "l0_conv_a", "l0_ffn_conv_w", "l0_ffn_conv_b",
                   "l1_norm1", "l1_norm2", "l1_o_norm", "l1_gate_fw_w", "l1_gate_fw_b", "l1_gate_bw_w", "l1_gate_bw_b",
                   "l1_ffn_conv_w", "l1_ffn_conv_b", "final_norm"]
    pieces = [("dm0", dmods[0][0]), ("dmc0", dmods[0][1]), ("dm1", dmods[1][0]), ("dmc1", dmods[1][1])]
    pieces += [("l0_ada_b", dmods[0][0] + dmods[0][1]), ("l1_ada_b", dmods[1][0] + dmods[1][1])]
    pieces += [(n, grads[n]) for n in small_names]
    offs, cur = {}, 0
    for n, a in pieces:
        offs[n] = (cur, a.size, a.shape)
        cur += -(-a.size // LANES) * LANES
    n_pad = -(-cur // 1024) * 1024
    flat = jnp.concatenate([jnp.pad(a.reshape(-1), (0, -a.size % LANES)) for _, a in pieces] + [jnp.zeros((n_pad - cur,), F32)])
    (small_g,) = _exchange("gather_small", [flat.reshape(n_pad // LANES, LANES)], "ag_all")
    small_sum = _sum_slots("sum_small", small_g).reshape(-1)
    small_all = small_g.reshape(8, -1)
    take = lambda n: small_sum[offs[n][0]:offs[n][0] + offs[n][1]].reshape(offs[n][2])

    cc_parts = []
    for l in (0, 1):
        o_m, sz, _ = offs["dm%d" % l]
        d16 = jnp.concatenate([small_all[:, o_m:o_m + sz], take("dmc%d" % l).reshape(1, -1), jnp.zeros((7, sz), F32)], axis=0)
        d16 = lax.dynamic_slice_in_dim(d16.reshape(16, 4, ADA_S), chip, 1, axis=1).reshape(16, ADA_S)
        grads["l%d_ada_w" % l] = _matmul("ada_dw", act16, d16, "tn", D, ADA_S, 16, tn_cap=512)
        cc_parts.append(_matmul("ada_dx", d16, W["l%d_ada_w" % l], "nt", 16, D, ADA_S, tn_cap=512))
        grads["l%d_ada_b" % l] = take("l%d_ada_b" % l).reshape(-1)
    cc_g = _exchange("gather_cc", cc_parts, "ag_xy")

    def cc_call():
        def body(a_ref, b_ref, c_ref, o_ref):
            tot = a_ref[0, 8:9, :] + b_ref[0, 8:9, :]
            for j in range(1, 4):
                tot = tot + (a_ref[j, 8:9, :] + b_ref[j, 8:9, :])
            _, pull = jax.vjp(_silu, c_ref[...])
            o_ref[...] = pull(tot)[0]
        return pl.pallas_call(body, name="c_ctx_grad", out_shape=jax.ShapeDtypeStruct((1, D), F32))(cc_g[0], cc_g[1], row(c_ctx))

    grads["c_ctx"] = cc_call().reshape(-1)

    pair_early = pair_sums("b", early)
    landed = dict(landed_late)
    for n, l, p in zip(early, _exchange("scatter_grads", pair_early, "a2a_xy", fill_own=False), pair_early):
        landed[n] = set_own(l, own_half(p, chip), chip)
    halves = [_sum_slots("sum_" + n[3:], landed[n]) for n in big]
    others = _exchange("pair_grads", halves, "swap_c")

    out_g, out_d, out_m, out_v = {}, {}, {}, {}
    for n, mine, other in zip(big, halves, others):
        g2 = jnp.where(my_c == 0, jnp.concatenate([mine, other], 0), jnp.concatenate([other, mine], 0))
        out_g[n], out_d[n], out_m[n], out_v[n] = _adam("adam_" + n[3:], W[n], [g2], MO[n], VO[n])
    for l in (0, 1):
        n = "l%d_ada_w" % l
        out_g[n], out_d[n], out_m[n], out_v[n] = _adam("adam_ada_w", W[n], [grads[n]], MO[n], VO[n])
    rest = [n for n in names if n not in out_g]
    g_rest = {}
    for n in rest:
        if n == "c_ctx" or n.endswith("ada_b"):
            g_rest[n] = grads[n]
        elif n in small_sh:
            gfull = take(n)
            cs = gfull.shape[1] // 4
            g_rest[n] = lax.dynamic_slice_in_dim(gfull, chip * cs, cs, axis=1)
        else:
            g_rest[n] = take(n).reshape(W[n].shape)
    sizes = [W[n].size for n in rest]
    tot = sum(-(-s // LANES) * LANES for s in sizes)
    tot_pad = -(-tot // 1024) * 1024

    def pack(d):
        parts = [jnp.pad(d[n].reshape(-1), (0, -d[n].size % LANES)) for n in rest]
        return jnp.concatenate(parts + [jnp.zeros((tot_pad - tot,), F32)]).reshape(tot_pad // LANES, LANES)

    packed = _adam("adam_small", pack(W), [pack(g_rest)], pack(MO), pack(VO))
    cur = 0
    for n, s in zip(rest, sizes):
        for dst, arr in zip((out_g, out_d, out_m, out_v), packed):
            dst[n] = arr.reshape(-1)[cur:cur + s].reshape(W[n].shape)
        cur += -(-s // LANES) * LANES

    return (loss, grad_x, *[out_g[n] for n in names], *[out_d[n] for n in names],
            *[out_m[n] for n in names], *[out_v[n] for n in names])
```

```python
import numpy as np

import jax
import jax.numpy as jnp
from jax import lax
from jax.experimental import pallas as pl
from jax.experimental.pallas import tpu as pltpu

F32, BF16 = jnp.float32, jnp.bfloat16
HIGHEST = lax.Precision.HIGHEST
MESH_ID = pl.DeviceIdType.MESH

EPS = 1e-6
N_MOD = 6
MLA_HEADS, QK_NOPE, QK_ROPE, V_HEAD, Q_LORA, KV_LORA = 8, 128, 64, 128, 512, 256
QK_PAD = 2 * QK_NOPE
ROPE_THETA, GRID_W = 10000.0, 64
GLA_HEADS, GATE_RANK, GATE_NORMALIZER, CHUNK = 4, 16, 16.0, 64
ADAM_LR, ADAM_B1, ADAM_B2, ADAM_EPS, ADAM_WD, ADAM_STEP = 0.001, 0.9, 0.999, 1e-08, 0.01, 10

LANES = 128
TR = 256
V7X_VMEM_BYTES = 64 * 2 ** 20
VMEM_LIMIT = V7X_VMEM_BYTES - 8 * 2 ** 20

NT_DIMS = (((1,), (1,)), ((), ()))
TN_DIMS = (((0,), (0,)), ((), ()))
NN_DIMS = (((1,), (0,)), ((), ()))

def _rot_cols(w):
    q = QK_ROPE // 4
    return jnp.concatenate([-w[..., q:2 * q], w[..., :q], -w[..., 3 * q:], w[..., 2 * q:3 * q]], axis=-1)


def _rot_cols_t(g):
    q = QK_ROPE // 4
    return jnp.concatenate([g[..., q:2 * q], -g[..., :q], g[..., 3 * q:], -g[..., 2 * q:3 * q]], axis=-1)


def _cparams(sem=None):
    return pltpu.CompilerParams(dimension_semantics=sem, vmem_limit_bytes=VMEM_LIMIT)


def _tile(dim, cap, quantum=LANES):
    if dim <= cap:
        return dim
    t = (cap // quantum) * quantum
    while t >= quantum:
        if dim % t == 0:
            return t
        t -= quantum
    return dim


_REL_XY = ((1, 0, 0), (0, 1, 0), (1, 1, 0))
_REL_ALL = tuple((a, b, c) for a in (0, 1) for b in (0, 1) for c in (0, 1))[1:]
_REL_C = ((0, 0, 1),)


_RELS = {"ag_xy": _REL_XY, "a2a_xy": _REL_XY, "ag_all": _REL_ALL, "swap_c": _REL_C}
_LEAD = {"ag_xy": (4,), "ag_all": (8,), "a2a_xy": (), "swap_c": ()}


def _remote_copies(mode, ins, outs, send, recv):
    x, y, c = lax.axis_index("x"), lax.axis_index("y"), lax.axis_index("c")
    chip, dev = 2 * x + y, 4 * x + 2 * y + c
    rels, plan = _RELS[mode], []
    for a, (i_ref, o) in enumerate(zip(ins, outs)):
        for r, (bx, by, bc) in enumerate(rels):
            px = 1 - x if bx else x
            py = 1 - y if by else y
            pc = 1 - c if bc else c
            pchip, pdev = 2 * px + py, 4 * px + 2 * py + pc
            if mode == "ag_xy":
                src, dst, mine = i_ref, o.at[chip], o.at[pchip]
            elif mode == "ag_all":
                src, dst, mine = i_ref, o.at[dev], o.at[pdev]
            elif mode == "a2a_xy":
                src, dst, mine = i_ref.at[pchip], o.at[chip], o.at[pchip]
            else:
                src, dst, mine = i_ref, o, o
            k = a * len(rels) + r
            mk = lambda d: pltpu.make_async_remote_copy(src_ref=src, dst_ref=d, send_sem=send.at[k], recv_sem=recv.at[k],
                                                        device_id=(px, py, pc), device_id_type=MESH_ID)
            plan.append((mk(dst), mk(mine)))
    return plan


def _exchange(name, arrs, mode, fill_own=True):
    n, nr = len(arrs), len(_RELS[mode])
    out_shape = tuple(jax.ShapeDtypeStruct(_LEAD[mode] + a.shape, a.dtype) for a in arrs)

    def body(*refs):
        ins, outs = refs[:n], refs[n:2 * n]
        send, recv, loc = refs[2 * n:]
        chip = 2 * lax.axis_index("x") + lax.axis_index("y")
        local = []
        if fill_own and mode != "swap_c":
            slot = 2 * chip + lax.axis_index("c") if mode == "ag_all" else chip
            for a in range(n):
                lc = pltpu.make_async_copy(ins[a].at[chip] if mode == "a2a_xy" else ins[a], outs[a].at[slot], loc.at[a])
                lc.start()
                local.append(lc)
        plan = _remote_copies(mode, ins, outs, send, recv)
        for cp, _ in plan:
            cp.start()
        for cp, landing in plan:
            cp.wait_send()
            landing.wait_recv()
        for lc in local:
            lc.wait()

    hbm = pl.BlockSpec(memory_space=pl.ANY)
    res = pl.pallas_call(
        body, name=name, out_shape=out_shape, in_specs=[hbm] * n, out_specs=tuple([hbm] * n),
        scratch_shapes=[pltpu.SemaphoreType.DMA((n * nr,)), pltpu.SemaphoreType.DMA((n * nr,)),
                        pltpu.SemaphoreType.DMA((max(n, 1),))],
    )(*arrs)
    return list(res)


class _Hosted:
    def __init__(self, arrs, mode):
        self.arrs, self.mode, self.n = list(arrs), mode, len(arrs)
        self.out_shape = [jax.ShapeDtypeStruct(_LEAD[mode] + a.shape, a.dtype) for a in arrs]
        self.specs = [pl.BlockSpec(memory_space=pl.ANY)] * self.n
        nsem = self.n * len(_RELS[mode])
        self.scratch = [pltpu.SemaphoreType.DMA((nsem,)), pltpu.SemaphoreType.DMA((nsem,))]

    def run(self, first, last, ins, outs, sems):
        @pl.when(first)
        def _():
            for cp, _ in _remote_copies(self.mode, ins, outs, *sems):
                cp.start()

        @pl.when(last)
        def _():
            for cp, landing in _remote_copies(self.mode, ins, outs, *sems):
                cp.wait_send()
                landing.wait_recv()


def _matmul(name, a, b, mode, M, N, K, out_dtype=F32, a_off=(0, 0), b_off=(0, 0), tm_cap=1024, tn_cap=1024, tk_cap=2048):
    tm, tn = _tile(M, tm_cap, LANES if M % LANES == 0 else 8), _tile(N, tn_cap)
    tk = K if K <= 4096 else _tile(K, tk_cap)
    nk = K // tk
    assert M % tm == 0 and N % tn == 0 and K % tk == 0, (name, M, N, K, tm, tn, tk)
    if mode == "nn":
        ab, bb, dims = (tm, tk), (tk, tn), NN_DIMS
        ai = lambda i, j, k: (i + a_off[0] // tm, k + a_off[1] // tk)
        bi = lambda i, j, k: (k + b_off[0] // tk, j + b_off[1] // tn)
        chk = (a_off[0] % tm, a_off[1] % tk, b_off[0] % tk, b_off[1] % tn)
    elif mode == "nt":
        ab, bb, dims = (tm, tk), (tn, tk), NT_DIMS
        ai = lambda i, j, k: (i + a_off[0] // tm, k + a_off[1] // tk)
        bi = lambda i, j, k: (j + b_off[0] // tn, k + b_off[1] // tk)
        chk = (a_off[0] % tm, a_off[1] % tk, b_off[0] % tn, b_off[1] % tk)
    else:
        ab, bb, dims = (tk, tm), (tk, tn), TN_DIMS
        ai = lambda i, j, k: (k + a_off[0] // tk, i + a_off[1] // tm)
        bi = lambda i, j, k: (k + b_off[0] // tk, j + b_off[1] // tn)
        chk = (a_off[0] % tk, a_off[1] % tm, b_off[0] % tk, b_off[1] % tn)
    assert not any(chk), (name, chk)

    def body(a_ref, b_ref, o_ref, *acc):
        p = lax.dot_general(a_ref[...].astype(BF16), b_ref[...].astype(BF16), dims, preferred_element_type=F32)
        if nk == 1:
            o_ref[...] = p.astype(out_dtype)
        else:
            k = pl.program_id(2)

            @pl.when(k == 0)
            def _():
                acc[0][...] = p

            @pl.when(k > 0)
            def _():
                acc[0][...] += p

            @pl.when(k == nk - 1)
            def _():
                o_ref[...] = acc[0][...].astype(out_dtype)

    return pl.pallas_call(
        body, name=name, out_shape=jax.ShapeDtypeStruct((M, N), out_dtype), grid=(M // tm, N // tn, nk),
        in_specs=[pl.BlockSpec(ab, ai), pl.BlockSpec(bb, bi)], out_specs=pl.BlockSpec((tm, tn), lambda i, j, k: (i, j)),
        scratch_shapes=[pltpu.VMEM((tm, tn), F32)] if nk > 1 else [],
        compiler_params=_cparams(("parallel", "parallel", "arbitrary")),
    )(a, b)


def _rowwise(name, fn, grid_n, tr, ins, outs, nlat=None):
    nlat = grid_n if nlat is None else nlat
    grp = lambda i: jnp.minimum(i // nlat, 1)
    in_specs, args = [], []
    for spec in ins:
        kind, arr = spec[0], spec[1]
        if kind == "row":
            in_specs.append(pl.BlockSpec((tr, spec[3]), lambda i, cb=spec[2]: (i, cb)))
        elif kind == "rowc":
            in_specs.append(pl.BlockSpec((tr, spec[3]), lambda i, cb=spec[2], mx=spec[4]: (jnp.minimum(i, mx), cb)))
        elif kind == "prev":
            in_specs.append(pl.BlockSpec((8, spec[3]), lambda i, cb=spec[2]: (jnp.maximum(i * (tr // 8) - 1, 0), cb)))
        elif kind == "next":
            nb = arr.shape[0] // 8
            in_specs.append(pl.BlockSpec((8, spec[3]), lambda i, cb=spec[2], nb=nb: (jnp.minimum((i + 1) * (tr // 8), nb - 1), cb)))
        elif kind == "grp":
            in_specs.append(pl.BlockSpec((None,) + arr.shape[1:], lambda i: (grp(i), 0, 0)))
        else:
            in_specs.append(pl.BlockSpec(arr.shape, lambda i: (0, 0)))
        args.append(arr)
    out_shape, out_specs = [], []
    for spec in outs:
        if spec[0] == "row":
            out_shape.append(jax.ShapeDtypeStruct((spec[1], spec[2]), spec[3]))
            out_specs.append(pl.BlockSpec((tr, spec[2]), lambda i: (i, 0)))
        elif spec[0] == "acc":
            out_shape.append(jax.ShapeDtypeStruct((spec[1], spec[2]), F32))
            out_specs.append(pl.BlockSpec((spec[1], spec[2]), lambda i: (0, 0)))
        else:
            out_shape.append(jax.ShapeDtypeStruct((2, spec[1], spec[2]), F32))
            out_specs.append(pl.BlockSpec((None, spec[1], spec[2]), lambda i: (grp(i), 0, 0)))
    n_in = len(ins)
    has_acc = any(s[0] != "row" for s in outs)

    def body(*refs):
        i = pl.program_id(0)
        res = fn(i, *[r[...] for r in refs[:n_in]])
        for spec, ref, val in zip(outs, refs[n_in:], res):
            if spec[0] == "row":
                ref[...] = val.astype(ref.dtype)
            else:
                first = (i == 0) if spec[0] == "acc" else jnp.logical_or(i == 0, i == nlat)

                @pl.when(first)
                def _(ref=ref, val=val):
                    ref[...] = val

                @pl.when(jnp.logical_not(first))
                def _(ref=ref, val=val):
                    ref[...] += val

    return pl.pallas_call(
        body, name=name, out_shape=tuple(out_shape), grid=(grid_n,), in_specs=in_specs, out_specs=tuple(out_specs),
        compiler_params=_cparams(("arbitrary",) if has_acc else ("parallel",)),
    )(*args)


def _vjp_of(fwd, n_in, wrt):
    def bwd(i, *args):
        _, pull = jax.vjp(lambda *a: fwd(i, *a), *args[:n_in])
        g = pull(tuple(args[n_in:]))
        return tuple(g[k] for k in wrt)
    return bwd


def _rms(x, w):
    return x * lax.rsqrt(jnp.mean(x * x, axis=-1, keepdims=True) + EPS) * w


def _modulate(x, w, shift, scale):
    return _rms(x, w) * (1.0 + scale) + shift


def _silu(x):
    return x * jax.nn.sigmoid(x)


def _log_sigmoid(x):
    return jnp.minimum(x, 0.0) - jnp.log(1.0 + jnp.exp(-jnp.abs(x)))


def _f_mod(i, x, w, sh, sc):
    return (_modulate(x, w, sh, sc),)


def _f_resmod(i, x, y, gate, w, sh, sc):
    x1 = x + gate * y
    return x1, _modulate(x1, w, sh, sc)


def _f_norms(i, kvl, ql, kvw, qw):
    return _rms(kvl, kvw), _rms(ql, qw)


def _f_silu(i, x):
    return (_silu(x),)


def _rope128(x2, cos, sin):
    return x2 * cos + pltpu.roll(x2, QK_ROPE, 1) * sin


def _rope128_t(d2, cos, sin):
    return d2 * cos + pltpu.roll(d2 * sin, QK_ROPE, 1)


def _shifters(i, tr, T, TA):
    assert T % tr == 0 and TA % tr == 0
    loc = lax.broadcasted_iota(jnp.int32, (tr, 1), 0)
    starts_seq = jnp.logical_or(i * tr == 0, i * tr == T)
    ends_seq = jnp.logical_or((i + 1) * tr == T, (i + 1) * tr == TA)

    def prev(x, halo):
        return jnp.where(loc == 0, jnp.where(starts_seq, 0.0, halo[7:8, :]), pltpu.roll(x, 1, 0))

    def nxt(x, halo):
        return jnp.where(loc == tr - 1, jnp.where(ends_seq, 0.0, halo[0:1, :]), pltpu.roll(x, tr - 1, 0))

    return prev, nxt


def _colsum(x):
    return jnp.sum(x, axis=0, keepdims=True)


MLA_SCALE = (QK_NOPE + QK_ROPE) ** -0.5
LOG2E, LN2 = 1.4426950408889634, 0.6931471805599453
MLA_QSCALE = MLA_SCALE * LOG2E
MLA_SUB = 256


def _mla_fwd_lat(q, kf, kv, T, TA, hosted):
    H = MLA_HEADS
    tq, tk = _tile(T, 1024, TR), _tile(TA, 1408)
    nq, nk = T // tq, TA // tk
    nh = hosted.n

    def body(q_ref, k_ref, v_ref, *rest):
        o_ref, lse_ref = rest[nh:nh + 2]
        m_sc, l_sc, acc_sc = rest[2 * nh + 2:2 * nh + 5]
        hi, qi, ki = pl.program_id(0), pl.program_id(1), pl.program_id(2)
        hosted.run(jnp.logical_and(hi == 0, jnp.logical_and(qi == 0, ki == 0)),
                   jnp.logical_and(hi == H - 1, jnp.logical_and(qi == nq - 1, ki == nk - 1)),
                   rest[:nh], rest[nh + 2:2 * nh + 2], rest[2 * nh + 5:])

        @pl.when(ki == 0)
        def _():
            m_sc[...] = jnp.full_like(m_sc, -jnp.inf)
            l_sc[...] = jnp.zeros_like(l_sc)
            acc_sc[...] = jnp.zeros_like(acc_sc)

        sb = min(tq, MLA_SUB)
        scores = lambda r: lax.dot_general(q_ref[r * sb:(r + 1) * sb, :], k_ref[...], NT_DIMS, preferred_element_type=F32)
        s_next = scores(0)
        for r in range(tq // sb):
            s, rows = s_next, slice(r * sb, (r + 1) * sb)
            if r + 1 < tq // sb:
                s_next = scores(r + 1)
            m_old = m_sc[rows, :]
            m_new = jnp.maximum(m_old, jnp.max(s, axis=-1, keepdims=True))
            alpha = jnp.exp2(m_old - m_new)
            p = jnp.exp2(s - m_new)
            l_sc[rows, :] = alpha * l_sc[rows, :] + jnp.sum(p, axis=-1, keepdims=True)
            acc_sc[rows, :] = alpha * acc_sc[rows, :] + jnp.dot(p.astype(BF16), v_ref[...], preferred_element_type=F32)
            m_sc[rows, :] = m_new

        @pl.when(ki == nk - 1)
        def _():
            o_ref[...] = (acc_sc[...] / l_sc[...]).astype(o_ref.dtype)
            lse_ref[...] = m_sc[...] + jnp.log2(l_sc[...])

    res = pl.pallas_call(
        body, name="mla_fwd_lat", grid=(H, nq, nk),
        out_shape=[jax.ShapeDtypeStruct((T, H * V_HEAD), BF16), jax.ShapeDtypeStruct((H, T, 1), F32)] + hosted.out_shape,
        in_specs=[pl.BlockSpec((tq, QK_PAD), lambda h, i, k: (i, h)), pl.BlockSpec((tk, QK_PAD), lambda h, i, k: (k, h)),
                  pl.BlockSpec((tk, V_HEAD), lambda h, i, k: (k, 2 * h + 1))] + hosted.specs,
        out_specs=[pl.BlockSpec((tq, V_HEAD), lambda h, i, k: (i, h)),
                   pl.BlockSpec((None, tq, 1), lambda h, i, k: (h, i, 0))] + hosted.specs,
        scratch_shapes=[pltpu.VMEM((tq, 1), F32), pltpu.VMEM((tq, 1), F32), pltpu.VMEM((tq, V_HEAD), F32)] + hosted.scratch,
        compiler_params=_cparams(("arbitrary", "arbitrary", "arbitrary")),
    )(q, kf, kv, *hosted.arrs)
    return res[0], res[1], list(res[2:])


def _mla_ctx_probs(q_ref, k_ref):
    s = lax.dot_general(q_ref[...], k_ref[...], NT_DIMS, preferred_element_type=F32)
    p = jnp.exp2(s - jnp.max(s, axis=-1, keepdims=True))
    return p, jnp.sum(p, axis=-1, keepdims=True)


def _mla_fwd_ctx(q, kf, kv, T):
    H, cb = MLA_HEADS, T // TR

    def body(q_ref, k_ref, v_ref, o_ref):
        p, l = _mla_ctx_probs(q_ref, k_ref)
        o_ref[...] = (jnp.dot(p.astype(BF16), v_ref[...], preferred_element_type=F32) / l).astype(o_ref.dtype)

    return pl.pallas_call(
        body, name="mla_fwd_ctx", grid=(H,), out_shape=jax.ShapeDtypeStruct((TR, H * V_HEAD), BF16),
        in_specs=[pl.BlockSpec((TR, QK_PAD), lambda h: (cb, h)), pl.BlockSpec((TR, QK_PAD), lambda h: (cb, h)),
                  pl.BlockSpec((TR, V_HEAD), lambda h: (cb, 2 * h + 1))],
        out_specs=pl.BlockSpec((TR, V_HEAD), lambda h: (0, h)), compiler_params=_cparams(("parallel",)),
    )(q, kf, kv)


def _mla_grads(p, q_ref, k_ref, v_ref, do, delta):
    dob = do.astype(BF16)
    dp = lax.dot_general(dob, v_ref[...], NT_DIMS, preferred_element_type=F32)
    ds = (p * (dp - delta)).astype(BF16)
    return (jnp.dot(ds, k_ref[...], preferred_element_type=F32),
            lax.dot_general(ds, q_ref[...], TN_DIMS, preferred_element_type=F32) * LN2,
            lax.dot_general(p.astype(BF16), dob, TN_DIMS, preferred_element_type=F32))


def _mla_bwd_lat(q, kf, kv, o, dcat, lse, do_cb0, T, TA, hosted):
    H = MLA_HEADS
    tq, tk = _tile(T, 1024, TR), _tile(TA, 1408)
    nq, nk = T // tq, TA // tk
    nh = hosted.n

    def body(q_ref, k_ref, v_ref, o_ref, do_ref, lse_ref, *rest):
        dq_ref, dk_ref, dv_ref = rest[nh:nh + 3]
        dk_acc, dv_acc = rest[2 * nh + 3:2 * nh + 5]
        hi, ki, qi = pl.program_id(0), pl.program_id(1), pl.program_id(2)
        hosted.run(jnp.logical_and(hi == 0, jnp.logical_and(qi == 0, ki == 0)),
                   jnp.logical_and(hi == H - 1, jnp.logical_and(qi == nq - 1, ki == nk - 1)),
                   rest[:nh], rest[nh + 3:2 * nh + 3], rest[2 * nh + 5:])

        @pl.when(jnp.logical_and(ki == 0, qi == 0))
        def _():
            dq_ref[...] = jnp.zeros_like(dq_ref)

        @pl.when(qi == 0)
        def _():
            dk_acc[...] = jnp.zeros_like(dk_acc)
            dv_acc[...] = jnp.zeros_like(dv_acc)

        sb = min(tq, MLA_SUB)

        def products(r):
            rows = slice(r * sb, (r + 1) * sb)
            dob = do_ref[rows, :].astype(BF16)
            return (lax.dot_general(q_ref[rows, :], k_ref[...], NT_DIMS, preferred_element_type=F32),
                    lax.dot_general(dob, v_ref[...], NT_DIMS, preferred_element_type=F32), dob)

        nxt, dk, dv = products(0), None, None
        for r in range(tq // sb):
            (s, dp, dob), rows = nxt, slice(r * sb, (r + 1) * sb)
            if r + 1 < tq // sb:
                nxt = products(r + 1)
            delta = jnp.sum(do_ref[rows, :] * o_ref[rows, :].astype(F32), axis=-1, keepdims=True)
            p = jnp.exp2(s - lse_ref[rows, :])
            ds = (p * (dp - delta)).astype(BF16)
            dq_ref[pl.ds(pl.multiple_of(qi * tq + r * sb, sb), sb), :] += jnp.dot(ds, k_ref[...], preferred_element_type=F32)
            dk_r = lax.dot_general(ds, q_ref[rows, :], TN_DIMS, preferred_element_type=F32)
            dv_r = lax.dot_general(p.astype(BF16), dob, TN_DIMS, preferred_element_type=F32)
            dk, dv = (dk_r, dv_r) if dk is None else (dk + dk_r, dv + dv_r)
        dk_acc[...] += dk
        dv_acc[...] += dv

        @pl.when(qi == nq - 1)
        def _():
            dk_ref[...] = dk_acc[...] * LN2
            dv_ref[...] = dv_acc[...].astype(dv_ref.dtype)

    res = pl.pallas_call(
        body, name="mla_bwd_lat", grid=(H, nk, nq),
        out_shape=[jax.ShapeDtypeStruct((T, H * QK_PAD), F32), jax.ShapeDtypeStruct((TA, H * QK_PAD), F32),
                   jax.ShapeDtypeStruct((TA, H * V_HEAD), BF16)] + hosted.out_shape,
        in_specs=[pl.BlockSpec((tq, QK_PAD), lambda h, k, i: (i, h)), pl.BlockSpec((tk, QK_PAD), lambda h, k, i: (k, h)),
                  pl.BlockSpec((tk, V_HEAD), lambda h, k, i: (k, 2 * h + 1)), pl.BlockSpec((tq, V_HEAD), lambda h, k, i: (i, h)),
                  pl.BlockSpec((tq, V_HEAD), lambda h, k, i: (i, do_cb0 + h)),
                  pl.BlockSpec((None, tq, 1), lambda h, k, i: (h, i, 0))] + hosted.specs,
        out_specs=[pl.BlockSpec((T, QK_PAD), lambda h, k, i: (0, h)), pl.BlockSpec((tk, QK_PAD), lambda h, k, i: (k, h)),
                   pl.BlockSpec((tk, V_HEAD), lambda h, k, i: (k, h))] + hosted.specs,
        scratch_shapes=[pltpu.VMEM((tk, QK_PAD), F32), pltpu.VMEM((tk, V_HEAD), F32)] + hosted.scratch,
        compiler_params=_cparams(("arbitrary", "arbitrary", "arbitrary")),
    )(q, kf, kv, o, dcat, lse, *hosted.arrs)
    return res[0], res[1], res[2], list(res[3:])


def _mla_bwd_ctx(q, kf, kv, dcat, dk_lat, dv_lat, do_cb0, T):
    H, cb = MLA_HEADS, T // TR

    def body(q_ref, k_ref, v_ref, do_ref, dkl_ref, dvl_ref, dq_ref, dk_ref, dv_ref):
        p, l = _mla_ctx_probs(q_ref, k_ref)
        do = do_ref[...]
        o = jnp.dot(p.astype(BF16), v_ref[...], preferred_element_type=F32) / l
        dq, dk, dv = _mla_grads(p / l, q_ref, k_ref, v_ref, do, jnp.sum(do * o, axis=-1, keepdims=True))
        dq_ref[...] = dq
        dk_ref[...] = dkl_ref[...] + dk
        dv_ref[...] = (dvl_ref[...].astype(F32) + dv).astype(dv_ref.dtype)

    at_ctx = lambda w, f: pl.BlockSpec((TR, w), lambda h: (cb, f(h)))
    at_0 = lambda w: pl.BlockSpec((TR, w), lambda h: (0, h))
    return pl.pallas_call(
        body, name="mla_bwd_ctx", grid=(H,),
        out_shape=(jax.ShapeDtypeStruct((TR, H * QK_PAD), F32), jax.ShapeDtypeStruct((TR, H * QK_PAD), F32),
                   jax.ShapeDtypeStruct((TR, H * V_HEAD), BF16)),
        in_specs=[at_ctx(QK_PAD, lambda h: h), at_ctx(QK_PAD, lambda h: h), at_ctx(V_HEAD, lambda h: 2 * h + 1),
                  at_ctx(V_HEAD, lambda h: do_cb0 + h), at_ctx(QK_PAD, lambda h: h), at_ctx(V_HEAD, lambda h: h)],
        out_specs=(at_0(QK_PAD), at_0(QK_PAD), at_0(V_HEAD)), compiler_params=_cparams(("parallel",)),
    )(q, kf, kv, dcat, dk_lat, dv_lat)


def _gla_tile(st, q, k, v, g, rev, q_scale):
    nc = q.shape[0] // CHUNK
    ii = lax.broadcasted_iota(jnp.int32, (CHUNK, CHUNK), 0)
    jj = lax.broadcasted_iota(jnp.int32, (CHUNK, CHUNK), 1)
    tri = (jj >= ii) if rev else (jj <= ii)
    ones = tri.astype(F32)
    outs = [None] * nc
    for ci in (range(nc - 1, -1, -1) if rev else range(nc)):
        sl = slice(ci * CHUNK, (ci + 1) * CHUNK)
        qc, kc, vc, gc = q[sl] * q_scale, k[sl], v[sl].astype(BF16), g[sl]
        b = jnp.dot(ones, gc, precision=HIGHEST, preferred_element_type=F32)
        bl = b[0:1] if rev else b[CHUNK - 1:CHUNK]
        kd = (kc * jnp.exp(bl - b)).astype(BF16)
        qe = (qc * jnp.exp(b)).astype(BF16)
        ke = (kc * jnp.exp(-b)).astype(BF16)
        att = jnp.where(tri, lax.dot_general(qe, ke, NT_DIMS, preferred_element_type=F32), 0.0)
        outs[ci] = (jnp.dot(att.astype(BF16), vc, preferred_element_type=F32)
                    + lax.dot_general(qe, st.astype(BF16), NT_DIMS, preferred_element_type=F32))
        st = st * jnp.exp(bl) + lax.dot_general(vc, kd, TN_DIMS, preferred_element_type=F32)
    return st, jnp.concatenate(outs, axis=0)


def _gla_tiles(nt, nlat):
    return (lambda p: (p + nlat) % nt), (lambda p: nt - 1 - p)


def _gla_fwd(z1, g, dims, nt, nlat):
    DK, DV, q_off, k_off = dims
    KEY, TA = GLA_HEADS * DK, nt * TR
    in_specs, out_o = [], []
    for d, tile in enumerate(_gla_tiles(nt, nlat)):
        in_specs += [pl.BlockSpec((TR, DK), lambda h, p, t=tile: (t(p), q_off // DK + h)),
                     pl.BlockSpec((TR, DK), lambda h, p, t=tile: (t(p), k_off // DK + h)),
                     pl.BlockSpec((TR, DV), lambda h, p, t=tile: (t(p), h)),
                     pl.BlockSpec((TR, DK), lambda h, p, t=tile, d=d: (t(p), d * GLA_HEADS + h))]
        out_o.append(pl.BlockSpec((TR, DV), lambda h, p, t=tile: (t(p), h)))
    st_spec = pl.BlockSpec((None, None, DV, DK), lambda h, p: (h, p, 0, 0))

    def body(qf, kf, vf, gf, qb, kb, vb, gb, of_ref, ob_ref, sf_ref, sb_ref, st_f, st_b):
        @pl.when(pl.program_id(1) == 0)
        def _():
            st_f[...] = jnp.zeros_like(st_f)
            st_b[...] = jnp.zeros_like(st_b)

        sf_ref[...] = st_f[...]
        sb_ref[...] = st_b[...]
        new_f, out_f = _gla_tile(st_f[...], qf[...], kf[...], vf[...], gf[...], False, DK ** -0.5)
        new_b, out_b = _gla_tile(st_b[...], qb[...], kb[...], vb[...], gb[...], True, DK ** -0.5)
        st_f[...] = new_f
        st_b[...] = new_b
        of_ref[...] = out_f
        ob_ref[...] = out_b

    o_shape = jax.ShapeDtypeStruct((TA, GLA_HEADS * DV), F32)
    s_shape = jax.ShapeDtypeStruct((GLA_HEADS, nt, DV, DK), F32)
    return pl.pallas_call(
        body, name="gla_fwd", grid=(GLA_HEADS, nt), out_shape=(o_shape, o_shape, s_shape, s_shape),
        in_specs=in_specs, out_specs=(out_o[0], out_o[1], st_spec, st_spec),
        scratch_shapes=[pltpu.VMEM((DV, DK), F32)] * 2, compiler_params=_cparams(("parallel", "arbitrary")),
    )(z1, z1, z1, g, z1, z1, z1, g)


def _gla_bwd(z1, g, st_f, st_b, do, dims, nt, nlat):
    DK, DV, q_off, k_off = dims
    KEY, VAL, TA = GLA_HEADS * DK, GLA_HEADS * DV, nt * TR
    pos = lambda s: nt - 1 - s
    in_specs, out_specs = [], []
    for d, tile in enumerate(_gla_tiles(nt, nlat)):
        at = lambda w, f, t=tile: pl.BlockSpec((TR, w), lambda h, s: (t(pos(s)), f(h)))
        in_specs += [at(DK, lambda h: q_off // DK + h), at(DK, lambda h: k_off // DK + h), at(DV, lambda h: h),
                     at(DK, lambda h, d=d: d * GLA_HEADS + h),
                     pl.BlockSpec((None, None, DV, DK), lambda h, s: (h, pos(s), 0, 0)), at(DV, lambda h: h)]
        out_specs += [at(DK, lambda h: h), at(DK, lambda h: h), at(DV, lambda h: h), at(DK, lambda h: h)]

    def one(refs, outs, dst, rev):
        q_ref, k_ref, v_ref, g_ref, st_ref, do_ref = refs
        _, pull = jax.vjp(lambda st, q, k, v, gg: _gla_tile(st, q, k, v, gg, rev, DK ** -0.5),
                          st_ref[...], q_ref[...], k_ref[...], v_ref[...], g_ref[...])
        grads = pull((dst[...], do_ref[...]))
        dst[...] = grads[0]
        for o_ref, val in zip(outs, grads[1:]):
            o_ref[...] = val

    def body(*refs):
        dst_f, dst_b = refs[20:]

        @pl.when(pl.program_id(1) == 0)
        def _():
            dst_f[...] = jnp.zeros_like(dst_f)
            dst_b[...] = jnp.zeros_like(dst_b)

        one(refs[0:6], refs[12:16], dst_f, False)
        one(refs[6:12], refs[16:20], dst_b, True)

    shapes = [jax.ShapeDtypeStruct((TA, KEY), F32), jax.ShapeDtypeStruct((TA, KEY), F32),
              jax.ShapeDtypeStruct((TA, VAL), F32), jax.ShapeDtypeStruct((TA, KEY), F32)]
    res = pl.pallas_call(
        body, name="gla_bwd", grid=(GLA_HEADS, nt), out_shape=shapes * 2, in_specs=in_specs, out_specs=out_specs,
        scratch_shapes=[pltpu.VMEM((DV, DK), F32)] * 2, compiler_params=_cparams(("parallel", "arbitrary")),
    )(z1, z1, z1, g, st_f, do, z1, z1, z1, g, st_b, do)
    return res[:4], res[4:]


def _sum_slots(name, arr):
    S, R, C = arr.shape
    tr = _tile(R, max(16, (2 ** 19) // max(C, 1) // 16 * 16), 16)

    def body(a_ref, o_ref):
        acc = a_ref[0].astype(F32)
        for s in range(1, S):
            acc = acc + a_ref[s].astype(F32)
        o_ref[...] = acc

    return pl.pallas_call(
        body, name=name, grid=(R // tr,), out_shape=jax.ShapeDtypeStruct((R, C), F32),
        in_specs=[pl.BlockSpec((S, tr, C), lambda i: (0, i, 0))], out_specs=pl.BlockSpec((tr, C), lambda i: (i, 0)),
        compiler_params=_cparams(("parallel",)),
    )(arr)


def _sum_pair(name, a, b, out_dtype):
    shape, C = a.shape, a.shape[-1]
    a2, b2 = a.reshape(-1, C), b.reshape(-1, C)
    R = a2.shape[0]
    tr = _tile(R, max(16, (2 ** 19) // max(C, 1) // 16 * 16), 16)
    fn = lambda i, p, q: (p.astype(F32) + q.astype(F32),)
    (res,) = _rowwise(name, fn, R // tr, tr, [("row", a2, 0, C), ("row", b2, 0, C)], [("row", R, C, out_dtype)])
    return res.reshape(shape)


def _adam_math(w, g, m, v):
    m2 = ADAM_B1 * m + (1.0 - ADAM_B1) * g
    v2 = ADAM_B2 * v + (1.0 - ADAM_B2) * (g * g)
    m_hat = m2 / (1.0 - ADAM_B1 ** ADAM_STEP)
    v_hat = v2 / (1.0 - ADAM_B2 ** ADAM_STEP)
    return -ADAM_LR * (m_hat / (jnp.sqrt(v_hat) + ADAM_EPS) + ADAM_WD * w), m2, v2


def _adam(name, w, gs, m, v):
    R, C = w.shape
    tr = _tile(R, max(8, (2 ** 19) // max(C, 1) // 8 * 8), 8)
    ng = len(gs)

    def fn(i, w_, *rest):
        g = rest[0] if ng == 1 else rest[0] + rest[1]
        d, m2, v2 = _adam_math(w_, g, rest[ng], rest[ng + 1])
        return g, d, m2, v2

    ins = [("row", a, 0, C) for a in (w, *gs, m, v)]
    return _rowwise(name, fn, R // tr, tr, ins, [("row", R, C, F32)] * 4)


def _ffn_fwd(tag, h, w_up, cw, cb, w_down, n_rows, T, TA, F):
    tr = 128
    u = _matmul(tag + "_up", h, w_up, "nn", n_rows, 2 * F, h.shape[1])
    cwg, cwv, cbg, cbv = cw[:, :F], cw[:, F:], cb[:, :F], cb[:, F:]

    def fn(i, ug, uv, pg, pv, ng, nv, wg, wv, bg, bv):
        prev, nxt = _shifters(i, tr, T, TA)
        cg = wg[0:1] * prev(ug, pg) + wg[1:2] * ug + wg[2:3] * nxt(ug, ng) + bg
        cv = wv[0:1] * prev(uv, pv) + wv[1:2] * uv + wv[2:3] * nxt(uv, nv) + bv
        return (_silu(cg) * cv,)

    ins = [("row", u, 0, F), ("row", u, 1, F), ("prev", u, 0, F), ("prev", u, 1, F), ("next", u, 0, F), ("next", u, 1, F),
           ("full", cwg), ("full", cwv), ("full", cbg), ("full", cbv)]
    (act,) = _rowwise(tag + "_conv", fn, n_rows // tr, tr, ins, [("row", n_rows, F, BF16)])
    f = _matmul(tag + "_down", act, w_down, "nn", n_rows, w_down.shape[1], F)
    return u, act, f


def _ffn_bwd(tag, df, h, u, act, w_up, cw, cb, w_down, n_rows, T, TA, F):
    tr = 128
    D = w_down.shape[1]
    dact = _matmul(tag + "_ddown_x", df, w_down, "nt", n_rows, F, D)
    g_down = _matmul(tag + "_ddown_w", act, df, "tn", F, D, n_rows, out_dtype=BF16)
    cwg, cwv, cbg, cbv = cw[:, :F], cw[:, F:], cb[:, :F], cb[:, F:]

    def fn_a(i, ug, uv, pg, pv, ng, nv, da, wg, wv, bg, bv):
        prev, nxt = _shifters(i, tr, T, TA)
        ugp, ugn, uvp, uvn = prev(ug, pg), nxt(ug, ng), prev(uv, pv), nxt(uv, nv)
        cg = wg[0:1] * ugp + wg[1:2] * ug + wg[2:3] * ugn + bg
        cv = wv[0:1] * uvp + wv[1:2] * uv + wv[2:3] * uvn + bv
        sg = jax.nn.sigmoid(cg)
        dcv = da * (cg * sg)
        dcg = da * cv * (sg * (1.0 + cg * (1.0 - sg)))
        return (jnp.concatenate([dcg, dcv], axis=1),
                _colsum(ugp * dcg), _colsum(ug * dcg), _colsum(ugn * dcg),
                _colsum(uvp * dcv), _colsum(uv * dcv), _colsum(uvn * dcv), _colsum(dcg), _colsum(dcv))

    ins = [("row", u, 0, F), ("row", u, 1, F), ("prev", u, 0, F), ("prev", u, 1, F), ("next", u, 0, F), ("next", u, 1, F),
           ("row", dact, 0, F), ("full", cwg), ("full", cwv), ("full", cbg), ("full", cbv)]
    res = _rowwise(tag + "_dconv_a", fn_a, n_rows // tr, tr, ins, [("row", n_rows, 2 * F, F32)] + [("acc", 1, F)] * 8)
    duc = res[0]
    g_cw = jnp.concatenate([jnp.concatenate(res[1:4], axis=0), jnp.concatenate(res[4:7], axis=0)], axis=1)
    g_cb = jnp.concatenate([res[7], res[8]], axis=1)

    def fn_b(i, dg, dv, pg, pv, ng, nv, wg, wv):
        prev, nxt = _shifters(i, tr, T, TA)
        dug = wg[0:1] * nxt(dg, ng) + wg[1:2] * dg + wg[2:3] * prev(dg, pg)
        duv = wv[0:1] * nxt(dv, nv) + wv[1:2] * dv + wv[2:3] * prev(dv, pv)
        return (jnp.concatenate([dug, duv], axis=1),)

    ins = [("row", duc, 0, F), ("row", duc, 1, F), ("prev", duc, 0, F), ("prev", duc, 1, F), ("next", duc, 0, F),
           ("next", duc, 1, F), ("full", cwg), ("full", cwv)]
    (du,) = _rowwise(tag + "_dconv_b", fn_b, n_rows // tr, tr, ins, [("row", n_rows, 2 * F, BF16)])
    dh = _matmul(tag + "_dup_x", du, w_up, "nt", n_rows, D, 2 * F)
    g_up = _matmul(tag + "_dup_w", h, du, "tn", D, 2 * F, n_rows, out_dtype=BF16)
    return dh, g_up, g_cw, g_cb, g_down


def _cols_to_shards(g):
    r, c = g.shape
    return g.reshape(r, 4, c // 4).transpose(1, 0, 2)


def _shards_to_cols(s):
    return s.transpose(1, 0, 2).reshape(s.shape[1], 4 * s.shape[2])


def kernel(x, c, ctx, c_ctx, l0_ada_w, l0_ada_b, l0_norm1, l0_w_in, l0_conv_a, l0_q_norm, l0_w_qb, l0_kv_norm, l0_w_kvb, l0_w_out, l0_norm2, l0_ffn_up, l0_ffn_conv_w, l0_ffn_conv_b, l0_ffn_down, l1_ada_w, l1_ada_b, l1_norm1, l1_w_in, l1_gate_fw_w, l1_gate_fw_b, l1_gate_bw_w, l1_gate_bw_b, l1_o_norm, l1_w_out, l1_norm2, l1_ffn_up, l1_ffn_conv_w, l1_ffn_conv_b, l1_ffn_down, final_norm, loss_target, m_c_ctx, m_l0_ada_w, m_l0_ada_b, m_l0_norm1, m_l0_w_in, m_l0_conv_a, m_l0_q_norm, m_l0_w_qb, m_l0_kv_norm, m_l0_w_kvb, m_l0_w_out, m_l0_norm2, m_l0_ffn_up, m_l0_ffn_conv_w, m_l0_ffn_conv_b, m_l0_ffn_down, m_l1_ada_w, m_l1_ada_b, m_l1_norm1, m_l1_w_in, m_l1_gate_fw_w, m_l1_gate_fw_b, m_l1_gate_bw_w, m_l1_gate_bw_b, m_l1_o_norm, m_l1_w_out, m_l1_norm2, m_l1_ffn_up, m_l1_ffn_conv_w, m_l1_ffn_conv_b, m_l1_ffn_down, m_final_norm, v_c_ctx, v_l0_ada_w, v_l0_ada_b, v_l0_norm1, v_l0_w_in, v_l0_conv_a, v_l0_q_norm, v_l0_w_qb, v_l0_kv_norm, v_l0_w_kvb, v_l0_w_out, v_l0_norm2, v_l0_ffn_up, v_l0_ffn_conv_w, v_l0_ffn_conv_b, v_l0_ffn_down, v_l1_ada_w, v_l1_ada_b, v_l1_norm1, v_l1_w_in, v_l1_gate_fw_w, v_l1_gate_fw_b, v_l1_gate_bw_w, v_l1_gate_bw_b, v_l1_o_norm, v_l1_w_out, v_l1_norm2, v_l1_ffn_up, v_l1_ffn_conv_w, v_l1_ffn_conv_b, v_l1_ffn_down, v_final_norm):
    W = dict(c_ctx=c_ctx, l0_ada_w=l0_ada_w, l0_ada_b=l0_ada_b, l0_norm1=l0_norm1, l0_w_in=l0_w_in, l0_conv_a=l0_conv_a, l0_q_norm=l0_q_norm, l0_w_qb=l0_w_qb, l0_kv_norm=l0_kv_norm, l0_w_kvb=l0_w_kvb, l0_w_out=l0_w_out, l0_norm2=l0_norm2, l0_ffn_up=l0_ffn_up, l0_ffn_conv_w=l0_ffn_conv_w, l0_ffn_conv_b=l0_ffn_conv_b, l0_ffn_down=l0_ffn_down, l1_ada_w=l1_ada_w, l1_ada_b=l1_ada_b, l1_norm1=l1_norm1, l1_w_in=l1_w_in, l1_gate_fw_w=l1_gate_fw_w, l1_gate_fw_b=l1_gate_fw_b, l1_gate_bw_w=l1_gate_bw_w, l1_gate_bw_b=l1_gate_bw_b, l1_o_norm=l1_o_norm, l1_w_out=l1_w_out, l1_norm2=l1_norm2, l1_ffn_up=l1_ffn_up, l1_ffn_conv_w=l1_ffn_conv_w, l1_ffn_conv_b=l1_ffn_conv_b, l1_ffn_down=l1_ffn_down, final_norm=final_norm)
    MO = dict(c_ctx=m_c_ctx, l0_ada_w=m_l0_ada_w, l0_ada_b=m_l0_ada_b, l0_norm1=m_l0_norm1, l0_w_in=m_l0_w_in, l0_conv_a=m_l0_conv_a, l0_q_norm=m_l0_q_norm, l0_w_qb=m_l0_w_qb, l0_kv_norm=m_l0_kv_norm, l0_w_kvb=m_l0_w_kvb, l0_w_out=m_l0_w_out, l0_norm2=m_l0_norm2, l0_ffn_up=m_l0_ffn_up, l0_ffn_conv_w=m_l0_ffn_conv_w, l0_ffn_conv_b=m_l0_ffn_conv_b, l0_ffn_down=m_l0_ffn_down, l1_ada_w=m_l1_ada_w, l1_ada_b=m_l1_ada_b, l1_norm1=m_l1_norm1, l1_w_in=m_l1_w_in, l1_gate_fw_w=m_l1_gate_fw_w, l1_gate_fw_b=m_l1_gate_fw_b, l1_gate_bw_w=m_l1_gate_bw_w, l1_gate_bw_b=m_l1_gate_bw_b, l1_o_norm=m_l1_o_norm, l1_w_out=m_l1_w_out, l1_norm2=m_l1_norm2, l1_ffn_up=m_l1_ffn_up, l1_ffn_conv_w=m_l1_ffn_conv_w, l1_ffn_conv_b=m_l1_ffn_conv_b, l1_ffn_down=m_l1_ffn_down, final_norm=m_final_norm)
    VO = dict(c_ctx=v_c_ctx, l0_ada_w=v_l0_ada_w, l0_ada_b=v_l0_ada_b, l0_norm1=v_l0_norm1, l0_w_in=v_l0_w_in, l0_conv_a=v_l0_conv_a, l0_q_norm=v_l0_q_norm, l0_w_qb=v_l0_w_qb, l0_kv_norm=v_l0_kv_norm, l0_w_kvb=v_l0_w_kvb, l0_w_out=v_l0_w_out, l0_norm2=v_l0_norm2, l0_ffn_up=v_l0_ffn_up, l0_ffn_conv_w=v_l0_ffn_conv_w, l0_ffn_conv_b=v_l0_ffn_conv_b, l0_ffn_down=v_l0_ffn_down, l1_ada_w=v_l1_ada_w, l1_ada_b=v_l1_ada_b, l1_norm1=v_l1_norm1, l1_w_in=v_l1_w_in, l1_gate_fw_w=v_l1_gate_fw_w, l1_gate_fw_b=v_l1_gate_fw_b, l1_gate_bw_w=v_l1_gate_bw_w, l1_gate_bw_b=v_l1_gate_bw_b, l1_o_norm=v_l1_o_norm, l1_w_out=v_l1_w_out, l1_norm2=v_l1_norm2, l1_ffn_up=v_l1_ffn_up, l1_ffn_conv_w=v_l1_ffn_conv_w, l1_ffn_conv_b=v_l1_ffn_conv_b, l1_ffn_down=v_l1_ffn_down, final_norm=v_final_norm)
    names = list(W)

    T, D = x.shape[1], x.shape[2]
    TC = ctx.shape[1]
    assert TC == TR and T % TR == 0
    TA = T + TC
    nt, nlat = TA // TR, T // TR
    CC = D // 2
    F = l0_ffn_down.shape[0] * 4
    DK, DV = D // 2 // GLA_HEADS, D // GLA_HEADS
    KEY, VAL = GLA_HEADS * DK, GLA_HEADS * DV
    ADA_S = l0_ada_w.shape[1]
    H = MLA_HEADS
    my_x, my_y, my_c = lax.axis_index("x"), lax.axis_index("y"), lax.axis_index("c")
    chip, dev = 2 * my_x + my_y, 4 * my_x + 2 * my_y + my_c
    row = lambda a: a.reshape(1, -1)

    big = ["l0_w_in", "l0_w_qb", "l0_w_kvb", "l0_w_out", "l0_ffn_up", "l0_ffn_down",
           "l1_w_in", "l1_w_out", "l1_ffn_up", "l1_ffn_down"]
    small_sh = ["l0_conv_a", "l0_ffn_conv_w", "l1_gate_fw_w", "l1_gate_bw_w", "l1_ffn_conv_w"]
    own_half = lambda a, k: lax.dynamic_index_in_dim(a, k, 0, keepdims=False)
    set_own = lambda arr, val, k: lax.dynamic_update_index_in_dim(arr, val[None], k, 0)
    in_core_order = lambda mine, other: jnp.where(my_c == 0, jnp.stack([mine, other], 1), jnp.stack([other, mine], 1))
    early, late = big[:3], big[3:]
    w_half = {n: own_half(W[n].astype(BF16).reshape(2, W[n].shape[0] // 2, -1), my_c) for n in big}
    col_sh = {"l0_w_in", "l0_w_qb", "l0_w_kvb", "l0_ffn_up", "l1_w_in", "l1_ffn_up", "l0_conv_a", "l0_ffn_conv_w",
              "l1_gate_fw_w", "l1_gate_bw_w", "l1_ffn_conv_w"}
    whole_w = lambda n, s: _shards_to_cols(s) if n in col_sh else s.reshape(-1, s.shape[-1])

    def pair_up(tag, group, got):
        got = [set_own(g, w_half[n], chip) for n, g in zip(group, got)]
        other = _exchange("pair_weights_" + tag, got, "swap_c")
        return {n: whole_w(n, in_core_order(g, o).reshape(4, 2 * g.shape[1], g.shape[2])) for n, g, o in zip(group, got, other)}

    full = pair_up("a", early, _exchange("gather_weights", [w_half[n] for n in early], "ag_xy", fill_own=False))
    for n, s in zip(small_sh, _exchange("gather_small_weights", [W[n] for n in small_sh], "ag_xy")):
        full[n] = whole_w(n, s)
    (c_all,) = _exchange("gather_c", [c], "ag_all")

    o_kv, o_kr, o_ql, o_ax = 0, KV_LORA, KV_LORA + QK_ROPE, KV_LORA + QK_ROPE + Q_LORA
    n_sel = 3 * CC + Q_LORA + KV_LORA + QK_ROPE
    W0 = -(-(n_sel + QK_ROPE) // 256) * 256
    QOFF0, KVOFF0, KROFF0 = 3 * CC, 3 * CC + Q_LORA, 3 * CC + Q_LORA + KV_LORA
    w_in0 = full["l0_w_in"]
    w_kr = w_in0[:, o_kr:o_kr + QK_ROPE]
    w_in0p = jnp.concatenate([w_in0[:, o_ax:], w_in0[:, o_ql:o_ax], w_in0[:, :o_kr], w_kr, _rot_cols(w_kr),
                              jnp.zeros((D, W0 - n_sel - QK_ROPE), BF16)], axis=1)
    wq3 = full["l0_w_qb"].reshape(Q_LORA, H, QK_NOPE + QK_ROPE)
    w_qbp = jnp.concatenate([wq3, _rot_cols(wq3[:, :, QK_NOPE:])], axis=2).reshape(Q_LORA, H * QK_PAD)
    o_v1, o_lr1, o_q1, o_og1 = KEY, KEY + VAL, KEY + VAL + 2 * GATE_RANK, 2 * KEY + VAL + 2 * GATE_RANK
    W1 = 2 * VAL + 2 * KEY + LANES
    OGOFF, KOFF1, QOFF1, LROFF = VAL, 2 * VAL, 2 * VAL + KEY, 2 * VAL + 2 * KEY
    w_gate = jnp.zeros((LANES, 2 * KEY), F32)
    w_gate = w_gate.at[:GATE_RANK, :KEY].set(full["l1_gate_fw_w"]).at[GATE_RANK:2 * GATE_RANK, KEY:].set(full["l1_gate_bw_w"])
    b_gate = jnp.concatenate([l1_gate_fw_b, l1_gate_bw_b]).reshape(1, -1)

    pos = np.arange(T)
    inv = ROPE_THETA ** (-np.arange(0, QK_ROPE // 2, 2, dtype=np.float32) / (QK_ROPE // 2))
    ar, ac = (pos // GRID_W).astype(np.float32)[:, None] * inv, (pos % GRID_W).astype(np.float32)[:, None] * inv
    ang = jnp.asarray(np.concatenate([ar, ar, ac, ac], axis=-1).astype(np.float32))
    zpad = jnp.zeros((TA, LANES - QK_ROPE), F32)
    cos_t = jnp.concatenate([jnp.concatenate([jnp.cos(ang), jnp.ones((TC, QK_ROPE), F32)], axis=0), zpad], axis=1)
    sin_t = jnp.concatenate([jnp.concatenate([jnp.sin(ang), jnp.zeros((TC, QK_ROPE), F32)], axis=0), zpad], axis=1)

    c16 = jnp.concatenate([c_all.reshape(8, D), c_ctx.reshape(1, D), jnp.zeros((7, D), F32)], axis=0)
    (act16,) = _rowwise("silu_c", _f_silu, 1, 16, [("row", c16, 0, D)], [("row", 16, D, F32)])
    mod_parts = [_matmul("ada_fwd", act16, W["l%d_ada_w" % l], "nn", 16, ADA_S, D, tn_cap=512) for l in (0, 1)]
    mod_g = _exchange("gather_mod", mod_parts, "ag_xy")
    mods = []
    for l in (0, 1):
        mfull = _shards_to_cols(mod_g[l]) + W["l%d_ada_b" % l][None, :]
        mine = lax.dynamic_slice_in_dim(mfull, dev, 1, axis=0)
        mods.append(jnp.concatenate([mine, mfull[8:9]], axis=0).reshape(2, N_MOD, D))
    P = lambda l, k: mods[l][:, k:k + 1, :]

    X = jnp.concatenate([x[0], ctx[0]], axis=0)
    n1_0, n2_0, n1_1, n2_1 = row(l0_norm1), row(l0_norm2), row(l1_norm1), row(l1_norm2)
    (h0,) = _rowwise("l0_mod1", _f_mod, nt, TR, [("row", X, 0, D), ("full", n1_0), ("grp", P(0, 0)), ("grp", P(0, 1))],
                     [("row", TA, D, BF16)], nlat)
    z0 = _matmul("l0_in", h0, w_in0p, "nn", TA, W0, D)
    kvn_w, qn_w = row(l0_kv_norm), row(l0_q_norm)
    norm_ins = [("row", z0, KVOFF0 // KV_LORA, KV_LORA), ("row", z0, QOFF0 // Q_LORA, Q_LORA), ("full", kvn_w), ("full", qn_w)]
    kvn, qn = _rowwise("l0_latnorm", _f_norms, nt, TR, norm_ins, [("row", TA, KV_LORA, BF16), ("row", TA, Q_LORA, BF16)])
    kv = _matmul("l0_kvb", kvn, full["l0_w_kvb"], "nn", TA, H * QK_PAD, KV_LORA, out_dtype=BF16)
    qraw = _matmul("l0_qb", qn, w_qbp, "nn", TA, H * QK_PAD, Q_LORA)

    def f_rope(i, qr, kvv, krr, cs, sn):
        kr = _rope128(krr, cs, sn).astype(BF16)
        qs, ks = [], []
        for h in range(H):
            qs += [qr[:, h * QK_PAD:h * QK_PAD + QK_NOPE], _rope128(qr[:, h * QK_PAD + QK_NOPE:(h + 1) * QK_PAD], cs, sn)]
            ks += [kvv[:, h * QK_PAD:h * QK_PAD + QK_NOPE], kr]
        return jnp.concatenate(qs, axis=1) * MLA_QSCALE, jnp.concatenate(ks, axis=1)

    rope_ins = [("row", qraw, 0, H * QK_PAD), ("row", kv, 0, H * QK_PAD), ("row", z0, KROFF0 // LANES, LANES),
                ("row", cos_t, 0, LANES), ("row", sin_t, 0, LANES)]
    q, kf = _rowwise("l0_rope", f_rope, nt, TR, rope_ins, [("row", TA, H * QK_PAD, BF16), ("row", TA, H * QK_PAD, BF16)])
    o_lat, lse, got_late = _mla_fwd_lat(q, kf, kv, T, TA, _Hosted([w_half[n] for n in late], "ag_xy"))
    o_ctx = _mla_fwd_ctx(q, kf, kv, T)
    full.update(pair_up("b", late, got_late))
    w_in1 = full["l1_w_in"]
    w_in1p = jnp.concatenate([w_in1[:, o_v1:o_lr1], w_in1[:, o_og1:], w_in1[:, :o_v1], w_in1[:, o_q1:o_og1],
                              w_in1[:, o_lr1:o_q1], jnp.zeros((D, LANES - 2 * GATE_RANK), BF16)], axis=1)

    conv_a = full["l0_conv_a"]

    def f_conva(i, ax, ab, ac_, pax, pac, nax, nac, w):
        prev, nxt = _shifters(i, TR, T, TA)
        p = ac_ * ax
        return (ab * (w[0:1] * prev(p, pac * pax) + w[1:2] * p + w[2:3] * nxt(p, nac * nax)),)

    conva_ins = [("row", z0, 0, CC), ("row", z0, 1, CC), ("row", z0, 2, CC), ("prev", z0, 0, CC), ("prev", z0, 2, CC),
                 ("next", z0, 0, CC), ("next", z0, 2, CC), ("full", conv_a)]
    (y_a,) = _rowwise("l0_conva", f_conva, nt, TR, conva_ins, [("row", TA, CC, BF16)])
    cat0 = jnp.concatenate([y_a, jnp.concatenate([o_lat, o_ctx], axis=0)], axis=1)
    y0 = _matmul("l0_out", cat0, full["l0_w_out"], "nn", TA, D, CC + H * V_HEAD)
    resmod_outs = [("row", TA, D, F32), ("row", TA, D, BF16)]
    X1, h2_0 = _rowwise("l0_resmod2", _f_resmod, nt, TR,
                        [("row", X, 0, D), ("row", y0, 0, D), ("grp", P(0, 2)), ("full", n2_0), ("grp", P(0, 3)), ("grp", P(0, 4))],
                        resmod_outs, nlat)
    cw0, cb0 = full["l0_ffn_conv_w"], row(l0_ffn_conv_b)
    u0, act0, f0 = _ffn_fwd("l0_ffn", h2_0, full["l0_ffn_up"], cw0, cb0, full["l0_ffn_down"], TA, T, TA, F)
    X2, h1 = _rowwise("l1_resmod1", _f_resmod, nt, TR,
                      [("row", X1, 0, D), ("row", f0, 0, D), ("grp", P(0, 5)), ("full", n1_1), ("grp", P(1, 0)), ("grp", P(1, 1))],
                      resmod_outs, nlat)

    z1 = _matmul("l1_in", h1, w_in1p, "nn", TA, W1, D)

    def f_gates(i, lr, wg, bg):
        pre = jnp.dot(lr.astype(BF16), wg.astype(BF16), preferred_element_type=F32) + bg
        return (_log_sigmoid(pre) / GATE_NORMALIZER,)

    gate_ins = [("row", z1, LROFF // LANES, LANES), ("full", w_gate), ("full", b_gate)]
    (gates,) = _rowwise("l1_gates", f_gates, nt, TR, gate_ins, [("row", TA, 2 * KEY, F32)])
    gla_dims = (DK, DV, QOFF1, KOFF1)
    o_f, o_b, st_f, st_b = _gla_fwd(z1, gates, gla_dims, nt, nlat)
    onw = row(l1_o_norm)

    def f_glaout(i, of, ob, og, w):
        o = of + ob
        on = jnp.concatenate([_rms(o[:, h * DV:(h + 1) * DV], w) for h in range(GLA_HEADS)], axis=1)
        return (on * _silu(og),)

    glaout_ins = [("row", o_f, 0, VAL), ("row", o_b, 0, VAL), ("row", z1, OGOFF // VAL, VAL), ("full", onw)]
    (go,) = _rowwise("l1_glaout", f_glaout, nlat, TR, glaout_ins, [("row", T, VAL, BF16)])
    y1 = _matmul("l1_out", go, full["l1_w_out"], "nn", T, D, VAL)
    X3, h2_1 = _rowwise("l1_resmod2", _f_resmod, nlat, TR,
                        [("row", X2, 0, D), ("row", y1, 0, D), ("grp", P(1, 2)), ("full", n2_1), ("grp", P(1, 3)), ("grp", P(1, 4))],
                        [("row", T, D, F32), ("row", T, D, BF16)])
    cw1, cb1 = full["l1_ffn_conv_w"], row(l1_ffn_conv_b)
    u1, act1, f1 = _ffn_fwd("l1_ffn", h2_1, full["l1_ffn_up"], cw1, cb1, full["l1_ffn_down"], T, T, T, F)

    fnw = row(final_norm)

    def f_head(i, x3, ff, g5, w, tgt):
        fin = lambda a, b, g_, w_: _rms(a + g_ * b, w_)
        y, pull = jax.vjp(fin, x3, ff, g5, w)
        err = y - tgt
        dx3, dff, dg5, dw = pull(err / D)
        loss = 0.5 * jnp.sum(jnp.mean(err * err, axis=-1, keepdims=True), axis=0, keepdims=True)
        return jnp.broadcast_to(loss, (1, LANES)), dx3, dff, dg5, dw

    head_ins = [("row", X3, 0, D), ("row", f1, 0, D), ("grp", P(1, 5)), ("full", fnw), ("row", loss_target[0], 0, D)]
    loss_acc, dX3, df1, dm5_1, g_final = _rowwise(
        "loss_head", f_head, nlat, TR, head_ins,
        [("acc", 1, LANES), ("row", T, D, F32), ("row", T, D, BF16), ("acc", 1, D), ("acc", 1, D)])
    loss = lax.psum(loss_acc[0, 0], ("x", "y", "c"))

    grads = {"final_norm": g_final}
    dh2_1, grads["l1_ffn_up"], grads["l1_ffn_conv_w"], grads["l1_ffn_conv_b"], grads["l1_ffn_down"] = _ffn_bwd(
        "l1_dffn", df1, h2_1, u1, act1, full["l1_ffn_up"], cw1, cb1, full["l1_ffn_down"], T, T, T, F)
    resmod_bwd = _vjp_of(_f_resmod, 6, (0, 1, 2, 3, 4, 5))
    dX2l, dy1, dm2_1, grads["l1_norm2"], dm3_1, dm4_1 = _rowwise(
        "l1_dresmod2", resmod_bwd, nlat, TR,
        [("row", X2, 0, D), ("row", y1, 0, D), ("grp", P(1, 2)), ("full", n2_1), ("grp", P(1, 3)), ("grp", P(1, 4)),
         ("row", dX3, 0, D), ("row", dh2_1, 0, D)],
        [("row", T, D, F32), ("row", T, D, BF16)] + [("acc", 1, D)] * 4)
    dgo = _matmul("l1_dout_x", dy1, full["l1_w_out"], "nt", T, VAL, D)
    grads["l1_w_out"] = _matmul("l1_dout_w", go, dy1, "tn", VAL, D, T, out_dtype=BF16)

    def f_glaout_bwd(i, of, ob, og, w, d):
        is_ctx = i >= nlat
        _, pull = jax.vjp(lambda a, b, c_, w_: f_glaout(i, a, b, c_, w_)[0], of, ob, og, w)
        dof, _, dog, dw = pull(jnp.where(is_ctx, 0.0, d))
        return dof, dog, dw

    glaout_b_ins = glaout_ins + [("rowc", dgo, 0, VAL, nlat - 1)]
    do_gla, dog, g_onorm = _rowwise("l1_dglaout", f_glaout_bwd, nt, TR, glaout_b_ins,
                                    [("row", TA, VAL, F32), ("row", TA, VAL, F32), ("acc", 1, DV)])
    grads["l1_o_norm"] = g_onorm
    (dq_f, dk_f, dv_f, dg_f), (dq_b, dk_b, dv_b, dg_b) = _gla_bwd(z1, gates, st_f, st_b, do_gla, gla_dims, nt, nlat)

    def f_dz1(i, dvf, dvb, dog_, dkf, dkb, dqf, dqb, dgf, dgb, lr, wg, bg):
        _, pull = jax.vjp(lambda a, b, c_: f_gates(i, a, b, c_)[0], lr, wg, bg)
        dlr, dwg, dbg = pull(jnp.concatenate([dgf, dgb], axis=1))
        return jnp.concatenate([dvf + dvb, dog_, dkf + dkb, dqf + dqb, dlr], axis=1), dwg, dbg

    dz1_ins = [("row", dv_f, 0, VAL), ("row", dv_b, 0, VAL), ("row", dog, 0, VAL), ("row", dk_f, 0, KEY), ("row", dk_b, 0, KEY),
               ("row", dq_f, 0, KEY), ("row", dq_b, 0, KEY), ("row", dg_f, 0, KEY), ("row", dg_b, 0, KEY)] + gate_ins
    dz1, g_wgate, g_bgate = _rowwise("l1_dz", f_dz1, nt, TR, dz1_ins, [("row", TA, W1, BF16), ("acc", LANES, 2 * KEY), ("acc", 1, 2 * KEY)])
    grads["l1_gate_fw_w"], grads["l1_gate_bw_w"] = g_wgate[:GATE_RANK, :KEY], g_wgate[GATE_RANK:2 * GATE_RANK, KEY:]
    grads["l1_gate_fw_b"], grads["l1_gate_bw_b"] = g_bgate[:, :KEY], g_bgate[:, KEY:]
    dh1 = _matmul("l1_din_x", dz1, w_in1p, "nt", TA, D, W1)
    g1p = _matmul("l1_din_w", h1, dz1, "tn", D, W1, TA, out_dtype=BF16)
    grads["l1_w_in"] = jnp.concatenate([g1p[:, KOFF1:QOFF1], g1p[:, :OGOFF], g1p[:, LROFF:LROFF + 2 * GATE_RANK],
                                        g1p[:, QOFF1:LROFF], g1p[:, OGOFF:KOFF1]], axis=1)

    def f_resmod1_bwd(i, x_, y_, g_, w_, sh, sc, dx2, dh):
        return resmod_bwd(i, x_, y_, g_, w_, sh, sc, jnp.where(i >= nlat, 0.0, dx2), dh)

    dX1, df0, dm5_0, grads["l1_norm1"], dm0_1, dm1_1 = _rowwise(
        "l1_dresmod1", f_resmod1_bwd, nt, TR,
        [("row", X1, 0, D), ("row", f0, 0, D), ("grp", P(0, 5)), ("full", n1_1), ("grp", P(1, 0)), ("grp", P(1, 1)),
         ("rowc", dX2l, 0, D, nlat - 1), ("row", dh1, 0, D)],
        [("row", TA, D, F32), ("row", TA, D, BF16), ("accg", 1, D), ("acc", 1, D), ("accg", 1, D), ("accg", 1, D)], nlat)

    dh2_0, grads["l0_ffn_up"], grads["l0_ffn_conv_w"], grads["l0_ffn_conv_b"], grads["l0_ffn_down"] = _ffn_bwd(
        "l0_dffn", df0, h2_0, u0, act0, full["l0_ffn_up"], cw0, cb0, full["l0_ffn_down"], TA, T, TA, F)
    dXd, dy0, dm2_0, grads["l0_norm2"], dm3_0, dm4_0 = _rowwise(
        "l0_dresmod2", resmod_bwd, nt, TR,
        [("row", X, 0, D), ("row", y0, 0, D), ("grp", P(0, 2)), ("full", n2_0), ("grp", P(0, 3)), ("grp", P(0, 4)),
         ("row", dX1, 0, D), ("row", dh2_0, 0, D)],
        [("row", TA, D, F32), ("row", TA, D, BF16), ("accg", 1, D), ("acc", 1, D), ("accg", 1, D), ("accg", 1, D)], nlat)
    dcat = _matmul("l0_dout_x", dy0, full["l0_w_out"], "nt", TA, CC + H * V_HEAD, D)
    grads["l0_w_out"] = _matmul("l0_dout_w", cat0, dy0, "tn", CC + H * V_HEAD, D, TA, out_dtype=BF16)

    def f_conva_bwd(i, ax, ab, ac_, pax, pab, pac, nax, nab, nac, dy, pdy, ndy, w):
        prev, nxt = _shifters(i, TR, T, TA)
        p = ac_ * ax
        pp, pn = prev(p, pac * pax), nxt(p, nac * nax)
        cv = w[0:1] * pp + w[1:2] * p + w[2:3] * pn
        dcv = dy * ab
        dp = w[0:1] * nxt(dcv, ndy * nab) + w[1:2] * dcv + w[2:3] * prev(dcv, pdy * pab)
        return (jnp.concatenate([dp * ac_, dy * cv, dp * ax], axis=1), _colsum(pp * dcv), _colsum(p * dcv), _colsum(pn * dcv))

    conva_b_ins = [("row", z0, 0, CC), ("row", z0, 1, CC), ("row", z0, 2, CC),
                   ("prev", z0, 0, CC), ("prev", z0, 1, CC), ("prev", z0, 2, CC),
                   ("next", z0, 0, CC), ("next", z0, 1, CC), ("next", z0, 2, CC),
                   ("row", dcat, 0, CC), ("prev", dcat, 0, CC), ("next", dcat, 0, CC), ("full", conv_a)]
    dz_a, ga0, ga1, ga2 = _rowwise("l0_dconva", f_conva_bwd, nt, TR, conva_b_ins, [("row", TA, 3 * CC, BF16)] + [("acc", 1, CC)] * 3)
    grads["l0_conv_a"] = jnp.concatenate([ga0, ga1, ga2], axis=0)
    do_cb0 = CC // V_HEAD
    row_sh = {"l0_w_out", "l1_w_out", "l0_ffn_down", "l1_ffn_down"}

    def pair_sums(tag, group):
        res = []
        for n in group:
            s = grads[n].reshape((4, -1) + grads[n].shape[1:]) if n in row_sh else _cols_to_shards(grads[n])
            res.append(s.reshape(4, 2, s.shape[1] // 2, s.shape[2]).transpose(1, 0, 2, 3))
        taken = _exchange("pair_split_grads_" + tag, [own_half(g, 1 - my_c) for g in res], "swap_c")
        return [_sum_pair("pairsum_" + n[3:], own_half(g, my_c), t, BF16) for n, g, t in zip(group, res, taken)]

    pair_late = pair_sums("a", late)
    dq_lat, dk_lat, dv_lat, got = _mla_bwd_lat(q, kf, kv, o_lat, dcat, lse, do_cb0, T, TA, _Hosted(pair_late, "a2a_xy"))
    landed_late = {n: set_own(l, own_half(p, chip), chip) for n, l, p in zip(late, got, pair_late)}
    dq_ctx, dk_ctx, dv_ctx = _mla_bwd_ctx(q, kf, kv, dcat, dk_lat, dv_lat, do_cb0, T)

    def f_rope_bwd(i, dql, dqc, dkl, dkc, dvl, dvc, cs, sn):
        is_ctx = i >= nlat
        dq_, dk_, dv_ = jnp.where(is_ctx, dqc, dql) * MLA_SCALE, jnp.where(is_ctx, dkc, dkl), jnp.where(is_ctx, dvc, dvl)
        dqs, dkvs, dkr = [], [], None
        for h in range(H):
            dqs += [dq_[:, h * QK_PAD:h * QK_PAD + QK_NOPE], _rope128_t(dq_[:, h * QK_PAD + QK_NOPE:(h + 1) * QK_PAD], cs, sn)]
            dkvs += [dk_[:, h * QK_PAD:h * QK_PAD + QK_NOPE].astype(BF16), dv_[:, h * V_HEAD:(h + 1) * V_HEAD]]
            part = dk_[:, h * QK_PAD + QK_NOPE:(h + 1) * QK_PAD]
            dkr = part if dkr is None else dkr + part
        return jnp.concatenate(dqs, axis=1), jnp.concatenate(dkvs, axis=1), _rope128_t(dkr, cs, sn)

    drope_ins = [("rowc", dq_lat, 0, H * QK_PAD, nlat - 1), ("full", dq_ctx), ("row", dk_lat, 0, H * QK_PAD), ("full", dk_ctx),
                 ("row", dv_lat, 0, H * V_HEAD), ("full", dv_ctx), ("row", cos_t, 0, LANES), ("row", sin_t, 0, LANES)]
    dqraw, dkv, dz_kr = _rowwise("l0_drope", f_rope_bwd, nt, TR, drope_ins,
                                 [("row", TA, H * QK_PAD, BF16), ("row", TA, H * QK_PAD, BF16), ("row", TA, LANES, BF16)])
    dqn = _matmul("l0_dqb_x", dqraw, w_qbp, "nt", TA, Q_LORA, H * QK_PAD)
    g_wqbp = _matmul("l0_dqb_w", qn, dqraw, "tn", Q_LORA, H * QK_PAD, TA).reshape(Q_LORA, H, QK_PAD)
    g_rope = g_wqbp[:, :, QK_NOPE:QK_NOPE + QK_ROPE] + _rot_cols_t(g_wqbp[:, :, QK_NOPE + QK_ROPE:])
    grads["l0_w_qb"] = jnp.concatenate([g_wqbp[:, :, :QK_NOPE], g_rope], axis=2).reshape(Q_LORA, H * (QK_NOPE + QK_ROPE)).astype(BF16)
    dkvn = _matmul("l0_dkvb_x", dkv, full["l0_w_kvb"], "nt", TA, KV_LORA, H * QK_PAD)
    grads["l0_w_kvb"] = _matmul("l0_dkvb_w", kvn, dkv, "tn", KV_LORA, H * QK_PAD, TA, out_dtype=BF16)
    norms_bwd = _vjp_of(_f_norms, 4, (0, 1, 2, 3))
    dz_kv, dz_q, grads["l0_kv_norm"], grads["l0_q_norm"] = _rowwise(
        "l0_dlatnorm", norms_bwd, nt, TR, norm_ins + [("row", dkvn, 0, KV_LORA), ("row", dqn, 0, Q_LORA)],
        [("row", TA, KV_LORA, BF16), ("row", TA, Q_LORA, BF16), ("acc", 1, KV_LORA), ("acc", 1, Q_LORA)])
    dz0 = jnp.concatenate([dz_a, dz_q, dz_kv, dz_kr, jnp.zeros((TA, W0 - KROFF0 - LANES), BF16)], axis=1)
    dh0 = _matmul("l0_din_x", dz0, w_in0p, "nt", TA, D, W0)
    g0p = _matmul("l0_din_w", h0, dz0, "tn", D, W0, TA)
    g_kr = g0p[:, KROFF0:KROFF0 + QK_ROPE] + _rot_cols_t(g0p[:, KROFF0 + QK_ROPE:KROFF0 + 2 * QK_ROPE])
    grads["l0_w_in"] = jnp.concatenate([g0p[:, KVOFF0:KROFF0], g_kr, g0p[:, QOFF0:KVOFF0], g0p[:, :QOFF0]], axis=1).astype(BF16)

    def f_mod_bwd(i, x_, w_, sh, sc, dh, dxd):
        _, pull = jax.vjp(lambda a, b, c_, d_: _modulate(a, b, c_, d_), x_, w_, sh, sc)
        dx, dw, dsh, dsc = pull(dh)
        return dx + dxd, dw, dsh, dsc

    dXf, grads["l0_norm1"], dm0_0, dm1_0 = _rowwise(
        "l0_dmod1", f_mod_bwd, nt, TR,
        [("row", X, 0, D), ("full", n1_0), ("grp", P(0, 0)), ("grp", P(0, 1)), ("row", dh0, 0, D), ("row", dXd, 0, D)],
        [("row", TA, D, F32), ("acc", 1, D), ("accg", 1, D), ("accg", 1, D)], nlat)
    grad_x = dXf[:T][None]

    zD = jnp.zeros((1, D), F32)
    lat = lambda a: a[0] if a.ndim == 3 else a
    cxt = lambda a: a[1] if a.ndim == 3 else zD
    dmods = []
    for parts in ((dm0_0, dm1_0, dm2_0, dm3_0, dm4_0, dm5_0), (dm0_1, dm1_1, dm2_1, dm3_1, dm4_1, dm5_1)):
        dmods.append((jnp.concatenate([lat(a) for a in parts], axis=1), jnp.concatenate([cxt(a) for a in parts], axis=1)))
    small_names = ["l0_norm1", "l0_norm2", "l0_kv_norm", "l0_q_norm", "l0_conv_a", "l0_ffn_conv_w", "l0_ffn_conv_b",
                   "l1_norm1", "l1_norm2", "l1_o_norm", "l1_gate_fw_w", "l1_gate_fw_b", "l1_gate_bw_w", "l1_gate_bw_b",
                   "l1_ffn_conv_w", "l1_ffn_conv_b", "final_norm"]
    pieces = [("dm0", dmods[0][0]), ("dmc0", dmods[0][1]), ("dm1", dmods[1][0]), ("dmc1", dmods[1][1])]
    pieces += [("l0_ada_b", dmods[0][0] + dmods[0][1]), ("l1_ada_b", dmods[1][0] + dmods[1][1])]
    pieces += [(n, grads[n]) for n in small_names]
    offs, cur = {}, 0
    for n, a in pieces:
        offs[n] = (cur, a.size, a.shape)
        cur += -(-a.size // LANES) * LANES
    n_pad = -(-cur // 1024) * 1024
    flat = jnp.concatenate([jnp.pad(a.reshape(-1), (0, -a.size % LANES)) for _, a in pieces] + [jnp.zeros((n_pad - cur,), F32)])
    (small_g,) = _exchange("gather_small", [flat.reshape(n_pad // LANES, LANES)], "ag_all")
    small_sum = _sum_slots("sum_small", small_g).reshape(-1)
    small_all = small_g.reshape(8, -1)
    take = lambda n: small_sum[offs[n][0]:offs[n][0] + offs[n][1]].reshape(offs[n][2])

    cc_parts = []
    for l in (0, 1):
        o_m, sz, _ = offs["dm%d" % l]
        d16 = jnp.concatenate([small_all[:, o_m:o_m + sz], take("dmc%d" % l).reshape(1, -1), jnp.zeros((7, sz), F32)], axis=0)
        d16 = lax.dynamic_slice_in_dim(d16.reshape(16, 4, ADA_S), chip, 1, axis=1).reshape(16, ADA_S)
        grads["l%d_ada_w" % l] = _matmul("ada_dw", act16, d16, "tn", D, ADA_S, 16, tn_cap=512)
        cc_parts.append(_matmul("ada_dx", d16, W["l%d_ada_w" % l], "nt", 16, D, ADA_S, tn_cap=512))
        grads["l%d_ada_b" % l] = take("l%d_ada_b" % l).reshape(-1)
    cc_g = _exchange("gather_cc", cc_parts, "ag_xy")

    def cc_call():
        def body(a_ref, b_ref, c_ref, o_ref):
            tot = a_ref[0, 8:9, :] + b_ref[0, 8:9, :]
            for j in range(1, 4):
                tot = tot + (a_ref[j, 8:9, :] + b_ref[j, 8:9, :])
            _, pull = jax.vjp(_silu, c_ref[...])
            o_ref[...] = pull(tot)[0]
        return pl.pallas_call(body, name="c_ctx_grad", out_shape=jax.ShapeDtypeStruct((1, D), F32))(cc_g[0], cc_g[1], row(c_ctx))

    grads["c_ctx"] = cc_call().reshape(-1)

    pair_early = pair_sums("b", early)
    landed = dict(landed_late)
    for n, l, p in zip(early, _exchange("scatter_grads", pair_early, "a2a_xy", fill_own=False), pair_early):
        landed[n] = set_own(l, own_half(p, chip), chip)
    halves = [_sum_slots("sum_" + n[3:], landed[n]) for n in big]
    others = _exchange("pair_grads", halves, "swap_c")

    out_g, out_d, out_m, out_v = {}, {}, {}, {}
    for n, mine, other in zip(big, halves, others):
        g2 = jnp.where(my_c == 0, jnp.concatenate([mine, other], 0), jnp.concatenate([other, mine], 0))
        out_g[n], out_d[n], out_m[n], out_v[n] = _adam("adam_" + n[3:], W[n], [g2], MO[n], VO[n])
    for l in (0, 1):
        n = "l%d_ada_w" % l
        out_g[n], out_d[n], out_m[n], out_v[n] = _adam("adam_ada_w", W[n], [grads[n]], MO[n], VO[n])
    rest = [n for n in names if n not in out_g]
    g_rest = {}
    for n in rest:
        if n == "c_ctx" or n.endswith("ada_b"):
            g_rest[n] = grads[n]
        elif n in small_sh:
            gfull = take(n)
            cs = gfull.shape[1] // 4
            g_rest[n] = lax.dynamic_slice_in_dim(gfull, chip * cs, cs, axis=1)
        else:
            g_rest[n] = take(n).reshape(W[n].shape)
    sizes = [W[n].size for n in rest]
    tot = sum(-(-s // LANES) * LANES for s in sizes)
    tot_pad = -(-tot // 1024) * 1024

    def pack(d):
        parts = [jnp.pad(d[n].reshape(-1), (0, -d[n].size % LANES)) for n in rest]
        return jnp.concatenate(parts + [jnp.zeros((tot_pad - tot,), F32)]).reshape(tot_pad // LANES, LANES)

    packed = _adam("adam_small", pack(W), [pack(g_rest)], pack(MO), pack(VO))
    cur = 0
    for n, s in zip(rest, sizes):
        for dst, arr in zip((out_g, out_d, out_m, out_v), packed):
            dst[n] = arr.reshape(-1)[cur:cur + s].reshape(W[n].shape)
        cur += -(-s // LANES) * LANES

    return (loss, grad_x, *[out_g[n] for n in names], *[out_d[n] for n in names],
            *[out_m[n] for n in names], *[out_v[n] for n in names])
```

```python
import numpy as np

import jax
import jax.numpy as jnp
from jax import lax
from jax.experimental import pallas as pl
from jax.experimental.pallas import tpu as pltpu

F32, BF16 = jnp.float32, jnp.bfloat16
MESH_ID = pl.DeviceIdType.MESH

EPS = 1e-6
N_MOD = 6
MLA_HEADS, QK_NOPE, QK_ROPE, V_HEAD, Q_LORA, KV_LORA = 8, 128, 64, 128, 512, 256
QK_PAD = 2 * QK_NOPE
ROPE_THETA, GRID_W = 10000.0, 64
GLA_HEADS, GATE_RANK, GATE_NORMALIZER, CHUNK = 4, 16, 16.0, 64
ADAM_LR, ADAM_B1, ADAM_B2, ADAM_EPS, ADAM_WD, ADAM_STEP = 0.001, 0.9, 0.999, 1e-08, 0.01, 10

LANES = 128
TR = 256
V7X_VMEM_BYTES = 64 * 2 ** 20
VMEM_LIMIT = V7X_VMEM_BYTES - 8 * 2 ** 20

NT_DIMS = (((1,), (1,)), ((), ()))
TN_DIMS = (((0,), (0,)), ((), ()))
NN_DIMS = (((1,), (0,)), ((), ()))

def _rot_cols(w):
    q = QK_ROPE // 4
    return jnp.concatenate([-w[..., q:2 * q], w[..., :q], -w[..., 3 * q:], w[..., 2 * q:3 * q]], axis=-1)


def _rot_cols_t(g):
    q = QK_ROPE // 4
    return jnp.concatenate([g[..., q:2 * q], -g[..., :q], g[..., 3 * q:], -g[..., 2 * q:3 * q]], axis=-1)


def _cparams(sem=None):
    return pltpu.CompilerParams(dimension_semantics=sem, vmem_limit_bytes=VMEM_LIMIT)


def _tile(dim, cap, quantum=LANES):
    if dim <= cap:
        return dim
    t = (cap // quantum) * quantum
    while t >= quantum:
        if dim % t == 0:
            return t
        t -= quantum
    return dim


_REL_XY = ((1, 0, 0), (0, 1, 0), (1, 1, 0))
_REL_ALL = tuple((a, b, c) for a in (0, 1) for b in (0, 1) for c in (0, 1))[1:]
_REL_C = ((0, 0, 1),)


_RELS = {"ag_xy": _REL_XY, "a2a_xy": _REL_XY, "ag_all": _REL_ALL, "swap_c": _REL_C}
_LEAD = {"ag_xy": (4,), "ag_all": (8,), "a2a_xy": (), "swap_c": ()}


def _remote_copies(mode, ins, outs, send, recv):
    x, y, c = lax.axis_index("x"), lax.axis_index("y"), lax.axis_index("c")
    chip, dev = 2 * x + y, 4 * x + 2 * y + c
    rels, plan = _RELS[mode], []
    for a, (i_ref, o) in enumerate(zip(ins, outs)):
        for r, (bx, by, bc) in enumerate(rels):
            px = 1 - x if bx else x
            py = 1 - y if by else y
            pc = 1 - c if bc else c
            pchip, pdev = 2 * px + py, 4 * px + 2 * py + pc
            if mode == "ag_xy":
                src, dst, mine = i_ref, o.at[chip], o.at[pchip]
            elif mode == "ag_all":
                src, dst, mine = i_ref, o.at[dev], o.at[pdev]
            elif mode == "a2a_xy":
                src, dst, mine = i_ref.at[pchip], o.at[chip], o.at[pchip]
            else:
                src, dst, mine = i_ref, o, o
            k = a * len(rels) + r
            mk = lambda d: pltpu.make_async_remote_copy(src_ref=src, dst_ref=d, send_sem=send.at[k], recv_sem=recv.at[k],
                                                        device_id=(px, py, pc), device_id_type=MESH_ID)
            plan.append((mk(dst), mk(mine)))
    return plan


def _exchange(name, arrs, mode, fill_own=True):
    n, nr = len(arrs), len(_RELS[mode])
    out_shape = tuple(jax.ShapeDtypeStruct(_LEAD[mode] + a.shape, a.dtype) for a in arrs)

    def body(*refs):
        ins, outs = refs[:n], refs[n:2 * n]
        send, recv, loc = refs[2 * n:]
        chip = 2 * lax.axis_index("x") + lax.axis_index("y")
        local = []
        if fill_own and mode != "swap_c":
            slot = 2 * chip + lax.axis_index("c") if mode == "ag_all" else chip
            for a in range(n):
                lc = pltpu.make_async_copy(ins[a].at[chip] if mode == "a2a_xy" else ins[a], outs[a].at[slot], loc.at[a])
                lc.start()
                local.append(lc)
        plan = _remote_copies(mode, ins, outs, send, recv)
        for cp, _ in plan:
            cp.start()
        for cp, landing in plan:
            cp.wait_send()
            landing.wait_recv()
        for lc in local:
            lc.wait()

    hbm = pl.BlockSpec(memory_space=pl.ANY)
    res = pl.pallas_call(
        body, name=name, out_shape=out_shape, in_specs=[hbm] * n, out_specs=tuple([hbm] * n),
        scratch_shapes=[pltpu.SemaphoreType.DMA((n * nr,)), pltpu.SemaphoreType.DMA((n * nr,)),
                        pltpu.SemaphoreType.DMA((max(n, 1),))],
    )(*arrs)
    return list(res)


class _Hosted:
    def __init__(self, arrs, mode):
        self.arrs, self.mode, self.n = list(arrs), mode, len(arrs)
        self.out_shape = [jax.ShapeDtypeStruct(_LEAD[mode] + a.shape, a.dtype) for a in arrs]
        self.specs = [pl.BlockSpec(memory_space=pl.ANY)] * self.n
        nsem = self.n * len(_RELS[mode])
        self.scratch = [pltpu.SemaphoreType.DMA((nsem,)), pltpu.SemaphoreType.DMA((nsem,))]

    def run(self, first, last, ins, outs, sems):
        @pl.when(first)
        def _():
            for cp, _ in _remote_copies(self.mode, ins, outs, *sems):
                cp.start()

        @pl.when(last)
        def _():
            for cp, landing in _remote_copies(self.mode, ins, outs, *sems):
                cp.wait_send()
                landing.wait_recv()


def _matmul(name, a, b, mode, M, N, K, out_dtype=F32, a_off=(0, 0), b_off=(0, 0), tm_cap=1024, tn_cap=1024, tk_cap=2048):
    tm, tn = _tile(M, tm_cap, LANES if M % LANES == 0 else 8), _tile(N, tn_cap)
    tk = K if K <= 4096 else _tile(K, tk_cap)
    nk = K // tk
    assert M % tm == 0 and N % tn == 0 and K % tk == 0, (name, M, N, K, tm, tn, tk)
    if mode == "nn":
        ab, bb, dims = (tm, tk), (tk, tn), NN_DIMS
        ai = lambda i, j, k: (i + a_off[0] // tm, k + a_off[1] // tk)
        bi = lambda i, j, k: (k + b_off[0] // tk, j + b_off[1] // tn)
        chk = (a_off[0] % tm, a_off[1] % tk, b_off[0] % tk, b_off[1] % tn)
    elif mode == "nt":
        ab, bb, dims = (tm, tk), (tn, tk), NT_DIMS
        ai = lambda i, j, k: (i + a_off[0] // tm, k + a_off[1] // tk)
        bi = lambda i, j, k: (j + b_off[0] // tn, k + b_off[1] // tk)
        chk = (a_off[0] % tm, a_off[1] % tk, b_off[0] % tn, b_off[1] % tk)
    else:
        ab, bb, dims = (tk, tm), (tk, tn), TN_DIMS
        ai = lambda i, j, k: (k + a_off[0] // tk, i + a_off[1] // tm)
        bi = lambda i, j, k: (k + b_off[0] // tk, j + b_off[1] // tn)
        chk = (a_off[0] % tk, a_off[1] % tm, b_off[0] % tk, b_off[1] % tn)
    assert not any(chk), (name, chk)

    def body(a_ref, b_ref, o_ref, *acc):
        p = lax.dot_general(a_ref[...].astype(BF16), b_ref[...].astype(BF16), dims, preferred_element_type=F32)
        if nk == 1:
            o_ref[...] = p.astype(out_dtype)
        else:
            k = pl.program_id(2)

            @pl.when(k == 0)
            def _():
                acc[0][...] = p

            @pl.when(k > 0)
            def _():
                acc[0][...] += p

            @pl.when(k == nk - 1)
            def _():
                o_ref[...] = acc[0][...].astype(out_dtype)

    return pl.pallas_call(
        body, name=name, out_shape=jax.ShapeDtypeStruct((M, N), out_dtype), grid=(M // tm, N // tn, nk),
        in_specs=[pl.BlockSpec(ab, ai), pl.BlockSpec(bb, bi)], out_specs=pl.BlockSpec((tm, tn), lambda i, j, k: (i, j)),
        scratch_shapes=[pltpu.VMEM((tm, tn), F32)] if nk > 1 else [],
        compiler_params=_cparams(("parallel", "parallel", "arbitrary")),
    )(a, b)


def _rowwise(name, fn, grid_n, tr, ins, outs, nlat=None):
    nlat = grid_n if nlat is None else nlat
    grp = lambda i: jnp.minimum(i // nlat, 1)
    in_specs, args = [], []
    for spec in ins:
        kind, arr = spec[0], spec[1]
        if kind == "row":
            in_specs.append(pl.BlockSpec((tr, spec[3]), lambda i, cb=spec[2]: (i, cb)))
        elif kind == "rowc":
            in_specs.append(pl.BlockSpec((tr, spec[3]), lambda i, cb=spec[2], mx=spec[4]: (jnp.minimum(i, mx), cb)))
        elif kind == "prev":
            in_specs.append(pl.BlockSpec((8, spec[3]), lambda i, cb=spec[2]: (jnp.maximum(i * (tr // 8) - 1, 0), cb)))
        elif kind == "next":
            nb = arr.shape[0] // 8
            in_specs.append(pl.BlockSpec((8, spec[3]), lambda i, cb=spec[2], nb=nb: (jnp.minimum((i + 1) * (tr // 8), nb - 1), cb)))
        elif kind == "grp":
            in_specs.append(pl.BlockSpec((None,) + arr.shape[1:], lambda i: (grp(i), 0, 0)))
        else:
            in_specs.append(pl.BlockSpec(arr.shape, lambda i: (0, 0)))
        args.append(arr)
    out_shape, out_specs = [], []
    for spec in outs:
        if spec[0] == "row":
            out_shape.append(jax.ShapeDtypeStruct((spec[1], spec[2]), spec[3]))
            out_specs.append(pl.BlockSpec((tr, spec[2]), lambda i: (i, 0)))
        elif spec[0] == "acc":
            out_shape.append(jax.ShapeDtypeStruct((spec[1], spec[2]), F32))
            out_specs.append(pl.BlockSpec((spec[1], spec[2]), lambda i: (0, 0)))
        else:
            out_shape.append(jax.ShapeDtypeStruct((2, spec[1], spec[2]), F32))
            out_specs.append(pl.BlockSpec((None, spec[1], spec[2]), lambda i: (grp(i), 0, 0)))
    n_in = len(ins)
    has_acc = any(s[0] != "row" for s in outs)

    def body(*refs):
        i = pl.program_id(0)
        res = fn(i, *[r[...] for r in refs[:n_in]])
        for spec, ref, val in zip(outs, refs[n_in:], res):
            if spec[0] == "row":
                ref[...] = val.astype(ref.dtype)
            else:
                first = (i == 0) if spec[0] == "acc" else jnp.logical_or(i == 0, i == nlat)

                @pl.when(first)
                def _(ref=ref, val=val):
                    ref[...] = val

                @pl.when(jnp.logical_not(first))
                def _(ref=ref, val=val):
                    ref[...] += val

    return pl.pallas_call(
        body, name=name, out_shape=tuple(out_shape), grid=(grid_n,), in_specs=in_specs, out_specs=tuple(out_specs),
        compiler_params=_cparams(("arbitrary",) if has_acc else ("parallel",)),
    )(*args)


def _vjp_of(fwd, n_in, wrt):
    def bwd(i, *args):
        _, pull = jax.vjp(lambda *a: fwd(i, *a), *args[:n_in])
        g = pull(tuple(args[n_in:]))
        return tuple(g[k] for k in wrt)
    return bwd


def _rms(x, w):
    return x * lax.rsqrt(jnp.mean(x * x, axis=-1, keepdims=True) + EPS) * w


def _modulate(x, w, shift, scale):
    return _rms(x, w) * (1.0 + scale) + shift


def _silu(x):
    return x * jax.nn.sigmoid(x)


def _log_sigmoid(x):
    return jnp.minimum(x, 0.0) - jnp.log(1.0 + jnp.exp(-jnp.abs(x)))


def _f_mod(i, x, w, sh, sc):
    return (_modulate(x, w, sh, sc),)


def _f_resmod(i, x, y, gate, w, sh, sc):
    x1 = x + gate * y
    return x1, _modulate(x1, w, sh, sc)


def _f_norms(i, kvl, ql, kvw, qw):
    return _rms(kvl, kvw), _rms(ql, qw)


def _f_silu(i, x):
    return (_silu(x),)


def _rope128(x2, cos, sin):
    return x2 * cos + pltpu.roll(x2, QK_ROPE, 1) * sin


def _rope128_t(d2, cos, sin):
    return d2 * cos + pltpu.roll(d2 * sin, QK_ROPE, 1)


def _shifters(i, tr, T, TA):
    assert T % tr == 0 and TA % tr == 0
    loc = lax.broadcasted_iota(jnp.int32, (tr, 1), 0)
    starts_seq = jnp.logical_or(i * tr == 0, i * tr == T)
    ends_seq = jnp.logical_or((i + 1) * tr == T, (i + 1) * tr == TA)

    def prev(x, halo):
        return jnp.where(loc == 0, jnp.where(starts_seq, 0.0, halo[7:8, :]), pltpu.roll(x, 1, 0))

    def nxt(x, halo):
        return jnp.where(loc == tr - 1, jnp.where(ends_seq, 0.0, halo[0:1, :]), pltpu.roll(x, tr - 1, 0))

    return prev, nxt


def _colsum(x):
    return jnp.sum(x, axis=0, keepdims=True)


MLA_SCALE = (QK_NOPE + QK_ROPE) ** -0.5
LOG2E, LN2 = 1.4426950408889634, 0.6931471805599453
MLA_QSCALE = MLA_SCALE * LOG2E
MLA_SUB = 256


def _mla_fwd_lat(q, kf, kv, T, TA, hosted):
    H = MLA_HEADS
    tq, tk = _tile(T, 1024, TR), _tile(TA, 1408)
    nq, nk = T // tq, TA // tk
    nh = hosted.n

    def body(q_ref, k_ref, v_ref, *rest):
        o_ref, lse_ref = rest[nh:nh + 2]
        m_sc, l_sc, acc_sc = rest[2 * nh + 2:2 * nh + 5]
        hi, qi, ki = pl.program_id(0), pl.program_id(1), pl.program_id(2)
        hosted.run(jnp.logical_and(hi == 0, jnp.logical_and(qi == 0, ki == 0)),
                   jnp.logical_and(hi == H - 1, jnp.logical_and(qi == nq - 1, ki == nk - 1)),
                   rest[:nh], rest[nh + 2:2 * nh + 2], rest[2 * nh + 5:])

        @pl.when(ki == 0)
        def _():
            m_sc[...] = jnp.full_like(m_sc, -jnp.inf)
            l_sc[...] = jnp.zeros_like(l_sc)
            acc_sc[...] = jnp.zeros_like(acc_sc)

        sb = min(tq, MLA_SUB)
        scores = lambda r: lax.dot_general(q_ref[r * sb:(r + 1) * sb, :], k_ref[...], NT_DIMS, preferred_element_type=F32)
        s_next = scores(0)
        for r in range(tq // sb):
            s, rows = s_next, slice(r * sb, (r + 1) * sb)
            if r + 1 < tq // sb:
                s_next = scores(r + 1)
            m_old = m_sc[rows, :]
            m_new = jnp.maximum(m_old, jnp.max(s, axis=-1, keepdims=True))
            alpha = jnp.exp2(m_old - m_new)
            p = jnp.exp2(s - m_new)
            l_sc[rows, :] = alpha * l_sc[rows, :] + jnp.sum(p, axis=-1, keepdims=True)
            acc_sc[rows, :] = alpha * acc_sc[rows, :] + jnp.dot(p.astype(BF16), v_ref[...], preferred_element_type=F32)
            m_sc[rows, :] = m_new

        @pl.when(ki == nk - 1)
        def _():
            o_ref[...] = (acc_sc[...] / l_sc[...]).astype(o_ref.dtype)
            lse_ref[...] = m_sc[...] + jnp.log2(l_sc[...])

    res = pl.pallas_call(
        body, name="mla_fwd_lat", grid=(H, nq, nk),
        out_shape=[jax.ShapeDtypeStruct((T, H * V_HEAD), BF16), jax.ShapeDtypeStruct((H, T, 1), F32)] + hosted.out_shape,
        in_specs=[pl.BlockSpec((tq, QK_PAD), lambda h, i, k: (i, h)), pl.BlockSpec((tk, QK_PAD), lambda h, i, k: (k, h)),
                  pl.BlockSpec((tk, V_HEAD), lambda h, i, k: (k, 2 * h + 1))] + hosted.specs,
        out_specs=[pl.BlockSpec((tq, V_HEAD), lambda h, i, k: (i, h)),
                   pl.BlockSpec((None, tq, 1), lambda h, i, k: (h, i, 0))] + hosted.specs,
        scratch_shapes=[pltpu.VMEM((tq, 1), F32), pltpu.VMEM((tq, 1), F32), pltpu.VMEM((tq, V_HEAD), F32)] + hosted.scratch,
        compiler_params=_cparams(("arbitrary", "arbitrary", "arbitrary")),
    )(q, kf, kv, *hosted.arrs)
    return res[0], res[1], list(res[2:])


def _mla_ctx_probs(q_ref, k_ref):
    s = lax.dot_general(q_ref[...], k_ref[...], NT_DIMS, preferred_element_type=F32)
    p = jnp.exp2(s - jnp.max(s, axis=-1, keepdims=True))
    return p, jnp.sum(p, axis=-1, keepdims=True)


def _mla_fwd_ctx(q, kf, kv, T):
    H, cb = MLA_HEADS, T // TR

    def body(q_ref, k_ref, v_ref, o_ref):
        p, l = _mla_ctx_probs(q_ref, k_ref)
        o_ref[...] = (jnp.dot(p.astype(BF16), v_ref[...], preferred_element_type=F32) / l).astype(o_ref.dtype)

    return pl.pallas_call(
        body, name="mla_fwd_ctx", grid=(H,), out_shape=jax.ShapeDtypeStruct((TR, H * V_HEAD), BF16),
        in_specs=[pl.BlockSpec((TR, QK_PAD), lambda h: (cb, h)), pl.BlockSpec((TR, QK_PAD), lambda h: (cb, h)),
                  pl.BlockSpec((TR, V_HEAD), lambda h: (cb, 2 * h + 1))],
        out_specs=pl.BlockSpec((TR, V_HEAD), lambda h: (0, h)), compiler_params=_cparams(("parallel",)),
    )(q, kf, kv)


def _mla_grads(p, q_ref, k_ref, v_ref, do, delta):
    dob = do.astype(BF16)
    dp = lax.dot_general(dob, v_ref[...], NT_DIMS, preferred_element_type=F32)
    ds = (p * (dp - delta)).astype(BF16)
    return (jnp.dot(ds, k_ref[...], preferred_element_type=F32),
            lax.dot_general(ds, q_ref[...], TN_DIMS, preferred_element_type=F32) * LN2,
            lax.dot_general(p.astype(BF16), dob, TN_DIMS, preferred_element_type=F32))


def _mla_bwd_lat(q, kf, kv, o, dcat, lse, do_cb0, T, TA, hosted):
    H = MLA_HEADS
    tq, tk = _tile(T, 1024, TR), _tile(TA, 1408)
    nq, nk = T // tq, TA // tk
    nh = hosted.n

    def body(q_ref, k_ref, v_ref, o_ref, do_ref, lse_ref, *rest):
        dq_ref, dk_ref, dv_ref = rest[nh:nh + 3]
        dk_acc, dv_acc = rest[2 * nh + 3:2 * nh + 5]
        hi, ki, qi = pl.program_id(0), pl.program_id(1), pl.program_id(2)
        hosted.run(jnp.logical_and(hi == 0, jnp.logical_and(qi == 0, ki == 0)),
                   jnp.logical_and(hi == H - 1, jnp.logical_and(qi == nq - 1, ki == nk - 1)),
                   rest[:nh], rest[nh + 3:2 * nh + 3], rest[2 * nh + 5:])

        @pl.when(jnp.logical_and(ki == 0, qi == 0))
        def _():
            dq_ref[...] = jnp.zeros_like(dq_ref)

        @pl.when(qi == 0)
        def _():
            dk_acc[...] = jnp.zeros_like(dk_acc)
            dv_acc[...] = jnp.zeros_like(dv_acc)

        sb = min(tq, MLA_SUB)

        def products(r):
            rows = slice(r * sb, (r + 1) * sb)
            dob = do_ref[rows, :].astype(BF16)
            return (lax.dot_general(q_ref[rows, :], k_ref[...], NT_DIMS, preferred_element_type=F32),
                    lax.dot_general(dob, v_ref[...], NT_DIMS, preferred_element_type=F32), dob)

        nxt, dk, dv = products(0), None, None
        for r in range(tq // sb):
            (s, dp, dob), rows = nxt, slice(r * sb, (r + 1) * sb)
            if r + 1 < tq // sb:
                nxt = products(r + 1)
            delta = jnp.sum(do_ref[rows, :] * o_ref[rows, :].astype(F32), axis=-1, keepdims=True)
            p = jnp.exp2(s - lse_ref[rows, :])
            ds = (p * (dp - delta)).astype(BF16)
            dq_ref[pl.ds(pl.multiple_of(qi * tq + r * sb, sb), sb), :] += jnp.dot(ds, k_ref[...], preferred_element_type=F32)
            dk_r = lax.dot_general(ds, q_ref[rows, :], TN_DIMS, preferred_element_type=F32)
            dv_r = lax.dot_general(p.astype(BF16), dob, TN_DIMS, preferred_element_type=F32)
            dk, dv = (dk_r, dv_r) if dk is None else (dk + dk_r, dv + dv_r)
        dk_acc[...] += dk
        dv_acc[...] += dv

        @pl.when(qi == nq - 1)
        def _():
            dk_ref[...] = dk_acc[...] * LN2
            dv_ref[...] = dv_acc[...].astype(dv_ref.dtype)

    res = pl.pallas_call(
        body, name="mla_bwd_lat", grid=(H, nk, nq),
        out_shape=[jax.ShapeDtypeStruct((T, H * QK_PAD), F32), jax.ShapeDtypeStruct((TA, H * QK_PAD), F32),
                   jax.ShapeDtypeStruct((TA, H * V_HEAD), BF16)] + hosted.out_shape,
        in_specs=[pl.BlockSpec((tq, QK_PAD), lambda h, k, i: (i, h)), pl.BlockSpec((tk, QK_PAD), lambda h, k, i: (k, h)),
                  pl.BlockSpec((tk, V_HEAD), lambda h, k, i: (k, 2 * h + 1)), pl.BlockSpec((tq, V_HEAD), lambda h, k, i: (i, h)),
                  pl.BlockSpec((tq, V_HEAD), lambda h, k, i: (i, do_cb0 + h)),
                  pl.BlockSpec((None, tq, 1), lambda h, k, i: (h, i, 0))] + hosted.specs,
        out_specs=[pl.BlockSpec((T, QK_PAD), lambda h, k, i: (0, h)), pl.BlockSpec((tk, QK_PAD), lambda h, k, i: (k, h)),
                   pl.BlockSpec((tk, V_HEAD), lambda h, k, i: (k, h))] + hosted.specs,
        scratch_shapes=[pltpu.VMEM((tk, QK_PAD), F32), pltpu.VMEM((tk, V_HEAD), F32)] + hosted.scratch,
        compiler_params=_cparams(("arbitrary", "arbitrary", "arbitrary")),
    )(q, kf, kv, o, dcat, lse, *hosted.arrs)
    return res[0], res[1], res[2], list(res[3:])


def _mla_bwd_ctx(q, kf, kv, dcat, dk_lat, dv_lat, do_cb0, T):
    H, cb = MLA_HEADS, T // TR

    def body(q_ref, k_ref, v_ref, do_ref, dkl_ref, dvl_ref, dq_ref, dk_ref, dv_ref):
        p, l = _mla_ctx_probs(q_ref, k_ref)
        do = do_ref[...]
        o = jnp.dot(p.astype(BF16), v_ref[...], preferred_element_type=F32) / l
        dq, dk, dv = _mla_grads(p / l, q_ref, k_ref, v_ref, do, jnp.sum(do * o, axis=-1, keepdims=True))
        dq_ref[...] = dq
        dk_ref[...] = dkl_ref[...] + dk
        dv_ref[...] = (dvl_ref[...].astype(F32) + dv).astype(dv_ref.dtype)

    at_ctx = lambda w, f: pl.BlockSpec((TR, w), lambda h: (cb, f(h)))
    at_0 = lambda w: pl.BlockSpec((TR, w), lambda h: (0, h))
    return pl.pallas_call(
        body, name="mla_bwd_ctx", grid=(H,),
        out_shape=(jax.ShapeDtypeStruct((TR, H * QK_PAD), F32), jax.ShapeDtypeStruct((TR, H * QK_PAD), F32),
                   jax.ShapeDtypeStruct((TR, H * V_HEAD), BF16)),
        in_specs=[at_ctx(QK_PAD, lambda h: h), at_ctx(QK_PAD, lambda h: h), at_ctx(V_HEAD, lambda h: 2 * h + 1),
                  at_ctx(V_HEAD, lambda h: do_cb0 + h), at_ctx(QK_PAD, lambda h: h), at_ctx(V_HEAD, lambda h: h)],
        out_specs=(at_0(QK_PAD), at_0(QK_PAD), at_0(V_HEAD)), compiler_params=_cparams(("parallel",)),
    )(q, kf, kv, dcat, dk_lat, dv_lat)


def _bdot_impl(a, b, dims):
    return lax.dot_general(a.astype(BF16), b.astype(BF16), dims, preferred_element_type=F32)


def _bdot(a, b, dims):
    @jax.custom_vjp
    def f(a_, b_):
        return _bdot_impl(a_, b_, dims)

    def fwd(a_, b_):
        return _bdot_impl(a_, b_, dims), (a_.astype(BF16), b_.astype(BF16))

    def bwd(res, ct):
        a_, b_ = res
        if dims == NN_DIMS:
            return _bdot_impl(ct, b_, NT_DIMS), _bdot_impl(a_, ct, TN_DIMS)
        if dims == NT_DIMS:
            return _bdot_impl(ct, b_, NN_DIMS), _bdot_impl(ct, a_, TN_DIMS)
        return _bdot_impl(b_, ct, NT_DIMS), _bdot_impl(a_, ct, NN_DIMS)

    f.defvjp(fwd, bwd)
    return f(a, b)


def _ones_dot_impl(ones, x, dims):
    hi = x.astype(BF16)
    r1 = x - hi.astype(F32)
    mid = r1.astype(BF16)
    lo = (r1 - mid.astype(F32)).astype(BF16)
    d = lambda t: lax.dot_general(ones, t, dims, preferred_element_type=F32)
    return (d(lo) + d(mid)) + d(hi)


@jax.custom_vjp
def _ones_dot(ones, x):
    return _ones_dot_impl(ones, x, NN_DIMS)


_ones_dot.defvjp(lambda ones, x: (_ones_dot_impl(ones, x, NN_DIMS), ones),
                 lambda ones, ct: (jnp.zeros_like(ones), _ones_dot_impl(ones, ct, TN_DIMS)))


def _gla_tile(st, q, k, v, g, rev, q_scale):
    nc = q.shape[0] // CHUNK
    ii = lax.broadcasted_iota(jnp.int32, (CHUNK, CHUNK), 0)
    jj = lax.broadcasted_iota(jnp.int32, (CHUNK, CHUNK), 1)
    tri = (jj >= ii) if rev else (jj <= ii)
    ones = tri.astype(BF16)
    outs = [None] * nc
    for ci in (range(nc - 1, -1, -1) if rev else range(nc)):
        sl = slice(ci * CHUNK, (ci + 1) * CHUNK)
        qc, kc, vc, gc = q[sl] * q_scale, k[sl], v[sl], g[sl]
        b = _ones_dot(ones, gc)
        bl = b[0:1] if rev else b[CHUNK - 1:CHUNK]
        kd, qe, ke = kc * jnp.exp(bl - b), qc * jnp.exp(b), kc * jnp.exp(-b)
        att = jnp.where(tri, _bdot(qe, ke, NT_DIMS), 0.0)
        outs[ci] = _bdot(att, vc, NN_DIMS) + _bdot(qe, st, NT_DIMS)
        st = st * jnp.exp(bl) + _bdot(vc, kd, TN_DIMS)
    return st, jnp.concatenate(outs, axis=0)


def _gla_tiles(nt, nlat):
    return (lambda p: (p + nlat) % nt), (lambda p: nt - 1 - p)


def _gla_fwd(z1, g, dims, nt, nlat):
    DK, DV, q_off, k_off = dims
    KEY, TA = GLA_HEADS * DK, nt * TR
    in_specs, out_o = [], []
    for d, tile in enumerate(_gla_tiles(nt, nlat)):
        in_specs += [pl.BlockSpec((TR, DK), lambda h, p, t=tile: (t(p), q_off // DK + h)),
                     pl.BlockSpec((TR, DK), lambda h, p, t=tile: (t(p), k_off // DK + h)),
                     pl.BlockSpec((TR, DV), lambda h, p, t=tile: (t(p), h)),
                     pl.BlockSpec((TR, DK), lambda h, p, t=tile, d=d: (t(p), d * GLA_HEADS + h))]
        out_o.append(pl.BlockSpec((TR, DV), lambda h, p, t=tile: (t(p), h)))
    st_spec = pl.BlockSpec((None, None, DV, DK), lambda h, p: (h, p, 0, 0))

    def body(qf, kf, vf, gf, qb, kb, vb, gb, of_ref, ob_ref, sf_ref, sb_ref, st_f, st_b):
        @pl.when(pl.program_id(1) == 0)
        def _():
            st_f[...] = jnp.zeros_like(st_f)
            st_b[...] = jnp.zeros_like(st_b)

        sf_ref[...] = st_f[...]
        sb_ref[...] = st_b[...]
        new_f, out_f = _gla_tile(st_f[...], qf[...], kf[...], vf[...], gf[...], False, DK ** -0.5)
        new_b, out_b = _gla_tile(st_b[...], qb[...], kb[...], vb[...], gb[...], True, DK ** -0.5)
        st_f[...] = new_f
        st_b[...] = new_b
        of_ref[...] = out_f
        ob_ref[...] = out_b

    o_shape = jax.ShapeDtypeStruct((TA, GLA_HEADS * DV), F32)
    s_shape = jax.ShapeDtypeStruct((GLA_HEADS, nt, DV, DK), F32)
    return pl.pallas_call(
        body, name="gla_fwd", grid=(GLA_HEADS, nt), out_shape=(o_shape, o_shape, s_shape, s_shape),
        in_specs=in_specs, out_specs=(out_o[0], out_o[1], st_spec, st_spec),
        scratch_shapes=[pltpu.VMEM((DV, DK), F32)] * 2, compiler_params=_cparams(("parallel", "arbitrary")),
    )(z1, z1, z1, g, z1, z1, z1, g)


def _gla_bwd(z1, g, st_f, st_b, do, dims, nt, nlat):
    DK, DV, q_off, k_off = dims
    KEY, VAL, TA = GLA_HEADS * DK, GLA_HEADS * DV, nt * TR
    pos = lambda s: nt - 1 - s
    in_specs, out_specs = [], []
    for d, tile in enumerate(_gla_tiles(nt, nlat)):
        at = lambda w, f, t=tile: pl.BlockSpec((TR, w), lambda h, s: (t(pos(s)), f(h)))
        in_specs += [at(DK, lambda h: q_off // DK + h), at(DK, lambda h: k_off // DK + h), at(DV, lambda h: h),
                     at(DK, lambda h, d=d: d * GLA_HEADS + h),
                     pl.BlockSpec((None, None, DV, DK), lambda h, s: (h, pos(s), 0, 0)), at(DV, lambda h: h)]
        out_specs += [at(DK, lambda h: h), at(DK, lambda h: h), at(DV, lambda h: h), at(DK, lambda h: h)]

    def one(refs, outs, dst, rev):
        q_ref, k_ref, v_ref, g_ref, st_ref, do_ref = refs
        _, pull = jax.vjp(lambda st, q, k, v, gg: _gla_tile(st, q, k, v, gg, rev, DK ** -0.5),
                          st_ref[...], q_ref[...], k_ref[...], v_ref[...], g_ref[...])
        grads = pull((dst[...], do_ref[...]))
        dst[...] = grads[0]
        for o_ref, val in zip(outs, grads[1:]):
            o_ref[...] = val

    def body(*refs):
        dst_f, dst_b = refs[20:]

        @pl.when(pl.program_id(1) == 0)
        def _():
            dst_f[...] = jnp.zeros_like(dst_f)
            dst_b[...] = jnp.zeros_like(dst_b)

        one(refs[0:6], refs[12:16], dst_f, False)
        one(refs[6:12], refs[16:20], dst_b, True)

    shapes = [jax.ShapeDtypeStruct((TA, KEY), F32), jax.ShapeDtypeStruct((TA, KEY), F32),
              jax.ShapeDtypeStruct((TA, VAL), F32), jax.ShapeDtypeStruct((TA, KEY), F32)]
    res = pl.pallas_call(
        body, name="gla_bwd", grid=(GLA_HEADS, nt), out_shape=shapes * 2, in_specs=in_specs, out_specs=out_specs,
        scratch_shapes=[pltpu.VMEM((DV, DK), F32)] * 2, compiler_params=_cparams(("parallel", "arbitrary")),
    )(z1, z1, z1, g, st_f, do, z1, z1, z1, g, st_b, do)
    return res[:4], res[4:]


def _sum_slots(name, arr):
    S, R, C = arr.shape
    tr = _tile(R, max(16, (2 ** 19) // max(C, 1) // 16 * 16), 16)

    def body(a_ref, o_ref):
        acc = a_ref[0].astype(F32)
        for s in range(1, S):
            acc = acc + a_ref[s].astype(F32)
        o_ref[...] = acc

    return pl.pallas_call(
        body, name=name, grid=(R // tr,), out_shape=jax.ShapeDtypeStruct((R, C), F32),
        in_specs=[pl.BlockSpec((S, tr, C), lambda i: (0, i, 0))], out_specs=pl.BlockSpec((tr, C), lambda i: (i, 0)),
        compiler_params=_cparams(("parallel",)),
    )(arr)


def _sum_pair(name, a, b, out_dtype):
    shape, C = a.shape, a.shape[-1]
    a2, b2 = a.reshape(-1, C), b.reshape(-1, C)
    R = a2.shape[0]
    tr = _tile(R, max(16, (2 ** 19) // max(C, 1) // 16 * 16), 16)
    fn = lambda i, p, q: (p.astype(F32) + q.astype(F32),)
    (res,) = _rowwise(name, fn, R // tr, tr, [("row", a2, 0, C), ("row", b2, 0, C)], [("row", R, C, out_dtype)])
    return res.reshape(shape)


def _adam_math(w, g, m, v):
    m2 = ADAM_B1 * m + (1.0 - ADAM_B1) * g
    v2 = ADAM_B2 * v + (1.0 - ADAM_B2) * (g * g)
    m_hat = m2 / (1.0 - ADAM_B1 ** ADAM_STEP)
    v_hat = v2 / (1.0 - ADAM_B2 ** ADAM_STEP)
    return -ADAM_LR * (m_hat / (jnp.sqrt(v_hat) + ADAM_EPS) + ADAM_WD * w), m2, v2


def _adam(name, w, gs, m, v):
    R, C = w.shape
    tr = _tile(R, max(8, (2 ** 19) // max(C, 1) // 8 * 8), 8)
    ng = len(gs)

    def fn(i, w_, *rest):
        g = rest[0] if ng == 1 else rest[0] + rest[1]
        d, m2, v2 = _adam_math(w_, g, rest[ng], rest[ng + 1])
        return g, d, m2, v2

    ins = [("row", a, 0, C) for a in (w, *gs, m, v)]
    return _rowwise(name, fn, R // tr, tr, ins, [("row", R, C, F32)] * 4)


def _ffn_fwd(tag, h, w_up, cw, cb, w_down, n_rows, T, TA, F):
    tr = 128
    u = _matmul(tag + "_up", h, w_up, "nn", n_rows, 2 * F, h.shape[1])
    cwg, cwv, cbg, cbv = cw[:, :F], cw[:, F:], cb[:, :F], cb[:, F:]

    def fn(i, ug, uv, pg, pv, ng, nv, wg, wv, bg, bv):
        prev, nxt = _shifters(i, tr, T, TA)
        cg = wg[0:1] * prev(ug, pg) + wg[1:2] * ug + wg[2:3] * nxt(ug, ng) + bg
        cv = wv[0:1] * prev(uv, pv) + wv[1:2] * uv + wv[2:3] * nxt(uv, nv) + bv
        return (_silu(cg) * cv,)

    ins = [("row", u, 0, F), ("row", u, 1, F), ("prev", u, 0, F), ("prev", u, 1, F), ("next", u, 0, F), ("next", u, 1, F),
           ("full", cwg), ("full", cwv), ("full", cbg), ("full", cbv)]
    (act,) = _rowwise(tag + "_conv", fn, n_rows // tr, tr, ins, [("row", n_rows, F, BF16)])
    f = _matmul(tag + "_down", act, w_down, "nn", n_rows, w_down.shape[1], F)
    return u, act, f


def _ffn_bwd(tag, df, h, u, act, w_up, cw, cb, w_down, n_rows, T, TA, F):
    tr = 128
    D = w_down.shape[1]
    dact = _matmul(tag + "_ddown_x", df, w_down, "nt", n_rows, F, D)
    g_down = _matmul(tag + "_ddown_w", act, df, "tn", F, D, n_rows, out_dtype=BF16)
    cwg, cwv, cbg, cbv = cw[:, :F], cw[:, F:], cb[:, :F], cb[:, F:]

    def fn_a(i, ug, uv, pg, pv, ng, nv, da, wg, wv, bg, bv):
        prev, nxt = _shifters(i, tr, T, TA)
        ugp, ugn, uvp, uvn = prev(ug, pg), nxt(ug, ng), prev(uv, pv), nxt(uv, nv)
        cg = wg[0:1] * ugp + wg[1:2] * ug + wg[2:3] * ugn + bg
        cv = wv[0:1] * uvp + wv[1:2] * uv + wv[2:3] * uvn + bv
        sg = jax.nn.sigmoid(cg)
        dcv = da * (cg * sg)
        dcg = da * cv * (sg * (1.0 + cg * (1.0 - sg)))
        return (jnp.concatenate([dcg, dcv], axis=1),
                _colsum(ugp * dcg), _colsum(ug * dcg), _colsum(ugn * dcg),
                _colsum(uvp * dcv), _colsum(uv * dcv), _colsum(uvn * dcv), _colsum(dcg), _colsum(dcv))

    ins = [("row", u, 0, F), ("row", u, 1, F), ("prev", u, 0, F), ("prev", u, 1, F), ("next", u, 0, F), ("next", u, 1, F),
           ("row", dact, 0, F), ("full", cwg), ("full", cwv), ("full", cbg), ("full", cbv)]
    res = _rowwise(tag + "_dconv_a", fn_a, n_rows // tr, tr, ins, [("row", n_rows, 2 * F, F32)] + [("acc", 1, F)] * 8)
    duc = res[0]
    g_cw = jnp.concatenate([jnp.concatenate(res[1:4], axis=0), jnp.concatenate(res[4:7], axis=0)], axis=1)
    g_cb = jnp.concatenate([res[7], res[8]], axis=1)

    def fn_b(i, dg, dv, pg, pv, ng, nv, wg, wv):
        prev, nxt = _shifters(i, tr, T, TA)
        dug = wg[0:1] * nxt(dg, ng) + wg[1:2] * dg + wg[2:3] * prev(dg, pg)
        duv = wv[0:1] * nxt(dv, nv) + wv[1:2] * dv + wv[2:3] * prev(dv, pv)
        return (jnp.concatenate([dug, duv], axis=1),)

    ins = [("row", duc, 0, F), ("row", duc, 1, F), ("prev", duc, 0, F), ("prev", duc, 1, F), ("next", duc, 0, F),
           ("next", duc, 1, F), ("full", cwg), ("full", cwv)]
    (du,) = _rowwise(tag + "_dconv_b", fn_b, n_rows // tr, tr, ins, [("row", n_rows, 2 * F, BF16)])
    dh = _matmul(tag + "_dup_x", du, w_up, "nt", n_rows, D, 2 * F)
    g_up = _matmul(tag + "_dup_w", h, du, "tn", D, 2 * F, n_rows, out_dtype=BF16)
    return dh, g_up, g_cw, g_cb, g_down


def _cols_to_shards(g):
    r, c = g.shape
    return g.reshape(r, 4, c // 4).transpose(1, 0, 2)


def _shards_to_cols(s):
    return s.transpose(1, 0, 2).reshape(s.shape[1], 4 * s.shape[2])


def kernel(x, c, ctx, c_ctx, l0_ada_w, l0_ada_b, l0_norm1, l0_w_in, l0_conv_a, l0_q_norm, l0_w_qb, l0_kv_norm, l0_w_kvb, l0_w_out, l0_norm2, l0_ffn_up, l0_ffn_conv_w, l0_ffn_conv_b, l0_ffn_down, l1_ada_w, l1_ada_b, l1_norm1, l1_w_in, l1_gate_fw_w, l1_gate_fw_b, l1_gate_bw_w, l1_gate_bw_b, l1_o_norm, l1_w_out, l1_norm2, l1_ffn_up, l1_ffn_conv_w, l1_ffn_conv_b, l1_ffn_down, final_norm, loss_target, m_c_ctx, m_l0_ada_w, m_l0_ada_b, m_l0_norm1, m_l0_w_in, m_l0_conv_a, m_l0_q_norm, m_l0_w_qb, m_l0_kv_norm, m_l0_w_kvb, m_l0_w_out, m_l0_norm2, m_l0_ffn_up, m_l0_ffn_conv_w, m_l0_ffn_conv_b, m_l0_ffn_down, m_l1_ada_w, m_l1_ada_b, m_l1_norm1, m_l1_w_in, m_l1_gate_fw_w, m_l1_gate_fw_b, m_l1_gate_bw_w, m_l1_gate_bw_b, m_l1_o_norm, m_l1_w_out, m_l1_norm2, m_l1_ffn_up, m_l1_ffn_conv_w, m_l1_ffn_conv_b, m_l1_ffn_down, m_final_norm, v_c_ctx, v_l0_ada_w, v_l0_ada_b, v_l0_norm1, v_l0_w_in, v_l0_conv_a, v_l0_q_norm, v_l0_w_qb, v_l0_kv_norm, v_l0_w_kvb, v_l0_w_out, v_l0_norm2, v_l0_ffn_up, v_l0_ffn_conv_w, v_l0_ffn_conv_b, v_l0_ffn_down, v_l1_ada_w, v_l1_ada_b, v_l1_norm1, v_l1_w_in, v_l1_gate_fw_w, v_l1_gate_fw_b, v_l1_gate_bw_w, v_l1_gate_bw_b, v_l1_o_norm, v_l1_w_out, v_l1_norm2, v_l1_ffn_up, v_l1_ffn_conv_w, v_l1_ffn_conv_b, v_l1_ffn_down, v_final_norm):
    W = dict(c_ctx=c_ctx, l0_ada_w=l0_ada_w, l0_ada_b=l0_ada_b, l0_norm1=l0_norm1, l0_w_in=l0_w_in, l0_conv_a=l0_conv_a, l0_q_norm=l0_q_norm, l0_w_qb=l0_w_qb, l0_kv_norm=l0_kv_norm, l0_w_kvb=l0_w_kvb, l0_w_out=l0_w_out, l0_norm2=l0_norm2, l0_ffn_up=l0_ffn_up, l0_ffn_conv_w=l0_ffn_conv_w, l0_ffn_conv_b=l0_ffn_conv_b, l0_ffn_down=l0_ffn_down, l1_ada_w=l1_ada_w, l1_ada_b=l1_ada_b, l1_norm1=l1_norm1, l1_w_in=l1_w_in, l1_gate_fw_w=l1_gate_fw_w, l1_gate_fw_b=l1_gate_fw_b, l1_gate_bw_w=l1_gate_bw_w, l1_gate_bw_b=l1_gate_bw_b, l1_o_norm=l1_o_norm, l1_w_out=l1_w_out, l1_norm2=l1_norm2, l1_ffn_up=l1_ffn_up, l1_ffn_conv_w=l1_ffn_conv_w, l1_ffn_conv_b=l1_ffn_conv_b, l1_ffn_down=l1_ffn_down, final_norm=final_norm)
    MO = dict(c_ctx=m_c_ctx, l0_ada_w=m_l0_ada_w, l0_ada_b=m_l0_ada_b, l0_norm1=m_l0_norm1, l0_w_in=m_l0_w_in, l0_conv_a=m_l0_conv_a, l0_q_norm=m_l0_q_norm, l0_w_qb=m_l0_w_qb, l0_kv_norm=m_l0_kv_norm, l0_w_kvb=m_l0_w_kvb, l0_w_out=m_l0_w_out, l0_norm2=m_l0_norm2, l0_ffn_up=m_l0_ffn_up, l0_ffn_conv_w=m_l0_ffn_conv_w, l0_ffn_conv_b=m_l0_ffn_conv_b, l0_ffn_down=m_l0_ffn_down, l1_ada_w=m_l1_ada_w, l1_ada_b=m_l1_ada_b, l1_norm1=m_l1_norm1, l1_w_in=m_l1_w_in, l1_gate_fw_w=m_l1_gate_fw_w, l1_gate_fw_b=m_l1_gate_fw_b, l1_gate_bw_w=m_l1_gate_bw_w, l1_gate_bw_b=m_l1_gate_bw_b, l1_o_norm=m_l1_o_norm, l1_w_out=m_l1_w_out, l1_norm2=m_l1_norm2, l1_ffn_up=m_l1_ffn_up, l1_ffn_conv_w=m_l1_ffn_conv_w, l1_ffn_conv_b=m_l1_ffn_conv_b, l1_ffn_down=m_l1_ffn_down, final_norm=m_final_norm)
    VO = dict(c_ctx=v_c_ctx, l0_ada_w=v_l0_ada_w, l0_ada_b=v_l0_ada_b, l0_norm1=v_l0_norm1, l0_w_in=v_l0_w_in, l0_conv_a=v_l0_conv_a, l0_q_norm=v_l0_q_norm, l0_w_qb=v_l0_w_qb, l0_kv_norm=v_l0_kv_norm, l0_w_kvb=v_l0_w_kvb, l0_w_out=v_l0_w_out, l0_norm2=v_l0_norm2, l0_ffn_up=v_l0_ffn_up, l0_ffn_conv_w=v_l0_ffn_conv_w, l0_ffn_conv_b=v_l0_ffn_conv_b, l0_ffn_down=v_l0_ffn_down, l1_ada_w=v_l1_ada_w, l1_ada_b=v_l1_ada_b, l1_norm1=v_l1_norm1, l1_w_in=v_l1_w_in, l1_gate_fw_w=v_l1_gate_fw_w, l1_gate_fw_b=v_l1_gate_fw_b, l1_gate_bw_w=v_l1_gate_bw_w, l1_gate_bw_b=v_l1_gate_bw_b, l1_o_norm=v_l1_o_norm, l1_w_out=v_l1_w_out, l1_norm2=v_l1_norm2, l1_ffn_up=v_l1_ffn_up, l1_ffn_conv_w=v_l1_ffn_conv_w, l1_ffn_conv_b=v_l1_ffn_conv_b, l1_ffn_down=v_l1_ffn_down, final_norm=v_final_norm)
    names = list(W)

    T, D = x.shape[1], x.shape[2]
    TC = ctx.shape[1]
    assert TC == TR and T % TR == 0
    TA = T + TC
    nt, nlat = TA // TR, T // TR
    CC = D // 2
    F = l0_ffn_down.shape[0] * 4
    DK, DV = D // 2 // GLA_HEADS, D // GLA_HEADS
    KEY, VAL = GLA_HEADS * DK, GLA_HEADS * DV
    ADA_S = l0_ada_w.shape[1]
    H = MLA_HEADS
    my_x, my_y, my_c = lax.axis_index("x"), lax.axis_index("y"), lax.axis_index("c")
    chip, dev = 2 * my_x + my_y, 4 * my_x + 2 * my_y + my_c
    row = lambda a: a.reshape(1, -1)

    big = ["l0_w_in", "l0_w_qb", "l0_w_kvb", "l0_w_out", "l0_ffn_up", "l0_ffn_down",
           "l1_w_in", "l1_w_out", "l1_ffn_up", "l1_ffn_down"]
    small_sh = ["l0_conv_a", "l0_ffn_conv_w", "l1_gate_fw_w", "l1_gate_bw_w", "l1_ffn_conv_w"]
    own_half = lambda a, k: lax.dynamic_index_in_dim(a, k, 0, keepdims=False)
    set_own = lambda arr, val, k: lax.dynamic_update_index_in_dim(arr, val[None], k, 0)
    in_core_order = lambda mine, other: jnp.where(my_c == 0, jnp.stack([mine, other], 1), jnp.stack([other, mine], 1))
    early, late = big[:3], big[3:]
    w_half = {n: own_half(W[n].astype(BF16).reshape(2, W[n].shape[0] // 2, -1), my_c) for n in big}
    col_sh = {"l0_w_in", "l0_w_qb", "l0_w_kvb", "l0_ffn_up", "l1_w_in", "l1_ffn_up", "l0_conv_a", "l0_ffn_conv_w",
              "l1_gate_fw_w", "l1_gate_bw_w", "l1_ffn_conv_w"}
    whole_w = lambda n, s: _shards_to_cols(s) if n in col_sh else s.reshape(-1, s.shape[-1])

    def pair_up(tag, group, got):
        got = [set_own(g, w_half[n], chip) for n, g in zip(group, got)]
        other = _exchange("pair_weights_" + tag, got, "swap_c")
        return {n: whole_w(n, in_core_order(g, o).reshape(4, 2 * g.shape[1], g.shape[2])) for n, g, o in zip(group, got, other)}

    full = pair_up("a", early, _exchange("gather_weights", [w_half[n] for n in early], "ag_xy", fill_own=False))
    for n, s in zip(small_sh, _exchange("gather_small_weights", [W[n] for n in small_sh], "ag_xy")):
        full[n] = whole_w(n, s)
    (c_all,) = _exchange("gather_c", [c], "ag_all")

    o_kv, o_kr, o_ql, o_ax = 0, KV_LORA, KV_LORA + QK_ROPE, KV_LORA + QK_ROPE + Q_LORA
    n_sel = 3 * CC + Q_LORA + KV_LORA + QK_ROPE
    W0 = -(-(n_sel + QK_ROPE) // 256) * 256
    QOFF0, KVOFF0, KROFF0 = 3 * CC, 3 * CC + Q_LORA, 3 * CC + Q_LORA + KV_LORA
    w_in0 = full["l0_w_in"]
    w_kr = w_in0[:, o_kr:o_kr + QK_ROPE]
    w_in0p = jnp.concatenate([w_in0[:, o_ax:], w_in0[:, o_ql:o_ax], w_in0[:, :o_kr], w_kr, _rot_cols(w_kr),
                              jnp.zeros((D, W0 - n_sel - QK_ROPE), BF16)], axis=1)
    wq3 = full["l0_w_qb"].reshape(Q_LORA, H, QK_NOPE + QK_ROPE)
    w_qbp = jnp.concatenate([wq3, _rot_cols(wq3[:, :, QK_NOPE:])], axis=2).reshape(Q_LORA, H * QK_PAD)
    o_v1, o_lr1, o_q1, o_og1 = KEY, KEY + VAL, KEY + VAL + 2 * GATE_RANK, 2 * KEY + VAL + 2 * GATE_RANK
    W1 = 2 * VAL + 2 * KEY + LANES
    OGOFF, KOFF1, QOFF1, LROFF = VAL, 2 * VAL, 2 * VAL + KEY, 2 * VAL + 2 * KEY
    w_gate = jnp.zeros((LANES, 2 * KEY), F32)
    w_gate = w_gate.at[:GATE_RANK, :KEY].set(full["l1_gate_fw_w"]).at[GATE_RANK:2 * GATE_RANK, KEY:].set(full["l1_gate_bw_w"])
    b_gate = jnp.concatenate([l1_gate_fw_b, l1_gate_bw_b]).reshape(1, -1)

    pos = np.arange(T)
    inv = ROPE_THETA ** (-np.arange(0, QK_ROPE // 2, 2, dtype=np.float32) / (QK_ROPE // 2))
    ar, ac = (pos // GRID_W).astype(np.float32)[:, None] * inv, (pos % GRID_W).astype(np.float32)[:, None] * inv
    ang = jnp.asarray(np.concatenate([ar, ar, ac, ac], axis=-1).astype(np.float32))
    zpad = jnp.zeros((TA, LANES - QK_ROPE), F32)
    cos_t = jnp.concatenate([jnp.concatenate([jnp.cos(ang), jnp.ones((TC, QK_ROPE), F32)], axis=0), zpad], axis=1)
    sin_t = jnp.concatenate([jnp.concatenate([jnp.sin(ang), jnp.zeros((TC, QK_ROPE), F32)], axis=0), zpad], axis=1)

    c16 = jnp.concatenate([c_all.reshape(8, D), c_ctx.reshape(1, D), jnp.zeros((7, D), F32)], axis=0)
    (act16,) = _rowwise("silu_c", _f_silu, 1, 16, [("row", c16, 0, D)], [("row", 16, D, F32)])
    mod_parts = [_matmul("ada_fwd", act16, W["l%d_ada_w" % l], "nn", 16, ADA_S, D, tn_cap=512) for l in (0, 1)]
    mod_g = _exchange("gather_mod", mod_parts, "ag_xy")
    mods = []
    for l in (0, 1):
        mfull = _shards_to_cols(mod_g[l]) + W["l%d_ada_b" % l][None, :]
        mine = lax.dynamic_slice_in_dim(mfull, dev, 1, axis=0)
        mods.append(jnp.concatenate([mine, mfull[8:9]], axis=0).reshape(2, N_MOD, D))
    P = lambda l, k: mods[l][:, k:k + 1, :]

    X = jnp.concatenate([x[0], ctx[0]], axis=0)
    n1_0, n2_0, n1_1, n2_1 = row(l0_norm1), row(l0_norm2), row(l1_norm1), row(l1_norm2)
    (h0,) = _rowwise("l0_mod1", _f_mod, nt, TR, [("row", X, 0, D), ("full", n1_0), ("grp", P(0, 0)), ("grp", P(0, 1))],
                     [("row", TA, D, BF16)], nlat)
    z0 = _matmul("l0_in", h0, w_in0p, "nn", TA, W0, D)
    kvn_w, qn_w = row(l0_kv_norm), row(l0_q_norm)
    norm_ins = [("row", z0, KVOFF0 // KV_LORA, KV_LORA), ("row", z0, QOFF0 // Q_LORA, Q_LORA), ("full", kvn_w), ("full", qn_w)]
    kvn, qn = _rowwise("l0_latnorm", _f_norms, nt, TR, norm_ins, [("row", TA, KV_LORA, BF16), ("row", TA, Q_LORA, BF16)])
    kv = _matmul("l0_kvb", kvn, full["l0_w_kvb"], "nn", TA, H * QK_PAD, KV_LORA, out_dtype=BF16)
    qraw = _matmul("l0_qb", qn, w_qbp, "nn", TA, H * QK_PAD, Q_LORA)

    def f_rope(i, qr, kvv, krr, cs, sn):
        kr = _rope128(krr, cs, sn).astype(BF16)
        qs, ks = [], []
        for h in range(H):
            qs += [qr[:, h * QK_PAD:h * QK_PAD + QK_NOPE], _rope128(qr[:, h * QK_PAD + QK_NOPE:(h + 1) * QK_PAD], cs, sn)]
            ks += [kvv[:, h * QK_PAD:h * QK_PAD + QK_NOPE], kr]
        return jnp.concatenate(qs, axis=1) * MLA_QSCALE, jnp.concatenate(ks, axis=1)

    rope_ins = [("row", qraw, 0, H * QK_PAD), ("row", kv, 0, H * QK_PAD), ("row", z0, KROFF0 // LANES, LANES),
                ("row", cos_t, 0, LANES), ("row", sin_t, 0, LANES)]
    q, kf = _rowwise("l0_rope", f_rope, nt, TR, rope_ins, [("row", TA, H * QK_PAD, BF16), ("row", TA, H * QK_PAD, BF16)])
    o_lat, lse, got_late = _mla_fwd_lat(q, kf, kv, T, TA, _Hosted([w_half[n] for n in late], "ag_xy"))
    o_ctx = _mla_fwd_ctx(q, kf, kv, T)
    full.update(pair_up("b", late, got_late))
    w_in1 = full["l1_w_in"]
    w_in1p = jnp.concatenate([w_in1[:, o_v1:o_lr1], w_in1[:, o_og1:], w_in1[:, :o_v1], w_in1[:, o_q1:o_og1],
                              w_in1[:, o_lr1:o_q1], jnp.zeros((D, LANES - 2 * GATE_RANK), BF16)], axis=1)

    conv_a = full["l0_conv_a"]

    def f_conva(i, ax, ab, ac_, pax, pac, nax, nac, w):
        prev, nxt = _shifters(i, TR, T, TA)
        p = ac_ * ax
        return (ab * (w[0:1] * prev(p, pac * pax) + w[1:2] * p + w[2:3] * nxt(p, nac * nax)),)

    conva_ins = [("row", z0, 0, CC), ("row", z0, 1, CC), ("row", z0, 2, CC), ("prev", z0, 0, CC), ("prev", z0, 2, CC),
                 ("next", z0, 0, CC), ("next", z0, 2, CC), ("full", conv_a)]
    (y_a,) = _rowwise("l0_conva", f_conva, nt, TR, conva_ins, [("row", TA, CC, BF16)])
    cat0 = jnp.concatenate([y_a, jnp.concatenate([o_lat, o_ctx], axis=0)], axis=1)
    y0 = _matmul("l0_out", cat0, full["l0_w_out"], "nn", TA, D, CC + H * V_HEAD)
    resmod_outs = [("row", TA, D, F32), ("row", TA, D, BF16)]
    X1, h2_0 = _rowwise("l0_resmod2", _f_resmod, nt, TR,
                        [("row", X, 0, D), ("row", y0, 0, D), ("grp", P(0, 2)), ("full", n2_0), ("grp", P(0, 3)), ("grp", P(0, 4))],
                        resmod_outs, nlat)
    cw0, cb0 = full["l0_ffn_conv_w"], row(l0_ffn_conv_b)
    u0, act0, f0 = _ffn_fwd("l0_ffn", h2_0, full["l0_ffn_up"], cw0, cb0, full["l0_ffn_down"], TA, T, TA, F)
    X2, h1 = _rowwise("l1_resmod1", _f_resmod, nt, TR,
                      [("row", X1, 0, D), ("row", f0, 0, D), ("grp", P(0, 5)), ("full", n1_1), ("grp", P(1, 0)), ("grp", P(1, 1))],
                      resmod_outs, nlat)

    z1 = _matmul("l1_in", h1, w_in1p, "nn", TA, W1, D)

    def f_gates(i, lr, wg, bg):
        pre = _bdot(lr, wg, NN_DIMS) + bg
        return (_log_sigmoid(pre) / GATE_NORMALIZER,)

    gate_ins = [("row", z1, LROFF // LANES, LANES), ("full", w_gate), ("full", b_gate)]
    (gates,) = _rowwise("l1_gates", f_gates, nt, TR, gate_ins, [("row", TA, 2 * KEY, F32)])
    gla_dims = (DK, DV, QOFF1, KOFF1)
    o_f, o_b, st_f, st_b = _gla_fwd(z1, gates, gla_dims, nt, nlat)
    onw = row(l1_o_norm)

    def f_glaout(i, of, ob, og, w):
        o = of + ob
        on = jnp.concatenate([_rms(o[:, h * DV:(h + 1) * DV], w) for h in range(GLA_HEADS)], axis=1)
        return (on * _silu(og),)

    glaout_ins = [("row", o_f, 0, VAL), ("row", o_b, 0, VAL), ("row", z1, OGOFF // VAL, VAL), ("full", onw)]
    (go,) = _rowwise("l1_glaout", f_glaout, nlat, TR, glaout_ins, [("row", T, VAL, BF16)])
    y1 = _matmul("l1_out", go, full["l1_w_out"], "nn", T, D, VAL)
    X3, h2_1 = _rowwise("l1_resmod2", _f_resmod, nlat, TR,
                        [("row", X2, 0, D), ("row", y1, 0, D), ("grp", P(1, 2)), ("full", n2_1), ("grp", P(1, 3)), ("grp", P(1, 4))],
                        [("row", T, D, F32), ("row", T, D, BF16)])
    cw1, cb1 = full["l1_ffn_conv_w"], row(l1_ffn_conv_b)
    u1, act1, f1 = _ffn_fwd("l1_ffn", h2_1, full["l1_ffn_up"], cw1, cb1, full["l1_ffn_down"], T, T, T, F)

    fnw = row(final_norm)

    def f_head(i, x3, ff, g5, w, tgt):
        fin = lambda a, b, g_, w_: _rms(a + g_ * b, w_)
        y, pull = jax.vjp(fin, x3, ff, g5, w)
        err = y - tgt
        dx3, dff, dg5, dw = pull(err / D)
        loss = 0.5 * jnp.sum(jnp.mean(err * err, axis=-1, keepdims=True), axis=0, keepdims=True)
        return jnp.broadcast_to(loss, (1, LANES)), dx3, dff, dg5, dw

    head_ins = [("row", X3, 0, D), ("row", f1, 0, D), ("grp", P(1, 5)), ("full", fnw), ("row", loss_target[0], 0, D)]
    loss_acc, dX3, df1, dm5_1, g_final = _rowwise(
        "loss_head", f_head, nlat, TR, head_ins,
        [("acc", 1, LANES), ("row", T, D, F32), ("row", T, D, BF16), ("acc", 1, D), ("acc", 1, D)])
    loss = lax.psum(loss_acc[0, 0], ("x", "y", "c"))

    grads = {"final_norm": g_final}
    dh2_1, grads["l1_ffn_up"], grads["l1_ffn_conv_w"], grads["l1_ffn_conv_b"], grads["l1_ffn_down"] = _ffn_bwd(
        "l1_dffn", df1, h2_1, u1, act1, full["l1_ffn_up"], cw1, cb1, full["l1_ffn_down"], T, T, T, F)
    resmod_bwd = _vjp_of(_f_resmod, 6, (0, 1, 2, 3, 4, 5))
    dX2l, dy1, dm2_1, grads["l1_norm2"], dm3_1, dm4_1 = _rowwise(
        "l1_dresmod2", resmod_bwd, nlat, TR,
        [("row", X2, 0, D), ("row", y1, 0, D), ("grp", P(1, 2)), ("full", n2_1), ("grp", P(1, 3)), ("grp", P(1, 4)),
         ("row", dX3, 0, D), ("row", dh2_1, 0, D)],
        [("row", T, D, F32), ("row", T, D, BF16)] + [("acc", 1, D)] * 4)
    dgo = _matmul("l1_dout_x", dy1, full["l1_w_out"], "nt", T, VAL, D)
    grads["l1_w_out"] = _matmul("l1_dout_w", go, dy1, "tn", VAL, D, T, out_dtype=BF16)

    def f_glaout_bwd(i, of, ob, og, w, d):
        is_ctx = i >= nlat
        _, pull = jax.vjp(lambda a, b, c_, w_: f_glaout(i, a, b, c_, w_)[0], of, ob, og, w)
        dof, _, dog, dw = pull(jnp.where(is_ctx, 0.0, d))
        return dof, dog, dw

    glaout_b_ins = glaout_ins + [("rowc", dgo, 0, VAL, nlat - 1)]
    do_gla, dog, g_onorm = _rowwise("l1_dglaout", f_glaout_bwd, nt, TR, glaout_b_ins,
                                    [("row", TA, VAL, F32), ("row", TA, VAL, F32), ("acc", 1, DV)])
    grads["l1_o_norm"] = g_onorm
    (dq_f, dk_f, dv_f, dg_f), (dq_b, dk_b, dv_b, dg_b) = _gla_bwd(z1, gates, st_f, st_b, do_gla, gla_dims, nt, nlat)

    def f_dz1(i, dvf, dvb, dog_, dkf, dkb, dqf, dqb, dgf, dgb, lr, wg, bg):
        _, pull = jax.vjp(lambda a, b, c_: f_gates(i, a, b, c_)[0], lr, wg, bg)
        dlr, dwg, dbg = pull(jnp.concatenate([dgf, dgb], axis=1))
        return jnp.concatenate([dvf + dvb, dog_, dkf + dkb, dqf + dqb, dlr], axis=1), dwg, dbg

    dz1_ins = [("row", dv_f, 0, VAL), ("row", dv_b, 0, VAL), ("row", dog, 0, VAL), ("row", dk_f, 0, KEY), ("row", dk_b, 0, KEY),
               ("row", dq_f, 0, KEY), ("row", dq_b, 0, KEY), ("row", dg_f, 0, KEY), ("row", dg_b, 0, KEY)] + gate_ins
    dz1, g_wgate, g_bgate = _rowwise("l1_dz", f_dz1, nt, TR, dz1_ins, [("row", TA, W1, BF16), ("acc", LANES, 2 * KEY), ("acc", 1, 2 * KEY)])
    grads["l1_gate_fw_w"], grads["l1_gate_bw_w"] = g_wgate[:GATE_RANK, :KEY], g_wgate[GATE_RANK:2 * GATE_RANK, KEY:]
    grads["l1_gate_fw_b"], grads["l1_gate_bw_b"] = g_bgate[:, :KEY], g_bgate[:, KEY:]
    dh1 = _matmul("l1_din_x", dz1, w_in1p, "nt", TA, D, W1)
    g1p = _matmul("l1_din_w", h1, dz1, "tn", D, W1, TA, out_dtype=BF16)
    grads["l1_w_in"] = jnp.concatenate([g1p[:, KOFF1:QOFF1], g1p[:, :OGOFF], g1p[:, LROFF:LROFF + 2 * GATE_RANK],
                                        g1p[:, QOFF1:LROFF], g1p[:, OGOFF:KOFF1]], axis=1)

    def f_resmod1_bwd(i, x_, y_, g_, w_, sh, sc, dx2, dh):
        return resmod_bwd(i, x_, y_, g_, w_, sh, sc, jnp.where(i >= nlat, 0.0, dx2), dh)

    dX1, df0, dm5_0, grads["l1_norm1"], dm0_1, dm1_1 = _rowwise(
        "l1_dresmod1", f_resmod1_bwd, nt, TR,
        [("row", X1, 0, D), ("row", f0, 0, D), ("grp", P(0, 5)), ("full", n1_1), ("grp", P(1, 0)), ("grp", P(1, 1)),
         ("rowc", dX2l, 0, D, nlat - 1), ("row", dh1, 0, D)],
        [("row", TA, D, F32), ("row", TA, D, BF16), ("accg", 1, D), ("acc", 1, D), ("accg", 1, D), ("accg", 1, D)], nlat)

    dh2_0, grads["l0_ffn_up"], grads["l0_ffn_conv_w"], grads["l0_ffn_conv_b"], grads["l0_ffn_down"] = _ffn_bwd(
        "l0_dffn", df0, h2_0, u0, act0, full["l0_ffn_up"], cw0, cb0, full["l0_ffn_down"], TA, T, TA, F)
    dXd, dy0, dm2_0, grads["l0_norm2"], dm3_0, dm4_0 = _rowwise(
        "l0_dresmod2", resmod_bwd, nt, TR,
        [("row", X, 0, D), ("row", y0, 0, D), ("grp", P(0, 2)), ("full", n2_0), ("grp", P(0, 3)), ("grp", P(0, 4)),
         ("row", dX1, 0, D), ("row", dh2_0, 0, D)],
        [("row", TA, D, F32), ("row", TA, D, BF16), ("accg", 1, D), ("acc", 1, D), ("accg", 1, D), ("accg", 1, D)], nlat)
    dcat = _matmul("l0_dout_x", dy0, full["l0_w_out"], "nt", TA, CC + H * V_HEAD, D)
    grads["l0_w_out"] = _matmul("l0_dout_w", cat0, dy0, "tn", CC + H * V_HEAD, D, TA, out_dtype=BF16)

    def f_conva_bwd(i, ax, ab, ac_, pax, pab, pac, nax, nab, nac, dy, pdy, ndy, w):
        prev, nxt = _shifters(i, TR, T, TA)
        p = ac_ * ax
        pp, pn = prev(p, pac * pax), nxt(p, nac * nax)
        cv = w[0:1] * pp + w[1:2] * p + w[2:3] * pn
        dcv = dy * ab
        dp = w[0:1] * nxt(dcv, ndy * nab) + w[1:2] * dcv + w[2:3] * prev(dcv, pdy * pab)
        return (jnp.concatenate([dp * ac_, dy * cv, dp * ax], axis=1), _colsum(pp * dcv), _colsum(p * dcv), _colsum(pn * dcv))

    conva_b_ins = [("row", z0, 0, CC), ("row", z0, 1, CC), ("row", z0, 2, CC),
                   ("prev", z0, 0, CC), ("prev", z0, 1, CC), ("prev", z0, 2, CC),
                   ("next", z0, 0, CC), ("next", z0, 1, CC), ("next", z0, 2, CC),
                   ("row", dcat, 0, CC), ("prev", dcat, 0, CC), ("next", dcat, 0, CC), ("full", conv_a)]
    dz_a, ga0, ga1, ga2 = _rowwise("l0_dconva", f_conva_bwd, nt, TR, conva_b_ins, [("row", TA, 3 * CC, BF16)] + [("acc", 1, CC)] * 3)
    grads["l0_conv_a"] = jnp.concatenate([ga0, ga1, ga2], axis=0)
    do_cb0 = CC // V_HEAD
    row_sh = {"l0_w_out", "l1_w_out", "l0_ffn_down", "l1_ffn_down"}

    def pair_sums(tag, group):
        res = []
        for n in group:
            s = grads[n].reshape((4, -1) + grads[n].shape[1:]) if n in row_sh else _cols_to_shards(grads[n])
            res.append(s.reshape(4, 2, s.shape[1] // 2, s.shape[2]).transpose(1, 0, 2, 3))
        taken = _exchange("pair_split_grads_" + tag, [own_half(g, 1 - my_c) for g in res], "swap_c")
        return [_sum_pair("pairsum_" + n[3:], own_half(g, my_c), t, BF16) for n, g, t in zip(group, res, taken)]

    pair_late = pair_sums("a", late)
    dq_lat, dk_lat, dv_lat, got = _mla_bwd_lat(q, kf, kv, o_lat, dcat, lse, do_cb0, T, TA, _Hosted(pair_late, "a2a_xy"))
    landed_late = {n: set_own(l, own_half(p, chip), chip) for n, l, p in zip(late, got, pair_late)}
    dq_ctx, dk_ctx, dv_ctx = _mla_bwd_ctx(q, kf, kv, dcat, dk_lat, dv_lat, do_cb0, T)

    def f_rope_bwd(i, dql, dqc, dkl, dkc, dvl, dvc, cs, sn):
        is_ctx = i >= nlat
        dq_, dk_, dv_ = jnp.where(is_ctx, dqc, dql) * MLA_SCALE, jnp.where(is_ctx, dkc, dkl), jnp.where(is_ctx, dvc, dvl)
        dqs, dkvs, dkr = [], [], None
        for h in range(H):
            dqs += [dq_[:, h * QK_PAD:h * QK_PAD + QK_NOPE], _rope128_t(dq_[:, h * QK_PAD + QK_NOPE:(h + 1) * QK_PAD], cs, sn)]
            dkvs += [dk_[:, h * QK_PAD:h * QK_PAD + QK_NOPE].astype(BF16), dv_[:, h * V_HEAD:(h + 1) * V_HEAD]]
            part = dk_[:, h * QK_PAD + QK_NOPE:(h + 1) * QK_PAD]
            dkr = part if dkr is None else dkr + part
        return jnp.concatenate(dqs, axis=1), jnp.concatenate(dkvs, axis=1), _rope128_t(dkr, cs, sn)

    drope_ins = [("rowc", dq_lat, 0, H * QK_PAD, nlat - 1), ("full", dq_ctx), ("row", dk_lat, 0, H * QK_PAD), ("full", dk_ctx),
                 ("row", dv_lat, 0, H * V_HEAD), ("full", dv_ctx), ("row", cos_t, 0, LANES), ("row", sin_t, 0, LANES)]
    dqraw, dkv, dz_kr = _rowwise("l0_drope", f_rope_bwd, nt, TR, drope_ins,
                                 [("row", TA, H * QK_PAD, BF16), ("row", TA, H * QK_PAD, BF16), ("row", TA, LANES, BF16)])
    dqn = _matmul("l0_dqb_x", dqraw, w_qbp, "nt", TA, Q_LORA, H * QK_PAD)
    g_wqbp = _matmul("l0_dqb_w", qn, dqraw, "tn", Q_LORA, H * QK_PAD, TA).reshape(Q_LORA, H, QK_PAD)
    g_rope = g_wqbp[:, :, QK_NOPE:QK_NOPE + QK_ROPE] + _rot_cols_t(g_wqbp[:, :, QK_NOPE + QK_ROPE:])
    grads["l0_w_qb"] = jnp.concatenate([g_wqbp[:, :, :QK_NOPE], g_rope], axis=2).reshape(Q_LORA, H * (QK_NOPE + QK_ROPE)).astype(BF16)
    dkvn = _matmul("l0_dkvb_x", dkv, full["l0_w_kvb"], "nt", TA, KV_LORA, H * QK_PAD)
    grads["l0_w_kvb"] = _matmul("l0_dkvb_w", kvn, dkv, "tn", KV_LORA, H * QK_PAD, TA, out_dtype=BF16)
    norms_bwd = _vjp_of(_f_norms, 4, (0, 1, 2, 3))
    dz_kv, dz_q, grads["l0_kv_norm"], grads["l0_q_norm"] = _rowwise(
        "l0_dlatnorm", norms_bwd, nt, TR, norm_ins + [("row", dkvn, 0, KV_LORA), ("row", dqn, 0, Q_LORA)],
        [("row", TA, KV_LORA, BF16), ("row", TA, Q_LORA, BF16), ("acc", 1, KV_LORA), ("acc", 1, Q_LORA)])
    dz0 = jnp.concatenate([dz_a, dz_q, dz_kv, dz_kr, jnp.zeros((TA, W0 - KROFF0 - LANES), BF16)], axis=1)
    dh0 = _matmul("l0_din_x", dz0, w_in0p, "nt", TA, D, W0)
    g0p = _matmul("l0_din_w", h0, dz0, "tn", D, W0, TA)
    g_kr = g0p[:, KROFF0:KROFF0 + QK_ROPE] + _rot_cols_t(g0p[:, KROFF0 + QK_ROPE:KROFF0 + 2 * QK_ROPE])
    grads["l0_w_in"] = jnp.concatenate([g0p[:, KVOFF0:KROFF0], g_kr, g0p[:, QOFF0:KVOFF0], g0p[:, :QOFF0]], axis=1).astype(BF16)

    def f_mod_bwd(i, x_, w_, sh, sc, dh, dxd):
        _, pull = jax.vjp(lambda a, b, c_, d_: _modulate(a, b, c_, d_), x_, w_, sh, sc)
        dx, dw, dsh, dsc = pull(dh)
        return dx + dxd, dw, dsh, dsc

    dXf, grads["l0_norm1"], dm0_0, dm1_0 = _rowwise(
        "l0_dmod1", f_mod_bwd, nt, TR,
        [("row", X, 0, D), ("full", n1_0), ("grp", P(0, 0)), ("grp", P(0, 1)), ("row", dh0, 0, D), ("row", dXd, 0, D)],
        [("row", TA, D, F32), ("acc", 1, D), ("accg", 1, D), ("accg", 1, D)], nlat)
    grad_x = dXf[:T][None]

    zD = jnp.zeros((1, D), F32)
    lat = lambda a: a[0] if a.ndim == 3 else a
    cxt = lambda a: a[1] if a.ndim == 3 else zD
    dmods = []
    for parts in ((dm0_0, dm1_0, dm2_0, dm3_0, dm4_0, dm5_0), (dm0_1, dm1_1, dm2_1, dm3_1, dm4_1, dm5_1)):
        dmods.append((jnp.concatenate([lat(a) for a in parts], axis=1), jnp.concatenate([cxt(a) for a in parts], axis=1)))
    small_names = ["l0_norm1", "l0_norm2", "l0_kv_norm", "l0_q_norm", "l0_conv_a", "l0_ffn_conv_w", "l0_ffn_conv_b",
                   "l1_norm1", "l1_norm2", "l1_o_norm", "l1_gate_fw_w", "l1_gate_fw_b", "l1_gate_bw_w", "l1_gate_bw_b",
                   "l1_ffn_conv_w", "l1_ffn_conv_b", "final_norm"]
    pieces = [("dm0", dmods[0][0]), ("dmc0", dmods[0][1]), ("dm1", dmods[1][0]), ("dmc1", dmods[1][1])]
    pieces += [("l0_ada_b", dmods[0][0] + dmods[0][1]), ("l1_ada_b", dmods[1][0] + dmods[1][1])]
    pieces += [(n, grads[n]) for n in small_names]
    offs, cur = {}, 0
    for n, a in pieces:
        offs[n] = (cur, a.size, a.shape)
        cur += -(-a.size // LANES) * LANES
    n_pad = -(-cur // 1024) * 1024
    flat = jnp.concatenate([jnp.pad(a.reshape(-1), (0, -a.size % LANES)) for _, a in pieces] + [jnp.zeros((n_pad - cur,), F32)])
    (small_g,) = _exchange("gather_small", [flat.reshape(n_pad // LANES, LANES)], "ag_all")
    small_sum = _sum_slots("sum_small", small_g).reshape(-1)
    small_all = small_g.reshape(8, -1)
    take = lambda n: small_sum[offs[n][0]:offs[n][0] + offs[n][1]].reshape(offs[n][2])

    cc_parts = []
    for l in (0, 1):
        o_m, sz, _ = offs["dm%d" % l]
        d16 = jnp.concatenate([small_all[:, o_m:o_m + sz], take("dmc%d" % l).reshape(1, -1), jnp.zeros((7, sz), F32)], axis=0)
        d16 = lax.dynamic_slice_in_dim(d16.reshape(16, 4, ADA_S), chip, 1, axis=1).reshape(16, ADA_S)
        grads["l%d_ada_w" % l] = _matmul("ada_dw", act16, d16, "tn", D, ADA_S, 16, tn_cap=512)
        cc_parts.append(_matmul("ada_dx", d16, W["l%d_ada_w" % l], "nt", 16, D, ADA_S, tn_cap=512))
        grads["l%d_ada_b" % l] = take("l%d_ada_b" % l).reshape(-1)
    cc_g = _exchange("gather_cc", cc_parts, "ag_xy")

    def cc_call():
        def body(a_ref, b_ref, c_ref, o_ref):
            tot = a_ref[0, 8:9, :] + b_ref[0, 8:9, :]
            for j in range(1, 4):
                tot = tot + (a_ref[j, 8:9, :] + b_ref[j, 8:9, :])
            _, pull = jax.vjp(_silu, c_ref[...])
            o_ref[...] = pull(tot)[0]
        return pl.pallas_call(body, name="c_ctx_grad", out_shape=jax.ShapeDtypeStruct((1, D), F32))(cc_g[0], cc_g[1], row(c_ctx))

    grads["c_ctx"] = cc_call().reshape(-1)

    pair_early = pair_sums("b", early)
    landed = dict(landed_late)
    for n, l, p in zip(early, _exchange("scatter_grads", pair_early, "a2a_xy", fill_own=False), pair_early):
        landed[n] = set_own(l, own_half(p, chip), chip)
    halves = [_sum_slots("sum_" + n[3:], landed[n]) for n in big]
    others = _exchange("pair_grads", halves, "swap_c")

    out_g, out_d, out_m, out_v = {}, {}, {}, {}
    for n, mine, other in zip(big, halves, others):
        g2 = jnp.where(my_c == 0, jnp.concatenate([mine, other], 0), jnp.concatenate([other, mine], 0))
        out_g[n], out_d[n], out_m[n], out_v[n] = _adam("adam_" + n[3:], W[n], [g2], MO[n], VO[n])
    for l in (0, 1):
        n = "l%d_ada_w" % l
        out_g[n], out_d[n], out_m[n], out_v[n] = _adam("adam_ada_w", W[n], [grads[n]], MO[n], VO[n])
    rest = [n for n in names if n not in out_g]
    g_rest = {}
    for n in rest:
        if n == "c_ctx" or n.endswith("ada_b"):
            g_rest[n] = grads[n]
        elif n in small_sh:
            gfull = take(n)
            cs = gfull.shape[1] // 4
            g_rest[n] = lax.dynamic_slice_in_dim(gfull, chip * cs, cs, axis=1)
        else:
            g_rest[n] = take(n).reshape(W[n].shape)
    sizes = [W[n].size for n in rest]
    tot = sum(-(-s // LANES) * LANES for s in sizes)
    tot_pad = -(-tot // 1024) * 1024

    def pack(d):
        parts = [jnp.pad(d[n].reshape(-1), (0, -d[n].size % LANES)) for n in rest]
        return jnp.concatenate(parts + [jnp.zeros((tot_pad - tot,), F32)]).reshape(tot_pad // LANES, LANES)

    packed = _adam("adam_small", pack(W), [pack(g_rest)], pack(MO), pack(VO))
    cur = 0
    for n, s in zip(rest, sizes):
        for dst, arr in zip((out_g, out_d, out_m, out_v), packed):
            dst[n] = arr.reshape(-1)[cur:cur + s].reshape(W[n].shape)
        cur += -(-s // LANES) * LANES

    return (loss, grad_x, *[out_g[n] for n in names], *[out_d[n] for n in names],
            *[out_m[n] for n in names], *[out_v[n] for n in names])
```

```python
import numpy as np

import jax
import jax.numpy as jnp
from jax import lax
from jax.experimental import pallas as pl
from jax.experimental.pallas import tpu as pltpu

F32, BF16 = jnp.float32, jnp.bfloat16
MESH_ID = pl.DeviceIdType.MESH

EPS = 1e-6
N_MOD = 6
MLA_HEADS, QK_NOPE, QK_ROPE, V_HEAD, Q_LORA, KV_LORA = 8, 128, 64, 128, 512, 256
QK_PAD = 2 * QK_NOPE
ROPE_THETA, GRID_W = 10000.0, 64
GLA_HEADS, GATE_RANK, GATE_NORMALIZER, CHUNK = 4, 16, 16.0, 64
ADAM_LR, ADAM_B1, ADAM_B2, ADAM_EPS, ADAM_WD, ADAM_STEP = 0.001, 0.9, 0.999, 1e-08, 0.01, 10

LANES = 128
TR = 256
V7X_VMEM_BYTES = 64 * 2 ** 20
VMEM_LIMIT = V7X_VMEM_BYTES - 8 * 2 ** 20

NT_DIMS = (((1,), (1,)), ((), ()))
TN_DIMS = (((0,), (0,)), ((), ()))
NN_DIMS = (((1,), (0,)), ((), ()))

def _rot_cols(w):
    q = QK_ROPE // 4
    return jnp.concatenate([-w[..., q:2 * q], w[..., :q], -w[..., 3 * q:], w[..., 2 * q:3 * q]], axis=-1)


def _rot_cols_t(g):
    q = QK_ROPE // 4
    return jnp.concatenate([g[..., q:2 * q], -g[..., :q], g[..., 3 * q:], -g[..., 2 * q:3 * q]], axis=-1)


def _cparams(sem=None):
    return pltpu.CompilerParams(dimension_semantics=sem, vmem_limit_bytes=VMEM_LIMIT)


def _tile(dim, cap, quantum=LANES):
    if dim <= cap:
        return dim
    t = (cap // quantum) * quantum
    while t >= quantum:
        if dim % t == 0:
            return t
        t -= quantum
    return dim


_REL_XY = ((1, 0, 0), (0, 1, 0), (1, 1, 0))
_REL_ALL = tuple((a, b, c) for a in (0, 1) for b in (0, 1) for c in (0, 1))[1:]
_REL_C = ((0, 0, 1),)


_RELS = {"ag_xy": _REL_XY, "a2a_xy": _REL_XY, "ag_all": _REL_ALL, "swap_c": _REL_C}
_LEAD = {"ag_xy": (4,), "ag_all": (8,), "a2a_xy": (), "swap_c": ()}


def _remote_copies(mode, ins, outs, send, recv):
    x, y, c = lax.axis_index("x"), lax.axis_index("y"), lax.axis_index("c")
    chip, dev = 2 * x + y, 4 * x + 2 * y + c
    rels, plan = _RELS[mode], []
    for a, (i_ref, o) in enumerate(zip(ins, outs)):
        for r, (bx, by, bc) in enumerate(rels):
            px = 1 - x if bx else x
            py = 1 - y if by else y
            pc = 1 - c if bc else c
            pchip, pdev = 2 * px + py, 4 * px + 2 * py + pc
            if mode == "ag_xy":
                src, dst, mine = i_ref, o.at[chip], o.at[pchip]
            elif mode == "ag_all":
                src, dst, mine = i_ref, o.at[dev], o.at[pdev]
            elif mode == "a2a_xy":
                src, dst, mine = i_ref.at[pchip], o.at[chip], o.at[pchip]
            else:
                src, dst, mine = i_ref, o, o
            k = a * len(rels) + r
            mk = lambda d: pltpu.make_async_remote_copy(src_ref=src, dst_ref=d, send_sem=send.at[k], recv_sem=recv.at[k],
                                                        device_id=(px, py, pc), device_id_type=MESH_ID)
            plan.append((mk(dst), mk(mine)))
    return plan


def _exchange(name, arrs, mode, fill_own=True):
    n, nr = len(arrs), len(_RELS[mode])
    out_shape = tuple(jax.ShapeDtypeStruct(_LEAD[mode] + a.shape, a.dtype) for a in arrs)

    def body(*refs):
        ins, outs = refs[:n], refs[n:2 * n]
        send, recv, loc = refs[2 * n:]
        chip = 2 * lax.axis_index("x") + lax.axis_index("y")
        local = []
        if fill_own and mode != "swap_c":
            slot = 2 * chip + lax.axis_index("c") if mode == "ag_all" else chip
            for a in range(n):
                lc = pltpu.make_async_copy(ins[a].at[chip] if mode == "a2a_xy" else ins[a], outs[a].at[slot], loc.at[a])
                lc.start()
                local.append(lc)
        plan = _remote_copies(mode, ins, outs, send, recv)
        for cp, _ in plan:
            cp.start()
        for cp, landing in plan:
            cp.wait_send()
            landing.wait_recv()
        for lc in local:
            lc.wait()

    hbm = pl.BlockSpec(memory_space=pl.ANY)
    res = pl.pallas_call(
        body, name=name, out_shape=out_shape, in_specs=[hbm] * n, out_specs=tuple([hbm] * n),
        scratch_shapes=[pltpu.SemaphoreType.DMA((n * nr,)), pltpu.SemaphoreType.DMA((n * nr,)),
                        pltpu.SemaphoreType.DMA((max(n, 1),))],
    )(*arrs)
    return list(res)


class _Hosted:
    def __init__(self, arrs, mode):
        self.arrs, self.mode, self.n = list(arrs), mode, len(arrs)
        self.out_shape = [jax.ShapeDtypeStruct(_LEAD[mode] + a.shape, a.dtype) for a in arrs]
        self.specs = [pl.BlockSpec(memory_space=pl.ANY)] * self.n
        nsem = self.n * len(_RELS[mode])
        self.scratch = [pltpu.SemaphoreType.DMA((nsem,)), pltpu.SemaphoreType.DMA((nsem,))]

    def run(self, first, last, ins, outs, sems):
        @pl.when(first)
        def _():
            for cp, _ in _remote_copies(self.mode, ins, outs, *sems):
                cp.start()

        @pl.when(last)
        def _():
            for cp, landing in _remote_copies(self.mode, ins, outs, *sems):
                cp.wait_send()
                landing.wait_recv()


def _matmul(name, a, b, mode, M, N, K, out_dtype=F32, a_off=(0, 0), b_off=(0, 0), tm_cap=1024, tn_cap=1024, tk_cap=2816):
    tm, tn = _tile(M, tm_cap, LANES if M % LANES == 0 else 8), _tile(N, tn_cap)
    tk = K if K <= 4096 else _tile(K, tk_cap)
    nk = K // tk
    assert M % tm == 0 and N % tn == 0 and K % tk == 0, (name, M, N, K, tm, tn, tk)
    if mode == "nn":
        ab, bb, dims = (tm, tk), (tk, tn), NN_DIMS
        ai = lambda i, j, k: (i + a_off[0] // tm, k + a_off[1] // tk)
        bi = lambda i, j, k: (k + b_off[0] // tk, j + b_off[1] // tn)
        chk = (a_off[0] % tm, a_off[1] % tk, b_off[0] % tk, b_off[1] % tn)
    elif mode == "nt":
        ab, bb, dims = (tm, tk), (tn, tk), NT_DIMS
        ai = lambda i, j, k: (i + a_off[0] // tm, k + a_off[1] // tk)
        bi = lambda i, j, k: (j + b_off[0] // tn, k + b_off[1] // tk)
        chk = (a_off[0] % tm, a_off[1] % tk, b_off[0] % tn, b_off[1] % tk)
    else:
        ab, bb, dims = (tk, tm), (tk, tn), TN_DIMS
        ai = lambda i, j, k: (k + a_off[0] // tk, i + a_off[1] // tm)
        bi = lambda i, j, k: (k + b_off[0] // tk, j + b_off[1] // tn)
        chk = (a_off[0] % tk, a_off[1] % tm, b_off[0] % tk, b_off[1] % tn)
    assert not any(chk), (name, chk)

    own_acc = nk > 1 and out_dtype != F32

    def body(a_ref, b_ref, o_ref, *acc):
        dot = lambda: lax.dot_general(a_ref[...].astype(BF16), b_ref[...].astype(BF16), dims, preferred_element_type=F32)
        if nk == 1:
            o_ref[...] = dot().astype(out_dtype)
            return
        k, dst = pl.program_id(2), (acc[0] if own_acc else o_ref)

        @pl.when(k == 0)
        def _():
            dst[...] = jnp.zeros_like(dst)

        dst[...] += dot()
        if own_acc:
            @pl.when(k == nk - 1)
            def _():
                o_ref[...] = acc[0][...].astype(out_dtype)

    return pl.pallas_call(
        body, name=name, out_shape=jax.ShapeDtypeStruct((M, N), out_dtype), grid=(M // tm, N // tn, nk),
        in_specs=[pl.BlockSpec(ab, ai), pl.BlockSpec(bb, bi)], out_specs=pl.BlockSpec((tm, tn), lambda i, j, k: (i, j)),
        scratch_shapes=[pltpu.VMEM((tm, tn), F32)] if own_acc else [],
        compiler_params=_cparams(("parallel", "parallel", "arbitrary")),
    )(a, b)


def _rowwise(name, fn, grid_n, tr, ins, outs, nlat=None):
    nlat = grid_n if nlat is None else nlat
    grp = lambda i: jnp.minimum(i // nlat, 1)
    in_specs, args = [], []
    for spec in ins:
        kind, arr = spec[0], spec[1]
        if kind == "row":
            in_specs.append(pl.BlockSpec((tr, spec[3]), lambda i, cb=spec[2]: (i, cb)))
        elif kind == "rowc":
            in_specs.append(pl.BlockSpec((tr, spec[3]), lambda i, cb=spec[2], mx=spec[4]: (jnp.minimum(i, mx), cb)))
        elif kind == "prev":
            in_specs.append(pl.BlockSpec((8, spec[3]), lambda i, cb=spec[2]: (jnp.maximum(i * (tr // 8) - 1, 0), cb)))
        elif kind == "next":
            nb = arr.shape[0] // 8
            in_specs.append(pl.BlockSpec((8, spec[3]), lambda i, cb=spec[2], nb=nb: (jnp.minimum((i + 1) * (tr // 8), nb - 1), cb)))
        elif kind == "grp":
            in_specs.append(pl.BlockSpec((None,) + arr.shape[1:], lambda i: (grp(i), 0, 0)))
        else:
            in_specs.append(pl.BlockSpec(arr.shape, lambda i: (0, 0)))
        args.append(arr)
    out_shape, out_specs = [], []
    for spec in outs:
        if spec[0] == "row":
            out_shape.append(jax.ShapeDtypeStruct((spec[1], spec[2]), spec[3]))
            out_specs.append(pl.BlockSpec((tr, spec[2]), lambda i: (i, 0)))
        elif spec[0] == "acc":
            out_shape.append(jax.ShapeDtypeStruct((spec[1], spec[2]), F32))
            out_specs.append(pl.BlockSpec((spec[1], spec[2]), lambda i: (0, 0)))
        else:
            out_shape.append(jax.ShapeDtypeStruct((2, spec[1], spec[2]), F32))
            out_specs.append(pl.BlockSpec((None, spec[1], spec[2]), lambda i: (grp(i), 0, 0)))
    n_in = len(ins)
    has_acc = any(s[0] != "row" for s in outs)

    def body(*refs):
        i = pl.program_id(0)
        res = fn(i, *[r[...] for r in refs[:n_in]])
        for spec, ref, val in zip(outs, refs[n_in:], res):
            if spec[0] == "row":
                ref[...] = val.astype(ref.dtype)
            else:
                first = (i == 0) if spec[0] == "acc" else jnp.logical_or(i == 0, i == nlat)

                @pl.when(first)
                def _(ref=ref, val=val):
                    ref[...] = val

                @pl.when(jnp.logical_not(first))
                def _(ref=ref, val=val):
                    ref[...] += val

    return pl.pallas_call(
        body, name=name, out_shape=tuple(out_shape), grid=(grid_n,), in_specs=in_specs, out_specs=tuple(out_specs),
        compiler_params=_cparams(("arbitrary",) if has_acc else ("parallel",)),
    )(*args)


def _vjp_of(fwd, n_in, wrt):
    def bwd(i, *args):
        _, pull = jax.vjp(lambda *a: fwd(i, *a), *args[:n_in])
        g = pull(tuple(args[n_in:]))
        return tuple(g[k] for k in wrt)
    return bwd


def _rms(x, w):
    return x * lax.rsqrt(jnp.mean(x * x, axis=-1, keepdims=True) + EPS) * w


def _modulate(x, w, shift, scale):
    return _rms(x, w) * (1.0 + scale) + shift


def _silu(x):
    return x * jax.nn.sigmoid(x)


def _log_sigmoid(x):
    return jnp.minimum(x, 0.0) - jnp.log(1.0 + jnp.exp(-jnp.abs(x)))


def _f_mod(i, x, w, sh, sc):
    return (_modulate(x, w, sh, sc),)


def _f_resmod(i, x, y, gate, w, sh, sc):
    x1 = x + gate * y
    return x1, _modulate(x1, w, sh, sc)


def _f_norms(i, kvl, ql, kvw, qw):
    return _rms(kvl, kvw), _rms(ql, qw)


def _f_silu(i, x):
    return (_silu(x),)


def _rope128(x2, cos, sin):
    return x2 * cos + pltpu.roll(x2, QK_ROPE, 1) * sin


def _rope128_t(d2, cos, sin):
    return d2 * cos + pltpu.roll(d2 * sin, QK_ROPE, 1)


def _shifters(i, tr, T, TA):
    assert T % tr == 0 and TA % tr == 0
    loc = lax.broadcasted_iota(jnp.int32, (tr, 1), 0)
    starts_seq = jnp.logical_or(i * tr == 0, i * tr == T)
    ends_seq = jnp.logical_or((i + 1) * tr == T, (i + 1) * tr == TA)

    def prev(x, halo):
        return jnp.where(loc == 0, jnp.where(starts_seq, 0.0, halo[7:8, :]), pltpu.roll(x, 1, 0))

    def nxt(x, halo):
        return jnp.where(loc == tr - 1, jnp.where(ends_seq, 0.0, halo[0:1, :]), pltpu.roll(x, tr - 1, 0))

    return prev, nxt


def _colsum(x):
    return jnp.sum(x, axis=0, keepdims=True)


MLA_SCALE = (QK_NOPE + QK_ROPE) ** -0.5
LOG2E, LN2 = 1.4426950408889634, 0.6931471805599453
MLA_QSCALE = MLA_SCALE * LOG2E
MLA_SUB = 256


def _mla_fwd_lat(q, kf, kv, T, TA, hosted):
    H = MLA_HEADS
    tq, tk = _tile(T, 1024, TR), _tile(TA, 1408)
    nq, nk = T // tq, TA // tk
    nh = hosted.n

    def body(q_ref, k_ref, v_ref, *rest):
        o_ref, lse_ref = rest[nh:nh + 2]
        m_sc, l_sc, acc_sc = rest[2 * nh + 2:2 * nh + 5]
        hi, qi, ki = pl.program_id(0), pl.program_id(1), pl.program_id(2)
        hosted.run(jnp.logical_and(hi == 0, jnp.logical_and(qi == 0, ki == 0)),
                   jnp.logical_and(hi == H - 1, jnp.logical_and(qi == nq - 1, ki == nk - 1)),
                   rest[:nh], rest[nh + 2:2 * nh + 2], rest[2 * nh + 5:])

        @pl.when(ki == 0)
        def _():
            m_sc[...] = jnp.full_like(m_sc, -jnp.inf)
            l_sc[...] = jnp.zeros_like(l_sc)
            acc_sc[...] = jnp.zeros_like(acc_sc)

        sb = min(tq, MLA_SUB)
        scores = lambda r: lax.dot_general(q_ref[r * sb:(r + 1) * sb, :], k_ref[...], NT_DIMS, preferred_element_type=F32)
        s_next = scores(0)
        for r in range(tq // sb):
            s, rows = s_next, slice(r * sb, (r + 1) * sb)
            if r + 1 < tq // sb:
                s_next = scores(r + 1)
            m_old = m_sc[rows, :]
            m_new = jnp.maximum(m_old, jnp.max(s, axis=-1, keepdims=True))
            alpha = jnp.exp2(m_old - m_new)
            p = jnp.exp2(s - m_new)
            l_sc[rows, :] = alpha * l_sc[rows, :] + jnp.sum(p, axis=-1, keepdims=True)
            acc_sc[rows, :] = alpha * acc_sc[rows, :] + jnp.dot(p.astype(BF16), v_ref[...], preferred_element_type=F32)
            m_sc[rows, :] = m_new

        @pl.when(ki == nk - 1)
        def _():
            o_ref[...] = (acc_sc[...] / l_sc[...]).astype(o_ref.dtype)
            lse_ref[...] = m_sc[...] + jnp.log2(l_sc[...])

    res = pl.pallas_call(
        body, name="mla_fwd_lat", grid=(H, nq, nk),
        out_shape=[jax.ShapeDtypeStruct((T, H * V_HEAD), BF16), jax.ShapeDtypeStruct((H, T, 1), F32)] + hosted.out_shape,
        in_specs=[pl.BlockSpec((tq, QK_PAD), lambda h, i, k: (i, h)), pl.BlockSpec((tk, QK_PAD), lambda h, i, k: (k, h)),
                  pl.BlockSpec((tk, V_HEAD), lambda h, i, k: (k, 2 * h + 1))] + hosted.specs,
        out_specs=[pl.BlockSpec((tq, V_HEAD), lambda h, i, k: (i, h)),
                   pl.BlockSpec((None, tq, 1), lambda h, i, k: (h, i, 0))] + hosted.specs,
        scratch_shapes=[pltpu.VMEM((tq, 1), F32), pltpu.VMEM((tq, 1), F32), pltpu.VMEM((tq, V_HEAD), F32)] + hosted.scratch,
        compiler_params=_cparams(("arbitrary", "arbitrary", "arbitrary")),
    )(q, kf, kv, *hosted.arrs)
    return res[0], res[1], list(res[2:])


def _mla_ctx_probs(q_ref, k_ref):
    s = lax.dot_general(q_ref[...], k_ref[...], NT_DIMS, preferred_element_type=F32)
    p = jnp.exp2(s - jnp.max(s, axis=-1, keepdims=True))
    return p, jnp.sum(p, axis=-1, keepdims=True)


def _mla_fwd_ctx(q, kf, kv, T):
    H, cb = MLA_HEADS, T // TR

    def body(q_ref, k_ref, v_ref, o_ref):
        p, l = _mla_ctx_probs(q_ref, k_ref)
        o_ref[...] = (jnp.dot(p.astype(BF16), v_ref[...], preferred_element_type=F32) / l).astype(o_ref.dtype)

    return pl.pallas_call(
        body, name="mla_fwd_ctx", grid=(H,), out_shape=jax.ShapeDtypeStruct((TR, H * V_HEAD), BF16),
        in_specs=[pl.BlockSpec((TR, QK_PAD), lambda h: (cb, h)), pl.BlockSpec((TR, QK_PAD), lambda h: (cb, h)),
                  pl.BlockSpec((TR, V_HEAD), lambda h: (cb, 2 * h + 1))],
        out_specs=pl.BlockSpec((TR, V_HEAD), lambda h: (0, h)), compiler_params=_cparams(("parallel",)),
    )(q, kf, kv)


def _mla_grads(p, q_ref, k_ref, v_ref, do, delta):
    dob = do.astype(BF16)
    dp = lax.dot_general(dob, v_ref[...], NT_DIMS, preferred_element_type=F32)
    ds = (p * (dp - delta)).astype(BF16)
    return (jnp.dot(ds, k_ref[...], preferred_element_type=F32),
            lax.dot_general(ds, q_ref[...], TN_DIMS, preferred_element_type=F32) * LN2,
            lax.dot_general(p.astype(BF16), dob, TN_DIMS, preferred_element_type=F32))


def _mla_bwd_lat(q, kf, kv, o, dcat, lse, do_cb0, T, TA, hosted):
    H = MLA_HEADS
    tq, tk = _tile(T, 1024, TR), _tile(TA, 1408)
    nq, nk = T // tq, TA // tk
    nh = hosted.n

    def body(q_ref, k_ref, v_ref, o_ref, do_ref, lse_ref, *rest):
        dq_ref, dk_ref, dv_ref = rest[nh:nh + 3]
        dk_acc, dv_acc = rest[2 * nh + 3:2 * nh + 5]
        hi, ki, qi = pl.program_id(0), pl.program_id(1), pl.program_id(2)
        hosted.run(jnp.logical_and(hi == 0, jnp.logical_and(qi == 0, ki == 0)),
                   jnp.logical_and(hi == H - 1, jnp.logical_and(qi == nq - 1, ki == nk - 1)),
                   rest[:nh], rest[nh + 3:2 * nh + 3], rest[2 * nh + 5:])

        @pl.when(jnp.logical_and(ki == 0, qi == 0))
        def _():
            dq_ref[...] = jnp.zeros_like(dq_ref)

        @pl.when(qi == 0)
        def _():
            dk_acc[...] = jnp.zeros_like(dk_acc)
            dv_acc[...] = jnp.zeros_like(dv_acc)

        sb = min(tq, MLA_SUB)

        def products(r):
            rows = slice(r * sb, (r + 1) * sb)
            dob = do_ref[rows, :].astype(BF16)
            return (lax.dot_general(q_ref[rows, :], k_ref[...], NT_DIMS, preferred_element_type=F32),
                    lax.dot_general(dob, v_ref[...], NT_DIMS, preferred_element_type=F32), dob)

        nxt, dk, dv = products(0), None, None
        for r in range(tq // sb):
            (s, dp, dob), rows = nxt, slice(r * sb, (r + 1) * sb)
            if r + 1 < tq // sb:
                nxt = products(r + 1)
            delta = jnp.sum(do_ref[rows, :] * o_ref[rows, :].astype(F32), axis=-1, keepdims=True)
            p = jnp.exp2(s - lse_ref[rows, :])
            ds = (p * (dp - delta)).astype(BF16)
            dq_ref[pl.ds(pl.multiple_of(qi * tq + r * sb, sb), sb), :] += jnp.dot(ds, k_ref[...], preferred_element_type=F32)
            dk_r = lax.dot_general(ds, q_ref[rows, :], TN_DIMS, preferred_element_type=F32)
            dv_r = lax.dot_general(p.astype(BF16), dob, TN_DIMS, preferred_element_type=F32)
            dk, dv = (dk_r, dv_r) if dk is None else (dk + dk_r, dv + dv_r)
        dk_acc[...] += dk
        dv_acc[...] += dv

        @pl.when(qi == nq - 1)
        def _():
            dk_ref[...] = dk_acc[...] * LN2
            dv_ref[...] = dv_acc[...].astype(dv_ref.dtype)

    res = pl.pallas_call(
        body, name="mla_bwd_lat", grid=(H, nk, nq),
        out_shape=[jax.ShapeDtypeStruct((T, H * QK_PAD), F32), jax.ShapeDtypeStruct((TA, H * QK_PAD), F32),
                   jax.ShapeDtypeStruct((TA, H * V_HEAD), BF16)] + hosted.out_shape,
        in_specs=[pl.BlockSpec((tq, QK_PAD), lambda h, k, i: (i, h)), pl.BlockSpec((tk, QK_PAD), lambda h, k, i: (k, h)),
                  pl.BlockSpec((tk, V_HEAD), lambda h, k, i: (k, 2 * h + 1)), pl.BlockSpec((tq, V_HEAD), lambda h, k, i: (i, h)),
                  pl.BlockSpec((tq, V_HEAD), lambda h, k, i: (i, do_cb0 + h)),
                  pl.BlockSpec((None, tq, 1), lambda h, k, i: (h, i, 0))] + hosted.specs,
        out_specs=[pl.BlockSpec((T, QK_PAD), lambda h, k, i: (0, h)), pl.BlockSpec((tk, QK_PAD), lambda h, k, i: (k, h)),
                   pl.BlockSpec((tk, V_HEAD), lambda h, k, i: (k, h))] + hosted.specs,
        scratch_shapes=[pltpu.VMEM((tk, QK_PAD), F32), pltpu.VMEM((tk, V_HEAD), F32)] + hosted.scratch,
        compiler_params=_cparams(("arbitrary", "arbitrary", "arbitrary")),
    )(q, kf, kv, o, dcat, lse, *hosted.arrs)
    return res[0], res[1], res[2], list(res[3:])


def _mla_bwd_ctx(q, kf, kv, dcat, dk_lat, dv_lat, do_cb0, T):
    H, cb = MLA_HEADS, T // TR

    def body(q_ref, k_ref, v_ref, do_ref, dkl_ref, dvl_ref, dq_ref, dk_ref, dv_ref):
        p, l = _mla_ctx_probs(q_ref, k_ref)
        do = do_ref[...]
        o = jnp.dot(p.astype(BF16), v_ref[...], preferred_element_type=F32) / l
        dq, dk, dv = _mla_grads(p / l, q_ref, k_ref, v_ref, do, jnp.sum(do * o, axis=-1, keepdims=True))
        dq_ref[...] = dq
        dk_ref[...] = dkl_ref[...] + dk
        dv_ref[...] = (dvl_ref[...].astype(F32) + dv).astype(dv_ref.dtype)

    at_ctx = lambda w, f: pl.BlockSpec((TR, w), lambda h: (cb, f(h)))
    at_0 = lambda w: pl.BlockSpec((TR, w), lambda h: (0, h))
    return pl.pallas_call(
        body, name="mla_bwd_ctx", grid=(H,),
        out_shape=(jax.ShapeDtypeStruct((TR, H * QK_PAD), F32), jax.ShapeDtypeStruct((TR, H * QK_PAD), F32),
                   jax.ShapeDtypeStruct((TR, H * V_HEAD), BF16)),
        in_specs=[at_ctx(QK_PAD, lambda h: h), at_ctx(QK_PAD, lambda h: h), at_ctx(V_HEAD, lambda h: 2 * h + 1),
                  at_ctx(V_HEAD, lambda h: do_cb0 + h), at_ctx(QK_PAD, lambda h: h), at_ctx(V_HEAD, lambda h: h)],
        out_specs=(at_0(QK_PAD), at_0(QK_PAD), at_0(V_HEAD)), compiler_params=_cparams(("parallel",)),
    )(q, kf, kv, dcat, dk_lat, dv_lat)


def _bdot_impl(a, b, dims):
    return lax.dot_general(a.astype(BF16), b.astype(BF16), dims, preferred_element_type=F32)


def _bdot(a, b, dims):
    @jax.custom_vjp
    def f(a_, b_):
        return _bdot_impl(a_, b_, dims)

    def fwd(a_, b_):
        return _bdot_impl(a_, b_, dims), (a_.astype(BF16), b_.astype(BF16))

    def bwd(res, ct):
        a_, b_ = res
        if dims == NN_DIMS:
            return _bdot_impl(ct, b_, NT_DIMS), _bdot_impl(a_, ct, TN_DIMS)
        if dims == NT_DIMS:
            return _bdot_impl(ct, b_, NN_DIMS), _bdot_impl(ct, a_, TN_DIMS)
        return _bdot_impl(b_, ct, NT_DIMS), _bdot_impl(a_, ct, NN_DIMS)

    f.defvjp(fwd, bwd)
    return f(a, b)


def _ones_dot_impl(ones, x, dims):
    hi = x.astype(BF16)
    r1 = x - hi.astype(F32)
    mid = r1.astype(BF16)
    lo = (r1 - mid.astype(F32)).astype(BF16)
    d = lambda t: lax.dot_general(ones, t, dims, preferred_element_type=F32)
    return (d(lo) + d(mid)) + d(hi)


@jax.custom_vjp
def _ones_dot(ones, x):
    return _ones_dot_impl(ones, x, NN_DIMS)


_ones_dot.defvjp(lambda ones, x: (_ones_dot_impl(ones, x, NN_DIMS), ones),
                 lambda ones, ct: (jnp.zeros_like(ones), _ones_dot_impl(ones, ct, TN_DIMS)))


def _gla_tile(st, q, k, v, g, rev, q_scale):
    nc = q.shape[0] // CHUNK
    ii = lax.broadcasted_iota(jnp.int32, (CHUNK, CHUNK), 0)
    jj = lax.broadcasted_iota(jnp.int32, (CHUNK, CHUNK), 1)
    tri = (jj >= ii) if rev else (jj <= ii)
    ones = tri.astype(BF16)
    outs = [None] * nc
    for ci in (range(nc - 1, -1, -1) if rev else range(nc)):
        sl = slice(ci * CHUNK, (ci + 1) * CHUNK)
        qc, kc, vc, gc = q[sl] * q_scale, k[sl], v[sl], g[sl]
        b = _ones_dot(ones, gc)
        bl = b[0:1] if rev else b[CHUNK - 1:CHUNK]
        kd, qe, ke = kc * jnp.exp(bl - b), qc * jnp.exp(b), kc * jnp.exp(-b)
        att = jnp.where(tri, _bdot(qe, ke, NT_DIMS), 0.0)
        outs[ci] = _bdot(att, vc, NN_DIMS) + _bdot(qe, st, NT_DIMS)
        st = st * jnp.exp(bl) + _bdot(vc, kd, TN_DIMS)
    return st, jnp.concatenate(outs, axis=0)


def _gla_tiles(nt, nlat):
    return (lambda p: (p + nlat) % nt), (lambda p: nt - 1 - p)


def _gla_fwd(z1, g, dims, nt, nlat):
    DK, DV, q_off, k_off = dims
    KEY, TA = GLA_HEADS * DK, nt * TR
    in_specs, out_o = [], []
    for d, tile in enumerate(_gla_tiles(nt, nlat)):
        in_specs += [pl.BlockSpec((TR, DK), lambda h, p, t=tile: (t(p), q_off // DK + h)),
                     pl.BlockSpec((TR, DK), lambda h, p, t=tile: (t(p), k_off // DK + h)),
                     pl.BlockSpec((TR, DV), lambda h, p, t=tile: (t(p), h)),
                     pl.BlockSpec((TR, DK), lambda h, p, t=tile, d=d: (t(p), d * GLA_HEADS + h))]
        out_o.append(pl.BlockSpec((TR, DV), lambda h, p, t=tile: (t(p), h)))
    st_spec = pl.BlockSpec((None, None, DV, DK), lambda h, p: (h, p, 0, 0))

    def body(qf, kf, vf, gf, qb, kb, vb, gb, of_ref, ob_ref, sf_ref, sb_ref, st_f, st_b):
        @pl.when(pl.program_id(1) == 0)
        def _():
            st_f[...] = jnp.zeros_like(st_f)
            st_b[...] = jnp.zeros_like(st_b)

        sf_ref[...] = st_f[...]
        sb_ref[...] = st_b[...]
        new_f, out_f = _gla_tile(st_f[...], qf[...], kf[...], vf[...], gf[...], False, DK ** -0.5)
        new_b, out_b = _gla_tile(st_b[...], qb[...], kb[...], vb[...], gb[...], True, DK ** -0.5)
        st_f[...] = new_f
        st_b[...] = new_b
        of_ref[...] = out_f
        ob_ref[...] = out_b

    o_shape = jax.ShapeDtypeStruct((TA, GLA_HEADS * DV), F32)
    s_shape = jax.ShapeDtypeStruct((GLA_HEADS, nt, DV, DK), F32)
    return pl.pallas_call(
        body, name="gla_fwd", grid=(GLA_HEADS, nt), out_shape=(o_shape, o_shape, s_shape, s_shape),
        in_specs=in_specs, out_specs=(out_o[0], out_o[1], st_spec, st_spec),
        scratch_shapes=[pltpu.VMEM((DV, DK), F32)] * 2, compiler_params=_cparams(("parallel", "arbitrary")),
    )(z1, z1, z1, g, z1, z1, z1, g)


def _gla_bwd(z1, g, st_f, st_b, do, dims, nt, nlat):
    DK, DV, q_off, k_off = dims
    KEY, VAL, TA = GLA_HEADS * DK, GLA_HEADS * DV, nt * TR
    pos = lambda s: nt - 1 - s
    in_specs, out_specs = [], []
    for d, tile in enumerate(_gla_tiles(nt, nlat)):
        at = lambda w, f, t=tile: pl.BlockSpec((TR, w), lambda h, s: (t(pos(s)), f(h)))
        in_specs += [at(DK, lambda h: q_off // DK + h), at(DK, lambda h: k_off // DK + h), at(DV, lambda h: h),
                     at(DK, lambda h, d=d: d * GLA_HEADS + h),
                     pl.BlockSpec((None, None, DV, DK), lambda h, s: (h, pos(s), 0, 0)), at(DV, lambda h: h)]
        out_specs += [at(DK, lambda h: h), at(DK, lambda h: h), at(DV, lambda h: h), at(DK, lambda h: h)]

    def one(refs, outs, dst, rev):
        q_ref, k_ref, v_ref, g_ref, st_ref, do_ref = refs
        _, pull = jax.vjp(lambda st, q, k, v, gg: _gla_tile(st, q, k, v, gg, rev, DK ** -0.5),
                          st_ref[...], q_ref[...], k_ref[...], v_ref[...], g_ref[...])
        grads = pull((dst[...], do_ref[...]))
        dst[...] = grads[0]
        for o_ref, val in zip(outs, grads[1:]):
            o_ref[...] = val

    def body(*refs):
        dst_f, dst_b = refs[20:]

        @pl.when(pl.program_id(1) == 0)
        def _():
            dst_f[...] = jnp.zeros_like(dst_f)
            dst_b[...] = jnp.zeros_like(dst_b)

        one(refs[0:6], refs[12:16], dst_f, False)
        one(refs[6:12], refs[16:20], dst_b, True)

    shapes = [jax.ShapeDtypeStruct((TA, KEY), F32), jax.ShapeDtypeStruct((TA, KEY), F32),
              jax.ShapeDtypeStruct((TA, VAL), F32), jax.ShapeDtypeStruct((TA, KEY), F32)]
    res = pl.pallas_call(
        body, name="gla_bwd", grid=(GLA_HEADS, nt), out_shape=shapes * 2, in_specs=in_specs, out_specs=out_specs,
        scratch_shapes=[pltpu.VMEM((DV, DK), F32)] * 2, compiler_params=_cparams(("parallel", "arbitrary")),
    )(z1, z1, z1, g, st_f, do, z1, z1, z1, g, st_b, do)
    return res[:4], res[4:]


def _sum_slots(name, arr):
    S, R, C = arr.shape
    tr = _tile(R, max(16, (2 ** 19) // max(C, 1) // 16 * 16), 16)

    def body(a_ref, o_ref):
        acc = a_ref[0].astype(F32)
        for s in range(1, S):
            acc = acc + a_ref[s].astype(F32)
        o_ref[...] = acc

    return pl.pallas_call(
        body, name=name, grid=(R // tr,), out_shape=jax.ShapeDtypeStruct((R, C), F32),
        in_specs=[pl.BlockSpec((S, tr, C), lambda i: (0, i, 0))], out_specs=pl.BlockSpec((tr, C), lambda i: (i, 0)),
        compiler_params=_cparams(("parallel",)),
    )(arr)


def _sum_pair(name, a, b, out_dtype):
    shape, C = a.shape, a.shape[-1]
    a2, b2 = a.reshape(-1, C), b.reshape(-1, C)
    R = a2.shape[0]
    tr = _tile(R, max(16, (2 ** 19) // max(C, 1) // 16 * 16), 16)
    fn = lambda i, p, q: (p.astype(F32) + q.astype(F32),)
    (res,) = _rowwise(name, fn, R // tr, tr, [("row", a2, 0, C), ("row", b2, 0, C)], [("row", R, C, out_dtype)])
    return res.reshape(shape)


def _adam_math(w, g, m, v):
    m2 = ADAM_B1 * m + (1.0 - ADAM_B1) * g
    v2 = ADAM_B2 * v + (1.0 - ADAM_B2) * (g * g)
    m_hat = m2 / (1.0 - ADAM_B1 ** ADAM_STEP)
    v_hat = v2 / (1.0 - ADAM_B2 ** ADAM_STEP)
    return -ADAM_LR * (m_hat / (jnp.sqrt(v_hat) + ADAM_EPS) + ADAM_WD * w), m2, v2


def _adam(name, w, gs, m, v):
    R, C = w.shape
    tr = _tile(R, max(8, (2 ** 19) // max(C, 1) // 8 * 8), 8)
    ng = len(gs)

    def fn(i, w_, *rest):
        g = rest[0] if ng == 1 else rest[0] + rest[1]
        d, m2, v2 = _adam_math(w_, g, rest[ng], rest[ng + 1])
        return g, d, m2, v2

    ins = [("row", a, 0, C) for a in (w, *gs, m, v)]
    return _rowwise(name, fn, R // tr, tr, ins, [("row", R, C, F32)] * 4)


def _ffn_fwd(tag, h, w_up, cw, cb, w_down, n_rows, T, TA, F):
    tr = 128
    u = _matmul(tag + "_up", h, w_up, "nn", n_rows, 2 * F, h.shape[1])
    cwg, cwv, cbg, cbv = cw[:, :F], cw[:, F:], cb[:, :F], cb[:, F:]

    def fn(i, ug, uv, pg, pv, ng, nv, wg, wv, bg, bv):
        prev, nxt = _shifters(i, tr, T, TA)
        cg = wg[0:1] * prev(ug, pg) + wg[1:2] * ug + wg[2:3] * nxt(ug, ng) + bg
        cv = wv[0:1] * prev(uv, pv) + wv[1:2] * uv + wv[2:3] * nxt(uv, nv) + bv
        return (_silu(cg) * cv,)

    ins = [("row", u, 0, F), ("row", u, 1, F), ("prev", u, 0, F), ("prev", u, 1, F), ("next", u, 0, F), ("next", u, 1, F),
           ("full", cwg), ("full", cwv), ("full", cbg), ("full", cbv)]
    (act,) = _rowwise(tag + "_conv", fn, n_rows // tr, tr, ins, [("row", n_rows, F, BF16)])
    f = _matmul(tag + "_down", act, w_down, "nn", n_rows, w_down.shape[1], F)
    return u, act, f


def _ffn_bwd(tag, df, h, u, act, w_up, cw, cb, w_down, n_rows, T, TA, F):
    tr = 128
    D = w_down.shape[1]
    dact = _matmul(tag + "_ddown_x", df, w_down, "nt", n_rows, F, D)
    g_down = _matmul(tag + "_ddown_w", act, df, "tn", F, D, n_rows, out_dtype=BF16)
    cwg, cwv, cbg, cbv = cw[:, :F], cw[:, F:], cb[:, :F], cb[:, F:]

    def fn_a(i, ug, uv, pg, pv, ng, nv, da, wg, wv, bg, bv):
        prev, nxt = _shifters(i, tr, T, TA)
        ugp, ugn, uvp, uvn = prev(ug, pg), nxt(ug, ng), prev(uv, pv), nxt(uv, nv)
        cg = wg[0:1] * ugp + wg[1:2] * ug + wg[2:3] * ugn + bg
        cv = wv[0:1] * uvp + wv[1:2] * uv + wv[2:3] * uvn + bv
        sg = jax.nn.sigmoid(cg)
        dcv = da * (cg * sg)
        dcg = da * cv * (sg * (1.0 + cg * (1.0 - sg)))
        return (jnp.concatenate([dcg, dcv], axis=1),
                _colsum(ugp * dcg), _colsum(ug * dcg), _colsum(ugn * dcg),
                _colsum(uvp * dcv), _colsum(uv * dcv), _colsum(uvn * dcv), _colsum(dcg), _colsum(dcv))

    ins = [("row", u, 0, F), ("row", u, 1, F), ("prev", u, 0, F), ("prev", u, 1, F), ("next", u, 0, F), ("next", u, 1, F),
           ("row", dact, 0, F), ("full", cwg), ("full", cwv), ("full", cbg), ("full", cbv)]
    res = _rowwise(tag + "_dconv_a", fn_a, n_rows // tr, tr, ins, [("row", n_rows, 2 * F, F32)] + [("acc", 1, F)] * 8)
    duc = res[0]
    g_cw = jnp.concatenate([jnp.concatenate(res[1:4], axis=0), jnp.concatenate(res[4:7], axis=0)], axis=1)
    g_cb = jnp.concatenate([res[7], res[8]], axis=1)

    def fn_b(i, dg, dv, pg, pv, ng, nv, wg, wv):
        prev, nxt = _shifters(i, tr, T, TA)
        dug = wg[0:1] * nxt(dg, ng) + wg[1:2] * dg + wg[2:3] * prev(dg, pg)
        duv = wv[0:1] * nxt(dv, nv) + wv[1:2] * dv + wv[2:3] * prev(dv, pv)
        return (jnp.concatenate([dug, duv], axis=1),)

    ins = [("row", duc, 0, F), ("row", duc, 1, F), ("prev", duc, 0, F), ("prev", duc, 1, F), ("next", duc, 0, F),
           ("next", duc, 1, F), ("full", cwg), ("full", cwv)]
    (du,) = _rowwise(tag + "_dconv_b", fn_b, n_rows // tr, tr, ins, [("row", n_rows, 2 * F, BF16)])
    dh = _matmul(tag + "_dup_x", du, w_up, "nt", n_rows, D, 2 * F)
    g_up = _matmul(tag + "_dup_w", h, du, "tn", D, 2 * F, n_rows, out_dtype=BF16)
    return dh, g_up, g_cw, g_cb, g_down


def _cols_to_shards(g):
    r, c = g.shape
    return g.reshape(r, 4, c // 4).transpose(1, 0, 2)


def _shards_to_cols(s):
    return s.transpose(1, 0, 2).reshape(s.shape[1], 4 * s.shape[2])


def kernel(x, c, ctx, c_ctx, l0_ada_w, l0_ada_b, l0_norm1, l0_w_in, l0_conv_a, l0_q_norm, l0_w_qb, l0_kv_norm, l0_w_kvb, l0_w_out, l0_norm2, l0_ffn_up, l0_ffn_conv_w, l0_ffn_conv_b, l0_ffn_down, l1_ada_w, l1_ada_b, l1_norm1, l1_w_in, l1_gate_fw_w, l1_gate_fw_b, l1_gate_bw_w, l1_gate_bw_b, l1_o_norm, l1_w_out, l1_norm2, l1_ffn_up, l1_ffn_conv_w, l1_ffn_conv_b, l1_ffn_down, final_norm, loss_target, m_c_ctx, m_l0_ada_w, m_l0_ada_b, m_l0_norm1, m_l0_w_in, m_l0_conv_a, m_l0_q_norm, m_l0_w_qb, m_l0_kv_norm, m_l0_w_kvb, m_l0_w_out, m_l0_norm2, m_l0_ffn_up, m_l0_ffn_conv_w, m_l0_ffn_conv_b, m_l0_ffn_down, m_l1_ada_w, m_l1_ada_b, m_l1_norm1, m_l1_w_in, m_l1_gate_fw_w, m_l1_gate_fw_b, m_l1_gate_bw_w, m_l1_gate_bw_b, m_l1_o_norm, m_l1_w_out, m_l1_norm2, m_l1_ffn_up, m_l1_ffn_conv_w, m_l1_ffn_conv_b, m_l1_ffn_down, m_final_norm, v_c_ctx, v_l0_ada_w, v_l0_ada_b, v_l0_norm1, v_l0_w_in, v_l0_conv_a, v_l0_q_norm, v_l0_w_qb, v_l0_kv_norm, v_l0_w_kvb, v_l0_w_out, v_l0_norm2, v_l0_ffn_up, v_l0_ffn_conv_w, v_l0_ffn_conv_b, v_l0_ffn_down, v_l1_ada_w, v_l1_ada_b, v_l1_norm1, v_l1_w_in, v_l1_gate_fw_w, v_l1_gate_fw_b, v_l1_gate_bw_w, v_l1_gate_bw_b, v_l1_o_norm, v_l1_w_out, v_l1_norm2, v_l1_ffn_up, v_l1_ffn_conv_w, v_l1_ffn_conv_b, v_l1_ffn_down, v_final_norm):
    W = dict(c_ctx=c_ctx, l0_ada_w=l0_ada_w, l0_ada_b=l0_ada_b, l0_norm1=l0_norm1, l0_w_in=l0_w_in, l0_conv_a=l0_conv_a, l0_q_norm=l0_q_norm, l0_w_qb=l0_w_qb, l0_kv_norm=l0_kv_norm, l0_w_kvb=l0_w_kvb, l0_w_out=l0_w_out, l0_norm2=l0_norm2, l0_ffn_up=l0_ffn_up, l0_ffn_conv_w=l0_ffn_conv_w, l0_ffn_conv_b=l0_ffn_conv_b, l0_ffn_down=l0_ffn_down, l1_ada_w=l1_ada_w, l1_ada_b=l1_ada_b, l1_norm1=l1_norm1, l1_w_in=l1_w_in, l1_gate_fw_w=l1_gate_fw_w, l1_gate_fw_b=l1_gate_fw_b, l1_gate_bw_w=l1_gate_bw_w, l1_gate_bw_b=l1_gate_bw_b, l1_o_norm=l1_o_norm, l1_w_out=l1_w_out, l1_norm2=l1_norm2, l1_ffn_up=l1_ffn_up, l1_ffn_conv_w=l1_ffn_conv_w, l1_ffn_conv_b=l1_ffn_conv_b, l1_ffn_down=l1_ffn_down, final_norm=final_norm)
    MO = dict(c_ctx=m_c_ctx, l0_ada_w=m_l0_ada_w, l0_ada_b=m_l0_ada_b, l0_norm1=m_l0_norm1, l0_w_in=m_l0_w_in, l0_conv_a=m_l0_conv_a, l0_q_norm=m_l0_q_norm, l0_w_qb=m_l0_w_qb, l0_kv_norm=m_l0_kv_norm, l0_w_kvb=m_l0_w_kvb, l0_w_out=m_l0_w_out, l0_norm2=m_l0_norm2, l0_ffn_up=m_l0_ffn_up, l0_ffn_conv_w=m_l0_ffn_conv_w, l0_ffn_conv_b=m_l0_ffn_conv_b, l0_ffn_down=m_l0_ffn_down, l1_ada_w=m_l1_ada_w, l1_ada_b=m_l1_ada_b, l1_norm1=m_l1_norm1, l1_w_in=m_l1_w_in, l1_gate_fw_w=m_l1_gate_fw_w, l1_gate_fw_b=m_l1_gate_fw_b, l1_gate_bw_w=m_l1_gate_bw_w, l1_gate_bw_b=m_l1_gate_bw_b, l1_o_norm=m_l1_o_norm, l1_w_out=m_l1_w_out, l1_norm2=m_l1_norm2, l1_ffn_up=m_l1_ffn_up, l1_ffn_conv_w=m_l1_ffn_conv_w, l1_ffn_conv_b=m_l1_ffn_conv_b, l1_ffn_down=m_l1_ffn_down, final_norm=m_final_norm)
    VO = dict(c_ctx=v_c_ctx, l0_ada_w=v_l0_ada_w, l0_ada_b=v_l0_ada_b, l0_norm1=v_l0_norm1, l0_w_in=v_l0_w_in, l0_conv_a=v_l0_conv_a, l0_q_norm=v_l0_q_norm, l0_w_qb=v_l0_w_qb, l0_kv_norm=v_l0_kv_norm, l0_w_kvb=v_l0_w_kvb, l0_w_out=v_l0_w_out, l0_norm2=v_l0_norm2, l0_ffn_up=v_l0_ffn_up, l0_ffn_conv_w=v_l0_ffn_conv_w, l0_ffn_conv_b=v_l0_ffn_conv_b, l0_ffn_down=v_l0_ffn_down, l1_ada_w=v_l1_ada_w, l1_ada_b=v_l1_ada_b, l1_norm1=v_l1_norm1, l1_w_in=v_l1_w_in, l1_gate_fw_w=v_l1_gate_fw_w, l1_gate_fw_b=v_l1_gate_fw_b, l1_gate_bw_w=v_l1_gate_bw_w, l1_gate_bw_b=v_l1_gate_bw_b, l1_o_norm=v_l1_o_norm, l1_w_out=v_l1_w_out, l1_norm2=v_l1_norm2, l1_ffn_up=v_l1_ffn_up, l1_ffn_conv_w=v_l1_ffn_conv_w, l1_ffn_conv_b=v_l1_ffn_conv_b, l1_ffn_down=v_l1_ffn_down, final_norm=v_final_norm)
    names = list(W)

    T, D = x.shape[1], x.shape[2]
    TC = ctx.shape[1]
    assert TC == TR and T % TR == 0
    TA = T + TC
    nt, nlat = TA // TR, T // TR
    CC = D // 2
    F = l0_ffn_down.shape[0] * 4
    DK, DV = D // 2 // GLA_HEADS, D // GLA_HEADS
    KEY, VAL = GLA_HEADS * DK, GLA_HEADS * DV
    ADA_S = l0_ada_w.shape[1]
    H = MLA_HEADS
    my_x, my_y, my_c = lax.axis_index("x"), lax.axis_index("y"), lax.axis_index("c")
    chip, dev = 2 * my_x + my_y, 4 * my_x + 2 * my_y + my_c
    row = lambda a: a.reshape(1, -1)

    big = ["l0_w_in", "l0_w_qb", "l0_w_kvb", "l0_w_out", "l0_ffn_up", "l0_ffn_down",
           "l1_w_in", "l1_w_out", "l1_ffn_up", "l1_ffn_down"]
    small_sh = ["l0_conv_a", "l0_ffn_conv_w", "l1_gate_fw_w", "l1_gate_bw_w", "l1_ffn_conv_w"]
    own_half = lambda a, k: lax.dynamic_index_in_dim(a, k, 0, keepdims=False)
    set_own = lambda arr, val, k: lax.dynamic_update_index_in_dim(arr, val[None], k, 0)
    in_core_order = lambda mine, other: jnp.where(my_c == 0, jnp.stack([mine, other], 1), jnp.stack([other, mine], 1))
    early, late = big[:3], big[3:]
    w_half = {n: own_half(W[n].astype(BF16).reshape(2, W[n].shape[0] // 2, -1), my_c) for n in big}
    col_sh = {"l0_w_in", "l0_w_qb", "l0_w_kvb", "l0_ffn_up", "l1_w_in", "l1_ffn_up", "l0_conv_a", "l0_ffn_conv_w",
              "l1_gate_fw_w", "l1_gate_bw_w", "l1_ffn_conv_w"}
    whole_w = lambda n, s: _shards_to_cols(s) if n in col_sh else s.reshape(-1, s.shape[-1])

    def pair_up(tag, group, got):
        got = [set_own(g, w_half[n], chip) for n, g in zip(group, got)]
        other = _exchange("pair_weights_" + tag, got, "swap_c")
        return {n: whole_w(n, in_core_order(g, o).reshape(4, 2 * g.shape[1], g.shape[2])) for n, g, o in zip(group, got, other)}

    full = pair_up("a", early, _exchange("gather_weights", [w_half[n] for n in early], "ag_xy", fill_own=False))
    for n, s in zip(small_sh, _exchange("gather_small_weights", [W[n] for n in small_sh], "ag_xy")):
        full[n] = whole_w(n, s)
    (c_all,) = _exchange("gather_c", [c], "ag_all")

    o_kv, o_kr, o_ql, o_ax = 0, KV_LORA, KV_LORA + QK_ROPE, KV_LORA + QK_ROPE + Q_LORA
    n_sel = 3 * CC + Q_LORA + KV_LORA + QK_ROPE
    W0 = -(-(n_sel + QK_ROPE) // 256) * 256
    QOFF0, KVOFF0, KROFF0 = 3 * CC, 3 * CC + Q_LORA, 3 * CC + Q_LORA + KV_LORA
    w_in0 = full["l0_w_in"]
    w_kr = w_in0[:, o_kr:o_kr + QK_ROPE]
    w_in0p = jnp.concatenate([w_in0[:, o_ax:], w_in0[:, o_ql:o_ax], w_in0[:, :o_kr], w_kr, _rot_cols(w_kr),
                              jnp.zeros((D, W0 - n_sel - QK_ROPE), BF16)], axis=1)
    wq3 = full["l0_w_qb"].reshape(Q_LORA, H, QK_NOPE + QK_ROPE)
    w_qbp = jnp.concatenate([wq3, _rot_cols(wq3[:, :, QK_NOPE:])], axis=2).reshape(Q_LORA, H * QK_PAD)
    o_v1, o_lr1, o_q1, o_og1 = KEY, KEY + VAL, KEY + VAL + 2 * GATE_RANK, 2 * KEY + VAL + 2 * GATE_RANK
    W1 = 2 * VAL + 2 * KEY + LANES
    OGOFF, KOFF1, QOFF1, LROFF = VAL, 2 * VAL, 2 * VAL + KEY, 2 * VAL + 2 * KEY
    w_gate = jnp.zeros((LANES, 2 * KEY), F32)
    w_gate = w_gate.at[:GATE_RANK, :KEY].set(full["l1_gate_fw_w"]).at[GATE_RANK:2 * GATE_RANK, KEY:].set(full["l1_gate_bw_w"])
    b_gate = jnp.concatenate([l1_gate_fw_b, l1_gate_bw_b]).reshape(1, -1)

    pos = np.arange(T)
    inv = ROPE_THETA ** (-np.arange(0, QK_ROPE // 2, 2, dtype=np.float32) / (QK_ROPE // 2))
    ar, ac = (pos // GRID_W).astype(np.float32)[:, None] * inv, (pos % GRID_W).astype(np.float32)[:, None] * inv
    ang = jnp.asarray(np.concatenate([ar, ar, ac, ac], axis=-1).astype(np.float32))
    zpad = jnp.zeros((TA, LANES - QK_ROPE), F32)
    cos_t = jnp.concatenate([jnp.concatenate([jnp.cos(ang), jnp.ones((TC, QK_ROPE), F32)], axis=0), zpad], axis=1)
    sin_t = jnp.concatenate([jnp.concatenate([jnp.sin(ang), jnp.zeros((TC, QK_ROPE), F32)], axis=0), zpad], axis=1)

    c16 = jnp.concatenate([c_all.reshape(8, D), c_ctx.reshape(1, D), jnp.zeros((7, D), F32)], axis=0)
    (act16,) = _rowwise("silu_c", _f_silu, 1, 16, [("row", c16, 0, D)], [("row", 16, D, F32)])
    mod_parts = [_matmul("ada_fwd", act16, W["l%d_ada_w" % l], "nn", 16, ADA_S, D, tn_cap=512) for l in (0, 1)]
    mod_g = _exchange("gather_mod", mod_parts, "ag_xy")
    mods = []
    for l in (0, 1):
        mfull = _shards_to_cols(mod_g[l]) + W["l%d_ada_b" % l][None, :]
        mine = lax.dynamic_slice_in_dim(mfull, dev, 1, axis=0)
        mods.append(jnp.concatenate([mine, mfull[8:9]], axis=0).reshape(2, N_MOD, D))
    P = lambda l, k: mods[l][:, k:k + 1, :]

    X = jnp.concatenate([x[0], ctx[0]], axis=0)
    n1_0, n2_0, n1_1, n2_1 = row(l0_norm1), row(l0_norm2), row(l1_norm1), row(l1_norm2)
    (h0,) = _rowwise("l0_mod1", _f_mod, nt, TR, [("row", X, 0, D), ("full", n1_0), ("grp", P(0, 0)), ("grp", P(0, 1))],
                     [("row", TA, D, BF16)], nlat)
    z0 = _matmul("l0_in", h0, w_in0p, "nn", TA, W0, D)
    kvn_w, qn_w = row(l0_kv_norm), row(l0_q_norm)
    norm_ins = [("row", z0, KVOFF0 // KV_LORA, KV_LORA), ("row", z0, QOFF0 // Q_LORA, Q_LORA), ("full", kvn_w), ("full", qn_w)]
    kvn, qn = _rowwise("l0_latnorm", _f_norms, nt, TR, norm_ins, [("row", TA, KV_LORA, BF16), ("row", TA, Q_LORA, BF16)])
    kv = _matmul("l0_kvb", kvn, full["l0_w_kvb"], "nn", TA, H * QK_PAD, KV_LORA, out_dtype=BF16)
    qraw = _matmul("l0_qb", qn, w_qbp, "nn", TA, H * QK_PAD, Q_LORA)

    def f_rope(i, qr, kvv, krr, cs, sn):
        kr = _rope128(krr, cs, sn).astype(BF16)
        qs, ks = [], []
        for h in range(H):
            qs += [qr[:, h * QK_PAD:h * QK_PAD + QK_NOPE], _rope128(qr[:, h * QK_PAD + QK_NOPE:(h + 1) * QK_PAD], cs, sn)]
            ks += [kvv[:, h * QK_PAD:h * QK_PAD + QK_NOPE], kr]
        return jnp.concatenate(qs, axis=1) * MLA_QSCALE, jnp.concatenate(ks, axis=1)

    rope_ins = [("row", qraw, 0, H * QK_PAD), ("row", kv, 0, H * QK_PAD), ("row", z0, KROFF0 // LANES, LANES),
                ("row", cos_t, 0, LANES), ("row", sin_t, 0, LANES)]
    q, kf = _rowwise("l0_rope", f_rope, nt, TR, rope_ins, [("row", TA, H * QK_PAD, BF16), ("row", TA, H * QK_PAD, BF16)])
    o_lat, lse, got_late = _mla_fwd_lat(q, kf, kv, T, TA, _Hosted([w_half[n] for n in late], "ag_xy"))
    o_ctx = _mla_fwd_ctx(q, kf, kv, T)
    full.update(pair_up("b", late, got_late))
    w_in1 = full["l1_w_in"]
    w_in1p = jnp.concatenate([w_in1[:, o_v1:o_lr1], w_in1[:, o_og1:], w_in1[:, :o_v1], w_in1[:, o_q1:o_og1],
                              w_in1[:, o_lr1:o_q1], jnp.zeros((D, LANES - 2 * GATE_RANK), BF16)], axis=1)

    conv_a = full["l0_conv_a"]

    def f_conva(i, ax, ab, ac_, pax, pac, nax, nac, w):
        prev, nxt = _shifters(i, TR, T, TA)
        p = ac_ * ax
        return (ab * (w[0:1] * prev(p, pac * pax) + w[1:2] * p + w[2:3] * nxt(p, nac * nax)),)

    conva_ins = [("row", z0, 0, CC), ("row", z0, 1, CC), ("row", z0, 2, CC), ("prev", z0, 0, CC), ("prev", z0, 2, CC),
                 ("next", z0, 0, CC), ("next", z0, 2, CC), ("full", conv_a)]
    (y_a,) = _rowwise("l0_conva", f_conva, nt, TR, conva_ins, [("row", TA, CC, BF16)])
    cat0 = jnp.concatenate([y_a, jnp.concatenate([o_lat, o_ctx], axis=0)], axis=1)
    y0 = _matmul("l0_out", cat0, full["l0_w_out"], "nn", TA, D, CC + H * V_HEAD)
    resmod_outs = [("row", TA, D, F32), ("row", TA, D, BF16)]
    X1, h2_0 = _rowwise("l0_resmod2", _f_resmod, nt, TR,
                        [("row", X, 0, D), ("row", y0, 0, D), ("grp", P(0, 2)), ("full", n2_0), ("grp", P(0, 3)), ("grp", P(0, 4))],
                        resmod_outs, nlat)
    cw0, cb0 = full["l0_ffn_conv_w"], row(l0_ffn_conv_b)
    u0, act0, f0 = _ffn_fwd("l0_ffn", h2_0, full["l0_ffn_up"], cw0, cb0, full["l0_ffn_down"], TA, T, TA, F)
    X2, h1 = _rowwise("l1_resmod1", _f_resmod, nt, TR,
                      [("row", X1, 0, D), ("row", f0, 0, D), ("grp", P(0, 5)), ("full", n1_1), ("grp", P(1, 0)), ("grp", P(1, 1))],
                      resmod_outs, nlat)

    z1 = _matmul("l1_in", h1, w_in1p, "nn", TA, W1, D)

    def f_gates(i, lr, wg, bg):
        pre = _bdot(lr, wg, NN_DIMS) + bg
        return (_log_sigmoid(pre) / GATE_NORMALIZER,)

    gate_ins = [("row", z1, LROFF // LANES, LANES), ("full", w_gate), ("full", b_gate)]
    (gates,) = _rowwise("l1_gates", f_gates, nt, TR, gate_ins, [("row", TA, 2 * KEY, F32)])
    gla_dims = (DK, DV, QOFF1, KOFF1)
    o_f, o_b, st_f, st_b = _gla_fwd(z1, gates, gla_dims, nt, nlat)
    onw = row(l1_o_norm)

    def f_glaout(i, of, ob, og, w):
        o = of + ob
        on = jnp.concatenate([_rms(o[:, h * DV:(h + 1) * DV], w) for h in range(GLA_HEADS)], axis=1)
        return (on * _silu(og),)

    glaout_ins = [("row", o_f, 0, VAL), ("row", o_b, 0, VAL), ("row", z1, OGOFF // VAL, VAL), ("full", onw)]
    (go,) = _rowwise("l1_glaout", f_glaout, nlat, TR, glaout_ins, [("row", T, VAL, BF16)])
    y1 = _matmul("l1_out", go, full["l1_w_out"], "nn", T, D, VAL)
    X3, h2_1 = _rowwise("l1_resmod2", _f_resmod, nlat, TR,
                        [("row", X2, 0, D), ("row", y1, 0, D), ("grp", P(1, 2)), ("full", n2_1), ("grp", P(1, 3)), ("grp", P(1, 4))],
                        [("row", T, D, F32), ("row", T, D, BF16)])
    cw1, cb1 = full["l1_ffn_conv_w"], row(l1_ffn_conv_b)
    u1, act1, f1 = _ffn_fwd("l1_ffn", h2_1, full["l1_ffn_up"], cw1, cb1, full["l1_ffn_down"], T, T, T, F)

    fnw = row(final_norm)

    def f_head(i, x3, ff, g5, w, tgt):
        fin = lambda a, b, g_, w_: _rms(a + g_ * b, w_)
        y, pull = jax.vjp(fin, x3, ff, g5, w)
        err = y - tgt
        dx3, dff, dg5, dw = pull(err / D)
        loss = 0.5 * jnp.sum(jnp.mean(err * err, axis=-1, keepdims=True), axis=0, keepdims=True)
        return jnp.broadcast_to(loss, (1, LANES)), dx3, dff, dg5, dw

    head_ins = [("row", X3, 0, D), ("row", f1, 0, D), ("grp", P(1, 5)), ("full", fnw), ("row", loss_target[0], 0, D)]
    loss_acc, dX3, df1, dm5_1, g_final = _rowwise(
        "loss_head", f_head, nlat, TR, head_ins,
        [("acc", 1, LANES), ("row", T, D, F32), ("row", T, D, BF16), ("acc", 1, D), ("acc", 1, D)])
    loss = lax.psum(loss_acc[0, 0], ("x", "y", "c"))

    grads = {"final_norm": g_final}
    dh2_1, grads["l1_ffn_up"], grads["l1_ffn_conv_w"], grads["l1_ffn_conv_b"], grads["l1_ffn_down"] = _ffn_bwd(
        "l1_dffn", df1, h2_1, u1, act1, full["l1_ffn_up"], cw1, cb1, full["l1_ffn_down"], T, T, T, F)
    resmod_bwd = _vjp_of(_f_resmod, 6, (0, 1, 2, 3, 4, 5))
    dX2l, dy1, dm2_1, grads["l1_norm2"], dm3_1, dm4_1 = _rowwise(
        "l1_dresmod2", resmod_bwd, nlat, TR,
        [("row", X2, 0, D), ("row", y1, 0, D), ("grp", P(1, 2)), ("full", n2_1), ("grp", P(1, 3)), ("grp", P(1, 4)),
         ("row", dX3, 0, D), ("row", dh2_1, 0, D)],
        [("row", T, D, F32), ("row", T, D, BF16)] + [("acc", 1, D)] * 4)
    dgo = _matmul("l1_dout_x", dy1, full["l1_w_out"], "nt", T, VAL, D)
    grads["l1_w_out"] = _matmul("l1_dout_w", go, dy1, "tn", VAL, D, T, out_dtype=BF16)

    def f_glaout_bwd(i, of, ob, og, w, d):
        is_ctx = i >= nlat
        _, pull = jax.vjp(lambda a, b, c_, w_: f_glaout(i, a, b, c_, w_)[0], of, ob, og, w)
        dof, _, dog, dw = pull(jnp.where(is_ctx, 0.0, d))
        return dof, dog, dw

    glaout_b_ins = glaout_ins + [("rowc", dgo, 0, VAL, nlat - 1)]
    do_gla, dog, g_onorm = _rowwise("l1_dglaout", f_glaout_bwd, nt, TR, glaout_b_ins,
                                    [("row", TA, VAL, F32), ("row", TA, VAL, F32), ("acc", 1, DV)])
    grads["l1_o_norm"] = g_onorm
    (dq_f, dk_f, dv_f, dg_f), (dq_b, dk_b, dv_b, dg_b) = _gla_bwd(z1, gates, st_f, st_b, do_gla, gla_dims, nt, nlat)

    def f_dz1(i, dvf, dvb, dog_, dkf, dkb, dqf, dqb, dgf, dgb, lr, wg, bg):
        _, pull = jax.vjp(lambda a, b, c_: f_gates(i, a, b, c_)[0], lr, wg, bg)
        dlr, dwg, dbg = pull(jnp.concatenate([dgf, dgb], axis=1))
        return jnp.concatenate([dvf + dvb, dog_, dkf + dkb, dqf + dqb, dlr], axis=1), dwg, dbg

    dz1_ins = [("row", dv_f, 0, VAL), ("row", dv_b, 0, VAL), ("row", dog, 0, VAL), ("row", dk_f, 0, KEY), ("row", dk_b, 0, KEY),
               ("row", dq_f, 0, KEY), ("row", dq_b, 0, KEY), ("row", dg_f, 0, KEY), ("row", dg_b, 0, KEY)] + gate_ins
    dz1, g_wgate, g_bgate = _rowwise("l1_dz", f_dz1, nt, TR, dz1_ins, [("row", TA, W1, BF16), ("acc", LANES, 2 * KEY), ("acc", 1, 2 * KEY)])
    grads["l1_gate_fw_w"], grads["l1_gate_bw_w"] = g_wgate[:GATE_RANK, :KEY], g_wgate[GATE_RANK:2 * GATE_RANK, KEY:]
    grads["l1_gate_fw_b"], grads["l1_gate_bw_b"] = g_bgate[:, :KEY], g_bgate[:, KEY:]
    dh1 = _matmul("l1_din_x", dz1, w_in1p, "nt", TA, D, W1)
    g1p = _matmul("l1_din_w", h1, dz1, "tn", D, W1, TA, out_dtype=BF16)
    grads["l1_w_in"] = jnp.concatenate([g1p[:, KOFF1:QOFF1], g1p[:, :OGOFF], g1p[:, LROFF:LROFF + 2 * GATE_RANK],
                                        g1p[:, QOFF1:LROFF], g1p[:, OGOFF:KOFF1]], axis=1)

    def f_resmod1_bwd(i, x_, y_, g_, w_, sh, sc, dx2, dh):
        return resmod_bwd(i, x_, y_, g_, w_, sh, sc, jnp.where(i >= nlat, 0.0, dx2), dh)

    dX1, df0, dm5_0, grads["l1_norm1"], dm0_1, dm1_1 = _rowwise(
        "l1_dresmod1", f_resmod1_bwd, nt, TR,
        [("row", X1, 0, D), ("row", f0, 0, D), ("grp", P(0, 5)), ("full", n1_1), ("grp", P(1, 0)), ("grp", P(1, 1)),
         ("rowc", dX2l, 0, D, nlat - 1), ("row", dh1, 0, D)],
        [("row", TA, D, F32), ("row", TA, D, BF16), ("accg", 1, D), ("acc", 1, D), ("accg", 1, D), ("accg", 1, D)], nlat)

    dh2_0, grads["l0_ffn_up"], grads["l0_ffn_conv_w"], grads["l0_ffn_conv_b"], grads["l0_ffn_down"] = _ffn_bwd(
        "l0_dffn", df0, h2_0, u0, act0, full["l0_ffn_up"], cw0, cb0, full["l0_ffn_down"], TA, T, TA, F)
    dXd, dy0, dm2_0, grads["l0_norm2"], dm3_0, dm4_0 = _rowwise(
        "l0_dresmod2", resmod_bwd, nt, TR,
        [("row", X, 0, D), ("row", y0, 0, D), ("grp", P(0, 2)), ("full", n2_0), ("grp", P(0, 3)), ("grp", P(0, 4)),
         ("row", dX1, 0, D), ("row", dh2_0, 0, D)],
        [("row", TA, D, F32), ("row", TA, D, BF16), ("accg", 1, D), ("acc", 1, D), ("accg", 1, D), ("accg", 1, D)], nlat)
    dcat = _matmul("l0_dout_x", dy0, full["l0_w_out"], "nt", TA, CC + H * V_HEAD, D)
    grads["l0_w_out"] = _matmul("l0_dout_w", cat0, dy0, "tn", CC + H * V_HEAD, D, TA, out_dtype=BF16)

    def f_conva_bwd(i, ax, ab, ac_, pax, pab, pac, nax, nab, nac, dy, pdy, ndy, w):
        prev, nxt = _shifters(i, TR, T, TA)
        p = ac_ * ax
        pp, pn = prev(p, pac * pax), nxt(p, nac * nax)
        cv = w[0:1] * pp + w[1:2] * p + w[2:3] * pn
        dcv = dy * ab
        dp = w[0:1] * nxt(dcv, ndy * nab) + w[1:2] * dcv + w[2:3] * prev(dcv, pdy * pab)
        return (jnp.concatenate([dp * ac_, dy * cv, dp * ax], axis=1), _colsum(pp * dcv), _colsum(p * dcv), _colsum(pn * dcv))

    conva_b_ins = [("row", z0, 0, CC), ("row", z0, 1, CC), ("row", z0, 2, CC),
                   ("prev", z0, 0, CC), ("prev", z0, 1, CC), ("prev", z0, 2, CC),
                   ("next", z0, 0, CC), ("next", z0, 1, CC), ("next", z0, 2, CC),
                   ("row", dcat, 0, CC), ("prev", dcat, 0, CC), ("next", dcat, 0, CC), ("full", conv_a)]
    dz_a, ga0, ga1, ga2 = _rowwise("l0_dconva", f_conva_bwd, nt, TR, conva_b_ins, [("row", TA, 3 * CC, BF16)] + [("acc", 1, CC)] * 3)
    grads["l0_conv_a"] = jnp.concatenate([ga0, ga1, ga2], axis=0)
    do_cb0 = CC // V_HEAD
    row_sh = {"l0_w_out", "l1_w_out", "l0_ffn_down", "l1_ffn_down"}

    def pair_sums(tag, group):
        res = []
        for n in group:
            s = grads[n].reshape((4, -1) + grads[n].shape[1:]) if n in row_sh else _cols_to_shards(grads[n])
            res.append(s.reshape(4, 2, s.shape[1] // 2, s.shape[2]).transpose(1, 0, 2, 3))
        taken = _exchange("pair_split_grads_" + tag, [own_half(g, 1 - my_c) for g in res], "swap_c")
        return [_sum_pair("pairsum_" + n[3:], own_half(g, my_c), t, BF16) for n, g, t in zip(group, res, taken)]

    pair_late = pair_sums("a", late)
    dq_lat, dk_lat, dv_lat, got = _mla_bwd_lat(q, kf, kv, o_lat, dcat, lse, do_cb0, T, TA, _Hosted(pair_late, "a2a_xy"))
    landed_late = {n: set_own(l, own_half(p, chip), chip) for n, l, p in zip(late, got, pair_late)}
    dq_ctx, dk_ctx, dv_ctx = _mla_bwd_ctx(q, kf, kv, dcat, dk_lat, dv_lat, do_cb0, T)

    def f_rope_bwd(i, dql, dqc, dkl, dkc, dvl, dvc, cs, sn):
        is_ctx = i >= nlat
        dq_, dk_, dv_ = jnp.where(is_ctx, dqc, dql) * MLA_SCALE, jnp.where(is_ctx, dkc, dkl), jnp.where(is_ctx, dvc, dvl)
        dqs, dkvs, dkr = [], [], None
        for h in range(H):
            dqs += [dq_[:, h * QK_PAD:h * QK_PAD + QK_NOPE], _rope128_t(dq_[:, h * QK_PAD + QK_NOPE:(h + 1) * QK_PAD], cs, sn)]
            dkvs += [dk_[:, h * QK_PAD:h * QK_PAD + QK_NOPE].astype(BF16), dv_[:, h * V_HEAD:(h + 1) * V_HEAD]]
            part = dk_[:, h * QK_PAD + QK_NOPE:(h + 1) * QK_PAD]
            dkr = part if dkr is None else dkr + part
        return jnp.concatenate(dqs, axis=1), jnp.concatenate(dkvs, axis=1), _rope128_t(dkr, cs, sn)

    drope_ins = [("rowc", dq_lat, 0, H * QK_PAD, nlat - 1), ("full", dq_ctx), ("row", dk_lat, 0, H * QK_PAD), ("full", dk_ctx),
                 ("row", dv_lat, 0, H * V_HEAD), ("full", dv_ctx), ("row", cos_t, 0, LANES), ("row", sin_t, 0, LANES)]
    dqraw, dkv, dz_kr = _rowwise("l0_drope", f_rope_bwd, nt, TR, drope_ins,
                                 [("row", TA, H * QK_PAD, BF16), ("row", TA, H * QK_PAD, BF16), ("row", TA, LANES, BF16)])
    dqn = _matmul("l0_dqb_x", dqraw, w_qbp, "nt", TA, Q_LORA, H * QK_PAD)
    g_wqbp = _matmul("l0_dqb_w", qn, dqraw, "tn", Q_LORA, H * QK_PAD, TA).reshape(Q_LORA, H, QK_PAD)
    g_rope = g_wqbp[:, :, QK_NOPE:QK_NOPE + QK_ROPE] + _rot_cols_t(g_wqbp[:, :, QK_NOPE + QK_ROPE:])
    grads["l0_w_qb"] = jnp.concatenate([g_wqbp[:, :, :QK_NOPE], g_rope], axis=2).reshape(Q_LORA, H * (QK_NOPE + QK_ROPE)).astype(BF16)
    dkvn = _matmul("l0_dkvb_x", dkv, full["l0_w_kvb"], "nt", TA, KV_LORA, H * QK_PAD)
    grads["l0_w_kvb"] = _matmul("l0_dkvb_w", kvn, dkv, "tn", KV_LORA, H * QK_PAD, TA, out_dtype=BF16)
    norms_bwd = _vjp_of(_f_norms, 4, (0, 1, 2, 3))
    dz_kv, dz_q, grads["l0_kv_norm"], grads["l0_q_norm"] = _rowwise(
        "l0_dlatnorm", norms_bwd, nt, TR, norm_ins + [("row", dkvn, 0, KV_LORA), ("row", dqn, 0, Q_LORA)],
        [("row", TA, KV_LORA, BF16), ("row", TA, Q_LORA, BF16), ("acc", 1, KV_LORA), ("acc", 1, Q_LORA)])
    dz0 = jnp.concatenate([dz_a, dz_q, dz_kv, dz_kr, jnp.zeros((TA, W0 - KROFF0 - LANES), BF16)], axis=1)
    dh0 = _matmul("l0_din_x", dz0, w_in0p, "nt", TA, D, W0)
    g0p = _matmul("l0_din_w", h0, dz0, "tn", D, W0, TA)
    g_kr = g0p[:, KROFF0:KROFF0 + QK_ROPE] + _rot_cols_t(g0p[:, KROFF0 + QK_ROPE:KROFF0 + 2 * QK_ROPE])
    grads["l0_w_in"] = jnp.concatenate([g0p[:, KVOFF0:KROFF0], g_kr, g0p[:, QOFF0:KVOFF0], g0p[:, :QOFF0]], axis=1).astype(BF16)

    def f_mod_bwd(i, x_, w_, sh, sc, dh, dxd):
        _, pull = jax.vjp(lambda a, b, c_, d_: _modulate(a, b, c_, d_), x_, w_, sh, sc)
        dx, dw, dsh, dsc = pull(dh)
        return dx + dxd, dw, dsh, dsc

    dXf, grads["l0_norm1"], dm0_0, dm1_0 = _rowwise(
        "l0_dmod1", f_mod_bwd, nt, TR,
        [("row", X, 0, D), ("full", n1_0), ("grp", P(0, 0)), ("grp", P(0, 1)), ("row", dh0, 0, D), ("row", dXd, 0, D)],
        [("row", TA, D, F32), ("acc", 1, D), ("accg", 1, D), ("accg", 1, D)], nlat)
    grad_x = dXf[:T][None]

    zD = jnp.zeros((1, D), F32)
    lat = lambda a: a[0] if a.ndim == 3 else a
    cxt = lambda a: a[1] if a.ndim == 3 else zD
    dmods = []
    for parts in ((dm0_0, dm1_0, dm2_0, dm3_0, dm4_0, dm5_0), (dm0_1, dm1_1, dm2_1, dm3_1, dm4_1, dm5_1)):
        dmods.append((jnp.concatenate([lat(a) for a in parts], axis=1), jnp.concatenate([cxt(a) for a in parts], axis=1)))
    small_names = ["l0_norm1", "l0_norm2", "l0_kv_norm", "l0_q_norm", "l0_conv_a", "l0_ffn_conv_w", "l0_ffn_conv_b",
                   "l1_norm1", "l1_norm2", "l1_o_norm", "l1_gate_fw_w", "l1_gate_fw_b", "l1_gate_bw_w", "l1_gate_bw_b",
                   "l1_ffn_conv_w", "l1_ffn_conv_b", "final_norm"]
    pieces = [("dm0", dmods[0][0]), ("dmc0", dmods[0][1]), ("dm1", dmods[1][0]), ("dmc1", dmods[1][1])]
    pieces += [("l0_ada_b", dmods[0][0] + dmods[0][1]), ("l1_ada_b", dmods[1][0] + dmods[1][1])]
    pieces += [(n, grads[n]) for n in small_names]
    offs, cur = {}, 0
    for n, a in pieces:
        offs[n] = (cur, a.size, a.shape)
        cur += -(-a.size // LANES) * LANES
    n_pad = -(-cur // 1024) * 1024
    flat = jnp.concatenate([jnp.pad(a.reshape(-1), (0, -a.size % LANES)) for _, a in pieces] + [jnp.zeros((n_pad - cur,), F32)])
    (small_g,) = _exchange("gather_small", [flat.reshape(n_pad // LANES, LANES)], "ag_all")
    small_sum = _sum_slots("sum_small", small_g).reshape(-1)
    small_all = small_g.reshape(8, -1)
    take = lambda n: small_sum[offs[n][0]:offs[n][0] + offs[n][1]].reshape(offs[n][2])

    cc_parts = []
    for l in (0, 1):
        o_m, sz, _ = offs["dm%d" % l]
        d16 = jnp.concatenate([small_all[:, o_m:o_m + sz], take("dmc%d" % l).reshape(1, -1), jnp.zeros((7, sz), F32)], axis=0)
        d16 = lax.dynamic_slice_in_dim(d16.reshape(16, 4, ADA_S), chip, 1, axis=1).reshape(16, ADA_S)
        grads["l%d_ada_w" % l] = _matmul("ada_dw", act16, d16, "tn", D, ADA_S, 16, tn_cap=512)
        cc_parts.append(_matmul("ada_dx", d16, W["l%d_ada_w" % l], "nt", 16, D, ADA_S, tn_cap=512))
        grads["l%d_ada_b" % l] = take("l%d_ada_b" % l).reshape(-1)
    cc_g = _exchange("gather_cc", cc_parts, "ag_xy")

    def cc_call():
        def body(a_ref, b_ref, c_ref, o_ref):
            tot = a_ref[0, 8:9, :] + b_ref[0, 8:9, :]
            for j in range(1, 4):
                tot = tot + (a_ref[j, 8:9, :] + b_ref[j, 8:9, :])
            _, pull = jax.vjp(_silu, c_ref[...])
            o_ref[...] = pull(tot)[0]
        return pl.pallas_call(body, name="c_ctx_grad", out_shape=jax.ShapeDtypeStruct((1, D), F32))(cc_g[0], cc_g[1], row(c_ctx))

    grads["c_ctx"] = cc_call().reshape(-1)

    pair_early = pair_sums("b", early)
    landed = dict(landed_late)
    for n, l, p in zip(early, _exchange("scatter_grads", pair_early, "a2a_xy", fill_own=False), pair_early):
        landed[n] = set_own(l, own_half(p, chip), chip)
    halves = [_sum_slots("sum_" + n[3:], landed[n]) for n in big]
    others = _exchange("pair_grads", halves, "swap_c")

    out_g, out_d, out_m, out_v = {}, {}, {}, {}
    for n, mine, other in zip(big, halves, others):
        g2 = jnp.where(my_c == 0, jnp.concatenate([mine, other], 0), jnp.concatenate([other, mine], 0))
        out_g[n], out_d[n], out_m[n], out_v[n] = _adam("adam_" + n[3:], W[n], [g2], MO[n], VO[n])
    for l in (0, 1):
        n = "l%d_ada_w" % l
        out_g[n], out_d[n], out_m[n], out_v[n] = _adam("adam_ada_w", W[n], [grads[n]], MO[n], VO[n])
    rest = [n for n in names if n not in out_g]
    g_rest = {}
    for n in rest:
        if n == "c_ctx" or n.endswith("ada_b"):
            g_rest[n] = grads[n]
        elif n in small_sh:
            gfull = take(n)
            cs = gfull.shape[1] // 4
            g_rest[n] = lax.dynamic_slice_in_dim(gfull, chip * cs, cs, axis=1)
        else:
            g_rest[n] = take(n).reshape(W[n].shape)
    sizes = [W[n].size for n in rest]
    tot = sum(-(-s // LANES) * LANES for s in sizes)
    tot_pad = -(-tot // 1024) * 1024

    def pack(d):
        parts = [jnp.pad(d[n].reshape(-1), (0, -d[n].size % LANES)) for n in rest]
        return jnp.concatenate(parts + [jnp.zeros((tot_pad - tot,), F32)]).reshape(tot_pad // LANES, LANES)

    packed = _adam("adam_small", pack(W), [pack(g_rest)], pack(MO), pack(VO))
    cur = 0
    for n, s in zip(rest, sizes):
        for dst, arr in zip((out_g, out_d, out_m, out_v), packed):
            dst[n] = arr.reshape(-1)[cur:cur + s].reshape(W[n].shape)
        cur += -(-s // LANES) * LANES

    return (loss, grad_x, *[out_g[n] for n in names], *[out_d[n] for n in names],
            *[out_m[n] for n in names], *[out_v[n] for n in names])
```

```python
import numpy as np

import jax
import jax.numpy as jnp
from jax import lax
from jax.experimental import pallas as pl
from jax.experimental.pallas import tpu as pltpu

F32, BF16 = jnp.float32, jnp.bfloat16
MESH_ID = pl.DeviceIdType.MESH

EPS = 1e-6
N_MOD = 6
MLA_HEADS, QK_NOPE, QK_ROPE, V_HEAD, Q_LORA, KV_LORA = 8, 128, 64, 128, 512, 256
QK_PAD = 2 * QK_NOPE
ROPE_THETA, GRID_W = 10000.0, 64
GLA_HEADS, GATE_RANK, GATE_NORMALIZER, CHUNK = 4, 16, 16.0, 64
ADAM_LR, ADAM_B1, ADAM_B2, ADAM_EPS, ADAM_WD, ADAM_STEP = 0.001, 0.9, 0.999, 1e-08, 0.01, 10

LANES = 128
TR = 256
V7X_VMEM_BYTES = 64 * 2 ** 20
VMEM_LIMIT = V7X_VMEM_BYTES - 8 * 2 ** 20

NT_DIMS = (((1,), (1,)), ((), ()))
TN_DIMS = (((0,), (0,)), ((), ()))
NN_DIMS = (((1,), (0,)), ((), ()))

def _rot_cols(w):
    q = QK_ROPE // 4
    return jnp.concatenate([-w[..., q:2 * q], w[..., :q], -w[..., 3 * q:], w[..., 2 * q:3 * q]], axis=-1)


def _rot_cols_t(g):
    q = QK_ROPE // 4
    return jnp.concatenate([g[..., q:2 * q], -g[..., :q], g[..., 3 * q:], -g[..., 2 * q:3 * q]], axis=-1)


def _cparams(sem=None):
    return pltpu.CompilerParams(dimension_semantics=sem, vmem_limit_bytes=VMEM_LIMIT)


def _tile(dim, cap, quantum=LANES):
    if dim <= cap:
        return dim
    t = (cap // quantum) * quantum
    while t >= quantum:
        if dim % t == 0:
            return t
        t -= quantum
    return dim


_REL_XY = ((1, 0, 0), (0, 1, 0), (1, 1, 0))
_REL_ALL = tuple((a, b, c) for a in (0, 1) for b in (0, 1) for c in (0, 1))[1:]
_REL_C = ((0, 0, 1),)


_RELS = {"ag_xy": _REL_XY, "a2a_xy": _REL_XY, "ag_all": _REL_ALL, "swap_c": _REL_C}
_LEAD = {"ag_xy": (4,), "ag_all": (8,), "a2a_xy": (), "swap_c": ()}


def _remote_copies(mode, ins, outs, send, recv):
    x, y, c = lax.axis_index("x"), lax.axis_index("y"), lax.axis_index("c")
    chip, dev = 2 * x + y, 4 * x + 2 * y + c
    rels, plan = _RELS[mode], []
    for a, (i_ref, o) in enumerate(zip(ins, outs)):
        for r, (bx, by, bc) in enumerate(rels):
            px = 1 - x if bx else x
            py = 1 - y if by else y
            pc = 1 - c if bc else c
            pchip, pdev = 2 * px + py, 4 * px + 2 * py + pc
            if mode == "ag_xy":
                src, dst, mine = i_ref, o.at[chip], o.at[pchip]
            elif mode == "ag_all":
                src, dst, mine = i_ref, o.at[dev], o.at[pdev]
            elif mode == "a2a_xy":
                src, dst, mine = i_ref.at[pchip], o.at[chip], o.at[pchip]
            else:
                src, dst, mine = i_ref, o, o
            k = a * len(rels) + r
            mk = lambda d: pltpu.make_async_remote_copy(src_ref=src, dst_ref=d, send_sem=send.at[k], recv_sem=recv.at[k],
                                                        device_id=(px, py, pc), device_id_type=MESH_ID)
            plan.append((mk(dst), mk(mine)))
    return plan


def _exchange(name, arrs, mode, fill_own=True):
    n, nr = len(arrs), len(_RELS[mode])
    out_shape = tuple(jax.ShapeDtypeStruct(_LEAD[mode] + a.shape, a.dtype) for a in arrs)

    def body(*refs):
        ins, outs = refs[:n], refs[n:2 * n]
        send, recv, loc = refs[2 * n:]
        chip = 2 * lax.axis_index("x") + lax.axis_index("y")
        local = []
        if fill_own and mode != "swap_c":
            slot = 2 * chip + lax.axis_index("c") if mode == "ag_all" else chip
            for a in range(n):
                lc = pltpu.make_async_copy(ins[a].at[chip] if mode == "a2a_xy" else ins[a], outs[a].at[slot], loc.at[a])
                lc.start()
                local.append(lc)
        plan = _remote_copies(mode, ins, outs, send, recv)
        for cp, _ in plan:
            cp.start()
        for cp, landing in plan:
            cp.wait_send()
            landing.wait_recv()
        for lc in local:
            lc.wait()

    hbm = pl.BlockSpec(memory_space=pl.ANY)
    res = pl.pallas_call(
        body, name=name, out_shape=out_shape, in_specs=[hbm] * n, out_specs=tuple([hbm] * n),
        scratch_shapes=[pltpu.SemaphoreType.DMA((n * nr,)), pltpu.SemaphoreType.DMA((n * nr,)),
                        pltpu.SemaphoreType.DMA((max(n, 1),))],
    )(*arrs)
    return list(res)


class _Hosted:
    def __init__(self, arrs, mode):
        self.arrs, self.mode, self.n = list(arrs), mode, len(arrs)
        self.out_shape = [jax.ShapeDtypeStruct(_LEAD[mode] + a.shape, a.dtype) for a in arrs]
        self.specs = [pl.BlockSpec(memory_space=pl.ANY)] * self.n
        nsem = self.n * len(_RELS[mode])
        self.scratch = [pltpu.SemaphoreType.DMA((nsem,)), pltpu.SemaphoreType.DMA((nsem,))]

    def run(self, first, last, ins, outs, sems):
        @pl.when(first)
        def _():
            for cp, _ in _remote_copies(self.mode, ins, outs, *sems):
                cp.start()

        @pl.when(last)
        def _():
            for cp, landing in _remote_copies(self.mode, ins, outs, *sems):
                cp.wait_send()
                landing.wait_recv()


def _matmul(name, a, b, mode, M, N, K, out_dtype=F32, a_off=(0, 0), b_off=(0, 0), tm_cap=1024, tn_cap=1024, tk_cap=2816):
    tm, tn = _tile(M, tm_cap, LANES if M % LANES == 0 else 8), _tile(N, tn_cap)
    tk = K if K <= 4096 else _tile(K, tk_cap)
    nk = K // tk
    assert M % tm == 0 and N % tn == 0 and K % tk == 0, (name, M, N, K, tm, tn, tk)
    if mode == "nn":
        ab, bb, dims = (tm, tk), (tk, tn), NN_DIMS
        ai = lambda i, j, k: (i + a_off[0] // tm, k + a_off[1] // tk)
        bi = lambda i, j, k: (k + b_off[0] // tk, j + b_off[1] // tn)
        chk = (a_off[0] % tm, a_off[1] % tk, b_off[0] % tk, b_off[1] % tn)
    elif mode == "nt":
        ab, bb, dims = (tm, tk), (tn, tk), NT_DIMS
        ai = lambda i, j, k: (i + a_off[0] // tm, k + a_off[1] // tk)
        bi = lambda i, j, k: (j + b_off[0] // tn, k + b_off[1] // tk)
        chk = (a_off[0] % tm, a_off[1] % tk, b_off[0] % tn, b_off[1] % tk)
    else:
        ab, bb, dims = (tk, tm), (tk, tn), TN_DIMS
        ai = lambda i, j, k: (k + a_off[0] // tk, i + a_off[1] // tm)
        bi = lambda i, j, k: (k + b_off[0] // tk, j + b_off[1] // tn)
        chk = (a_off[0] % tk, a_off[1] % tm, b_off[0] % tk, b_off[1] % tn)
    assert not any(chk), (name, chk)

    own_acc = nk > 1 and out_dtype != F32

    def body(a_ref, b_ref, o_ref, *acc):
        dot = lambda: lax.dot_general(a_ref[...].astype(BF16), b_ref[...].astype(BF16), dims, preferred_element_type=F32)
        if nk == 1:
            o_ref[...] = dot().astype(out_dtype)
            return
        k, dst = pl.program_id(2), (acc[0] if own_acc else o_ref)

        @pl.when(k == 0)
        def _():
            dst[...] = jnp.zeros_like(dst)

        dst[...] += dot()
        if own_acc:
            @pl.when(k == nk - 1)
            def _():
                o_ref[...] = acc[0][...].astype(out_dtype)

    return pl.pallas_call(
        body, name=name, out_shape=jax.ShapeDtypeStruct((M, N), out_dtype), grid=(M // tm, N // tn, nk),
        in_specs=[pl.BlockSpec(ab, ai), pl.BlockSpec(bb, bi)], out_specs=pl.BlockSpec((tm, tn), lambda i, j, k: (i, j)),
        scratch_shapes=[pltpu.VMEM((tm, tn), F32)] if own_acc else [],
        compiler_params=_cparams(("parallel", "parallel", "arbitrary")),
    )(a, b)


def _rowwise(name, fn, grid_n, tr, ins, outs, nlat=None):
    nlat = grid_n if nlat is None else nlat
    grp = lambda i: jnp.minimum(i // nlat, 1)
    in_specs, args = [], []
    for spec in ins:
        kind, arr = spec[0], spec[1]
        if kind == "row":
            in_specs.append(pl.BlockSpec((tr, spec[3]), lambda i, cb=spec[2]: (i, cb)))
        elif kind == "rowc":
            in_specs.append(pl.BlockSpec((tr, spec[3]), lambda i, cb=spec[2], mx=spec[4]: (jnp.minimum(i, mx), cb)))
        elif kind == "prev":
            in_specs.append(pl.BlockSpec((8, spec[3]), lambda i, cb=spec[2]: (jnp.maximum(i * (tr // 8) - 1, 0), cb)))
        elif kind == "next":
            nb = arr.shape[0] // 8
            in_specs.append(pl.BlockSpec((8, spec[3]), lambda i, cb=spec[2], nb=nb: (jnp.minimum((i + 1) * (tr // 8), nb - 1), cb)))
        elif kind == "grp":
            in_specs.append(pl.BlockSpec((None,) + arr.shape[1:], lambda i: (grp(i), 0, 0)))
        else:
            in_specs.append(pl.BlockSpec(arr.shape, lambda i: (0, 0)))
        args.append(arr)
    out_shape, out_specs = [], []
    for spec in outs:
        if spec[0] == "row":
            out_shape.append(jax.ShapeDtypeStruct((spec[1], spec[2]), spec[3]))
            out_specs.append(pl.BlockSpec((tr, spec[2]), lambda i: (i, 0)))
        elif spec[0] == "acc":
            out_shape.append(jax.ShapeDtypeStruct((spec[1], spec[2]), F32))
            out_specs.append(pl.BlockSpec((spec[1], spec[2]), lambda i: (0, 0)))
        else:
            out_shape.append(jax.ShapeDtypeStruct((2, spec[1], spec[2]), F32))
            out_specs.append(pl.BlockSpec((None, spec[1], spec[2]), lambda i: (grp(i), 0, 0)))
    n_in = len(ins)
    has_acc = any(s[0] != "row" for s in outs)

    def body(*refs):
        i = pl.program_id(0)
        res = fn(i, *[r[...] for r in refs[:n_in]])
        for spec, ref, val in zip(outs, refs[n_in:], res):
            if spec[0] == "row":
                ref[...] = val.astype(ref.dtype)
            else:
                first = (i == 0) if spec[0] == "acc" else jnp.logical_or(i == 0, i == nlat)

                @pl.when(first)
                def _(ref=ref, val=val):
                    ref[...] = val

                @pl.when(jnp.logical_not(first))
                def _(ref=ref, val=val):
                    ref[...] += val

    return pl.pallas_call(
        body, name=name, out_shape=tuple(out_shape), grid=(grid_n,), in_specs=in_specs, out_specs=tuple(out_specs),
        compiler_params=_cparams(("arbitrary",) if has_acc else ("parallel",)),
    )(*args)


def _vjp_of(fwd, n_in, wrt):
    def bwd(i, *args):
        _, pull = jax.vjp(lambda *a: fwd(i, *a), *args[:n_in])
        g = pull(tuple(args[n_in:]))
        return tuple(g[k] for k in wrt)
    return bwd


def _rms(x, w):
    return x * lax.rsqrt(jnp.mean(x * x, axis=-1, keepdims=True) + EPS) * w


def _modulate(x, w, shift, scale):
    return _rms(x, w) * (1.0 + scale) + shift


def _silu(x):
    return x * jax.nn.sigmoid(x)


def _log_sigmoid(x):
    return jnp.minimum(x, 0.0) - jnp.log(1.0 + jnp.exp(-jnp.abs(x)))


def _f_mod(i, x, w, sh, sc):
    return (_modulate(x, w, sh, sc),)


def _f_resmod(i, x, y, gate, w, sh, sc):
    x1 = x + gate * y
    return x1, _modulate(x1, w, sh, sc)


def _f_norms(i, kvl, ql, kvw, qw):
    return _rms(kvl, kvw), _rms(ql, qw)


def _f_silu(i, x):
    return (_silu(x),)


def _rope128(x2, cos, sin):
    return x2 * cos + pltpu.roll(x2, QK_ROPE, 1) * sin


def _rope128_t(d2, cos, sin):
    return d2 * cos + pltpu.roll(d2 * sin, QK_ROPE, 1)


def _shifters(i, tr, T, TA):
    assert T % tr == 0 and TA % tr == 0
    loc = lax.broadcasted_iota(jnp.int32, (tr, 1), 0)
    starts_seq = jnp.logical_or(i * tr == 0, i * tr == T)
    ends_seq = jnp.logical_or((i + 1) * tr == T, (i + 1) * tr == TA)

    def prev(x, halo):
        return jnp.where(loc == 0, jnp.where(starts_seq, 0.0, halo[7:8, :]), pltpu.roll(x, 1, 0))

    def nxt(x, halo):
        return jnp.where(loc == tr - 1, jnp.where(ends_seq, 0.0, halo[0:1, :]), pltpu.roll(x, tr - 1, 0))

    return prev, nxt


def _colsum(x):
    return jnp.sum(x, axis=0, keepdims=True)


MLA_SCALE = (QK_NOPE + QK_ROPE) ** -0.5
LOG2E, LN2 = 1.4426950408889634, 0.6931471805599453
MLA_QSCALE = MLA_SCALE * LOG2E
MLA_SUB = 256
MLA_KV_TILE = 2816


def _mla_fwd_lat(q, kf, kv, T, TA, hosted):
    H = MLA_HEADS
    tq, tk = _tile(T, 1024, TR), _tile(TA, MLA_KV_TILE)
    nq, nk = T // tq, TA // tk
    nh = hosted.n

    def body(q_ref, k_ref, v_ref, *rest):
        o_ref, lse_ref = rest[nh:nh + 2]
        m_sc, l_sc, acc_sc = rest[2 * nh + 2:2 * nh + 5]
        hi, qi, ki = pl.program_id(0), pl.program_id(1), pl.program_id(2)
        hosted.run(jnp.logical_and(hi == 0, jnp.logical_and(qi == 0, ki == 0)),
                   jnp.logical_and(hi == H - 1, jnp.logical_and(qi == nq - 1, ki == nk - 1)),
                   rest[:nh], rest[nh + 2:2 * nh + 2], rest[2 * nh + 5:])

        @pl.when(ki == 0)
        def _():
            m_sc[...] = jnp.full_like(m_sc, -jnp.inf)
            l_sc[...] = jnp.zeros_like(l_sc)
            acc_sc[...] = jnp.zeros_like(acc_sc)

        sb = min(tq, MLA_SUB)
        scores = lambda r: lax.dot_general(q_ref[r * sb:(r + 1) * sb, :], k_ref[...], NT_DIMS, preferred_element_type=F32)
        s_next = scores(0)
        for r in range(tq // sb):
            s, rows = s_next, slice(r * sb, (r + 1) * sb)
            if r + 1 < tq // sb:
                s_next = scores(r + 1)
            m_old = m_sc[rows, :]
            m_new = jnp.maximum(m_old, jnp.max(s, axis=-1, keepdims=True))
            alpha = jnp.exp2(m_old - m_new)
            p = jnp.exp2(s - m_new)
            l_sc[rows, :] = alpha * l_sc[rows, :] + jnp.sum(p, axis=-1, keepdims=True)
            acc_sc[rows, :] = alpha * acc_sc[rows, :] + jnp.dot(p.astype(BF16), v_ref[...], preferred_element_type=F32)
            m_sc[rows, :] = m_new

        @pl.when(ki == nk - 1)
        def _():
            o_ref[...] = (acc_sc[...] / l_sc[...]).astype(o_ref.dtype)
            lse_ref[...] = m_sc[...] + jnp.log2(l_sc[...])

    res = pl.pallas_call(
        body, name="mla_fwd_lat", grid=(H, nq, nk),
        out_shape=[jax.ShapeDtypeStruct((T, H * V_HEAD), BF16), jax.ShapeDtypeStruct((H, T, 1), F32)] + hosted.out_shape,
        in_specs=[pl.BlockSpec((tq, QK_PAD), lambda h, i, k: (i, h)), pl.BlockSpec((tk, QK_PAD), lambda h, i, k: (k, h)),
                  pl.BlockSpec((tk, V_HEAD), lambda h, i, k: (k, 2 * h + 1))] + hosted.specs,
        out_specs=[pl.BlockSpec((tq, V_HEAD), lambda h, i, k: (i, h)),
                   pl.BlockSpec((None, tq, 1), lambda h, i, k: (h, i, 0))] + hosted.specs,
        scratch_shapes=[pltpu.VMEM((tq, 1), F32), pltpu.VMEM((tq, 1), F32), pltpu.VMEM((tq, V_HEAD), F32)] + hosted.scratch,
        compiler_params=_cparams(("arbitrary", "arbitrary", "arbitrary")),
    )(q, kf, kv, *hosted.arrs)
    return res[0], res[1], list(res[2:])


def _mla_ctx_probs(q_ref, k_ref):
    s = lax.dot_general(q_ref[...], k_ref[...], NT_DIMS, preferred_element_type=F32)
    p = jnp.exp2(s - jnp.max(s, axis=-1, keepdims=True))
    return p, jnp.sum(p, axis=-1, keepdims=True)


def _mla_fwd_ctx(q, kf, kv, T):
    H, cb = MLA_HEADS, T // TR

    def body(q_ref, k_ref, v_ref, o_ref):
        p, l = _mla_ctx_probs(q_ref, k_ref)
        o_ref[...] = (jnp.dot(p.astype(BF16), v_ref[...], preferred_element_type=F32) / l).astype(o_ref.dtype)

    return pl.pallas_call(
        body, name="mla_fwd_ctx", grid=(H,), out_shape=jax.ShapeDtypeStruct((TR, H * V_HEAD), BF16),
        in_specs=[pl.BlockSpec((TR, QK_PAD), lambda h: (cb, h)), pl.BlockSpec((TR, QK_PAD), lambda h: (cb, h)),
                  pl.BlockSpec((TR, V_HEAD), lambda h: (cb, 2 * h + 1))],
        out_specs=pl.BlockSpec((TR, V_HEAD), lambda h: (0, h)), compiler_params=_cparams(("parallel",)),
    )(q, kf, kv)


def _mla_grads(p, q_ref, k_ref, v_ref, do, delta):
    dob = do.astype(BF16)
    dp = lax.dot_general(dob, v_ref[...], NT_DIMS, preferred_element_type=F32)
    ds = (p * (dp - delta)).astype(BF16)
    return (jnp.dot(ds, k_ref[...], preferred_element_type=F32),
            lax.dot_general(ds, q_ref[...], TN_DIMS, preferred_element_type=F32) * LN2,
            lax.dot_general(p.astype(BF16), dob, TN_DIMS, preferred_element_type=F32))


def _mla_bwd_lat(q, kf, kv, o, dcat, lse, do_cb0, T, TA, hosted):
    H = MLA_HEADS
    tq, tk = _tile(T, 1024, TR), _tile(TA, MLA_KV_TILE)
    nq, nk = T // tq, TA // tk
    nh = hosted.n

    def body(q_ref, k_ref, v_ref, o_ref, do_ref, lse_ref, *rest):
        dq_ref, dk_ref, dv_ref = rest[nh:nh + 3]
        dk_acc, dv_acc = rest[2 * nh + 3:2 * nh + 5]
        hi, ki, qi = pl.program_id(0), pl.program_id(1), pl.program_id(2)
        hosted.run(jnp.logical_and(hi == 0, jnp.logical_and(qi == 0, ki == 0)),
                   jnp.logical_and(hi == H - 1, jnp.logical_and(qi == nq - 1, ki == nk - 1)),
                   rest[:nh], rest[nh + 3:2 * nh + 3], rest[2 * nh + 5:])

        @pl.when(jnp.logical_and(ki == 0, qi == 0))
        def _():
            dq_ref[...] = jnp.zeros_like(dq_ref)

        @pl.when(qi == 0)
        def _():
            dk_acc[...] = jnp.zeros_like(dk_acc)
            dv_acc[...] = jnp.zeros_like(dv_acc)

        sb = min(tq, MLA_SUB)

        def products(r):
            rows = slice(r * sb, (r + 1) * sb)
            dob = do_ref[rows, :].astype(BF16)
            return (lax.dot_general(q_ref[rows, :], k_ref[...], NT_DIMS, preferred_element_type=F32),
                    lax.dot_general(dob, v_ref[...], NT_DIMS, preferred_element_type=F32), dob)

        nxt, dk, dv = products(0), None, None
        for r in range(tq // sb):
            (s, dp, dob), rows = nxt, slice(r * sb, (r + 1) * sb)
            if r + 1 < tq // sb:
                nxt = products(r + 1)
            delta = jnp.sum(do_ref[rows, :] * o_ref[rows, :].astype(F32), axis=-1, keepdims=True)
            p = jnp.exp2(s - lse_ref[rows, :])
            ds = (p * (dp - delta)).astype(BF16)
            dq_ref[pl.ds(pl.multiple_of(qi * tq + r * sb, sb), sb), :] += jnp.dot(ds, k_ref[...], preferred_element_type=F32)
            dk_r = lax.dot_general(ds, q_ref[rows, :], TN_DIMS, preferred_element_type=F32)
            dv_r = lax.dot_general(p.astype(BF16), dob, TN_DIMS, preferred_element_type=F32)
            dk, dv = (dk_r, dv_r) if dk is None else (dk + dk_r, dv + dv_r)
        dk_acc[...] += dk
        dv_acc[...] += dv

        @pl.when(qi == nq - 1)
        def _():
            dk_ref[...] = dk_acc[...] * LN2
            dv_ref[...] = dv_acc[...].astype(dv_ref.dtype)

    res = pl.pallas_call(
        body, name="mla_bwd_lat", grid=(H, nk, nq),
        out_shape=[jax.ShapeDtypeStruct((T, H * QK_PAD), F32), jax.ShapeDtypeStruct((TA, H * QK_PAD), F32),
                   jax.ShapeDtypeStruct((TA, H * V_HEAD), BF16)] + hosted.out_shape,
        in_specs=[pl.BlockSpec((tq, QK_PAD), lambda h, k, i: (i, h)), pl.BlockSpec((tk, QK_PAD), lambda h, k, i: (k, h)),
                  pl.BlockSpec((tk, V_HEAD), lambda h, k, i: (k, 2 * h + 1)), pl.BlockSpec((tq, V_HEAD), lambda h, k, i: (i, h)),
                  pl.BlockSpec((tq, V_HEAD), lambda h, k, i: (i, do_cb0 + h)),
                  pl.BlockSpec((None, tq, 1), lambda h, k, i: (h, i, 0))] + hosted.specs,
        out_specs=[pl.BlockSpec((T, QK_PAD), lambda h, k, i: (0, h)), pl.BlockSpec((tk, QK_PAD), lambda h, k, i: (k, h)),
                   pl.BlockSpec((tk, V_HEAD), lambda h, k, i: (k, h))] + hosted.specs,
        scratch_shapes=[pltpu.VMEM((tk, QK_PAD), F32), pltpu.VMEM((tk, V_HEAD), F32)] + hosted.scratch,
        compiler_params=_cparams(("arbitrary", "arbitrary", "arbitrary")),
    )(q, kf, kv, o, dcat, lse, *hosted.arrs)
    return res[0], res[1], res[2], list(res[3:])


def _mla_bwd_ctx(q, kf, kv, dcat, dk_lat, dv_lat, do_cb0, T):
    H, cb = MLA_HEADS, T // TR

    def body(q_ref, k_ref, v_ref, do_ref, dkl_ref, dvl_ref, dq_ref, dk_ref, dv_ref):
        p, l = _mla_ctx_probs(q_ref, k_ref)
        do = do_ref[...]
        o = jnp.dot(p.astype(BF16), v_ref[...], preferred_element_type=F32) / l
        dq, dk, dv = _mla_grads(p / l, q_ref, k_ref, v_ref, do, jnp.sum(do * o, axis=-1, keepdims=True))
        dq_ref[...] = dq
        dk_ref[...] = dkl_ref[...] + dk
        dv_ref[...] = (dvl_ref[...].astype(F32) + dv).astype(dv_ref.dtype)

    at_ctx = lambda w, f: pl.BlockSpec((TR, w), lambda h: (cb, f(h)))
    at_0 = lambda w: pl.BlockSpec((TR, w), lambda h: (0, h))
    return pl.pallas_call(
        body, name="mla_bwd_ctx", grid=(H,),
        out_shape=(jax.ShapeDtypeStruct((TR, H * QK_PAD), F32), jax.ShapeDtypeStruct((TR, H * QK_PAD), F32),
                   jax.ShapeDtypeStruct((TR, H * V_HEAD), BF16)),
        in_specs=[at_ctx(QK_PAD, lambda h: h), at_ctx(QK_PAD, lambda h: h), at_ctx(V_HEAD, lambda h: 2 * h + 1),
                  at_ctx(V_HEAD, lambda h: do_cb0 + h), at_ctx(QK_PAD, lambda h: h), at_ctx(V_HEAD, lambda h: h)],
        out_specs=(at_0(QK_PAD), at_0(QK_PAD), at_0(V_HEAD)), compiler_params=_cparams(("parallel",)),
    )(q, kf, kv, dcat, dk_lat, dv_lat)


def _bdot_impl(a, b, dims):
    return lax.dot_general(a.astype(BF16), b.astype(BF16), dims, preferred_element_type=F32)


def _bdot(a, b, dims):
    @jax.custom_vjp
    def f(a_, b_):
        return _bdot_impl(a_, b_, dims)

    def fwd(a_, b_):
        return _bdot_impl(a_, b_, dims), (a_.astype(BF16), b_.astype(BF16))

    def bwd(res, ct):
        a_, b_ = res
        if dims == NN_DIMS:
            return _bdot_impl(ct, b_, NT_DIMS), _bdot_impl(a_, ct, TN_DIMS)
        if dims == NT_DIMS:
            return _bdot_impl(ct, b_, NN_DIMS), _bdot_impl(ct, a_, TN_DIMS)
        return _bdot_impl(b_, ct, NT_DIMS), _bdot_impl(a_, ct, NN_DIMS)

    f.defvjp(fwd, bwd)
    return f(a, b)


def _ones_dot_impl(ones, x, dims):
    hi = x.astype(BF16)
    r1 = x - hi.astype(F32)
    mid = r1.astype(BF16)
    lo = (r1 - mid.astype(F32)).astype(BF16)
    d = lambda t: lax.dot_general(ones, t, dims, preferred_element_type=F32)
    return (d(lo) + d(mid)) + d(hi)


@jax.custom_vjp
def _ones_dot(ones, x):
    return _ones_dot_impl(ones, x, NN_DIMS)


_ones_dot.defvjp(lambda ones, x: (_ones_dot_impl(ones, x, NN_DIMS), ones),
                 lambda ones, ct: (jnp.zeros_like(ones), _ones_dot_impl(ones, ct, TN_DIMS)))


def _gla_tile(st, q, k, v, g, rev, q_scale):
    nc = q.shape[0] // CHUNK
    ii = lax.broadcasted_iota(jnp.int32, (CHUNK, CHUNK), 0)
    jj = lax.broadcasted_iota(jnp.int32, (CHUNK, CHUNK), 1)
    tri = (jj >= ii) if rev else (jj <= ii)
    ones = tri.astype(BF16)
    outs = [None] * nc
    for ci in (range(nc - 1, -1, -1) if rev else range(nc)):
        sl = slice(ci * CHUNK, (ci + 1) * CHUNK)
        qc, kc, vc, gc = q[sl] * q_scale, k[sl], v[sl], g[sl]
        b = _ones_dot(ones, gc)
        bl = b[0:1] if rev else b[CHUNK - 1:CHUNK]
        kd, qe, ke = kc * jnp.exp(bl - b), qc * jnp.exp(b), kc * jnp.exp(-b)
        att = jnp.where(tri, _bdot(qe, ke, NT_DIMS), 0.0)
        outs[ci] = _bdot(att, vc, NN_DIMS) + _bdot(qe, st, NT_DIMS)
        st = st * jnp.exp(bl) + _bdot(vc, kd, TN_DIMS)
    return st, jnp.concatenate(outs, axis=0)


def _gla_tiles(nt, nlat):
    return (lambda p: (p + nlat) % nt), (lambda p: nt - 1 - p)


def _gla_fwd(z1, g, dims, nt, nlat):
    DK, DV, q_off, k_off = dims
    KEY, TA = GLA_HEADS * DK, nt * TR
    in_specs, out_o = [], []
    for d, tile in enumerate(_gla_tiles(nt, nlat)):
        in_specs += [pl.BlockSpec((TR, DK), lambda h, p, t=tile: (t(p), q_off // DK + h)),
                     pl.BlockSpec((TR, DK), lambda h, p, t=tile: (t(p), k_off // DK + h)),
                     pl.BlockSpec((TR, DV), lambda h, p, t=tile: (t(p), h)),
                     pl.BlockSpec((TR, DK), lambda h, p, t=tile, d=d: (t(p), d * GLA_HEADS + h))]
        out_o.append(pl.BlockSpec((TR, DV), lambda h, p, t=tile: (t(p), h)))
    st_spec = pl.BlockSpec((None, None, DV, DK), lambda h, p: (h, p, 0, 0))

    def body(qf, kf, vf, gf, qb, kb, vb, gb, of_ref, ob_ref, sf_ref, sb_ref, st_f, st_b):
        @pl.when(pl.program_id(1) == 0)
        def _():
            st_f[...] = jnp.zeros_like(st_f)
            st_b[...] = jnp.zeros_like(st_b)

        sf_ref[...] = st_f[...]
        sb_ref[...] = st_b[...]
        new_f, out_f = _gla_tile(st_f[...], qf[...], kf[...], vf[...], gf[...], False, DK ** -0.5)
        new_b, out_b = _gla_tile(st_b[...], qb[...], kb[...], vb[...], gb[...], True, DK ** -0.5)
        st_f[...] = new_f
        st_b[...] = new_b
        of_ref[...] = out_f
        ob_ref[...] = out_b

    o_shape = jax.ShapeDtypeStruct((TA, GLA_HEADS * DV), F32)
    s_shape = jax.ShapeDtypeStruct((GLA_HEADS, nt, DV, DK), F32)
    return pl.pallas_call(
        body, name="gla_fwd", grid=(GLA_HEADS, nt), out_shape=(o_shape, o_shape, s_shape, s_shape),
        in_specs=in_specs, out_specs=(out_o[0], out_o[1], st_spec, st_spec),
        scratch_shapes=[pltpu.VMEM((DV, DK), F32)] * 2, compiler_params=_cparams(("parallel", "arbitrary")),
    )(z1, z1, z1, g, z1, z1, z1, g)


def _gla_bwd(z1, g, st_f, st_b, do, dims, nt, nlat):
    DK, DV, q_off, k_off = dims
    KEY, VAL, TA = GLA_HEADS * DK, GLA_HEADS * DV, nt * TR
    pos = lambda s: nt - 1 - s
    in_specs, out_specs = [], []
    for d, tile in enumerate(_gla_tiles(nt, nlat)):
        at = lambda w, f, t=tile: pl.BlockSpec((TR, w), lambda h, s: (t(pos(s)), f(h)))
        in_specs += [at(DK, lambda h: q_off // DK + h), at(DK, lambda h: k_off // DK + h), at(DV, lambda h: h),
                     at(DK, lambda h, d=d: d * GLA_HEADS + h),
                     pl.BlockSpec((None, None, DV, DK), lambda h, s: (h, pos(s), 0, 0)), at(DV, lambda h: h)]
        out_specs += [at(DK, lambda h: h), at(DK, lambda h: h), at(DV, lambda h: h), at(DK, lambda h: h)]

    def one(refs, outs, dst, rev):
        q_ref, k_ref, v_ref, g_ref, st_ref, do_ref = refs
        _, pull = jax.vjp(lambda st, q, k, v, gg: _gla_tile(st, q, k, v, gg, rev, DK ** -0.5),
                          st_ref[...], q_ref[...], k_ref[...], v_ref[...], g_ref[...])
        grads = pull((dst[...], do_ref[...]))
        dst[...] = grads[0]
        for o_ref, val in zip(outs, grads[1:]):
            o_ref[...] = val

    def body(*refs):
        dst_f, dst_b = refs[20:]

        @pl.when(pl.program_id(1) == 0)
        def _():
            dst_f[...] = jnp.zeros_like(dst_f)
            dst_b[...] = jnp.zeros_like(dst_b)

        one(refs[0:6], refs[12:16], dst_f, False)
        one(refs[6:12], refs[16:20], dst_b, True)

    shapes = [jax.ShapeDtypeStruct((TA, KEY), F32), jax.ShapeDtypeStruct((TA, KEY), F32),
              jax.ShapeDtypeStruct((TA, VAL), F32), jax.ShapeDtypeStruct((TA, KEY), F32)]
    res = pl.pallas_call(
        body, name="gla_bwd", grid=(GLA_HEADS, nt), out_shape=shapes * 2, in_specs=in_specs, out_specs=out_specs,
        scratch_shapes=[pltpu.VMEM((DV, DK), F32)] * 2, compiler_params=_cparams(("parallel", "arbitrary")),
    )(z1, z1, z1, g, st_f, do, z1, z1, z1, g, st_b, do)
    return res[:4], res[4:]


def _sum_slots(name, arr):
    S, R, C = arr.shape
    tr = _tile(R, max(16, (2 ** 19) // max(C, 1) // 16 * 16), 16)

    def body(a_ref, o_ref):
        acc = a_ref[0].astype(F32)
        for s in range(1, S):
            acc = acc + a_ref[s].astype(F32)
        o_ref[...] = acc

    return pl.pallas_call(
        body, name=name, grid=(R // tr,), out_shape=jax.ShapeDtypeStruct((R, C), F32),
        in_specs=[pl.BlockSpec((S, tr, C), lambda i: (0, i, 0))], out_specs=pl.BlockSpec((tr, C), lambda i: (i, 0)),
        compiler_params=_cparams(("parallel",)),
    )(arr)


def _sum_pair(name, a, b, out_dtype):
    shape, C = a.shape, a.shape[-1]
    a2, b2 = a.reshape(-1, C), b.reshape(-1, C)
    R = a2.shape[0]
    tr = _tile(R, max(16, (2 ** 19) // max(C, 1) // 16 * 16), 16)
    fn = lambda i, p, q: (p.astype(F32) + q.astype(F32),)
    (res,) = _rowwise(name, fn, R // tr, tr, [("row", a2, 0, C), ("row", b2, 0, C)], [("row", R, C, out_dtype)])
    return res.reshape(shape)


def _adam_math(w, g, m, v):
    m2 = ADAM_B1 * m + (1.0 - ADAM_B1) * g
    v2 = ADAM_B2 * v + (1.0 - ADAM_B2) * (g * g)
    m_hat = m2 / (1.0 - ADAM_B1 ** ADAM_STEP)
    v_hat = v2 / (1.0 - ADAM_B2 ** ADAM_STEP)
    return -ADAM_LR * (m_hat / (jnp.sqrt(v_hat) + ADAM_EPS) + ADAM_WD * w), m2, v2


def _adam(name, w, gs, m, v):
    R, C = w.shape
    tr = _tile(R, max(8, (2 ** 19) // max(C, 1) // 8 * 8), 8)
    ng = len(gs)

    def fn(i, w_, *rest):
        g = rest[0] if ng == 1 else rest[0] + rest[1]
        d, m2, v2 = _adam_math(w_, g, rest[ng], rest[ng + 1])
        return g, d, m2, v2

    ins = [("row", a, 0, C) for a in (w, *gs, m, v)]
    return _rowwise(name, fn, R // tr, tr, ins, [("row", R, C, F32)] * 4)


def _ffn_fwd(tag, h, w_up, cw, cb, w_down, n_rows, T, TA, F):
    tr = 128
    u = _matmul(tag + "_up", h, w_up, "nn", n_rows, 2 * F, h.shape[1])
    cwg, cwv, cbg, cbv = cw[:, :F], cw[:, F:], cb[:, :F], cb[:, F:]

    def fn(i, ug, uv, pg, pv, ng, nv, wg, wv, bg, bv):
        prev, nxt = _shifters(i, tr, T, TA)
        cg = wg[0:1] * prev(ug, pg) + wg[1:2] * ug + wg[2:3] * nxt(ug, ng) + bg
        cv = wv[0:1] * prev(uv, pv) + wv[1:2] * uv + wv[2:3] * nxt(uv, nv) + bv
        return (_silu(cg) * cv,)

    ins = [("row", u, 0, F), ("row", u, 1, F), ("prev", u, 0, F), ("prev", u, 1, F), ("next", u, 0, F), ("next", u, 1, F),
           ("full", cwg), ("full", cwv), ("full", cbg), ("full", cbv)]
    (act,) = _rowwise(tag + "_conv", fn, n_rows // tr, tr, ins, [("row", n_rows, F, BF16)])
    f = _matmul(tag + "_down", act, w_down, "nn", n_rows, w_down.shape[1], F)
    return u, act, f


def _ffn_bwd(tag, df, h, u, act, w_up, cw, cb, w_down, n_rows, T, TA, F):
    tr = 128
    D = w_down.shape[1]
    dact = _matmul(tag + "_ddown_x", df, w_down, "nt", n_rows, F, D, tm_cap=1408)
    g_down = _matmul(tag + "_ddown_w", act, df, "tn", F, D, n_rows, out_dtype=BF16)
    cwg, cwv, cbg, cbv = cw[:, :F], cw[:, F:], cb[:, :F], cb[:, F:]

    def fn_a(i, ug, uv, pg, pv, ng, nv, da, wg, wv, bg, bv):
        prev, nxt = _shifters(i, tr, T, TA)
        ugp, ugn, uvp, uvn = prev(ug, pg), nxt(ug, ng), prev(uv, pv), nxt(uv, nv)
        cg = wg[0:1] * ugp + wg[1:2] * ug + wg[2:3] * ugn + bg
        cv = wv[0:1] * uvp + wv[1:2] * uv + wv[2:3] * uvn + bv
        sg = jax.nn.sigmoid(cg)
        dcv = da * (cg * sg)
        dcg = da * cv * (sg * (1.0 + cg * (1.0 - sg)))
        return (jnp.concatenate([dcg, dcv], axis=1),
                _colsum(ugp * dcg), _colsum(ug * dcg), _colsum(ugn * dcg),
                _colsum(uvp * dcv), _colsum(uv * dcv), _colsum(uvn * dcv), _colsum(dcg), _colsum(dcv))

    ins = [("row", u, 0, F), ("row", u, 1, F), ("prev", u, 0, F), ("prev", u, 1, F), ("next", u, 0, F), ("next", u, 1, F),
           ("row", dact, 0, F), ("full", cwg), ("full", cwv), ("full", cbg), ("full", cbv)]
    res = _rowwise(tag + "_dconv_a", fn_a, n_rows // tr, tr, ins, [("row", n_rows, 2 * F, F32)] + [("acc", 1, F)] * 8)
    duc = res[0]
    g_cw = jnp.concatenate([jnp.concatenate(res[1:4], axis=0), jnp.concatenate(res[4:7], axis=0)], axis=1)
    g_cb = jnp.concatenate([res[7], res[8]], axis=1)

    def fn_b(i, dg, dv, pg, pv, ng, nv, wg, wv):
        prev, nxt = _shifters(i, tr, T, TA)
        dug = wg[0:1] * nxt(dg, ng) + wg[1:2] * dg + wg[2:3] * prev(dg, pg)
        duv = wv[0:1] * nxt(dv, nv) + wv[1:2] * dv + wv[2:3] * prev(dv, pv)
        return (jnp.concatenate([dug, duv], axis=1),)

    ins = [("row", duc, 0, F), ("row", duc, 1, F), ("prev", duc, 0, F), ("prev", duc, 1, F), ("next", duc, 0, F),
           ("next", duc, 1, F), ("full", cwg), ("full", cwv)]
    (du,) = _rowwise(tag + "_dconv_b", fn_b, n_rows // tr, tr, ins, [("row", n_rows, 2 * F, BF16)])
    dh = _matmul(tag + "_dup_x", du, w_up, "nt", n_rows, D, 2 * F)
    g_up = _matmul(tag + "_dup_w", h, du, "tn", D, 2 * F, n_rows, out_dtype=BF16)
    return dh, g_up, g_cw, g_cb, g_down


def _cols_to_shards(g):
    r, c = g.shape
    return g.reshape(r, 4, c // 4).transpose(1, 0, 2)


def _shards_to_cols(s):
    return s.transpose(1, 0, 2).reshape(s.shape[1], 4 * s.shape[2])


def kernel(x, c, ctx, c_ctx, l0_ada_w, l0_ada_b, l0_norm1, l0_w_in, l0_conv_a, l0_q_norm, l0_w_qb, l0_kv_norm, l0_w_kvb, l0_w_out, l0_norm2, l0_ffn_up, l0_ffn_conv_w, l0_ffn_conv_b, l0_ffn_down, l1_ada_w, l1_ada_b, l1_norm1, l1_w_in, l1_gate_fw_w, l1_gate_fw_b, l1_gate_bw_w, l1_gate_bw_b, l1_o_norm, l1_w_out, l1_norm2, l1_ffn_up, l1_ffn_conv_w, l1_ffn_conv_b, l1_ffn_down, final_norm, loss_target, m_c_ctx, m_l0_ada_w, m_l0_ada_b, m_l0_norm1, m_l0_w_in, m_l0_conv_a, m_l0_q_norm, m_l0_w_qb, m_l0_kv_norm, m_l0_w_kvb, m_l0_w_out, m_l0_norm2, m_l0_ffn_up, m_l0_ffn_conv_w, m_l0_ffn_conv_b, m_l0_ffn_down, m_l1_ada_w, m_l1_ada_b, m_l1_norm1, m_l1_w_in, m_l1_gate_fw_w, m_l1_gate_fw_b, m_l1_gate_bw_w, m_l1_gate_bw_b, m_l1_o_norm, m_l1_w_out, m_l1_norm2, m_l1_ffn_up, m_l1_ffn_conv_w, m_l1_ffn_conv_b, m_l1_ffn_down, m_final_norm, v_c_ctx, v_l0_ada_w, v_l0_ada_b, v_l0_norm1, v_l0_w_in, v_l0_conv_a, v_l0_q_norm, v_l0_w_qb, v_l0_kv_norm, v_l0_w_kvb, v_l0_w_out, v_l0_norm2, v_l0_ffn_up, v_l0_ffn_conv_w, v_l0_ffn_conv_b, v_l0_ffn_down, v_l1_ada_w, v_l1_ada_b, v_l1_norm1, v_l1_w_in, v_l1_gate_fw_w, v_l1_gate_fw_b, v_l1_gate_bw_w, v_l1_gate_bw_b, v_l1_o_norm, v_l1_w_out, v_l1_norm2, v_l1_ffn_up, v_l1_ffn_conv_w, v_l1_ffn_conv_b, v_l1_ffn_down, v_final_norm):
    W = dict(c_ctx=c_ctx, l0_ada_w=l0_ada_w, l0_ada_b=l0_ada_b, l0_norm1=l0_norm1, l0_w_in=l0_w_in, l0_conv_a=l0_conv_a, l0_q_norm=l0_q_norm, l0_w_qb=l0_w_qb, l0_kv_norm=l0_kv_norm, l0_w_kvb=l0_w_kvb, l0_w_out=l0_w_out, l0_norm2=l0_norm2, l0_ffn_up=l0_ffn_up, l0_ffn_conv_w=l0_ffn_conv_w, l0_ffn_conv_b=l0_ffn_conv_b, l0_ffn_down=l0_ffn_down, l1_ada_w=l1_ada_w, l1_ada_b=l1_ada_b, l1_norm1=l1_norm1, l1_w_in=l1_w_in, l1_gate_fw_w=l1_gate_fw_w, l1_gate_fw_b=l1_gate_fw_b, l1_gate_bw_w=l1_gate_bw_w, l1_gate_bw_b=l1_gate_bw_b, l1_o_norm=l1_o_norm, l1_w_out=l1_w_out, l1_norm2=l1_norm2, l1_ffn_up=l1_ffn_up, l1_ffn_conv_w=l1_ffn_conv_w, l1_ffn_conv_b=l1_ffn_conv_b, l1_ffn_down=l1_ffn_down, final_norm=final_norm)
    MO = dict(c_ctx=m_c_ctx, l0_ada_w=m_l0_ada_w, l0_ada_b=m_l0_ada_b, l0_norm1=m_l0_norm1, l0_w_in=m_l0_w_in, l0_conv_a=m_l0_conv_a, l0_q_norm=m_l0_q_norm, l0_w_qb=m_l0_w_qb, l0_kv_norm=m_l0_kv_norm, l0_w_kvb=m_l0_w_kvb, l0_w_out=m_l0_w_out, l0_norm2=m_l0_norm2, l0_ffn_up=m_l0_ffn_up, l0_ffn_conv_w=m_l0_ffn_conv_w, l0_ffn_conv_b=m_l0_ffn_conv_b, l0_ffn_down=m_l0_ffn_down, l1_ada_w=m_l1_ada_w, l1_ada_b=m_l1_ada_b, l1_norm1=m_l1_norm1, l1_w_in=m_l1_w_in, l1_gate_fw_w=m_l1_gate_fw_w, l1_gate_fw_b=m_l1_gate_fw_b, l1_gate_bw_w=m_l1_gate_bw_w, l1_gate_bw_b=m_l1_gate_bw_b, l1_o_norm=m_l1_o_norm, l1_w_out=m_l1_w_out, l1_norm2=m_l1_norm2, l1_ffn_up=m_l1_ffn_up, l1_ffn_conv_w=m_l1_ffn_conv_w, l1_ffn_conv_b=m_l1_ffn_conv_b, l1_ffn_down=m_l1_ffn_down, final_norm=m_final_norm)
    VO = dict(c_ctx=v_c_ctx, l0_ada_w=v_l0_ada_w, l0_ada_b=v_l0_ada_b, l0_norm1=v_l0_norm1, l0_w_in=v_l0_w_in, l0_conv_a=v_l0_conv_a, l0_q_norm=v_l0_q_norm, l0_w_qb=v_l0_w_qb, l0_kv_norm=v_l0_kv_norm, l0_w_kvb=v_l0_w_kvb, l0_w_out=v_l0_w_out, l0_norm2=v_l0_norm2, l0_ffn_up=v_l0_ffn_up, l0_ffn_conv_w=v_l0_ffn_conv_w, l0_ffn_conv_b=v_l0_ffn_conv_b, l0_ffn_down=v_l0_ffn_down, l1_ada_w=v_l1_ada_w, l1_ada_b=v_l1_ada_b, l1_norm1=v_l1_norm1, l1_w_in=v_l1_w_in, l1_gate_fw_w=v_l1_gate_fw_w, l1_gate_fw_b=v_l1_gate_fw_b, l1_gate_bw_w=v_l1_gate_bw_w, l1_gate_bw_b=v_l1_gate_bw_b, l1_o_norm=v_l1_o_norm, l1_w_out=v_l1_w_out, l1_norm2=v_l1_norm2, l1_ffn_up=v_l1_ffn_up, l1_ffn_conv_w=v_l1_ffn_conv_w, l1_ffn_conv_b=v_l1_ffn_conv_b, l1_ffn_down=v_l1_ffn_down, final_norm=v_final_norm)
    names = list(W)

    T, D = x.shape[1], x.shape[2]
    TC = ctx.shape[1]
    assert TC == TR and T % TR == 0
    TA = T + TC
    nt, nlat = TA // TR, T // TR
    CC = D // 2
    F = l0_ffn_down.shape[0] * 4
    DK, DV = D // 2 // GLA_HEADS, D // GLA_HEADS
    KEY, VAL = GLA_HEADS * DK, GLA_HEADS * DV
    ADA_S = l0_ada_w.shape[1]
    H = MLA_HEADS
    my_x, my_y, my_c = lax.axis_index("x"), lax.axis_index("y"), lax.axis_index("c")
    chip, dev = 2 * my_x + my_y, 4 * my_x + 2 * my_y + my_c
    row = lambda a: a.reshape(1, -1)

    big = ["l0_w_in", "l0_w_qb", "l0_w_kvb", "l0_w_out", "l0_ffn_up", "l0_ffn_down",
           "l1_w_in", "l1_w_out", "l1_ffn_up", "l1_ffn_down"]
    small_sh = ["l0_conv_a", "l0_ffn_conv_w", "l1_gate_fw_w", "l1_gate_bw_w", "l1_ffn_conv_w"]
    own_half = lambda a, k: lax.dynamic_index_in_dim(a, k, 0, keepdims=False)
    set_own = lambda arr, val, k: lax.dynamic_update_index_in_dim(arr, val[None], k, 0)
    in_core_order = lambda mine, other: jnp.where(my_c == 0, jnp.stack([mine, other], 1), jnp.stack([other, mine], 1))
    early, late = big[:3], big[3:]
    w_half = {n: own_half(W[n].astype(BF16).reshape(2, W[n].shape[0] // 2, -1), my_c) for n in big}
    col_sh = {"l0_w_in", "l0_w_qb", "l0_w_kvb", "l0_ffn_up", "l1_w_in", "l1_ffn_up", "l0_conv_a", "l0_ffn_conv_w",
              "l1_gate_fw_w", "l1_gate_bw_w", "l1_ffn_conv_w"}
    whole_w = lambda n, s: _shards_to_cols(s) if n in col_sh else s.reshape(-1, s.shape[-1])

    def pair_up(tag, group, got):
        got = [set_own(g, w_half[n], chip) for n, g in zip(group, got)]
        other = _exchange("pair_weights_" + tag, got, "swap_c")
        return {n: whole_w(n, in_core_order(g, o).reshape(4, 2 * g.shape[1], g.shape[2])) for n, g, o in zip(group, got, other)}

    full = pair_up("a", early, _exchange("gather_weights", [w_half[n] for n in early], "ag_xy", fill_own=False))
    for n, s in zip(small_sh, _exchange("gather_small_weights", [W[n] for n in small_sh], "ag_xy")):
        full[n] = whole_w(n, s)
    (c_all,) = _exchange("gather_c", [c], "ag_all")

    o_kv, o_kr, o_ql, o_ax = 0, KV_LORA, KV_LORA + QK_ROPE, KV_LORA + QK_ROPE + Q_LORA
    n_sel = 3 * CC + Q_LORA + KV_LORA + QK_ROPE
    W0 = -(-(n_sel + QK_ROPE) // 256) * 256
    QOFF0, KVOFF0, KROFF0 = 3 * CC, 3 * CC + Q_LORA, 3 * CC + Q_LORA + KV_LORA
    w_in0 = full["l0_w_in"]
    w_kr = w_in0[:, o_kr:o_kr + QK_ROPE]
    w_in0p = jnp.concatenate([w_in0[:, o_ax:], w_in0[:, o_ql:o_ax], w_in0[:, :o_kr], w_kr, _rot_cols(w_kr),
                              jnp.zeros((D, W0 - n_sel - QK_ROPE), BF16)], axis=1)
    wq3 = full["l0_w_qb"].reshape(Q_LORA, H, QK_NOPE + QK_ROPE)
    w_qbp = jnp.concatenate([wq3, _rot_cols(wq3[:, :, QK_NOPE:])], axis=2).reshape(Q_LORA, H * QK_PAD)
    o_v1, o_lr1, o_q1, o_og1 = KEY, KEY + VAL, KEY + VAL + 2 * GATE_RANK, 2 * KEY + VAL + 2 * GATE_RANK
    W1 = 2 * VAL + 2 * KEY + LANES
    OGOFF, KOFF1, QOFF1, LROFF = VAL, 2 * VAL, 2 * VAL + KEY, 2 * VAL + 2 * KEY
    w_gate = jnp.zeros((LANES, 2 * KEY), F32)
    w_gate = w_gate.at[:GATE_RANK, :KEY].set(full["l1_gate_fw_w"]).at[GATE_RANK:2 * GATE_RANK, KEY:].set(full["l1_gate_bw_w"])
    b_gate = jnp.concatenate([l1_gate_fw_b, l1_gate_bw_b]).reshape(1, -1)

    pos = np.arange(T)
    inv = ROPE_THETA ** (-np.arange(0, QK_ROPE // 2, 2, dtype=np.float32) / (QK_ROPE // 2))
    ar, ac = (pos // GRID_W).astype(np.float32)[:, None] * inv, (pos % GRID_W).astype(np.float32)[:, None] * inv
    ang = jnp.asarray(np.concatenate([ar, ar, ac, ac], axis=-1).astype(np.float32))
    zpad = jnp.zeros((TA, LANES - QK_ROPE), F32)
    cos_t = jnp.concatenate([jnp.concatenate([jnp.cos(ang), jnp.ones((TC, QK_ROPE), F32)], axis=0), zpad], axis=1)
    sin_t = jnp.concatenate([jnp.concatenate([jnp.sin(ang), jnp.zeros((TC, QK_ROPE), F32)], axis=0), zpad], axis=1)

    c16 = jnp.concatenate([c_all.reshape(8, D), c_ctx.reshape(1, D), jnp.zeros((7, D), F32)], axis=0)
    (act16,) = _rowwise("silu_c", _f_silu, 1, 16, [("row", c16, 0, D)], [("row", 16, D, F32)])
    mod_parts = [_matmul("ada_fwd", act16, W["l%d_ada_w" % l], "nn", 16, ADA_S, D, tn_cap=512) for l in (0, 1)]
    mod_g = _exchange("gather_mod", mod_parts, "ag_xy")
    mods = []
    for l in (0, 1):
        mfull = _shards_to_cols(mod_g[l]) + W["l%d_ada_b" % l][None, :]
        mine = lax.dynamic_slice_in_dim(mfull, dev, 1, axis=0)
        mods.append(jnp.concatenate([mine, mfull[8:9]], axis=0).reshape(2, N_MOD, D))
    P = lambda l, k: mods[l][:, k:k + 1, :]

    X = jnp.concatenate([x[0], ctx[0]], axis=0)
    n1_0, n2_0, n1_1, n2_1 = row(l0_norm1), row(l0_norm2), row(l1_norm1), row(l1_norm2)
    (h0,) = _rowwise("l0_mod1", _f_mod, nt, TR, [("row", X, 0, D), ("full", n1_0), ("grp", P(0, 0)), ("grp", P(0, 1))],
                     [("row", TA, D, BF16)], nlat)
    z0 = _matmul("l0_in", h0, w_in0p, "nn", TA, W0, D)
    kvn_w, qn_w = row(l0_kv_norm), row(l0_q_norm)
    norm_ins = [("row", z0, KVOFF0 // KV_LORA, KV_LORA), ("row", z0, QOFF0 // Q_LORA, Q_LORA), ("full", kvn_w), ("full", qn_w)]
    kvn, qn = _rowwise("l0_latnorm", _f_norms, nt, TR, norm_ins, [("row", TA, KV_LORA, BF16), ("row", TA, Q_LORA, BF16)])
    kv = _matmul("l0_kvb", kvn, full["l0_w_kvb"], "nn", TA, H * QK_PAD, KV_LORA, out_dtype=BF16)
    qraw = _matmul("l0_qb", qn, w_qbp, "nn", TA, H * QK_PAD, Q_LORA)

    def f_rope(i, qr, kvv, krr, cs, sn):
        kr = _rope128(krr, cs, sn).astype(BF16)
        qs, ks = [], []
        for h in range(H):
            qs += [qr[:, h * QK_PAD:h * QK_PAD + QK_NOPE], _rope128(qr[:, h * QK_PAD + QK_NOPE:(h + 1) * QK_PAD], cs, sn)]
            ks += [kvv[:, h * QK_PAD:h * QK_PAD + QK_NOPE], kr]
        return jnp.concatenate(qs, axis=1) * MLA_QSCALE, jnp.concatenate(ks, axis=1)

    rope_ins = [("row", qraw, 0, H * QK_PAD), ("row", kv, 0, H * QK_PAD), ("row", z0, KROFF0 // LANES, LANES),
                ("row", cos_t, 0, LANES), ("row", sin_t, 0, LANES)]
    q, kf = _rowwise("l0_rope", f_rope, nt, TR, rope_ins, [("row", TA, H * QK_PAD, BF16), ("row", TA, H * QK_PAD, BF16)])
    o_lat, lse, got_late = _mla_fwd_lat(q, kf, kv, T, TA, _Hosted([w_half[n] for n in late], "ag_xy"))
    o_ctx = _mla_fwd_ctx(q, kf, kv, T)
    full.update(pair_up("b", late, got_late))
    w_in1 = full["l1_w_in"]
    w_in1p = jnp.concatenate([w_in1[:, o_v1:o_lr1], w_in1[:, o_og1:], w_in1[:, :o_v1], w_in1[:, o_q1:o_og1],
                              w_in1[:, o_lr1:o_q1], jnp.zeros((D, LANES - 2 * GATE_RANK), BF16)], axis=1)

    conv_a = full["l0_conv_a"]

    def f_conva(i, ax, ab, ac_, pax, pac, nax, nac, w):
        prev, nxt = _shifters(i, TR, T, TA)
        p = ac_ * ax
        return (ab * (w[0:1] * prev(p, pac * pax) + w[1:2] * p + w[2:3] * nxt(p, nac * nax)),)

    conva_ins = [("row", z0, 0, CC), ("row", z0, 1, CC), ("row", z0, 2, CC), ("prev", z0, 0, CC), ("prev", z0, 2, CC),
                 ("next", z0, 0, CC), ("next", z0, 2, CC), ("full", conv_a)]
    (y_a,) = _rowwise("l0_conva", f_conva, nt, TR, conva_ins, [("row", TA, CC, BF16)])
    cat0 = jnp.concatenate([y_a, jnp.concatenate([o_lat, o_ctx], axis=0)], axis=1)
    y0 = _matmul("l0_out", cat0, full["l0_w_out"], "nn", TA, D, CC + H * V_HEAD)
    resmod_outs = [("row", TA, D, F32), ("row", TA, D, BF16)]
    X1, h2_0 = _rowwise("l0_resmod2", _f_resmod, nt, TR,
                        [("row", X, 0, D), ("row", y0, 0, D), ("grp", P(0, 2)), ("full", n2_0), ("grp", P(0, 3)), ("grp", P(0, 4))],
                        resmod_outs, nlat)
    cw0, cb0 = full["l0_ffn_conv_w"], row(l0_ffn_conv_b)
    u0, act0, f0 = _ffn_fwd("l0_ffn", h2_0, full["l0_ffn_up"], cw0, cb0, full["l0_ffn_down"], TA, T, TA, F)
    X2, h1 = _rowwise("l1_resmod1", _f_resmod, nt, TR,
                      [("row", X1, 0, D), ("row", f0, 0, D), ("grp", P(0, 5)), ("full", n1_1), ("grp", P(1, 0)), ("grp", P(1, 1))],
                      resmod_outs, nlat)

    z1 = _matmul("l1_in", h1, w_in1p, "nn", TA, W1, D)

    def f_gates(i, lr, wg, bg):
        pre = _bdot(lr, wg, NN_DIMS) + bg
        return (_log_sigmoid(pre) / GATE_NORMALIZER,)

    gate_ins = [("row", z1, LROFF // LANES, LANES), ("full", w_gate), ("full", b_gate)]
    (gates,) = _rowwise("l1_gates", f_gates, nt, TR, gate_ins, [("row", TA, 2 * KEY, F32)])
    gla_dims = (DK, DV, QOFF1, KOFF1)
    o_f, o_b, st_f, st_b = _gla_fwd(z1, gates, gla_dims, nt, nlat)
    onw = row(l1_o_norm)

    def f_glaout(i, of, ob, og, w):
        o = of + ob
        on = jnp.concatenate([_rms(o[:, h * DV:(h + 1) * DV], w) for h in range(GLA_HEADS)], axis=1)
        return (on * _silu(og),)

    glaout_ins = [("row", o_f, 0, VAL), ("row", o_b, 0, VAL), ("row", z1, OGOFF // VAL, VAL), ("full", onw)]
    (go,) = _rowwise("l1_glaout", f_glaout, nlat, TR, glaout_ins, [("row", T, VAL, BF16)])
    y1 = _matmul("l1_out", go, full["l1_w_out"], "nn", T, D, VAL)
    X3, h2_1 = _rowwise("l1_resmod2", _f_resmod, nlat, TR,
                        [("row", X2, 0, D), ("row", y1, 0, D), ("grp", P(1, 2)), ("full", n2_1), ("grp", P(1, 3)), ("grp", P(1, 4))],
                        [("row", T, D, F32), ("row", T, D, BF16)])
    cw1, cb1 = full["l1_ffn_conv_w"], row(l1_ffn_conv_b)
    u1, act1, f1 = _ffn_fwd("l1_ffn", h2_1, full["l1_ffn_up"], cw1, cb1, full["l1_ffn_down"], T, T, T, F)

    fnw = row(final_norm)

    def f_head(i, x3, ff, g5, w, tgt):
        fin = lambda a, b, g_, w_: _rms(a + g_ * b, w_)
        y, pull = jax.vjp(fin, x3, ff, g5, w)
        err = y - tgt
        dx3, dff, dg5, dw = pull(err / D)
        loss = 0.5 * jnp.sum(jnp.mean(err * err, axis=-1, keepdims=True), axis=0, keepdims=True)
        return jnp.broadcast_to(loss, (1, LANES)), dx3, dff, dg5, dw

    head_ins = [("row", X3, 0, D), ("row", f1, 0, D), ("grp", P(1, 5)), ("full", fnw), ("row", loss_target[0], 0, D)]
    loss_acc, dX3, df1, dm5_1, g_final = _rowwise(
        "loss_head", f_head, nlat, TR, head_ins,
        [("acc", 1, LANES), ("row", T, D, F32), ("row", T, D, BF16), ("acc", 1, D), ("acc", 1, D)])
    loss = lax.psum(loss_acc[0, 0], ("x", "y", "c"))

    grads = {"final_norm": g_final}
    dh2_1, grads["l1_ffn_up"], grads["l1_ffn_conv_w"], grads["l1_ffn_conv_b"], grads["l1_ffn_down"] = _ffn_bwd(
        "l1_dffn", df1, h2_1, u1, act1, full["l1_ffn_up"], cw1, cb1, full["l1_ffn_down"], T, T, T, F)
    resmod_bwd = _vjp_of(_f_resmod, 6, (0, 1, 2, 3, 4, 5))
    dX2l, dy1, dm2_1, grads["l1_norm2"], dm3_1, dm4_1 = _rowwise(
        "l1_dresmod2", resmod_bwd, nlat, TR,
        [("row", X2, 0, D), ("row", y1, 0, D), ("grp", P(1, 2)), ("full", n2_1), ("grp", P(1, 3)), ("grp", P(1, 4)),
         ("row", dX3, 0, D), ("row", dh2_1, 0, D)],
        [("row", T, D, F32), ("row", T, D, BF16)] + [("acc", 1, D)] * 4)
    dgo = _matmul("l1_dout_x", dy1, full["l1_w_out"], "nt", T, VAL, D)
    grads["l1_w_out"] = _matmul("l1_dout_w", go, dy1, "tn", VAL, D, T, out_dtype=BF16)

    def f_glaout_bwd(i, of, ob, og, w, d):
        is_ctx = i >= nlat
        _, pull = jax.vjp(lambda a, b, c_, w_: f_glaout(i, a, b, c_, w_)[0], of, ob, og, w)
        dof, _, dog, dw = pull(jnp.where(is_ctx, 0.0, d))
        return dof, dog, dw

    glaout_b_ins = glaout_ins + [("rowc", dgo, 0, VAL, nlat - 1)]
    do_gla, dog, g_onorm = _rowwise("l1_dglaout", f_glaout_bwd, nt, TR, glaout_b_ins,
                                    [("row", TA, VAL, F32), ("row", TA, VAL, F32), ("acc", 1, DV)])
    grads["l1_o_norm"] = g_onorm
    (dq_f, dk_f, dv_f, dg_f), (dq_b, dk_b, dv_b, dg_b) = _gla_bwd(z1, gates, st_f, st_b, do_gla, gla_dims, nt, nlat)

    def f_dz1(i, dvf, dvb, dog_, dkf, dkb, dqf, dqb, dgf, dgb, lr, wg, bg):
        _, pull = jax.vjp(lambda a, b, c_: f_gates(i, a, b, c_)[0], lr, wg, bg)
        dlr, dwg, dbg = pull(jnp.concatenate([dgf, dgb], axis=1))
        return jnp.concatenate([dvf + dvb, dog_, dkf + dkb, dqf + dqb, dlr], axis=1), dwg, dbg

    dz1_ins = [("row", dv_f, 0, VAL), ("row", dv_b, 0, VAL), ("row", dog, 0, VAL), ("row", dk_f, 0, KEY), ("row", dk_b, 0, KEY),
               ("row", dq_f, 0, KEY), ("row", dq_b, 0, KEY), ("row", dg_f, 0, KEY), ("row", dg_b, 0, KEY)] + gate_ins
    dz1, g_wgate, g_bgate = _rowwise("l1_dz", f_dz1, nt, TR, dz1_ins, [("row", TA, W1, BF16), ("acc", LANES, 2 * KEY), ("acc", 1, 2 * KEY)])
    grads["l1_gate_fw_w"], grads["l1_gate_bw_w"] = g_wgate[:GATE_RANK, :KEY], g_wgate[GATE_RANK:2 * GATE_RANK, KEY:]
    grads["l1_gate_fw_b"], grads["l1_gate_bw_b"] = g_bgate[:, :KEY], g_bgate[:, KEY:]
    dh1 = _matmul("l1_din_x", dz1, w_in1p, "nt", TA, D, W1)
    g1p = _matmul("l1_din_w", h1, dz1, "tn", D, W1, TA, out_dtype=BF16)
    grads["l1_w_in"] = jnp.concatenate([g1p[:, KOFF1:QOFF1], g1p[:, :OGOFF], g1p[:, LROFF:LROFF + 2 * GATE_RANK],
                                        g1p[:, QOFF1:LROFF], g1p[:, OGOFF:KOFF1]], axis=1)

    def f_resmod1_bwd(i, x_, y_, g_, w_, sh, sc, dx2, dh):
        return resmod_bwd(i, x_, y_, g_, w_, sh, sc, jnp.where(i >= nlat, 0.0, dx2), dh)

    dX1, df0, dm5_0, grads["l1_norm1"], dm0_1, dm1_1 = _rowwise(
        "l1_dresmod1", f_resmod1_bwd, nt, TR,
        [("row", X1, 0, D), ("row", f0, 0, D), ("grp", P(0, 5)), ("full", n1_1), ("grp", P(1, 0)), ("grp", P(1, 1)),
         ("rowc", dX2l, 0, D, nlat - 1), ("row", dh1, 0, D)],
        [("row", TA, D, F32), ("row", TA, D, BF16), ("accg", 1, D), ("acc", 1, D), ("accg", 1, D), ("accg", 1, D)], nlat)

    dh2_0, grads["l0_ffn_up"], grads["l0_ffn_conv_w"], grads["l0_ffn_conv_b"], grads["l0_ffn_down"] = _ffn_bwd(
        "l0_dffn", df0, h2_0, u0, act0, full["l0_ffn_up"], cw0, cb0, full["l0_ffn_down"], TA, T, TA, F)
    dXd, dy0, dm2_0, grads["l0_norm2"], dm3_0, dm4_0 = _rowwise(
        "l0_dresmod2", resmod_bwd, nt, TR,
        [("row", X, 0, D), ("row", y0, 0, D), ("grp", P(0, 2)), ("full", n2_0), ("grp", P(0, 3)), ("grp", P(0, 4)),
         ("row", dX1, 0, D), ("row", dh2_0, 0, D)],
        [("row", TA, D, F32), ("row", TA, D, BF16), ("accg", 1, D), ("acc", 1, D), ("accg", 1, D), ("accg", 1, D)], nlat)
    dcat = _matmul("l0_dout_x", dy0, full["l0_w_out"], "nt", TA, CC + H * V_HEAD, D)
    grads["l0_w_out"] = _matmul("l0_dout_w", cat0, dy0, "tn", CC + H * V_HEAD, D, TA, out_dtype=BF16)

    def f_conva_bwd(i, ax, ab, ac_, pax, pab, pac, nax, nab, nac, dy, pdy, ndy, w):
        prev, nxt = _shifters(i, TR, T, TA)
        p = ac_ * ax
        pp, pn = prev(p, pac * pax), nxt(p, nac * nax)
        cv = w[0:1] * pp + w[1:2] * p + w[2:3] * pn
        dcv = dy * ab
        dp = w[0:1] * nxt(dcv, ndy * nab) + w[1:2] * dcv + w[2:3] * prev(dcv, pdy * pab)
        return (jnp.concatenate([dp * ac_, dy * cv, dp * ax], axis=1), _colsum(pp * dcv), _colsum(p * dcv), _colsum(pn * dcv))

    conva_b_ins = [("row", z0, 0, CC), ("row", z0, 1, CC), ("row", z0, 2, CC),
                   ("prev", z0, 0, CC), ("prev", z0, 1, CC), ("prev", z0, 2, CC),
                   ("next", z0, 0, CC), ("next", z0, 1, CC), ("next", z0, 2, CC),
                   ("row", dcat, 0, CC), ("prev", dcat, 0, CC), ("next", dcat, 0, CC), ("full", conv_a)]
    dz_a, ga0, ga1, ga2 = _rowwise("l0_dconva", f_conva_bwd, nt, TR, conva_b_ins, [("row", TA, 3 * CC, BF16)] + [("acc", 1, CC)] * 3)
    grads["l0_conv_a"] = jnp.concatenate([ga0, ga1, ga2], axis=0)
    do_cb0 = CC // V_HEAD
    row_sh = {"l0_w_out", "l1_w_out", "l0_ffn_down", "l1_ffn_down"}

    def pair_sums(tag, group):
        res = []
        for n in group:
            s = grads[n].reshape((4, -1) + grads[n].shape[1:]) if n in row_sh else _cols_to_shards(grads[n])
            res.append(s.reshape(4, 2, s.shape[1] // 2, s.shape[2]).transpose(1, 0, 2, 3))
        taken = _exchange("pair_split_grads_" + tag, [own_half(g, 1 - my_c) for g in res], "swap_c")
        return [_sum_pair("pairsum_" + n[3:], own_half(g, my_c), t, BF16) for n, g, t in zip(group, res, taken)]

    pair_late = pair_sums("a", late)
    dq_lat, dk_lat, dv_lat, got = _mla_bwd_lat(q, kf, kv, o_lat, dcat, lse, do_cb0, T, TA, _Hosted(pair_late, "a2a_xy"))
    landed_late = {n: set_own(l, own_half(p, chip), chip) for n, l, p in zip(late, got, pair_late)}
    dq_ctx, dk_ctx, dv_ctx = _mla_bwd_ctx(q, kf, kv, dcat, dk_lat, dv_lat, do_cb0, T)

    def f_rope_bwd(i, dql, dqc, dkl, dkc, dvl, dvc, cs, sn):
        is_ctx = i >= nlat
        dq_, dk_, dv_ = jnp.where(is_ctx, dqc, dql) * MLA_SCALE, jnp.where(is_ctx, dkc, dkl), jnp.where(is_ctx, dvc, dvl)
        dqs, dkvs, dkr = [], [], None
        for h in range(H):
            dqs += [dq_[:, h * QK_PAD:h * QK_PAD + QK_NOPE], _rope128_t(dq_[:, h * QK_PAD + QK_NOPE:(h + 1) * QK_PAD], cs, sn)]
            dkvs += [dk_[:, h * QK_PAD:h * QK_PAD + QK_NOPE].astype(BF16), dv_[:, h * V_HEAD:(h + 1) * V_HEAD]]
            part = dk_[:, h * QK_PAD + QK_NOPE:(h + 1) * QK_PAD]
            dkr = part if dkr is None else dkr + part
        return jnp.concatenate(dqs, axis=1), jnp.concatenate(dkvs, axis=1), _rope128_t(dkr, cs, sn)

    drope_ins = [("rowc", dq_lat, 0, H * QK_PAD, nlat - 1), ("full", dq_ctx), ("row", dk_lat, 0, H * QK_PAD), ("full", dk_ctx),
                 ("row", dv_lat, 0, H * V_HEAD), ("full", dv_ctx), ("row", cos_t, 0, LANES), ("row", sin_t, 0, LANES)]
    dqraw, dkv, dz_kr = _rowwise("l0_drope", f_rope_bwd, nt, TR, drope_ins,
                                 [("row", TA, H * QK_PAD, BF16), ("row", TA, H * QK_PAD, BF16), ("row", TA, LANES, BF16)])
    dqn = _matmul("l0_dqb_x", dqraw, w_qbp, "nt", TA, Q_LORA, H * QK_PAD)
    g_wqbp = _matmul("l0_dqb_w", qn, dqraw, "tn", Q_LORA, H * QK_PAD, TA).reshape(Q_LORA, H, QK_PAD)
    g_rope = g_wqbp[:, :, QK_NOPE:QK_NOPE + QK_ROPE] + _rot_cols_t(g_wqbp[:, :, QK_NOPE + QK_ROPE:])
    grads["l0_w_qb"] = jnp.concatenate([g_wqbp[:, :, :QK_NOPE], g_rope], axis=2).reshape(Q_LORA, H * (QK_NOPE + QK_ROPE)).astype(BF16)
    dkvn = _matmul("l0_dkvb_x", dkv, full["l0_w_kvb"], "nt", TA, KV_LORA, H * QK_PAD)
    grads["l0_w_kvb"] = _matmul("l0_dkvb_w", kvn, dkv, "tn", KV_LORA, H * QK_PAD, TA, out_dtype=BF16)
    norms_bwd = _vjp_of(_f_norms, 4, (0, 1, 2, 3))
    dz_kv, dz_q, grads["l0_kv_norm"], grads["l0_q_norm"] = _rowwise(
        "l0_dlatnorm", norms_bwd, nt, TR, norm_ins + [("row", dkvn, 0, KV_LORA), ("row", dqn, 0, Q_LORA)],
        [("row", TA, KV_LORA, BF16), ("row", TA, Q_LORA, BF16), ("acc", 1, KV_LORA), ("acc", 1, Q_LORA)])
    dz0 = jnp.concatenate([dz_a, dz_q, dz_kv, dz_kr, jnp.zeros((TA, W0 - KROFF0 - LANES), BF16)], axis=1)
    dh0 = _matmul("l0_din_x", dz0, w_in0p, "nt", TA, D, W0)
    g0p = _matmul("l0_din_w", h0, dz0, "tn", D, W0, TA)
    g_kr = g0p[:, KROFF0:KROFF0 + QK_ROPE] + _rot_cols_t(g0p[:, KROFF0 + QK_ROPE:KROFF0 + 2 * QK_ROPE])
    grads["l0_w_in"] = jnp.concatenate([g0p[:, KVOFF0:KROFF0], g_kr, g0p[:, QOFF0:KVOFF0], g0p[:, :QOFF0]], axis=1).astype(BF16)

    def f_mod_bwd(i, x_, w_, sh, sc, dh, dxd):
        _, pull = jax.vjp(lambda a, b, c_, d_: _modulate(a, b, c_, d_), x_, w_, sh, sc)
        dx, dw, dsh, dsc = pull(dh)
        return dx + dxd, dw, dsh, dsc

    dXf, grads["l0_norm1"], dm0_0, dm1_0 = _rowwise(
        "l0_dmod1", f_mod_bwd, nt, TR,
        [("row", X, 0, D), ("full", n1_0), ("grp", P(0, 0)), ("grp", P(0, 1)), ("row", dh0, 0, D), ("row", dXd, 0, D)],
        [("row", TA, D, F32), ("acc", 1, D), ("accg", 1, D), ("accg", 1, D)], nlat)
    grad_x = dXf[:T][None]

    zD = jnp.zeros((1, D), F32)
    lat = lambda a: a[0] if a.ndim == 3 else a
    cxt = lambda a: a[1] if a.ndim == 3 else zD
    dmods = []
    for parts in ((dm0_0, dm1_0, dm2_0, dm3_0, dm4_0, dm5_0), (dm0_1, dm1_1, dm2_1, dm3_1, dm4_1, dm5_1)):
        dmods.append((jnp.concatenate([lat(a) for a in parts], axis=1), jnp.concatenate([cxt(a) for a in parts], axis=1)))
    small_names = ["l0_norm1", "l0_norm2", "l0_kv_norm", "l0_q_norm", "l0_conv_a", "l0_ffn_conv_w", "l0_ffn_conv_b",
                   "l1_norm1", "l1_norm2", "l1_o_norm", "l1_gate_fw_w", "l1_gate_fw_b", "l1_gate_bw_w", "l1_gate_bw_b",
                   "l1_ffn_conv_w", "l1_ffn_conv_b", "final_norm"]
    pieces = [("dm0", dmods[0][0]), ("dmc0", dmods[0][1]), ("dm1", dmods[1][0]), ("dmc1", dmods[1][1])]
    pieces += [("l0_ada_b", dmods[0][0] + dmods[0][1]), ("l1_ada_b", dmods[1][0] + dmods[1][1])]
    pieces += [(n, grads[n]) for n in small_names]
    offs, cur = {}, 0
    for n, a in pieces:
        offs[n] = (cur, a.size, a.shape)
        cur += -(-a.size // LANES) * LANES
    n_pad = -(-cur // 1024) * 1024
    flat = jnp.concatenate([jnp.pad(a.reshape(-1), (0, -a.size % LANES)) for _, a in pieces] + [jnp.zeros((n_pad - cur,), F32)])
    (small_g,) = _exchange("gather_small", [flat.reshape(n_pad // LANES, LANES)], "ag_all")
    small_sum = _sum_slots("sum_small", small_g).reshape(-1)
    small_all = small_g.reshape(8, -1)
    take = lambda n: small_sum[offs[n][0]:offs[n][0] + offs[n][1]].reshape(offs[n][2])

    cc_parts = []
    for l in (0, 1):
        o_m, sz, _ = offs["dm%d" % l]
        d16 = jnp.concatenate([small_all[:, o_m:o_m + sz], take("dmc%d" % l).reshape(1, -1), jnp.zeros((7, sz), F32)], axis=0)
        d16 = lax.dynamic_slice_in_dim(d16.reshape(16, 4, ADA_S), chip, 1, axis=1).reshape(16, ADA_S)
        grads["l%d_ada_w" % l] = _matmul("ada_dw", act16, d16, "tn", D, ADA_S, 16, tn_cap=512)
        cc_parts.append(_matmul("ada_dx", d16, W["l%d_ada_w" % l], "nt", 16, D, ADA_S, tn_cap=512))
        grads["l%d_ada_b" % l] = take("l%d_ada_b" % l).reshape(-1)
    cc_g = _exchange("gather_cc", cc_parts, "ag_xy")

    def cc_call():
        def body(a_ref, b_ref, c_ref, o_ref):
            tot = a_ref[0, 8:9, :] + b_ref[0, 8:9, :]
            for j in range(1, 4):
                tot = tot + (a_ref[j, 8:9, :] + b_ref[j, 8:9, :])
            _, pull = jax.vjp(_silu, c_ref[...])
            o_ref[...] = pull(tot)[0]
        return pl.pallas_call(body, name="c_ctx_grad", out_shape=jax.ShapeDtypeStruct((1, D), F32))(cc_g[0], cc_g[1], row(c_ctx))

    grads["c_ctx"] = cc_call().reshape(-1)

    pair_early = pair_sums("b", early)
    landed = dict(landed_late)
    for n, l, p in zip(early, _exchange("scatter_grads", pair_early, "a2a_xy", fill_own=False), pair_early):
        landed[n] = set_own(l, own_half(p, chip), chip)
    halves = [_sum_slots("sum_" + n[3:], landed[n]) for n in big]
    others = _exchange("pair_grads", halves, "swap_c")

    out_g, out_d, out_m, out_v = {}, {}, {}, {}
    for n, mine, other in zip(big, halves, others):
        g2 = jnp.where(my_c == 0, jnp.concatenate([mine, other], 0), jnp.concatenate([other, mine], 0))
        out_g[n], out_d[n], out_m[n], out_v[n] = _adam("adam_" + n[3:], W[n], [g2], MO[n], VO[n])
    for l in (0, 1):
        n = "l%d_ada_w" % l
        out_g[n], out_d[n], out_m[n], out_v[n] = _adam("adam_ada_w", W[n], [grads[n]], MO[n], VO[n])
    rest = [n for n in names if n not in out_g]
    g_rest = {}
    for n in rest:
        if n == "c_ctx" or n.endswith("ada_b"):
            g_rest[n] = grads[n]
        elif n in small_sh:
            gfull = take(n)
            cs = gfull.shape[1] // 4
            g_rest[n] = lax.dynamic_slice_in_dim(gfull, chip * cs, cs, axis=1)
        else:
            g_rest[n] = take(n).reshape(W[n].shape)
    sizes = [W[n].size for n in rest]
    tot = sum(-(-s // LANES) * LANES for s in sizes)
    tot_pad = -(-tot // 1024) * 1024

    def pack(d):
        parts = [jnp.pad(d[n].reshape(-1), (0, -d[n].size % LANES)) for n in rest]
        return jnp.concatenate(parts + [jnp.zeros((tot_pad - tot,), F32)]).reshape(tot_pad // LANES, LANES)

    packed = _adam("adam_small", pack(W), [pack(g_rest)], pack(MO), pack(VO))
    cur = 0
    for n, s in zip(rest, sizes):
        for dst, arr in zip((out_g, out_d, out_m, out_v), packed):
            dst[n] = arr.reshape(-1)[cur:cur + s].reshape(W[n].shape)
        cur += -(-s // LANES) * LANES

    return (loss, grad_x, *[out_g[n] for n in names], *[out_d[n] for n in names],
            *[out_m[n] for n in names], *[out_v[n] for n in names])
```

```python
import numpy as np

import jax
import jax.numpy as jnp
from jax import lax
from jax.experimental import pallas as pl
from jax.experimental.pallas import tpu as pltpu

F32, BF16 = jnp.float32, jnp.bfloat16
MESH_ID = pl.DeviceIdType.MESH

EPS = 1e-6
N_MOD = 6
MLA_HEADS, QK_NOPE, QK_ROPE, V_HEAD, Q_LORA, KV_LORA = 8, 128, 64, 128, 512, 256
QK_PAD = 2 * QK_NOPE
ROPE_THETA, GRID_W = 10000.0, 64
GLA_HEADS, GATE_RANK, GATE_NORMALIZER, CHUNK = 4, 16, 16.0, 64
ADAM_LR, ADAM_B1, ADAM_B2, ADAM_EPS, ADAM_WD, ADAM_STEP = 0.001, 0.9, 0.999, 1e-08, 0.01, 10

LANES = 128
TR = 256
V7X_VMEM_BYTES = 64 * 2 ** 20
VMEM_LIMIT = V7X_VMEM_BYTES - 8 * 2 ** 20

NT_DIMS = (((1,), (1,)), ((), ()))
TN_DIMS = (((0,), (0,)), ((), ()))
NN_DIMS = (((1,), (0,)), ((), ()))

def _rot_cols(w):
    q = QK_ROPE // 4
    return jnp.concatenate([-w[..., q:2 * q], w[..., :q], -w[..., 3 * q:], w[..., 2 * q:3 * q]], axis=-1)


def _rot_cols_t(g):
    q = QK_ROPE // 4
    return jnp.concatenate([g[..., q:2 * q], -g[..., :q], g[..., 3 * q:], -g[..., 2 * q:3 * q]], axis=-1)


def _cparams(sem=None):
    return pltpu.CompilerParams(dimension_semantics=sem, vmem_limit_bytes=VMEM_LIMIT)


def _tile(dim, cap, quantum=LANES):
    if dim <= cap:
        return dim
    t = (cap // quantum) * quantum
    while t >= quantum:
        if dim % t == 0:
            return t
        t -= quantum
    return dim


_REL_XY = ((1, 0, 0), (0, 1, 0), (1, 1, 0))
_REL_ALL = tuple((a, b, c) for a in (0, 1) for b in (0, 1) for c in (0, 1))[1:]
_REL_C = ((0, 0, 1),)


_RELS = {"ag_xy": _REL_XY, "a2a_xy": _REL_XY, "ag_all": _REL_ALL, "swap_c": _REL_C}
_LEAD = {"ag_xy": (4,), "ag_all": (8,), "a2a_xy": (), "swap_c": ()}


def _remote_copies(mode, ins, outs, send, recv):
    x, y, c = lax.axis_index("x"), lax.axis_index("y"), lax.axis_index("c")
    chip, dev = 2 * x + y, 4 * x + 2 * y + c
    rels, plan = _RELS[mode], []
    for a, (i_ref, o) in enumerate(zip(ins, outs)):
        for r, (bx, by, bc) in enumerate(rels):
            px = 1 - x if bx else x
            py = 1 - y if by else y
            pc = 1 - c if bc else c
            pchip, pdev = 2 * px + py, 4 * px + 2 * py + pc
            if mode == "ag_xy":
                src, dst, mine = i_ref, o.at[chip], o.at[pchip]
            elif mode == "ag_all":
                src, dst, mine = i_ref, o.at[dev], o.at[pdev]
            elif mode == "a2a_xy":
                src, dst, mine = i_ref.at[pchip], o.at[chip], o.at[pchip]
            else:
                src, dst, mine = i_ref, o, o
            k = a * len(rels) + r
            mk = lambda d: pltpu.make_async_remote_copy(src_ref=src, dst_ref=d, send_sem=send.at[k], recv_sem=recv.at[k],
                                                        device_id=(px, py, pc), device_id_type=MESH_ID)
            plan.append((mk(dst), mk(mine)))
    return plan


def _exchange(name, arrs, mode, fill_own=True):
    n, nr = len(arrs), len(_RELS[mode])
    out_shape = tuple(jax.ShapeDtypeStruct(_LEAD[mode] + a.shape, a.dtype) for a in arrs)

    def body(*refs):
        ins, outs = refs[:n], refs[n:2 * n]
        send, recv, loc = refs[2 * n:]
        chip = 2 * lax.axis_index("x") + lax.axis_index("y")
        local = []
        if fill_own and mode != "swap_c":
            slot = 2 * chip + lax.axis_index("c") if mode == "ag_all" else chip
            for a in range(n):
                lc = pltpu.make_async_copy(ins[a].at[chip] if mode == "a2a_xy" else ins[a], outs[a].at[slot], loc.at[a])
                lc.start()
                local.append(lc)
        plan = _remote_copies(mode, ins, outs, send, recv)
        for cp, _ in plan:
            cp.start()
        for cp, landing in plan:
            cp.wait_send()
            landing.wait_recv()
        for lc in local:
            lc.wait()

    hbm = pl.BlockSpec(memory_space=pl.ANY)
    res = pl.pallas_call(
        body, name=name, out_shape=out_shape, in_specs=[hbm] * n, out_specs=tuple([hbm] * n),
        scratch_shapes=[pltpu.SemaphoreType.DMA((n * nr,)), pltpu.SemaphoreType.DMA((n * nr,)),
                        pltpu.SemaphoreType.DMA((max(n, 1),))],
    )(*arrs)
    return list(res)


class _Hosted:
    def __init__(self, arrs, mode):
        self.arrs, self.mode, self.n = list(arrs), mode, len(arrs)
        self.out_shape = [jax.ShapeDtypeStruct(_LEAD[mode] + a.shape, a.dtype) for a in arrs]
        self.specs = [pl.BlockSpec(memory_space=pl.ANY)] * self.n
        nsem = self.n * len(_RELS[mode])
        self.scratch = [pltpu.SemaphoreType.DMA((nsem,)), pltpu.SemaphoreType.DMA((nsem,))]

    def run(self, first, last, ins, outs, sems):
        @pl.when(first)
        def _():
            for cp, _ in _remote_copies(self.mode, ins, outs, *sems):
                cp.start()

        @pl.when(last)
        def _():
            for cp, landing in _remote_copies(self.mode, ins, outs, *sems):
                cp.wait_send()
                landing.wait_recv()


def _matmul(name, a, b, mode, M, N, K, out_dtype=F32, a_off=(0, 0), b_off=(0, 0), tm_cap=1024, tn_cap=1024, tk_cap=2816):
    tk = K if K <= 4096 else _tile(K, tk_cap)
    nk = K // tk
    if nk == 1 and K <= 2048 and mode != "tn":
        tm_cap = max(tm_cap, 1408)
    tm, tn = _tile(M, tm_cap, LANES if M % LANES == 0 else 8), _tile(N, tn_cap)
    assert M % tm == 0 and N % tn == 0 and K % tk == 0, (name, M, N, K, tm, tn, tk)
    if mode == "nn":
        ab, bb, dims = (tm, tk), (tk, tn), NN_DIMS
        ai = lambda i, j, k: (i + a_off[0] // tm, k + a_off[1] // tk)
        bi = lambda i, j, k: (k + b_off[0] // tk, j + b_off[1] // tn)
        chk = (a_off[0] % tm, a_off[1] % tk, b_off[0] % tk, b_off[1] % tn)
    elif mode == "nt":
        ab, bb, dims = (tm, tk), (tn, tk), NT_DIMS
        ai = lambda i, j, k: (i + a_off[0] // tm, k + a_off[1] // tk)
        bi = lambda i, j, k: (j + b_off[0] // tn, k + b_off[1] // tk)
        chk = (a_off[0] % tm, a_off[1] % tk, b_off[0] % tn, b_off[1] % tk)
    else:
        ab, bb, dims = (tk, tm), (tk, tn), TN_DIMS
        ai = lambda i, j, k: (k + a_off[0] // tk, i + a_off[1] // tm)
        bi = lambda i, j, k: (k + b_off[0] // tk, j + b_off[1] // tn)
        chk = (a_off[0] % tk, a_off[1] % tm, b_off[0] % tk, b_off[1] % tn)
    assert not any(chk), (name, chk)

    own_acc = nk > 1 and out_dtype != F32

    def body(a_ref, b_ref, o_ref, *acc):
        dot = lambda: lax.dot_general(a_ref[...].astype(BF16), b_ref[...].astype(BF16), dims, preferred_element_type=F32)
        if nk == 1:
            o_ref[...] = dot().astype(out_dtype)
            return
        k, dst = pl.program_id(2), (acc[0] if own_acc else o_ref)

        @pl.when(k == 0)
        def _():
            dst[...] = jnp.zeros_like(dst)

        dst[...] += dot()
        if own_acc:
            @pl.when(k == nk - 1)
            def _():
                o_ref[...] = acc[0][...].astype(out_dtype)

    return pl.pallas_call(
        body, name=name, out_shape=jax.ShapeDtypeStruct((M, N), out_dtype), grid=(M // tm, N // tn, nk),
        in_specs=[pl.BlockSpec(ab, ai), pl.BlockSpec(bb, bi)], out_specs=pl.BlockSpec((tm, tn), lambda i, j, k: (i, j)),
        scratch_shapes=[pltpu.VMEM((tm, tn), F32)] if own_acc else [],
        compiler_params=_cparams(("parallel", "parallel", "arbitrary")),
    )(a, b)


def _rowwise(name, fn, grid_n, tr, ins, outs, nlat=None):
    nlat = grid_n if nlat is None else nlat
    grp = lambda i: jnp.minimum(i // nlat, 1)
    in_specs, args = [], []
    for spec in ins:
        kind, arr = spec[0], spec[1]
        if kind == "row":
            in_specs.append(pl.BlockSpec((tr, spec[3]), lambda i, cb=spec[2]: (i, cb)))
        elif kind == "rowc":
            in_specs.append(pl.BlockSpec((tr, spec[3]), lambda i, cb=spec[2], mx=spec[4]: (jnp.minimum(i, mx), cb)))
        elif kind == "prev":
            in_specs.append(pl.BlockSpec((8, spec[3]), lambda i, cb=spec[2]: (jnp.maximum(i * (tr // 8) - 1, 0), cb)))
        elif kind == "next":
            nb = arr.shape[0] // 8
            in_specs.append(pl.BlockSpec((8, spec[3]), lambda i, cb=spec[2], nb=nb: (jnp.minimum((i + 1) * (tr // 8), nb - 1), cb)))
        elif kind == "grp":
            in_specs.append(pl.BlockSpec((None,) + arr.shape[1:], lambda i: (grp(i), 0, 0)))
        else:
            in_specs.append(pl.BlockSpec(arr.shape, lambda i: (0, 0)))
        args.append(arr)
    out_shape, out_specs = [], []
    for spec in outs:
        if spec[0] == "row":
            out_shape.append(jax.ShapeDtypeStruct((spec[1], spec[2]), spec[3]))
            out_specs.append(pl.BlockSpec((tr, spec[2]), lambda i: (i, 0)))
        elif spec[0] == "acc":
            out_shape.append(jax.ShapeDtypeStruct((spec[1], spec[2]), F32))
            out_specs.append(pl.BlockSpec((spec[1], spec[2]), lambda i: (0, 0)))
        else:
            out_shape.append(jax.ShapeDtypeStruct((2, spec[1], spec[2]), F32))
            out_specs.append(pl.BlockSpec((None, spec[1], spec[2]), lambda i: (grp(i), 0, 0)))
    n_in = len(ins)
    has_acc = any(s[0] != "row" for s in outs)

    def body(*refs):
        i = pl.program_id(0)
        res = fn(i, *[r[...] for r in refs[:n_in]])
        for spec, ref, val in zip(outs, refs[n_in:], res):
            if spec[0] == "row":
                ref[...] = val.astype(ref.dtype)
            else:
                first = (i == 0) if spec[0] == "acc" else jnp.logical_or(i == 0, i == nlat)

                @pl.when(first)
                def _(ref=ref, val=val):
                    ref[...] = val

                @pl.when(jnp.logical_not(first))
                def _(ref=ref, val=val):
                    ref[...] += val

    return pl.pallas_call(
        body, name=name, out_shape=tuple(out_shape), grid=(grid_n,), in_specs=in_specs, out_specs=tuple(out_specs),
        compiler_params=_cparams(("arbitrary",) if has_acc else ("parallel",)),
    )(*args)


def _vjp_of(fwd, n_in, wrt):
    def bwd(i, *args):
        _, pull = jax.vjp(lambda *a: fwd(i, *a), *args[:n_in])
        g = pull(tuple(args[n_in:]))
        return tuple(g[k] for k in wrt)
    return bwd


def _rms(x, w):
    return x * lax.rsqrt(jnp.mean(x * x, axis=-1, keepdims=True) + EPS) * w


def _modulate(x, w, shift, scale):
    return _rms(x, w) * (1.0 + scale) + shift


def _silu(x):
    return x * jax.nn.sigmoid(x)


def _log_sigmoid(x):
    return jnp.minimum(x, 0.0) - jnp.log(1.0 + jnp.exp(-jnp.abs(x)))


def _f_mod(i, x, w, sh, sc):
    return (_modulate(x, w, sh, sc),)


def _f_resmod(i, x, y, gate, w, sh, sc):
    x1 = x + gate * y
    return x1, _modulate(x1, w, sh, sc)


def _f_norms(i, kvl, ql, kvw, qw):
    return _rms(kvl, kvw), _rms(ql, qw)


def _f_silu(i, x):
    return (_silu(x),)


def _rope128(x2, cos, sin):
    return x2 * cos + pltpu.roll(x2, QK_ROPE, 1) * sin


def _rope128_t(d2, cos, sin):
    return d2 * cos + pltpu.roll(d2 * sin, QK_ROPE, 1)


def _shifters(i, tr, T, TA):
    assert T % tr == 0 and TA % tr == 0
    loc = lax.broadcasted_iota(jnp.int32, (tr, 1), 0)
    starts_seq = jnp.logical_or(i * tr == 0, i * tr == T)
    ends_seq = jnp.logical_or((i + 1) * tr == T, (i + 1) * tr == TA)

    def prev(x, halo):
        return jnp.where(loc == 0, jnp.where(starts_seq, 0.0, halo[7:8, :]), pltpu.roll(x, 1, 0))

    def nxt(x, halo):
        return jnp.where(loc == tr - 1, jnp.where(ends_seq, 0.0, halo[0:1, :]), pltpu.roll(x, tr - 1, 0))

    return prev, nxt


def _colsum(x):
    return jnp.sum(x, axis=0, keepdims=True)


MLA_SCALE = (QK_NOPE + QK_ROPE) ** -0.5
LOG2E, LN2 = 1.4426950408889634, 0.6931471805599453
MLA_QSCALE = MLA_SCALE * LOG2E
MLA_SUB = 256
MLA_KV_TILE = 2816


def _mla_fwd_lat(q, kf, kv, T, TA, hosted):
    H = MLA_HEADS
    tq, tk = _tile(T, 1024, TR), _tile(TA, MLA_KV_TILE)
    nq, nk = T // tq, TA // tk
    nh = hosted.n

    def body(q_ref, k_ref, v_ref, *rest):
        o_ref, lse_ref = rest[nh:nh + 2]
        m_sc, l_sc, acc_sc = rest[2 * nh + 2:2 * nh + 5]
        hi, qi, ki = pl.program_id(0), pl.program_id(1), pl.program_id(2)
        hosted.run(jnp.logical_and(hi == 0, jnp.logical_and(qi == 0, ki == 0)),
                   jnp.logical_and(hi == H - 1, jnp.logical_and(qi == nq - 1, ki == nk - 1)),
                   rest[:nh], rest[nh + 2:2 * nh + 2], rest[2 * nh + 5:])

        @pl.when(ki == 0)
        def _():
            m_sc[...] = jnp.full_like(m_sc, -jnp.inf)
            l_sc[...] = jnp.zeros_like(l_sc)
            acc_sc[...] = jnp.zeros_like(acc_sc)

        sb = min(tq, MLA_SUB)
        scores = lambda r: lax.dot_general(q_ref[r * sb:(r + 1) * sb, :], k_ref[...], NT_DIMS, preferred_element_type=F32)
        s_next = scores(0)
        for r in range(tq // sb):
            s, rows = s_next, slice(r * sb, (r + 1) * sb)
            if r + 1 < tq // sb:
                s_next = scores(r + 1)
            m_old = m_sc[rows, :]
            m_new = jnp.maximum(m_old, jnp.max(s, axis=-1, keepdims=True))
            alpha = jnp.exp2(m_old - m_new)
            p = jnp.exp2(s - m_new)
            l_sc[rows, :] = alpha * l_sc[rows, :] + jnp.sum(p, axis=-1, keepdims=True)
            acc_sc[rows, :] = alpha * acc_sc[rows, :] + jnp.dot(p.astype(BF16), v_ref[...], preferred_element_type=F32)
            m_sc[rows, :] = m_new

        @pl.when(ki == nk - 1)
        def _():
            o_ref[...] = (acc_sc[...] / l_sc[...]).astype(o_ref.dtype)
            lse_ref[...] = m_sc[...] + jnp.log2(l_sc[...])

    res = pl.pallas_call(
        body, name="mla_fwd_lat", grid=(H, nq, nk),
        out_shape=[jax.ShapeDtypeStruct((T, H * V_HEAD), BF16), jax.ShapeDtypeStruct((H, T, 1), F32)] + hosted.out_shape,
        in_specs=[pl.BlockSpec((tq, QK_PAD), lambda h, i, k: (i, h)), pl.BlockSpec((tk, QK_PAD), lambda h, i, k: (k, h)),
                  pl.BlockSpec((tk, V_HEAD), lambda h, i, k: (k, 2 * h + 1))] + hosted.specs,
        out_specs=[pl.BlockSpec((tq, V_HEAD), lambda h, i, k: (i, h)),
                   pl.BlockSpec((None, tq, 1), lambda h, i, k: (h, i, 0))] + hosted.specs,
        scratch_shapes=[pltpu.VMEM((tq, 1), F32), pltpu.VMEM((tq, 1), F32), pltpu.VMEM((tq, V_HEAD), F32)] + hosted.scratch,
        compiler_params=_cparams(("arbitrary", "arbitrary", "arbitrary")),
    )(q, kf, kv, *hosted.arrs)
    return res[0], res[1], list(res[2:])


def _mla_ctx_probs(q_ref, k_ref):
    s = lax.dot_general(q_ref[...], k_ref[...], NT_DIMS, preferred_element_type=F32)
    p = jnp.exp2(s - jnp.max(s, axis=-1, keepdims=True))
    return p, jnp.sum(p, axis=-1, keepdims=True)


def _mla_fwd_ctx(q, kf, kv, T):
    H, cb = MLA_HEADS, T // TR

    def body(q_ref, k_ref, v_ref, o_ref):
        p, l = _mla_ctx_probs(q_ref, k_ref)
        o_ref[...] = (jnp.dot(p.astype(BF16), v_ref[...], preferred_element_type=F32) / l).astype(o_ref.dtype)

    return pl.pallas_call(
        body, name="mla_fwd_ctx", grid=(H,), out_shape=jax.ShapeDtypeStruct((TR, H * V_HEAD), BF16),
        in_specs=[pl.BlockSpec((TR, QK_PAD), lambda h: (cb, h)), pl.BlockSpec((TR, QK_PAD), lambda h: (cb, h)),
                  pl.BlockSpec((TR, V_HEAD), lambda h: (cb, 2 * h + 1))],
        out_specs=pl.BlockSpec((TR, V_HEAD), lambda h: (0, h)), compiler_params=_cparams(("parallel",)),
    )(q, kf, kv)


def _mla_grads(p, q_ref, k_ref, v_ref, do, delta):
    dob = do.astype(BF16)
    dp = lax.dot_general(dob, v_ref[...], NT_DIMS, preferred_element_type=F32)
    ds = (p * (dp - delta)).astype(BF16)
    return (jnp.dot(ds, k_ref[...], preferred_element_type=F32),
            lax.dot_general(ds, q_ref[...], TN_DIMS, preferred_element_type=F32) * LN2,
            lax.dot_general(p.astype(BF16), dob, TN_DIMS, preferred_element_type=F32))


def _mla_bwd_lat(q, kf, kv, o, dcat, lse, do_cb0, T, TA, hosted):
    H = MLA_HEADS
    tq, tk = _tile(T, 1024, TR), _tile(TA, MLA_KV_TILE)
    nq, nk = T // tq, TA // tk
    nh = hosted.n

    def body(q_ref, k_ref, v_ref, o_ref, do_ref, lse_ref, *rest):
        dq_ref, dk_ref, dv_ref = rest[nh:nh + 3]
        dk_acc, dv_acc = rest[2 * nh + 3:2 * nh + 5]
        hi, ki, qi = pl.program_id(0), pl.program_id(1), pl.program_id(2)
        hosted.run(jnp.logical_and(hi == 0, jnp.logical_and(qi == 0, ki == 0)),
                   jnp.logical_and(hi == H - 1, jnp.logical_and(qi == nq - 1, ki == nk - 1)),
                   rest[:nh], rest[nh + 3:2 * nh + 3], rest[2 * nh + 5:])

        @pl.when(jnp.logical_and(ki == 0, qi == 0))
        def _():
            dq_ref[...] = jnp.zeros_like(dq_ref)

        @pl.when(qi == 0)
        def _():
            dk_acc[...] = jnp.zeros_like(dk_acc)
            dv_acc[...] = jnp.zeros_like(dv_acc)

        sb = min(tq, MLA_SUB)

        def products(r):
            rows = slice(r * sb, (r + 1) * sb)
            dob = do_ref[rows, :].astype(BF16)
            return (lax.dot_general(q_ref[rows, :], k_ref[...], NT_DIMS, preferred_element_type=F32),
                    lax.dot_general(dob, v_ref[...], NT_DIMS, preferred_element_type=F32), dob)

        nxt, dk, dv = products(0), None, None
        for r in range(tq // sb):
            (s, dp, dob), rows = nxt, slice(r * sb, (r + 1) * sb)
            if r + 1 < tq // sb:
                nxt = products(r + 1)
            delta = jnp.sum(do_ref[rows, :] * o_ref[rows, :].astype(F32), axis=-1, keepdims=True)
            p = jnp.exp2(s - lse_ref[rows, :])
            ds = (p * (dp - delta)).astype(BF16)
            dq_ref[pl.ds(pl.multiple_of(qi * tq + r * sb, sb), sb), :] += jnp.dot(ds, k_ref[...], preferred_element_type=F32)
            dk_r = lax.dot_general(ds, q_ref[rows, :], TN_DIMS, preferred_element_type=F32)
            dv_r = lax.dot_general(p.astype(BF16), dob, TN_DIMS, preferred_element_type=F32)
            dk, dv = (dk_r, dv_r) if dk is None else (dk + dk_r, dv + dv_r)
        dk_acc[...] += dk
        dv_acc[...] += dv

        @pl.when(qi == nq - 1)
        def _():
            dk_ref[...] = dk_acc[...] * LN2
            dv_ref[...] = dv_acc[...].astype(dv_ref.dtype)

    res = pl.pallas_call(
        body, name="mla_bwd_lat", grid=(H, nk, nq),
        out_shape=[jax.ShapeDtypeStruct((T, H * QK_PAD), F32), jax.ShapeDtypeStruct((TA, H * QK_PAD), F32),
                   jax.ShapeDtypeStruct((TA, H * V_HEAD), BF16)] + hosted.out_shape,
        in_specs=[pl.BlockSpec((tq, QK_PAD), lambda h, k, i: (i, h)), pl.BlockSpec((tk, QK_PAD), lambda h, k, i: (k, h)),
                  pl.BlockSpec((tk, V_HEAD), lambda h, k, i: (k, 2 * h + 1)), pl.BlockSpec((tq, V_HEAD), lambda h, k, i: (i, h)),
                  pl.BlockSpec((tq, V_HEAD), lambda h, k, i: (i, do_cb0 + h)),
                  pl.BlockSpec((None, tq, 1), lambda h, k, i: (h, i, 0))] + hosted.specs,
        out_specs=[pl.BlockSpec((T, QK_PAD), lambda h, k, i: (0, h)), pl.BlockSpec((tk, QK_PAD), lambda h, k, i: (k, h)),
                   pl.BlockSpec((tk, V_HEAD), lambda h, k, i: (k, h))] + hosted.specs,
        scratch_shapes=[pltpu.VMEM((tk, QK_PAD), F32), pltpu.VMEM((tk, V_HEAD), F32)] + hosted.scratch,
        compiler_params=_cparams(("arbitrary", "arbitrary", "arbitrary")),
    )(q, kf, kv, o, dcat, lse, *hosted.arrs)
    return res[0], res[1], res[2], list(res[3:])


def _mla_bwd_ctx(q, kf, kv, dcat, dk_lat, dv_lat, do_cb0, T):
    H, cb = MLA_HEADS, T // TR

    def body(q_ref, k_ref, v_ref, do_ref, dkl_ref, dvl_ref, dq_ref, dk_ref, dv_ref):
        p, l = _mla_ctx_probs(q_ref, k_ref)
        do = do_ref[...]
        o = jnp.dot(p.astype(BF16), v_ref[...], preferred_element_type=F32) / l
        dq, dk, dv = _mla_grads(p / l, q_ref, k_ref, v_ref, do, jnp.sum(do * o, axis=-1, keepdims=True))
        dq_ref[...] = dq
        dk_ref[...] = dkl_ref[...] + dk
        dv_ref[...] = (dvl_ref[...].astype(F32) + dv).astype(dv_ref.dtype)

    at_ctx = lambda w, f: pl.BlockSpec((TR, w), lambda h: (cb, f(h)))
    at_0 = lambda w: pl.BlockSpec((TR, w), lambda h: (0, h))
    return pl.pallas_call(
        body, name="mla_bwd_ctx", grid=(H,),
        out_shape=(jax.ShapeDtypeStruct((TR, H * QK_PAD), F32), jax.ShapeDtypeStruct((TR, H * QK_PAD), F32),
                   jax.ShapeDtypeStruct((TR, H * V_HEAD), BF16)),
        in_specs=[at_ctx(QK_PAD, lambda h: h), at_ctx(QK_PAD, lambda h: h), at_ctx(V_HEAD, lambda h: 2 * h + 1),
                  at_ctx(V_HEAD, lambda h: do_cb0 + h), at_ctx(QK_PAD, lambda h: h), at_ctx(V_HEAD, lambda h: h)],
        out_specs=(at_0(QK_PAD), at_0(QK_PAD), at_0(V_HEAD)), compiler_params=_cparams(("parallel",)),
    )(q, kf, kv, dcat, dk_lat, dv_lat)


def _bdot_impl(a, b, dims):
    return lax.dot_general(a.astype(BF16), b.astype(BF16), dims, preferred_element_type=F32)


def _bdot(a, b, dims):
    @jax.custom_vjp
    def f(a_, b_):
        return _bdot_impl(a_, b_, dims)

    def fwd(a_, b_):
        return _bdot_impl(a_, b_, dims), (a_.astype(BF16), b_.astype(BF16))

    def bwd(res, ct):
        a_, b_ = res
        if dims == NN_DIMS:
            return _bdot_impl(ct, b_, NT_DIMS), _bdot_impl(a_, ct, TN_DIMS)
        if dims == NT_DIMS:
            return _bdot_impl(ct, b_, NN_DIMS), _bdot_impl(ct, a_, TN_DIMS)
        return _bdot_impl(b_, ct, NT_DIMS), _bdot_impl(a_, ct, NN_DIMS)

    f.defvjp(fwd, bwd)
    return f(a, b)


def _ones_dot_impl(ones, x, dims):
    hi = x.astype(BF16)
    r1 = x - hi.astype(F32)
    mid = r1.astype(BF16)
    lo = (r1 - mid.astype(F32)).astype(BF16)
    d = lambda t: lax.dot_general(ones, t, dims, preferred_element_type=F32)
    return (d(lo) + d(mid)) + d(hi)


@jax.custom_vjp
def _ones_dot(ones, x):
    return _ones_dot_impl(ones, x, NN_DIMS)


_ones_dot.defvjp(lambda ones, x: (_ones_dot_impl(ones, x, NN_DIMS), ones),
                 lambda ones, ct: (jnp.zeros_like(ones), _ones_dot_impl(ones, ct, TN_DIMS)))


def _gla_tile(st, q, k, v, g, rev, q_scale):
    nc = q.shape[0] // CHUNK
    ii = lax.broadcasted_iota(jnp.int32, (CHUNK, CHUNK), 0)
    jj = lax.broadcasted_iota(jnp.int32, (CHUNK, CHUNK), 1)
    tri = (jj >= ii) if rev else (jj <= ii)
    ones = tri.astype(BF16)
    outs = [None] * nc
    for ci in (range(nc - 1, -1, -1) if rev else range(nc)):
        sl = slice(ci * CHUNK, (ci + 1) * CHUNK)
        qc, kc, vc, gc = q[sl] * q_scale, k[sl], v[sl], g[sl]
        b = _ones_dot(ones, gc)
        bl = b[0:1] if rev else b[CHUNK - 1:CHUNK]
        kd, qe, ke = kc * jnp.exp(bl - b), qc * jnp.exp(b), kc * jnp.exp(-b)
        att = jnp.where(tri, _bdot(qe, ke, NT_DIMS), 0.0)
        outs[ci] = _bdot(att, vc, NN_DIMS) + _bdot(qe, st, NT_DIMS)
        st = st * jnp.exp(bl) + _bdot(vc, kd, TN_DIMS)
    return st, jnp.concatenate(outs, axis=0)


def _gla_tiles(nt, nlat):
    return (lambda p: (p + nlat) % nt), (lambda p: nt - 1 - p)


def _gla_fwd(z1, g, dims, nt, nlat):
    DK, DV, q_off, k_off = dims
    KEY, TA = GLA_HEADS * DK, nt * TR
    in_specs, out_o = [], []
    for d, tile in enumerate(_gla_tiles(nt, nlat)):
        in_specs += [pl.BlockSpec((TR, DK), lambda h, p, t=tile: (t(p), q_off // DK + h)),
                     pl.BlockSpec((TR, DK), lambda h, p, t=tile: (t(p), k_off // DK + h)),
                     pl.BlockSpec((TR, DV), lambda h, p, t=tile: (t(p), h)),
                     pl.BlockSpec((TR, DK), lambda h, p, t=tile, d=d: (t(p), d * GLA_HEADS + h))]
        out_o.append(pl.BlockSpec((TR, DV), lambda h, p, t=tile: (t(p), h)))
    st_spec = pl.BlockSpec((None, None, DV, DK), lambda h, p: (h, p, 0, 0))

    def body(qf, kf, vf, gf, qb, kb, vb, gb, of_ref, ob_ref, sf_ref, sb_ref, st_f, st_b):
        @pl.when(pl.program_id(1) == 0)
        def _():
            st_f[...] = jnp.zeros_like(st_f)
            st_b[...] = jnp.zeros_like(st_b)

        sf_ref[...] = st_f[...]
        sb_ref[...] = st_b[...]
        new_f, out_f = _gla_tile(st_f[...], qf[...], kf[...], vf[...], gf[...], False, DK ** -0.5)
        new_b, out_b = _gla_tile(st_b[...], qb[...], kb[...], vb[...], gb[...], True, DK ** -0.5)
        st_f[...] = new_f
        st_b[...] = new_b
        of_ref[...] = out_f
        ob_ref[...] = out_b

    o_shape = jax.ShapeDtypeStruct((TA, GLA_HEADS * DV), F32)
    s_shape = jax.ShapeDtypeStruct((GLA_HEADS, nt, DV, DK), F32)
    return pl.pallas_call(
        body, name="gla_fwd", grid=(GLA_HEADS, nt), out_shape=(o_shape, o_shape, s_shape, s_shape),
        in_specs=in_specs, out_specs=(out_o[0], out_o[1], st_spec, st_spec),
        scratch_shapes=[pltpu.VMEM((DV, DK), F32)] * 2, compiler_params=_cparams(("parallel", "arbitrary")),
    )(z1, z1, z1, g, z1, z1, z1, g)


def _gla_bwd(z1, g, st_f, st_b, do, dims, nt, nlat):
    DK, DV, q_off, k_off = dims
    KEY, VAL, TA = GLA_HEADS * DK, GLA_HEADS * DV, nt * TR
    pos = lambda s: nt - 1 - s
    in_specs, out_specs = [], []
    for d, tile in enumerate(_gla_tiles(nt, nlat)):
        at = lambda w, f, t=tile: pl.BlockSpec((TR, w), lambda h, s: (t(pos(s)), f(h)))
        in_specs += [at(DK, lambda h: q_off // DK + h), at(DK, lambda h: k_off // DK + h), at(DV, lambda h: h),
                     at(DK, lambda h, d=d: d * GLA_HEADS + h),
                     pl.BlockSpec((None, None, DV, DK), lambda h, s: (h, pos(s), 0, 0)), at(DV, lambda h: h)]
        out_specs += [at(DK, lambda h: h), at(DK, lambda h: h), at(DV, lambda h: h), at(DK, lambda h: h)]

    def one(refs, outs, dst, rev):
        q_ref, k_ref, v_ref, g_ref, st_ref, do_ref = refs
        _, pull = jax.vjp(lambda st, q, k, v, gg: _gla_tile(st, q, k, v, gg, rev, DK ** -0.5),
                          st_ref[...], q_ref[...], k_ref[...], v_ref[...], g_ref[...])
        grads = pull((dst[...], do_ref[...]))
        dst[...] = grads[0]
        for o_ref, val in zip(outs, grads[1:]):
            o_ref[...] = val

    def body(*refs):
        dst_f, dst_b = refs[20:]

        @pl.when(pl.program_id(1) == 0)
        def _():
            dst_f[...] = jnp.zeros_like(dst_f)
            dst_b[...] = jnp.zeros_like(dst_b)

        one(refs[0:6], refs[12:16], dst_f, False)
        one(refs[6:12], refs[16:20], dst_b, True)

    shapes = [jax.ShapeDtypeStruct((TA, KEY), F32), jax.ShapeDtypeStruct((TA, KEY), F32),
              jax.ShapeDtypeStruct((TA, VAL), F32), jax.ShapeDtypeStruct((TA, KEY), F32)]
    res = pl.pallas_call(
        body, name="gla_bwd", grid=(GLA_HEADS, nt), out_shape=shapes * 2, in_specs=in_specs, out_specs=out_specs,
        scratch_shapes=[pltpu.VMEM((DV, DK), F32)] * 2, compiler_params=_cparams(("parallel", "arbitrary")),
    )(z1, z1, z1, g, st_f, do, z1, z1, z1, g, st_b, do)
    return res[:4], res[4:]


def _sum_slots(name, arr):
    S, R, C = arr.shape
    tr = _tile(R, max(16, (2 ** 19) // max(C, 1) // 16 * 16), 16)

    def body(a_ref, o_ref):
        acc = a_ref[0].astype(F32)
        for s in range(1, S):
            acc = acc + a_ref[s].astype(F32)
        o_ref[...] = acc

    return pl.pallas_call(
        body, name=name, grid=(R // tr,), out_shape=jax.ShapeDtypeStruct((R, C), F32),
        in_specs=[pl.BlockSpec((S, tr, C), lambda i: (0, i, 0))], out_specs=pl.BlockSpec((tr, C), lambda i: (i, 0)),
        compiler_params=_cparams(("parallel",)),
    )(arr)


def _sum_pair(name, a, b, out_dtype):
    shape, C = a.shape, a.shape[-1]
    a2, b2 = a.reshape(-1, C), b.reshape(-1, C)
    R = a2.shape[0]
    tr = _tile(R, max(16, (2 ** 19) // max(C, 1) // 16 * 16), 16)
    fn = lambda i, p, q: (p.astype(F32) + q.astype(F32),)
    (res,) = _rowwise(name, fn, R // tr, tr, [("row", a2, 0, C), ("row", b2, 0, C)], [("row", R, C, out_dtype)])
    return res.reshape(shape)


def _adam_math(w, g, m, v):
    m2 = ADAM_B1 * m + (1.0 - ADAM_B1) * g
    v2 = ADAM_B2 * v + (1.0 - ADAM_B2) * (g * g)
    m_hat = m2 / (1.0 - ADAM_B1 ** ADAM_STEP)
    v_hat = v2 / (1.0 - ADAM_B2 ** ADAM_STEP)
    return -ADAM_LR * (m_hat / (jnp.sqrt(v_hat) + ADAM_EPS) + ADAM_WD * w), m2, v2


def _adam(name, w, gs, m, v):
    R, C = w.shape
    tr = _tile(R, max(8, (2 ** 19) // max(C, 1) // 8 * 8), 8)
    ng = len(gs)

    def fn(i, w_, *rest):
        g = rest[0] if ng == 1 else rest[0] + rest[1]
        d, m2, v2 = _adam_math(w_, g, rest[ng], rest[ng + 1])
        return g, d, m2, v2

    ins = [("row", a, 0, C) for a in (w, *gs, m, v)]
    return _rowwise(name, fn, R // tr, tr, ins, [("row", R, C, F32)] * 4)


def _ffn_fwd(tag, h, w_up, cw, cb, w_down, n_rows, T, TA, F):
    tr = 128
    u = _matmul(tag + "_up", h, w_up, "nn", n_rows, 2 * F, h.shape[1])
    cwg, cwv, cbg, cbv = cw[:, :F], cw[:, F:], cb[:, :F], cb[:, F:]

    def fn(i, ug, uv, pg, pv, ng, nv, wg, wv, bg, bv):
        prev, nxt = _shifters(i, tr, T, TA)
        cg = wg[0:1] * prev(ug, pg) + wg[1:2] * ug + wg[2:3] * nxt(ug, ng) + bg
        cv = wv[0:1] * prev(uv, pv) + wv[1:2] * uv + wv[2:3] * nxt(uv, nv) + bv
        return (_silu(cg) * cv,)

    ins = [("row", u, 0, F), ("row", u, 1, F), ("prev", u, 0, F), ("prev", u, 1, F), ("next", u, 0, F), ("next", u, 1, F),
           ("full", cwg), ("full", cwv), ("full", cbg), ("full", cbv)]
    (act,) = _rowwise(tag + "_conv", fn, n_rows // tr, tr, ins, [("row", n_rows, F, BF16)])
    f = _matmul(tag + "_down", act, w_down, "nn", n_rows, w_down.shape[1], F)
    return u, act, f


def _ffn_bwd(tag, df, h, u, act, w_up, cw, cb, w_down, n_rows, T, TA, F):
    tr = 128
    D = w_down.shape[1]
    dact = _matmul(tag + "_ddown_x", df, w_down, "nt", n_rows, F, D, tm_cap=1408)
    g_down = _matmul(tag + "_ddown_w", act, df, "tn", F, D, n_rows, out_dtype=BF16)
    cwg, cwv, cbg, cbv = cw[:, :F], cw[:, F:], cb[:, :F], cb[:, F:]

    def fn_a(i, ug, uv, pg, pv, ng, nv, da, wg, wv, bg, bv):
        prev, nxt = _shifters(i, tr, T, TA)
        ugp, ugn, uvp, uvn = prev(ug, pg), nxt(ug, ng), prev(uv, pv), nxt(uv, nv)
        cg = wg[0:1] * ugp + wg[1:2] * ug + wg[2:3] * ugn + bg
        cv = wv[0:1] * uvp + wv[1:2] * uv + wv[2:3] * uvn + bv
        sg = jax.nn.sigmoid(cg)
        dcv = da * (cg * sg)
        dcg = da * cv * (sg * (1.0 + cg * (1.0 - sg)))
        return (jnp.concatenate([dcg, dcv], axis=1),
                _colsum(ugp * dcg), _colsum(ug * dcg), _colsum(ugn * dcg),
                _colsum(uvp * dcv), _colsum(uv * dcv), _colsum(uvn * dcv), _colsum(dcg), _colsum(dcv))

    ins = [("row", u, 0, F), ("row", u, 1, F), ("prev", u, 0, F), ("prev", u, 1, F), ("next", u, 0, F), ("next", u, 1, F),
           ("row", dact, 0, F), ("full", cwg), ("full", cwv), ("full", cbg), ("full", cbv)]
    res = _rowwise(tag + "_dconv_a", fn_a, n_rows // tr, tr, ins, [("row", n_rows, 2 * F, F32)] + [("acc", 1, F)] * 8)
    duc = res[0]
    g_cw = jnp.concatenate([jnp.concatenate(res[1:4], axis=0), jnp.concatenate(res[4:7], axis=0)], axis=1)
    g_cb = jnp.concatenate([res[7], res[8]], axis=1)

    def fn_b(i, dg, dv, pg, pv, ng, nv, wg, wv):
        prev, nxt = _shifters(i, tr, T, TA)
        dug = wg[0:1] * nxt(dg, ng) + wg[1:2] * dg + wg[2:3] * prev(dg, pg)
        duv = wv[0:1] * nxt(dv, nv) + wv[1:2] * dv + wv[2:3] * prev(dv, pv)
        return (jnp.concatenate([dug, duv], axis=1),)

    ins = [("row", duc, 0, F), ("row", duc, 1, F), ("prev", duc, 0, F), ("prev", duc, 1, F), ("next", duc, 0, F),
           ("next", duc, 1, F), ("full", cwg), ("full", cwv)]
    (du,) = _rowwise(tag + "_dconv_b", fn_b, n_rows // tr, tr, ins, [("row", n_rows, 2 * F, BF16)])
    dh = _matmul(tag + "_dup_x", du, w_up, "nt", n_rows, D, 2 * F)
    g_up = _matmul(tag + "_dup_w", h, du, "tn", D, 2 * F, n_rows, out_dtype=BF16)
    return dh, g_up, g_cw, g_cb, g_down


def _cols_to_shards(g):
    r, c = g.shape
    return g.reshape(r, 4, c // 4).transpose(1, 0, 2)


def _shards_to_cols(s):
    return s.transpose(1, 0, 2).reshape(s.shape[1], 4 * s.shape[2])


def kernel(x, c, ctx, c_ctx, l0_ada_w, l0_ada_b, l0_norm1, l0_w_in, l0_conv_a, l0_q_norm, l0_w_qb, l0_kv_norm, l0_w_kvb, l0_w_out, l0_norm2, l0_ffn_up, l0_ffn_conv_w, l0_ffn_conv_b, l0_ffn_down, l1_ada_w, l1_ada_b, l1_norm1, l1_w_in, l1_gate_fw_w, l1_gate_fw_b, l1_gate_bw_w, l1_gate_bw_b, l1_o_norm, l1_w_out, l1_norm2, l1_ffn_up, l1_ffn_conv_w, l1_ffn_conv_b, l1_ffn_down, final_norm, loss_target, m_c_ctx, m_l0_ada_w, m_l0_ada_b, m_l0_norm1, m_l0_w_in, m_l0_conv_a, m_l0_q_norm, m_l0_w_qb, m_l0_kv_norm, m_l0_w_kvb, m_l0_w_out, m_l0_norm2, m_l0_ffn_up, m_l0_ffn_conv_w, m_l0_ffn_conv_b, m_l0_ffn_down, m_l1_ada_w, m_l1_ada_b, m_l1_norm1, m_l1_w_in, m_l1_gate_fw_w, m_l1_gate_fw_b, m_l1_gate_bw_w, m_l1_gate_bw_b, m_l1_o_norm, m_l1_w_out, m_l1_norm2, m_l1_ffn_up, m_l1_ffn_conv_w, m_l1_ffn_conv_b, m_l1_ffn_down, m_final_norm, v_c_ctx, v_l0_ada_w, v_l0_ada_b, v_l0_norm1, v_l0_w_in, v_l0_conv_a, v_l0_q_norm, v_l0_w_qb, v_l0_kv_norm, v_l0_w_kvb, v_l0_w_out, v_l0_norm2, v_l0_ffn_up, v_l0_ffn_conv_w, v_l0_ffn_conv_b, v_l0_ffn_down, v_l1_ada_w, v_l1_ada_b, v_l1_norm1, v_l1_w_in, v_l1_gate_fw_w, v_l1_gate_fw_b, v_l1_gate_bw_w, v_l1_gate_bw_b, v_l1_o_norm, v_l1_w_out, v_l1_norm2, v_l1_ffn_up, v_l1_ffn_conv_w, v_l1_ffn_conv_b, v_l1_ffn_down, v_final_norm):
    W = dict(c_ctx=c_ctx, l0_ada_w=l0_ada_w, l0_ada_b=l0_ada_b, l0_norm1=l0_norm1, l0_w_in=l0_w_in, l0_conv_a=l0_conv_a, l0_q_norm=l0_q_norm, l0_w_qb=l0_w_qb, l0_kv_norm=l0_kv_norm, l0_w_kvb=l0_w_kvb, l0_w_out=l0_w_out, l0_norm2=l0_norm2, l0_ffn_up=l0_ffn_up, l0_ffn_conv_w=l0_ffn_conv_w, l0_ffn_conv_b=l0_ffn_conv_b, l0_ffn_down=l0_ffn_down, l1_ada_w=l1_ada_w, l1_ada_b=l1_ada_b, l1_norm1=l1_norm1, l1_w_in=l1_w_in, l1_gate_fw_w=l1_gate_fw_w, l1_gate_fw_b=l1_gate_fw_b, l1_gate_bw_w=l1_gate_bw_w, l1_gate_bw_b=l1_gate_bw_b, l1_o_norm=l1_o_norm, l1_w_out=l1_w_out, l1_norm2=l1_norm2, l1_ffn_up=l1_ffn_up, l1_ffn_conv_w=l1_ffn_conv_w, l1_ffn_conv_b=l1_ffn_conv_b, l1_ffn_down=l1_ffn_down, final_norm=final_norm)
    MO = dict(c_ctx=m_c_ctx, l0_ada_w=m_l0_ada_w, l0_ada_b=m_l0_ada_b, l0_norm1=m_l0_norm1, l0_w_in=m_l0_w_in, l0_conv_a=m_l0_conv_a, l0_q_norm=m_l0_q_norm, l0_w_qb=m_l0_w_qb, l0_kv_norm=m_l0_kv_norm, l0_w_kvb=m_l0_w_kvb, l0_w_out=m_l0_w_out, l0_norm2=m_l0_norm2, l0_ffn_up=m_l0_ffn_up, l0_ffn_conv_w=m_l0_ffn_conv_w, l0_ffn_conv_b=m_l0_ffn_conv_b, l0_ffn_down=m_l0_ffn_down, l1_ada_w=m_l1_ada_w, l1_ada_b=m_l1_ada_b, l1_norm1=m_l1_norm1, l1_w_in=m_l1_w_in, l1_gate_fw_w=m_l1_gate_fw_w, l1_gate_fw_b=m_l1_gate_fw_b, l1_gate_bw_w=m_l1_gate_bw_w, l1_gate_bw_b=m_l1_gate_bw_b, l1_o_norm=m_l1_o_norm, l1_w_out=m_l1_w_out, l1_norm2=m_l1_norm2, l1_ffn_up=m_l1_ffn_up, l1_ffn_conv_w=m_l1_ffn_conv_w, l1_ffn_conv_b=m_l1_ffn_conv_b, l1_ffn_down=m_l1_ffn_down, final_norm=m_final_norm)
    VO = dict(c_ctx=v_c_ctx, l0_ada_w=v_l0_ada_w, l0_ada_b=v_l0_ada_b, l0_norm1=v_l0_norm1, l0_w_in=v_l0_w_in, l0_conv_a=v_l0_conv_a, l0_q_norm=v_l0_q_norm, l0_w_qb=v_l0_w_qb, l0_kv_norm=v_l0_kv_norm, l0_w_kvb=v_l0_w_kvb, l0_w_out=v_l0_w_out, l0_norm2=v_l0_norm2, l0_ffn_up=v_l0_ffn_up, l0_ffn_conv_w=v_l0_ffn_conv_w, l0_ffn_conv_b=v_l0_ffn_conv_b, l0_ffn_down=v_l0_ffn_down, l1_ada_w=v_l1_ada_w, l1_ada_b=v_l1_ada_b, l1_norm1=v_l1_norm1, l1_w_in=v_l1_w_in, l1_gate_fw_w=v_l1_gate_fw_w, l1_gate_fw_b=v_l1_gate_fw_b, l1_gate_bw_w=v_l1_gate_bw_w, l1_gate_bw_b=v_l1_gate_bw_b, l1_o_norm=v_l1_o_norm, l1_w_out=v_l1_w_out, l1_norm2=v_l1_norm2, l1_ffn_up=v_l1_ffn_up, l1_ffn_conv_w=v_l1_ffn_conv_w, l1_ffn_conv_b=v_l1_ffn_conv_b, l1_ffn_down=v_l1_ffn_down, final_norm=v_final_norm)
    names = list(W)

    T, D = x.shape[1], x.shape[2]
    TC = ctx.shape[1]
    assert TC == TR and T % TR == 0
    TA = T + TC
    nt, nlat = TA // TR, T // TR
    CC = D // 2
    F = l0_ffn_down.shape[0] * 4
    DK, DV = D // 2 // GLA_HEADS, D // GLA_HEADS
    KEY, VAL = GLA_HEADS * DK, GLA_HEADS * DV
    ADA_S = l0_ada_w.shape[1]
    H = MLA_HEADS
    my_x, my_y, my_c = lax.axis_index("x"), lax.axis_index("y"), lax.axis_index("c")
    chip, dev = 2 * my_x + my_y, 4 * my_x + 2 * my_y + my_c
    row = lambda a: a.reshape(1, -1)

    big = ["l0_w_in", "l0_w_qb", "l0_w_kvb", "l0_w_out", "l0_ffn_up", "l0_ffn_down",
           "l1_w_in", "l1_w_out", "l1_ffn_up", "l1_ffn_down"]
    small_sh = ["l0_conv_a", "l0_ffn_conv_w", "l1_gate_fw_w", "l1_gate_bw_w", "l1_ffn_conv_w"]
    own_half = lambda a, k: lax.dynamic_index_in_dim(a, k, 0, keepdims=False)
    set_own = lambda arr, val, k: lax.dynamic_update_index_in_dim(arr, val[None], k, 0)
    in_core_order = lambda mine, other: jnp.where(my_c == 0, jnp.stack([mine, other], 1), jnp.stack([other, mine], 1))
    early, late = big[:3], big[3:]
    w_half = {n: own_half(W[n].astype(BF16).reshape(2, W[n].shape[0] // 2, -1), my_c) for n in big}
    col_sh = {"l0_w_in", "l0_w_qb", "l0_w_kvb", "l0_ffn_up", "l1_w_in", "l1_ffn_up", "l0_conv_a", "l0_ffn_conv_w",
              "l1_gate_fw_w", "l1_gate_bw_w", "l1_ffn_conv_w"}
    whole_w = lambda n, s: _shards_to_cols(s) if n in col_sh else s.reshape(-1, s.shape[-1])

    def pair_up(tag, group, got):
        got = [set_own(g, w_half[n], chip) for n, g in zip(group, got)]
        other = _exchange("pair_weights_" + tag, got, "swap_c")
        return {n: whole_w(n, in_core_order(g, o).reshape(4, 2 * g.shape[1], g.shape[2])) for n, g, o in zip(group, got, other)}

    full = pair_up("a", early, _exchange("gather_weights", [w_half[n] for n in early], "ag_xy", fill_own=False))
    for n, s in zip(small_sh, _exchange("gather_small_weights", [W[n] for n in small_sh], "ag_xy")):
        full[n] = whole_w(n, s)
    (c_all,) = _exchange("gather_c", [c], "ag_all")

    o_kv, o_kr, o_ql, o_ax = 0, KV_LORA, KV_LORA + QK_ROPE, KV_LORA + QK_ROPE + Q_LORA
    n_sel = 3 * CC + Q_LORA + KV_LORA + QK_ROPE
    W0 = -(-(n_sel + QK_ROPE) // 256) * 256
    QOFF0, KVOFF0, KROFF0 = 3 * CC, 3 * CC + Q_LORA, 3 * CC + Q_LORA + KV_LORA
    w_in0 = full["l0_w_in"]
    w_kr = w_in0[:, o_kr:o_kr + QK_ROPE]
    w_in0p = jnp.concatenate([w_in0[:, o_ax:], w_in0[:, o_ql:o_ax], w_in0[:, :o_kr], w_kr, _rot_cols(w_kr),
                              jnp.zeros((D, W0 - n_sel - QK_ROPE), BF16)], axis=1)
    wq3 = full["l0_w_qb"].reshape(Q_LORA, H, QK_NOPE + QK_ROPE)
    w_qbp = jnp.concatenate([wq3, _rot_cols(wq3[:, :, QK_NOPE:])], axis=2).reshape(Q_LORA, H * QK_PAD)
    o_v1, o_lr1, o_q1, o_og1 = KEY, KEY + VAL, KEY + VAL + 2 * GATE_RANK, 2 * KEY + VAL + 2 * GATE_RANK
    W1 = 2 * VAL + 2 * KEY + LANES
    OGOFF, KOFF1, QOFF1, LROFF = VAL, 2 * VAL, 2 * VAL + KEY, 2 * VAL + 2 * KEY
    w_gate = jnp.zeros((LANES, 2 * KEY), F32)
    w_gate = w_gate.at[:GATE_RANK, :KEY].set(full["l1_gate_fw_w"]).at[GATE_RANK:2 * GATE_RANK, KEY:].set(full["l1_gate_bw_w"])
    b_gate = jnp.concatenate([l1_gate_fw_b, l1_gate_bw_b]).reshape(1, -1)

    pos = np.arange(T)
    inv = ROPE_THETA ** (-np.arange(0, QK_ROPE // 2, 2, dtype=np.float32) / (QK_ROPE // 2))
    ar, ac = (pos // GRID_W).astype(np.float32)[:, None] * inv, (pos % GRID_W).astype(np.float32)[:, None] * inv
    ang = jnp.asarray(np.concatenate([ar, ar, ac, ac], axis=-1).astype(np.float32))
    zpad = jnp.zeros((TA, LANES - QK_ROPE), F32)
    cos_t = jnp.concatenate([jnp.concatenate([jnp.cos(ang), jnp.ones((TC, QK_ROPE), F32)], axis=0), zpad], axis=1)
    sin_t = jnp.concatenate([jnp.concatenate([jnp.sin(ang), jnp.zeros((TC, QK_ROPE), F32)], axis=0), zpad], axis=1)

    c16 = jnp.concatenate([c_all.reshape(8, D), c_ctx.reshape(1, D), jnp.zeros((7, D), F32)], axis=0)
    (act16,) = _rowwise("silu_c", _f_silu, 1, 16, [("row", c16, 0, D)], [("row", 16, D, F32)])
    mod_parts = [_matmul("ada_fwd", act16, W["l%d_ada_w" % l], "nn", 16, ADA_S, D, tn_cap=512) for l in (0, 1)]
    mod_g = _exchange("gather_mod", mod_parts, "ag_xy")
    mods = []
    for l in (0, 1):
        mfull = _shards_to_cols(mod_g[l]) + W["l%d_ada_b" % l][None, :]
        mine = lax.dynamic_slice_in_dim(mfull, dev, 1, axis=0)
        mods.append(jnp.concatenate([mine, mfull[8:9]], axis=0).reshape(2, N_MOD, D))
    P = lambda l, k: mods[l][:, k:k + 1, :]

    X = jnp.concatenate([x[0], ctx[0]], axis=0)
    n1_0, n2_0, n1_1, n2_1 = row(l0_norm1), row(l0_norm2), row(l1_norm1), row(l1_norm2)
    (h0,) = _rowwise("l0_mod1", _f_mod, nt, TR, [("row", X, 0, D), ("full", n1_0), ("grp", P(0, 0)), ("grp", P(0, 1))],
                     [("row", TA, D, BF16)], nlat)
    z0 = _matmul("l0_in", h0, w_in0p, "nn", TA, W0, D)
    kvn_w, qn_w = row(l0_kv_norm), row(l0_q_norm)
    norm_ins = [("row", z0, KVOFF0 // KV_LORA, KV_LORA), ("row", z0, QOFF0 // Q_LORA, Q_LORA), ("full", kvn_w), ("full", qn_w)]
    kvn, qn = _rowwise("l0_latnorm", _f_norms, nt, TR, norm_ins, [("row", TA, KV_LORA, BF16), ("row", TA, Q_LORA, BF16)])
    kv = _matmul("l0_kvb", kvn, full["l0_w_kvb"], "nn", TA, H * QK_PAD, KV_LORA, out_dtype=BF16)
    qraw = _matmul("l0_qb", qn, w_qbp, "nn", TA, H * QK_PAD, Q_LORA)

    def f_rope(i, qr, kvv, krr, cs, sn):
        kr = _rope128(krr, cs, sn).astype(BF16)
        qs, ks = [], []
        for h in range(H):
            qs += [qr[:, h * QK_PAD:h * QK_PAD + QK_NOPE], _rope128(qr[:, h * QK_PAD + QK_NOPE:(h + 1) * QK_PAD], cs, sn)]
            ks += [kvv[:, h * QK_PAD:h * QK_PAD + QK_NOPE], kr]
        return jnp.concatenate(qs, axis=1) * MLA_QSCALE, jnp.concatenate(ks, axis=1)

    rope_ins = [("row", qraw, 0, H * QK_PAD), ("row", kv, 0, H * QK_PAD), ("row", z0, KROFF0 // LANES, LANES),
                ("row", cos_t, 0, LANES), ("row", sin_t, 0, LANES)]
    q, kf = _rowwise("l0_rope", f_rope, nt, TR, rope_ins, [("row", TA, H * QK_PAD, BF16), ("row", TA, H * QK_PAD, BF16)])
    o_lat, lse, got_late = _mla_fwd_lat(q, kf, kv, T, TA, _Hosted([w_half[n] for n in late], "ag_xy"))
    o_ctx = _mla_fwd_ctx(q, kf, kv, T)
    full.update(pair_up("b", late, got_late))
    w_in1 = full["l1_w_in"]
    w_in1p = jnp.concatenate([w_in1[:, o_v1:o_lr1], w_in1[:, o_og1:], w_in1[:, :o_v1], w_in1[:, o_q1:o_og1],
                              w_in1[:, o_lr1:o_q1], jnp.zeros((D, LANES - 2 * GATE_RANK), BF16)], axis=1)

    conv_a = full["l0_conv_a"]

    def f_conva(i, ax, ab, ac_, pax, pac, nax, nac, w):
        prev, nxt = _shifters(i, TR, T, TA)
        p = ac_ * ax
        return (ab * (w[0:1] * prev(p, pac * pax) + w[1:2] * p + w[2:3] * nxt(p, nac * nax)),)

    conva_ins = [("row", z0, 0, CC), ("row", z0, 1, CC), ("row", z0, 2, CC), ("prev", z0, 0, CC), ("prev", z0, 2, CC),
                 ("next", z0, 0, CC), ("next", z0, 2, CC), ("full", conv_a)]
    (y_a,) = _rowwise("l0_conva", f_conva, nt, TR, conva_ins, [("row", TA, CC, BF16)])
    cat0 = jnp.concatenate([y_a, jnp.concatenate([o_lat, o_ctx], axis=0)], axis=1)
    y0 = _matmul("l0_out", cat0, full["l0_w_out"], "nn", TA, D, CC + H * V_HEAD)
    resmod_outs = [("row", TA, D, F32), ("row", TA, D, BF16)]
    X1, h2_0 = _rowwise("l0_resmod2", _f_resmod, nt, TR,
                        [("row", X, 0, D), ("row", y0, 0, D), ("grp", P(0, 2)), ("full", n2_0), ("grp", P(0, 3)), ("grp", P(0, 4))],
                        resmod_outs, nlat)
    cw0, cb0 = full["l0_ffn_conv_w"], row(l0_ffn_conv_b)
    u0, act0, f0 = _ffn_fwd("l0_ffn", h2_0, full["l0_ffn_up"], cw0, cb0, full["l0_ffn_down"], TA, T, TA, F)
    X2, h1 = _rowwise("l1_resmod1", _f_resmod, nt, TR,
                      [("row", X1, 0, D), ("row", f0, 0, D), ("grp", P(0, 5)), ("full", n1_1), ("grp", P(1, 0)), ("grp", P(1, 1))],
                      resmod_outs, nlat)

    z1 = _matmul("l1_in", h1, w_in1p, "nn", TA, W1, D)

    def f_gates(i, lr, wg, bg):
        pre = _bdot(lr, wg, NN_DIMS) + bg
        return (_log_sigmoid(pre) / GATE_NORMALIZER,)

    gate_ins = [("row", z1, LROFF // LANES, LANES), ("full", w_gate), ("full", b_gate)]
    (gates,) = _rowwise("l1_gates", f_gates, nt, TR, gate_ins, [("row", TA, 2 * KEY, F32)])
    gla_dims = (DK, DV, QOFF1, KOFF1)
    o_f, o_b, st_f, st_b = _gla_fwd(z1, gates, gla_dims, nt, nlat)
    onw = row(l1_o_norm)

    def f_glaout(i, of, ob, og, w):
        o = of + ob
        on = jnp.concatenate([_rms(o[:, h * DV:(h + 1) * DV], w) for h in range(GLA_HEADS)], axis=1)
        return (on * _silu(og),)

    glaout_ins = [("row", o_f, 0, VAL), ("row", o_b, 0, VAL), ("row", z1, OGOFF // VAL, VAL), ("full", onw)]
    (go,) = _rowwise("l1_glaout", f_glaout, nlat, TR, glaout_ins, [("row", T, VAL, BF16)])
    y1 = _matmul("l1_out", go, full["l1_w_out"], "nn", T, D, VAL)
    X3, h2_1 = _rowwise("l1_resmod2", _f_resmod, nlat, TR,
                        [("row", X2, 0, D), ("row", y1, 0, D), ("grp", P(1, 2)), ("full", n2_1), ("grp", P(1, 3)), ("grp", P(1, 4))],
                        [("row", T, D, F32), ("row", T, D, BF16)])
    cw1, cb1 = full["l1_ffn_conv_w"], row(l1_ffn_conv_b)
    u1, act1, f1 = _ffn_fwd("l1_ffn", h2_1, full["l1_ffn_up"], cw1, cb1, full["l1_ffn_down"], T, T, T, F)

    fnw = row(final_norm)

    def f_head(i, x3, ff, g5, w, tgt):
        fin = lambda a, b, g_, w_: _rms(a + g_ * b, w_)
        y, pull = jax.vjp(fin, x3, ff, g5, w)
        err = y - tgt
        dx3, dff, dg5, dw = pull(err / D)
        loss = 0.5 * jnp.sum(jnp.mean(err * err, axis=-1, keepdims=True), axis=0, keepdims=True)
        return jnp.broadcast_to(loss, (1, LANES)), dx3, dff, dg5, dw

    head_ins = [("row", X3, 0, D), ("row", f1, 0, D), ("grp", P(1, 5)), ("full", fnw), ("row", loss_target[0], 0, D)]
    loss_acc, dX3, df1, dm5_1, g_final = _rowwise(
        "loss_head", f_head, nlat, TR, head_ins,
        [("acc", 1, LANES), ("row", T, D, F32), ("row", T, D, BF16), ("acc", 1, D), ("acc", 1, D)])
    loss = lax.psum(loss_acc[0, 0], ("x", "y", "c"))

    grads = {"final_norm": g_final}
    dh2_1, grads["l1_ffn_up"], grads["l1_ffn_conv_w"], grads["l1_ffn_conv_b"], grads["l1_ffn_down"] = _ffn_bwd(
        "l1_dffn", df1, h2_1, u1, act1, full["l1_ffn_up"], cw1, cb1, full["l1_ffn_down"], T, T, T, F)
    resmod_bwd = _vjp_of(_f_resmod, 6, (0, 1, 2, 3, 4, 5))
    dX2l, dy1, dm2_1, grads["l1_norm2"], dm3_1, dm4_1 = _rowwise(
        "l1_dresmod2", resmod_bwd, nlat, TR,
        [("row", X2, 0, D), ("row", y1, 0, D), ("grp", P(1, 2)), ("full", n2_1), ("grp", P(1, 3)), ("grp", P(1, 4)),
         ("row", dX3, 0, D), ("row", dh2_1, 0, D)],
        [("row", T, D, F32), ("row", T, D, BF16)] + [("acc", 1, D)] * 4)
    dgo = _matmul("l1_dout_x", dy1, full["l1_w_out"], "nt", T, VAL, D)
    grads["l1_w_out"] = _matmul("l1_dout_w", go, dy1, "tn", VAL, D, T, out_dtype=BF16)

    def f_glaout_bwd(i, of, ob, og, w, d):
        is_ctx = i >= nlat
        _, pull = jax.vjp(lambda a, b, c_, w_: f_glaout(i, a, b, c_, w_)[0], of, ob, og, w)
        dof, _, dog, dw = pull(jnp.where(is_ctx, 0.0, d))
        return dof, dog, dw

    glaout_b_ins = glaout_ins + [("rowc", dgo, 0, VAL, nlat - 1)]
    do_gla, dog, g_onorm = _rowwise("l1_dglaout", f_glaout_bwd, nt, TR, glaout_b_ins,
                                    [("row", TA, VAL, F32), ("row", TA, VAL, F32), ("acc", 1, DV)])
    grads["l1_o_norm"] = g_onorm
    (dq_f, dk_f, dv_f, dg_f), (dq_b, dk_b, dv_b, dg_b) = _gla_bwd(z1, gates, st_f, st_b, do_gla, gla_dims, nt, nlat)

    def f_dz1(i, dvf, dvb, dog_, dkf, dkb, dqf, dqb, dgf, dgb, lr, wg, bg):
        _, pull = jax.vjp(lambda a, b, c_: f_gates(i, a, b, c_)[0], lr, wg, bg)
        dlr, dwg, dbg = pull(jnp.concatenate([dgf, dgb], axis=1))
        return jnp.concatenate([dvf + dvb, dog_, dkf + dkb, dqf + dqb, dlr], axis=1), dwg, dbg

    dz1_ins = [("row", dv_f, 0, VAL), ("row", dv_b, 0, VAL), ("row", dog, 0, VAL), ("row", dk_f, 0, KEY), ("row", dk_b, 0, KEY),
               ("row", dq_f, 0, KEY), ("row", dq_b, 0, KEY), ("row", dg_f, 0, KEY), ("row", dg_b, 0, KEY)] + gate_ins
    dz1, g_wgate, g_bgate = _rowwise("l1_dz", f_dz1, nt, TR, dz1_ins, [("row", TA, W1, BF16), ("acc", LANES, 2 * KEY), ("acc", 1, 2 * KEY)])
    grads["l1_gate_fw_w"], grads["l1_gate_bw_w"] = g_wgate[:GATE_RANK, :KEY], g_wgate[GATE_RANK:2 * GATE_RANK, KEY:]
    grads["l1_gate_fw_b"], grads["l1_gate_bw_b"] = g_bgate[:, :KEY], g_bgate[:, KEY:]
    dh1 = _matmul("l1_din_x", dz1, w_in1p, "nt", TA, D, W1)
    g1p = _matmul("l1_din_w", h1, dz1, "tn", D, W1, TA, out_dtype=BF16)
    grads["l1_w_in"] = jnp.concatenate([g1p[:, KOFF1:QOFF1], g1p[:, :OGOFF], g1p[:, LROFF:LROFF + 2 * GATE_RANK],
                                        g1p[:, QOFF1:LROFF], g1p[:, OGOFF:KOFF1]], axis=1)

    def f_resmod1_bwd(i, x_, y_, g_, w_, sh, sc, dx2, dh):
        return resmod_bwd(i, x_, y_, g_, w_, sh, sc, jnp.where(i >= nlat, 0.0, dx2), dh)

    dX1, df0, dm5_0, grads["l1_norm1"], dm0_1, dm1_1 = _rowwise(
        "l1_dresmod1", f_resmod1_bwd, nt, TR,
        [("row", X1, 0, D), ("row", f0, 0, D), ("grp", P(0, 5)), ("full", n1_1), ("grp", P(1, 0)), ("grp", P(1, 1)),
         ("rowc", dX2l, 0, D, nlat - 1), ("row", dh1, 0, D)],
        [("row", TA, D, F32), ("row", TA, D, BF16), ("accg", 1, D), ("acc", 1, D), ("accg", 1, D), ("accg", 1, D)], nlat)

    dh2_0, grads["l0_ffn_up"], grads["l0_ffn_conv_w"], grads["l0_ffn_conv_b"], grads["l0_ffn_down"] = _ffn_bwd(
        "l0_dffn", df0, h2_0, u0, act0, full["l0_ffn_up"], cw0, cb0, full["l0_ffn_down"], TA, T, TA, F)
    dXd, dy0, dm2_0, grads["l0_norm2"], dm3_0, dm4_0 = _rowwise(
        "l0_dresmod2", resmod_bwd, nt, TR,
        [("row", X, 0, D), ("row", y0, 0, D), ("grp", P(0, 2)), ("full", n2_0), ("grp", P(0, 3)), ("grp", P(0, 4)),
         ("row", dX1, 0, D), ("row", dh2_0, 0, D)],
        [("row", TA, D, F32), ("row", TA, D, BF16), ("accg", 1, D), ("acc", 1, D), ("accg", 1, D), ("accg", 1, D)], nlat)
    dcat = _matmul("l0_dout_x", dy0, full["l0_w_out"], "nt", TA, CC + H * V_HEAD, D)
    grads["l0_w_out"] = _matmul("l0_dout_w", cat0, dy0, "tn", CC + H * V_HEAD, D, TA, out_dtype=BF16)

    def f_conva_bwd(i, ax, ab, ac_, pax, pab, pac, nax, nab, nac, dy, pdy, ndy, w):
        prev, nxt = _shifters(i, TR, T, TA)
        p = ac_ * ax
        pp, pn = prev(p, pac * pax), nxt(p, nac * nax)
        cv = w[0:1] * pp + w[1:2] * p + w[2:3] * pn
        dcv = dy * ab
        dp = w[0:1] * nxt(dcv, ndy * nab) + w[1:2] * dcv + w[2:3] * prev(dcv, pdy * pab)
        return (jnp.concatenate([dp * ac_, dy * cv, dp * ax], axis=1), _colsum(pp * dcv), _colsum(p * dcv), _colsum(pn * dcv))

    conva_b_ins = [("row", z0, 0, CC), ("row", z0, 1, CC), ("row", z0, 2, CC),
                   ("prev", z0, 0, CC), ("prev", z0, 1, CC), ("prev", z0, 2, CC),
                   ("next", z0, 0, CC), ("next", z0, 1, CC), ("next", z0, 2, CC),
                   ("row", dcat, 0, CC), ("prev", dcat, 0, CC), ("next", dcat, 0, CC), ("full", conv_a)]
    dz_a, ga0, ga1, ga2 = _rowwise("l0_dconva", f_conva_bwd, nt, TR, conva_b_ins, [("row", TA, 3 * CC, BF16)] + [("acc", 1, CC)] * 3)
    grads["l0_conv_a"] = jnp.concatenate([ga0, ga1, ga2], axis=0)
    do_cb0 = CC // V_HEAD
    row_sh = {"l0_w_out", "l1_w_out", "l0_ffn_down", "l1_ffn_down"}

    def pair_sums(tag, group):
        res = []
        for n in group:
            s = grads[n].reshape((4, -1) + grads[n].shape[1:]) if n in row_sh else _cols_to_shards(grads[n])
            res.append(s.reshape(4, 2, s.shape[1] // 2, s.shape[2]).transpose(1, 0, 2, 3))
        taken = _exchange("pair_split_grads_" + tag, [own_half(g, 1 - my_c) for g in res], "swap_c")
        return [_sum_pair("pairsum_" + n[3:], own_half(g, my_c), t, BF16) for n, g, t in zip(group, res, taken)]

    pair_late = pair_sums("a", late)
    dq_lat, dk_lat, dv_lat, got = _mla_bwd_lat(q, kf, kv, o_lat, dcat, lse, do_cb0, T, TA, _Hosted(pair_late, "a2a_xy"))
    landed_late = {n: set_own(l, own_half(p, chip), chip) for n, l, p in zip(late, got, pair_late)}
    dq_ctx, dk_ctx, dv_ctx = _mla_bwd_ctx(q, kf, kv, dcat, dk_lat, dv_lat, do_cb0, T)

    def f_rope_bwd(i, dql, dqc, dkl, dkc, dvl, dvc, cs, sn):
        is_ctx = i >= nlat
        dq_, dk_, dv_ = jnp.where(is_ctx, dqc, dql) * MLA_SCALE, jnp.where(is_ctx, dkc, dkl), jnp.where(is_ctx, dvc, dvl)
        dqs, dkvs, dkr = [], [], None
        for h in range(H):
            dqs += [dq_[:, h * QK_PAD:h * QK_PAD + QK_NOPE], _rope128_t(dq_[:, h * QK_PAD + QK_NOPE:(h + 1) * QK_PAD], cs, sn)]
            dkvs += [dk_[:, h * QK_PAD:h * QK_PAD + QK_NOPE].astype(BF16), dv_[:, h * V_HEAD:(h + 1) * V_HEAD]]
            part = dk_[:, h * QK_PAD + QK_NOPE:(h + 1) * QK_PAD]
            dkr = part if dkr is None else dkr + part
        return jnp.concatenate(dqs, axis=1), jnp.concatenate(dkvs, axis=1), _rope128_t(dkr, cs, sn)

    drope_ins = [("rowc", dq_lat, 0, H * QK_PAD, nlat - 1), ("full", dq_ctx), ("row", dk_lat, 0, H * QK_PAD), ("full", dk_ctx),
                 ("row", dv_lat, 0, H * V_HEAD), ("full", dv_ctx), ("row", cos_t, 0, LANES), ("row", sin_t, 0, LANES)]
    dqraw, dkv, dz_kr = _rowwise("l0_drope", f_rope_bwd, nt, TR, drope_ins,
                                 [("row", TA, H * QK_PAD, BF16), ("row", TA, H * QK_PAD, BF16), ("row", TA, LANES, BF16)])
    dqn = _matmul("l0_dqb_x", dqraw, w_qbp, "nt", TA, Q_LORA, H * QK_PAD)
    g_wqbp = _matmul("l0_dqb_w", qn, dqraw, "tn", Q_LORA, H * QK_PAD, TA).reshape(Q_LORA, H, QK_PAD)
    g_rope = g_wqbp[:, :, QK_NOPE:QK_NOPE + QK_ROPE] + _rot_cols_t(g_wqbp[:, :, QK_NOPE + QK_ROPE:])
    grads["l0_w_qb"] = jnp.concatenate([g_wqbp[:, :, :QK_NOPE], g_rope], axis=2).reshape(Q_LORA, H * (QK_NOPE + QK_ROPE)).astype(BF16)
    dkvn = _matmul("l0_dkvb_x", dkv, full["l0_w_kvb"], "nt", TA, KV_LORA, H * QK_PAD)
    grads["l0_w_kvb"] = _matmul("l0_dkvb_w", kvn, dkv, "tn", KV_LORA, H * QK_PAD, TA, out_dtype=BF16)
    norms_bwd = _vjp_of(_f_norms, 4, (0, 1, 2, 3))
    dz_kv, dz_q, grads["l0_kv_norm"], grads["l0_q_norm"] = _rowwise(
        "l0_dlatnorm", norms_bwd, nt, TR, norm_ins + [("row", dkvn, 0, KV_LORA), ("row", dqn, 0, Q_LORA)],
        [("row", TA, KV_LORA, BF16), ("row", TA, Q_LORA, BF16), ("acc", 1, KV_LORA), ("acc", 1, Q_LORA)])
    dz0 = jnp.concatenate([dz_a, dz_q, dz_kv, dz_kr, jnp.zeros((TA, W0 - KROFF0 - LANES), BF16)], axis=1)
    dh0 = _matmul("l0_din_x", dz0, w_in0p, "nt", TA, D, W0)
    g0p = _matmul("l0_din_w", h0, dz0, "tn", D, W0, TA)
    g_kr = g0p[:, KROFF0:KROFF0 + QK_ROPE] + _rot_cols_t(g0p[:, KROFF0 + QK_ROPE:KROFF0 + 2 * QK_ROPE])
    grads["l0_w_in"] = jnp.concatenate([g0p[:, KVOFF0:KROFF0], g_kr, g0p[:, QOFF0:KVOFF0], g0p[:, :QOFF0]], axis=1).astype(BF16)

    def f_mod_bwd(i, x_, w_, sh, sc, dh, dxd):
        _, pull = jax.vjp(lambda a, b, c_, d_: _modulate(a, b, c_, d_), x_, w_, sh, sc)
        dx, dw, dsh, dsc = pull(dh)
        return dx + dxd, dw, dsh, dsc

    dXf, grads["l0_norm1"], dm0_0, dm1_0 = _rowwise(
        "l0_dmod1", f_mod_bwd, nt, TR,
        [("row", X, 0, D), ("full", n1_0), ("grp", P(0, 0)), ("grp", P(0, 1)), ("row", dh0, 0, D), ("row", dXd, 0, D)],
        [("row", TA, D, F32), ("acc", 1, D), ("accg", 1, D), ("accg", 1, D)], nlat)
    grad_x = dXf[:T][None]

    zD = jnp.zeros((1, D), F32)
    lat = lambda a: a[0] if a.ndim == 3 else a
    cxt = lambda a: a[1] if a.ndim == 3 else zD
    dmods = []
    for parts in ((dm0_0, dm1_0, dm2_0, dm3_0, dm4_0, dm5_0), (dm0_1, dm1_1, dm2_1, dm3_1, dm4_1, dm5_1)):
        dmods.append((jnp.concatenate([lat(a) for a in parts], axis=1), jnp.concatenate([cxt(a) for a in parts], axis=1)))
    small_names = ["l0_norm1", "l0_norm2", "l0_kv_norm", "l0_q_norm", "l0_conv_a", "l0_ffn_conv_w", "l0_ffn_conv_b",
                   "l1_norm1", "l1_norm2", "l1_o_norm", "l1_gate_fw_w", "l1_gate_fw_b", "l1_gate_bw_w", "l1_gate_bw_b",
                   "l1_ffn_conv_w", "l1_ffn_conv_b", "final_norm"]
    pieces = [("dm0", dmods[0][0]), ("dmc0", dmods[0][1]), ("dm1", dmods[1][0]), ("dmc1", dmods[1][1])]
    pieces += [("l0_ada_b", dmods[0][0] + dmods[0][1]), ("l1_ada_b", dmods[1][0] + dmods[1][1])]
    pieces += [(n, grads[n]) for n in small_names]
    offs, cur = {}, 0
    for n, a in pieces:
        offs[n] = (cur, a.size, a.shape)
        cur += -(-a.size // LANES) * LANES
    n_pad = -(-cur // 1024) * 1024
    flat = jnp.concatenate([jnp.pad(a.reshape(-1), (0, -a.size % LANES)) for _, a in pieces] + [jnp.zeros((n_pad - cur,), F32)])
    (small_g,) = _exchange("gather_small", [flat.reshape(n_pad // LANES, LANES)], "ag_all")
    small_sum = _sum_slots("sum_small", small_g).reshape(-1)
    small_all = small_g.reshape(8, -1)
    take = lambda n: small_sum[offs[n][0]:offs[n][0] + offs[n][1]].reshape(offs[n][2])

    cc_parts = []
    for l in (0, 1):
        o_m, sz, _ = offs["dm%d" % l]
        d16 = jnp.concatenate([small_all[:, o_m:o_m + sz], take("dmc%d" % l).reshape(1, -1), jnp.zeros((7, sz), F32)], axis=0)
        d16 = lax.dynamic_slice_in_dim(d16.reshape(16, 4, ADA_S), chip, 1, axis=1).reshape(16, ADA_S)
        grads["l%d_ada_w" % l] = _matmul("ada_dw", act16, d16, "tn", D, ADA_S, 16, tn_cap=512)
        cc_parts.append(_matmul("ada_dx", d16, W["l%d_ada_w" % l], "nt", 16, D, ADA_S, tn_cap=512))
        grads["l%d_ada_b" % l] = take("l%d_ada_b" % l).reshape(-1)
    cc_g = _exchange("gather_cc", cc_parts, "ag_xy")

    def cc_call():
        def body(a_ref, b_ref, c_ref, o_ref):
            tot = a_ref[0, 8:9, :] + b_ref[0, 8:9, :]
            for j in range(1, 4):
                tot = tot + (a_ref[j, 8:9, :] + b_ref[j, 8:9, :])
            _, pull = jax.vjp(_silu, c_ref[...])
            o_ref[...] = pull(tot)[0]
        return pl.pallas_call(body, name="c_ctx_grad", out_shape=jax.ShapeDtypeStruct((1, D), F32))(cc_g[0], cc_g[1], row(c_ctx))

    grads["c_ctx"] = cc_call().reshape(-1)

    pair_early = pair_sums("b", early)
    landed = dict(landed_late)
    for n, l, p in zip(early, _exchange("scatter_grads", pair_early, "a2a_xy", fill_own=False), pair_early):
        landed[n] = set_own(l, own_half(p, chip), chip)
    halves = [_sum_slots("sum_" + n[3:], landed[n]) for n in big]
    others = _exchange("pair_grads", halves, "swap_c")

    out_g, out_d, out_m, out_v = {}, {}, {}, {}
    for n, mine, other in zip(big, halves, others):
        g2 = jnp.where(my_c == 0, jnp.concatenate([mine, other], 0), jnp.concatenate([other, mine], 0))
        out_g[n], out_d[n], out_m[n], out_v[n] = _adam("adam_" + n[3:], W[n], [g2], MO[n], VO[n])
    for l in (0, 1):
        n = "l%d_ada_w" % l
        out_g[n], out_d[n], out_m[n], out_v[n] = _adam("adam_ada_w", W[n], [grads[n]], MO[n], VO[n])
    rest = [n for n in names if n not in out_g]
    g_rest = {}
    for n in rest:
        if n == "c_ctx" or n.endswith("ada_b"):
            g_rest[n] = grads[n]
        elif n in small_sh:
            gfull = take(n)
            cs = gfull.shape[1] // 4
            g_rest[n] = lax.dynamic_slice_in_dim(gfull, chip * cs, cs, axis=1)
        else:
            g_rest[n] = take(n).reshape(W[n].shape)
    sizes = [W[n].size for n in rest]
    tot = sum(-(-s // LANES) * LANES for s in sizes)
    tot_pad = -(-tot // 1024) * 1024

    def pack(d):
        parts = [jnp.pad(d[n].reshape(-1), (0, -d[n].size % LANES)) for n in rest]
        return jnp.concatenate(parts + [jnp.zeros((tot_pad - tot,), F32)]).reshape(tot_pad // LANES, LANES)

    packed = _adam("adam_small", pack(W), [pack(g_rest)], pack(MO), pack(VO))
    cur = 0
    for n, s in zip(rest, sizes):
        for dst, arr in zip((out_g, out_d, out_m, out_v), packed):
            dst[n] = arr.reshape(-1)[cur:cur + s].reshape(W[n].shape)
        cur += -(-s // LANES) * LANES

    return (loss, grad_x, *[out_g[n] for n in names], *[out_d[n] for n in names],
            *[out_m[n] for n in names], *[out_v[n] for n in names])
```

```python
import numpy as np

import jax
import jax.numpy as jnp
from jax import lax
from jax.experimental import pallas as pl
from jax.experimental.pallas import tpu as pltpu

F32, BF16 = jnp.float32, jnp.bfloat16
MESH_ID = pl.DeviceIdType.MESH

EPS = 1e-6
N_MOD = 6
MLA_HEADS, QK_NOPE, QK_ROPE, V_HEAD, Q_LORA, KV_LORA = 8, 128, 64, 128, 512, 256
QK_PAD = 2 * QK_NOPE
ROPE_THETA, GRID_W = 10000.0, 64
GLA_HEADS, GATE_RANK, GATE_NORMALIZER, CHUNK = 4, 16, 16.0, 64
ADAM_LR, ADAM_B1, ADAM_B2, ADAM_EPS, ADAM_WD, ADAM_STEP = 0.001, 0.9, 0.999, 1e-08, 0.01, 10

LANES = 128
TR = 256
V7X_VMEM_BYTES = 64 * 2 ** 20
VMEM_LIMIT = V7X_VMEM_BYTES - 8 * 2 ** 20

NT_DIMS = (((1,), (1,)), ((), ()))
TN_DIMS = (((0,), (0,)), ((), ()))
NN_DIMS = (((1,), (0,)), ((), ()))

def _rot_cols(w):
    q = QK_ROPE // 4
    return jnp.concatenate([-w[..., q:2 * q], w[..., :q], -w[..., 3 * q:], w[..., 2 * q:3 * q]], axis=-1)


def _rot_cols_t(g):
    q = QK_ROPE // 4
    return jnp.concatenate([g[..., q:2 * q], -g[..., :q], g[..., 3 * q:], -g[..., 2 * q:3 * q]], axis=-1)


def _cparams(sem=None):
    return pltpu.CompilerParams(dimension_semantics=sem, vmem_limit_bytes=VMEM_LIMIT)


def _tile(dim, cap, quantum=LANES):
    if dim <= cap:
        return dim
    t = (cap // quantum) * quantum
    while t >= quantum:
        if dim % t == 0:
            return t
        t -= quantum
    return dim


_REL_XY = ((1, 0, 0), (0, 1, 0), (1, 1, 0))
_REL_ALL = tuple((a, b, c) for a in (0, 1) for b in (0, 1) for c in (0, 1))[1:]
_REL_C = ((0, 0, 1),)


_RELS = {"ag_xy": _REL_XY, "a2a_xy": _REL_XY, "ag_all": _REL_ALL, "swap_c": _REL_C}
_LEAD = {"ag_xy": (4,), "ag_all": (8,), "a2a_xy": (), "swap_c": ()}


def _remote_copies(mode, ins, outs, send, recv):
    x, y, c = lax.axis_index("x"), lax.axis_index("y"), lax.axis_index("c")
    chip, dev = 2 * x + y, 4 * x + 2 * y + c
    rels, plan = _RELS[mode], []
    for a, (i_ref, o) in enumerate(zip(ins, outs)):
        for r, (bx, by, bc) in enumerate(rels):
            px = 1 - x if bx else x
            py = 1 - y if by else y
            pc = 1 - c if bc else c
            pchip, pdev = 2 * px + py, 4 * px + 2 * py + pc
            if mode == "ag_xy":
                src, dst, mine = i_ref, o.at[chip], o.at[pchip]
            elif mode == "ag_all":
                src, dst, mine = i_ref, o.at[dev], o.at[pdev]
            elif mode == "a2a_xy":
                src, dst, mine = i_ref.at[pchip], o.at[chip], o.at[pchip]
            else:
                src, dst, mine = i_ref, o, o
            k = a * len(rels) + r
            mk = lambda d: pltpu.make_async_remote_copy(src_ref=src, dst_ref=d, send_sem=send.at[k], recv_sem=recv.at[k],
                                                        device_id=(px, py, pc), device_id_type=MESH_ID)
            plan.append((mk(dst), mk(mine)))
    return plan


def _exchange(name, arrs, mode, fill_own=True):
    n, nr = len(arrs), len(_RELS[mode])
    out_shape = tuple(jax.ShapeDtypeStruct(_LEAD[mode] + a.shape, a.dtype) for a in arrs)

    def body(*refs):
        ins, outs = refs[:n], refs[n:2 * n]
        send, recv, loc = refs[2 * n:]
        chip = 2 * lax.axis_index("x") + lax.axis_index("y")
        local = []
        if fill_own and mode != "swap_c":
            slot = 2 * chip + lax.axis_index("c") if mode == "ag_all" else chip
            for a in range(n):
                lc = pltpu.make_async_copy(ins[a].at[chip] if mode == "a2a_xy" else ins[a], outs[a].at[slot], loc.at[a])
                lc.start()
                local.append(lc)
        plan = _remote_copies(mode, ins, outs, send, recv)
        for cp, _ in plan:
            cp.start()
        for cp, landing in plan:
            cp.wait_send()
            landing.wait_recv()
        for lc in local:
            lc.wait()

    hbm = pl.BlockSpec(memory_space=pl.ANY)
    res = pl.pallas_call(
        body, name=name, out_shape=out_shape, in_specs=[hbm] * n, out_specs=tuple([hbm] * n),
        scratch_shapes=[pltpu.SemaphoreType.DMA((n * nr,)), pltpu.SemaphoreType.DMA((n * nr,)),
                        pltpu.SemaphoreType.DMA((max(n, 1),))],
    )(*arrs)
    return list(res)


class _Hosted:
    def __init__(self, arrs, mode):
        self.arrs, self.mode, self.n = list(arrs), mode, len(arrs)
        self.out_shape = [jax.ShapeDtypeStruct(_LEAD[mode] + a.shape, a.dtype) for a in arrs]
        self.specs = [pl.BlockSpec(memory_space=pl.ANY)] * self.n
        nsem = self.n * len(_RELS[mode])
        self.scratch = [pltpu.SemaphoreType.DMA((nsem,)), pltpu.SemaphoreType.DMA((nsem,))]

    def run(self, first, last, ins, outs, sems):
        @pl.when(first)
        def _():
            for cp, _ in _remote_copies(self.mode, ins, outs, *sems):
                cp.start()

        @pl.when(last)
        def _():
            for cp, landing in _remote_copies(self.mode, ins, outs, *sems):
                cp.wait_send()
                landing.wait_recv()


def _matmul(name, a, b, mode, M, N, K, out_dtype=F32, a_off=(0, 0), b_off=(0, 0), tm_cap=1024, tn_cap=1024, tk_cap=2816):
    tk = K if K <= 4096 else _tile(K, tk_cap)
    nk = K // tk
    if nk == 1 and K <= 2048 and mode != "tn":
        tm_cap = max(tm_cap, 1408)
    tm, tn = _tile(M, tm_cap, LANES if M % LANES == 0 else 8), _tile(N, tn_cap)
    assert M % tm == 0 and N % tn == 0 and K % tk == 0, (name, M, N, K, tm, tn, tk)
    if mode == "nn":
        ab, bb, dims = (tm, tk), (tk, tn), NN_DIMS
        ai = lambda i, j, k: (i + a_off[0] // tm, k + a_off[1] // tk)
        bi = lambda i, j, k: (k + b_off[0] // tk, j + b_off[1] // tn)
        chk = (a_off[0] % tm, a_off[1] % tk, b_off[0] % tk, b_off[1] % tn)
    elif mode == "nt":
        ab, bb, dims = (tm, tk), (tn, tk), NT_DIMS
        ai = lambda i, j, k: (i + a_off[0] // tm, k + a_off[1] // tk)
        bi = lambda i, j, k: (j + b_off[0] // tn, k + b_off[1] // tk)
        chk = (a_off[0] % tm, a_off[1] % tk, b_off[0] % tn, b_off[1] % tk)
    else:
        ab, bb, dims = (tk, tm), (tk, tn), TN_DIMS
        ai = lambda i, j, k: (k + a_off[0] // tk, i + a_off[1] // tm)
        bi = lambda i, j, k: (k + b_off[0] // tk, j + b_off[1] // tn)
        chk = (a_off[0] % tk, a_off[1] % tm, b_off[0] % tk, b_off[1] % tn)
    assert not any(chk), (name, chk)

    own_acc = nk > 1 and out_dtype != F32

    def body(a_ref, b_ref, o_ref, *acc):
        dot = lambda: lax.dot_general(a_ref[...].astype(BF16), b_ref[...].astype(BF16), dims, preferred_element_type=F32)
        if nk == 1:
            o_ref[...] = dot().astype(out_dtype)
            return
        k, dst = pl.program_id(2), (acc[0] if own_acc else o_ref)

        @pl.when(k == 0)
        def _():
            dst[...] = jnp.zeros_like(dst)

        dst[...] += dot()
        if own_acc:
            @pl.when(k == nk - 1)
            def _():
                o_ref[...] = acc[0][...].astype(out_dtype)

    return pl.pallas_call(
        body, name=name, out_shape=jax.ShapeDtypeStruct((M, N), out_dtype), grid=(M // tm, N // tn, nk),
        in_specs=[pl.BlockSpec(ab, ai), pl.BlockSpec(bb, bi)], out_specs=pl.BlockSpec((tm, tn), lambda i, j, k: (i, j)),
        scratch_shapes=[pltpu.VMEM((tm, tn), F32)] if own_acc else [],
        compiler_params=_cparams(("parallel", "parallel", "arbitrary")),
    )(a, b)


def _rowwise(name, fn, grid_n, tr, ins, outs, nlat=None):
    nlat = grid_n if nlat is None else nlat
    grp = lambda i: jnp.minimum(i // nlat, 1)
    in_specs, args = [], []
    for spec in ins:
        kind, arr = spec[0], spec[1]
        if kind == "row":
            in_specs.append(pl.BlockSpec((tr, spec[3]), lambda i, cb=spec[2]: (i, cb)))
        elif kind == "rowc":
            in_specs.append(pl.BlockSpec((tr, spec[3]), lambda i, cb=spec[2], mx=spec[4]: (jnp.minimum(i, mx), cb)))
        elif kind == "prev":
            in_specs.append(pl.BlockSpec((8, spec[3]), lambda i, cb=spec[2]: (jnp.maximum(i * (tr // 8) - 1, 0), cb)))
        elif kind == "next":
            nb = arr.shape[0] // 8
            in_specs.append(pl.BlockSpec((8, spec[3]), lambda i, cb=spec[2], nb=nb: (jnp.minimum((i + 1) * (tr // 8), nb - 1), cb)))
        elif kind == "grp":
            in_specs.append(pl.BlockSpec((None,) + arr.shape[1:], lambda i: (grp(i), 0, 0)))
        else:
            in_specs.append(pl.BlockSpec(arr.shape, lambda i: (0, 0)))
        args.append(arr)
    out_shape, out_specs = [], []
    for spec in outs:
        if spec[0] == "row":
            out_shape.append(jax.ShapeDtypeStruct((spec[1], spec[2]), spec[3]))
            out_specs.append(pl.BlockSpec((tr, spec[2]), lambda i: (i, 0)))
        elif spec[0] == "acc":
            out_shape.append(jax.ShapeDtypeStruct((spec[1], spec[2]), F32))
            out_specs.append(pl.BlockSpec((spec[1], spec[2]), lambda i: (0, 0)))
        else:
            out_shape.append(jax.ShapeDtypeStruct((2, spec[1], spec[2]), F32))
            out_specs.append(pl.BlockSpec((None, spec[1], spec[2]), lambda i: (grp(i), 0, 0)))
    n_in = len(ins)
    has_acc = any(s[0] != "row" for s in outs)

    def body(*refs):
        i = pl.program_id(0)
        res = fn(i, *[r[...] for r in refs[:n_in]])
        for spec, ref, val in zip(outs, refs[n_in:], res):
            if spec[0] == "row":
                ref[...] = val.astype(ref.dtype)
            else:
                first = (i == 0) if spec[0] == "acc" else jnp.logical_or(i == 0, i == nlat)

                @pl.when(first)
                def _(ref=ref, val=val):
                    ref[...] = val

                @pl.when(jnp.logical_not(first))
                def _(ref=ref, val=val):
                    ref[...] += val

    return pl.pallas_call(
        body, name=name, out_shape=tuple(out_shape), grid=(grid_n,), in_specs=in_specs, out_specs=tuple(out_specs),
        compiler_params=_cparams(("arbitrary",) if has_acc else ("parallel",)),
    )(*args)


def _vjp_of(fwd, n_in, wrt):
    def bwd(i, *args):
        _, pull = jax.vjp(lambda *a: fwd(i, *a), *args[:n_in])
        g = pull(tuple(args[n_in:]))
        return tuple(g[k] for k in wrt)
    return bwd


def _rms(x, w):
    return x * lax.rsqrt(jnp.mean(x * x, axis=-1, keepdims=True) + EPS) * w


def _modulate(x, w, shift, scale):
    return _rms(x, w) * (1.0 + scale) + shift


def _sigmoid(x):
    return 0.5 * jnp.tanh(0.5 * x) + 0.5


def _silu(x):
    return x * _sigmoid(x)


def _log_sigmoid(x):
    return jnp.minimum(x, 0.0) - jnp.log(1.0 + jnp.exp(-jnp.abs(x)))


def _f_mod(i, x, w, sh, sc):
    return (_modulate(x, w, sh, sc),)


def _f_resmod(i, x, y, gate, w, sh, sc):
    x1 = x + gate * y
    return x1, _modulate(x1, w, sh, sc)


def _f_norms(i, kvl, ql, kvw, qw):
    return _rms(kvl, kvw), _rms(ql, qw)


def _f_silu(i, x):
    return (_silu(x),)


def _rope128(x2, cos, sin):
    return x2 * cos + pltpu.roll(x2, QK_ROPE, 1) * sin


def _rope128_t(d2, cos, sin):
    return d2 * cos + pltpu.roll(d2 * sin, QK_ROPE, 1)


def _shifters(i, tr, T, TA):
    assert T % tr == 0 and TA % tr == 0
    loc = lax.broadcasted_iota(jnp.int32, (tr, 1), 0)
    starts_seq = jnp.logical_or(i * tr == 0, i * tr == T)
    ends_seq = jnp.logical_or((i + 1) * tr == T, (i + 1) * tr == TA)

    def prev(x, halo):
        return jnp.where(loc == 0, jnp.where(starts_seq, 0.0, halo[7:8, :]), pltpu.roll(x, 1, 0))

    def nxt(x, halo):
        return jnp.where(loc == tr - 1, jnp.where(ends_seq, 0.0, halo[0:1, :]), pltpu.roll(x, tr - 1, 0))

    return prev, nxt


def _colsum(x):
    return jnp.sum(x, axis=0, keepdims=True)


MLA_SCALE = (QK_NOPE + QK_ROPE) ** -0.5
LOG2E, LN2 = 1.4426950408889634, 0.6931471805599453
MLA_QSCALE = MLA_SCALE * LOG2E
MLA_SUB = 256
MLA_KV_TILE = 2816


def _mla_fwd_lat(q, kf, kv, T, TA, hosted):
    H = MLA_HEADS
    tq, tk = _tile(T, 1024, TR), _tile(TA, MLA_KV_TILE)
    nq, nk = T // tq, TA // tk
    nh = hosted.n

    def body(q_ref, k_ref, v_ref, *rest):
        o_ref, lse_ref = rest[nh:nh + 2]
        m_sc, l_sc, acc_sc = rest[2 * nh + 2:2 * nh + 5]
        hi, qi, ki = pl.program_id(0), pl.program_id(1), pl.program_id(2)
        hosted.run(jnp.logical_and(hi == 0, jnp.logical_and(qi == 0, ki == 0)),
                   jnp.logical_and(hi == H - 1, jnp.logical_and(qi == nq - 1, ki == nk - 1)),
                   rest[:nh], rest[nh + 2:2 * nh + 2], rest[2 * nh + 5:])

        @pl.when(ki == 0)
        def _():
            m_sc[...] = jnp.full_like(m_sc, -jnp.inf)
            l_sc[...] = jnp.zeros_like(l_sc)
            acc_sc[...] = jnp.zeros_like(acc_sc)

        sb = min(tq, MLA_SUB)
        scores = lambda r: lax.dot_general(q_ref[r * sb:(r + 1) * sb, :], k_ref[...], NT_DIMS, preferred_element_type=F32)
        s_next = scores(0)
        for r in range(tq // sb):
            s, rows = s_next, slice(r * sb, (r + 1) * sb)
            if r + 1 < tq // sb:
                s_next = scores(r + 1)
            m_old = m_sc[rows, :]
            m_new = jnp.maximum(m_old, jnp.max(s, axis=-1, keepdims=True))
            alpha = jnp.exp2(m_old - m_new)
            p = jnp.exp2(s - m_new)
            l_sc[rows, :] = alpha * l_sc[rows, :] + jnp.sum(p, axis=-1, keepdims=True)
            acc_sc[rows, :] = alpha * acc_sc[rows, :] + jnp.dot(p.astype(BF16), v_ref[...], preferred_element_type=F32)
            m_sc[rows, :] = m_new

        @pl.when(ki == nk - 1)
        def _():
            o_ref[...] = (acc_sc[...] / l_sc[...]).astype(o_ref.dtype)
            lse_ref[...] = m_sc[...] + jnp.log2(l_sc[...])

    res = pl.pallas_call(
        body, name="mla_fwd_lat", grid=(H, nq, nk),
        out_shape=[jax.ShapeDtypeStruct((T, H * V_HEAD), BF16), jax.ShapeDtypeStruct((H, T, 1), F32)] + hosted.out_shape,
        in_specs=[pl.BlockSpec((tq, QK_PAD), lambda h, i, k: (i, h)), pl.BlockSpec((tk, QK_PAD), lambda h, i, k: (k, h)),
                  pl.BlockSpec((tk, V_HEAD), lambda h, i, k: (k, 2 * h + 1))] + hosted.specs,
        out_specs=[pl.BlockSpec((tq, V_HEAD), lambda h, i, k: (i, h)),
                   pl.BlockSpec((None, tq, 1), lambda h, i, k: (h, i, 0))] + hosted.specs,
        scratch_shapes=[pltpu.VMEM((tq, 1), F32), pltpu.VMEM((tq, 1), F32), pltpu.VMEM((tq, V_HEAD), F32)] + hosted.scratch,
        compiler_params=_cparams(("arbitrary", "arbitrary", "arbitrary")),
    )(q, kf, kv, *hosted.arrs)
    return res[0], res[1], list(res[2:])


def _mla_ctx_probs(q_ref, k_ref):
    s = lax.dot_general(q_ref[...], k_ref[...], NT_DIMS, preferred_element_type=F32)
    p = jnp.exp2(s - jnp.max(s, axis=-1, keepdims=True))
    return p, jnp.sum(p, axis=-1, keepdims=True)


def _mla_fwd_ctx(q, kf, kv, T):
    H, cb = MLA_HEADS, T // TR

    def body(q_ref, k_ref, v_ref, o_ref):
        p, l = _mla_ctx_probs(q_ref, k_ref)
        o_ref[...] = (jnp.dot(p.astype(BF16), v_ref[...], preferred_element_type=F32) / l).astype(o_ref.dtype)

    return pl.pallas_call(
        body, name="mla_fwd_ctx", grid=(H,), out_shape=jax.ShapeDtypeStruct((TR, H * V_HEAD), BF16),
        in_specs=[pl.BlockSpec((TR, QK_PAD), lambda h: (cb, h)), pl.BlockSpec((TR, QK_PAD), lambda h: (cb, h)),
                  pl.BlockSpec((TR, V_HEAD), lambda h: (cb, 2 * h + 1))],
        out_specs=pl.BlockSpec((TR, V_HEAD), lambda h: (0, h)), compiler_params=_cparams(("parallel",)),
    )(q, kf, kv)


def _mla_grads(p, q_ref, k_ref, v_ref, do, delta):
    dob = do.astype(BF16)
    dp = lax.dot_general(dob, v_ref[...], NT_DIMS, preferred_element_type=F32)
    ds = (p * (dp - delta)).astype(BF16)
    return (jnp.dot(ds, k_ref[...], preferred_element_type=F32),
            lax.dot_general(ds, q_ref[...], TN_DIMS, preferred_element_type=F32) * LN2,
            lax.dot_general(p.astype(BF16), dob, TN_DIMS, preferred_element_type=F32))


def _mla_bwd_lat(q, kf, kv, o, dcat, lse, do_cb0, T, TA, hosted):
    H = MLA_HEADS
    tq, tk = _tile(T, 1024, TR), _tile(TA, MLA_KV_TILE)
    nq, nk = T // tq, TA // tk
    nh = hosted.n

    def body(q_ref, k_ref, v_ref, o_ref, do_ref, lse_ref, *rest):
        dq_ref, dk_ref, dv_ref = rest[nh:nh + 3]
        dk_acc, dv_acc = rest[2 * nh + 3:2 * nh + 5]
        hi, ki, qi = pl.program_id(0), pl.program_id(1), pl.program_id(2)
        hosted.run(jnp.logical_and(hi == 0, jnp.logical_and(qi == 0, ki == 0)),
                   jnp.logical_and(hi == H - 1, jnp.logical_and(qi == nq - 1, ki == nk - 1)),
                   rest[:nh], rest[nh + 3:2 * nh + 3], rest[2 * nh + 5:])

        @pl.when(jnp.logical_and(ki == 0, qi == 0))
        def _():
            dq_ref[...] = jnp.zeros_like(dq_ref)

        @pl.when(qi == 0)
        def _():
            dk_acc[...] = jnp.zeros_like(dk_acc)
            dv_acc[...] = jnp.zeros_like(dv_acc)

        sb = min(tq, MLA_SUB)

        def products(r):
            rows = slice(r * sb, (r + 1) * sb)
            dob = do_ref[rows, :].astype(BF16)
            return (lax.dot_general(q_ref[rows, :], k_ref[...], NT_DIMS, preferred_element_type=F32),
                    lax.dot_general(dob, v_ref[...], NT_DIMS, preferred_element_type=F32), dob)

        nxt, dk, dv = products(0), None, None
        for r in range(tq // sb):
            (s, dp, dob), rows = nxt, slice(r * sb, (r + 1) * sb)
            if r + 1 < tq // sb:
                nxt = products(r + 1)
            delta = jnp.sum(do_ref[rows, :] * o_ref[rows, :].astype(F32), axis=-1, keepdims=True)
            p = jnp.exp2(s - lse_ref[rows, :])
            ds = (p * (dp - delta)).astype(BF16)
            dq_ref[pl.ds(pl.multiple_of(qi * tq + r * sb, sb), sb), :] += jnp.dot(ds, k_ref[...], preferred_element_type=F32)
            dk_r = lax.dot_general(ds, q_ref[rows, :], TN_DIMS, preferred_element_type=F32)
            dv_r = lax.dot_general(p.astype(BF16), dob, TN_DIMS, preferred_element_type=F32)
            dk, dv = (dk_r, dv_r) if dk is None else (dk + dk_r, dv + dv_r)
        dk_acc[...] += dk
        dv_acc[...] += dv

        @pl.when(qi == nq - 1)
        def _():
            dk_ref[...] = dk_acc[...] * LN2
            dv_ref[...] = dv_acc[...].astype(dv_ref.dtype)

    res = pl.pallas_call(
        body, name="mla_bwd_lat", grid=(H, nk, nq),
        out_shape=[jax.ShapeDtypeStruct((T, H * QK_PAD), F32), jax.ShapeDtypeStruct((TA, H * QK_PAD), F32),
                   jax.ShapeDtypeStruct((TA, H * V_HEAD), BF16)] + hosted.out_shape,
        in_specs=[pl.BlockSpec((tq, QK_PAD), lambda h, k, i: (i, h)), pl.BlockSpec((tk, QK_PAD), lambda h, k, i: (k, h)),
                  pl.BlockSpec((tk, V_HEAD), lambda h, k, i: (k, 2 * h + 1)), pl.BlockSpec((tq, V_HEAD), lambda h, k, i: (i, h)),
                  pl.BlockSpec((tq, V_HEAD), lambda h, k, i: (i, do_cb0 + h)),
                  pl.BlockSpec((None, tq, 1), lambda h, k, i: (h, i, 0))] + hosted.specs,
        out_specs=[pl.BlockSpec((T, QK_PAD), lambda h, k, i: (0, h)), pl.BlockSpec((tk, QK_PAD), lambda h, k, i: (k, h)),
                   pl.BlockSpec((tk, V_HEAD), lambda h, k, i: (k, h))] + hosted.specs,
        scratch_shapes=[pltpu.VMEM((tk, QK_PAD), F32), pltpu.VMEM((tk, V_HEAD), F32)] + hosted.scratch,
        compiler_params=_cparams(("arbitrary", "arbitrary", "arbitrary")),
    )(q, kf, kv, o, dcat, lse, *hosted.arrs)
    return res[0], res[1], res[2], list(res[3:])


def _mla_bwd_ctx(q, kf, kv, dcat, dk_lat, dv_lat, do_cb0, T):
    H, cb = MLA_HEADS, T // TR

    def body(q_ref, k_ref, v_ref, do_ref, dkl_ref, dvl_ref, dq_ref, dk_ref, dv_ref):
        p, l = _mla_ctx_probs(q_ref, k_ref)
        do = do_ref[...]
        o = jnp.dot(p.astype(BF16), v_ref[...], preferred_element_type=F32) / l
        dq, dk, dv = _mla_grads(p / l, q_ref, k_ref, v_ref, do, jnp.sum(do * o, axis=-1, keepdims=True))
        dq_ref[...] = dq
        dk_ref[...] = dkl_ref[...] + dk
        dv_ref[...] = (dvl_ref[...].astype(F32) + dv).astype(dv_ref.dtype)

    at_ctx = lambda w, f: pl.BlockSpec((TR, w), lambda h: (cb, f(h)))
    at_0 = lambda w: pl.BlockSpec((TR, w), lambda h: (0, h))
    return pl.pallas_call(
        body, name="mla_bwd_ctx", grid=(H,),
        out_shape=(jax.ShapeDtypeStruct((TR, H * QK_PAD), F32), jax.ShapeDtypeStruct((TR, H * QK_PAD), F32),
                   jax.ShapeDtypeStruct((TR, H * V_HEAD), BF16)),
        in_specs=[at_ctx(QK_PAD, lambda h: h), at_ctx(QK_PAD, lambda h: h), at_ctx(V_HEAD, lambda h: 2 * h + 1),
                  at_ctx(V_HEAD, lambda h: do_cb0 + h), at_ctx(QK_PAD, lambda h: h), at_ctx(V_HEAD, lambda h: h)],
        out_specs=(at_0(QK_PAD), at_0(QK_PAD), at_0(V_HEAD)), compiler_params=_cparams(("parallel",)),
    )(q, kf, kv, dcat, dk_lat, dv_lat)


def _bdot_impl(a, b, dims):
    return lax.dot_general(a.astype(BF16), b.astype(BF16), dims, preferred_element_type=F32)


def _bdot(a, b, dims):
    @jax.custom_vjp
    def f(a_, b_):
        return _bdot_impl(a_, b_, dims)

    def fwd(a_, b_):
        return _bdot_impl(a_, b_, dims), (a_.astype(BF16), b_.astype(BF16))

    def bwd(res, ct):
        a_, b_ = res
        if dims == NN_DIMS:
            return _bdot_impl(ct, b_, NT_DIMS), _bdot_impl(a_, ct, TN_DIMS)
        if dims == NT_DIMS:
            return _bdot_impl(ct, b_, NN_DIMS), _bdot_impl(ct, a_, TN_DIMS)
        return _bdot_impl(b_, ct, NT_DIMS), _bdot_impl(a_, ct, NN_DIMS)

    f.defvjp(fwd, bwd)
    return f(a, b)


def _ones_dot_impl(ones, x, dims):
    hi = x.astype(BF16)
    r1 = x - hi.astype(F32)
    mid = r1.astype(BF16)
    lo = (r1 - mid.astype(F32)).astype(BF16)
    d = lambda t: lax.dot_general(ones, t, dims, preferred_element_type=F32)
    return (d(lo) + d(mid)) + d(hi)


@jax.custom_vjp
def _ones_dot(ones, x):
    return _ones_dot_impl(ones, x, NN_DIMS)


_ones_dot.defvjp(lambda ones, x: (_ones_dot_impl(ones, x, NN_DIMS), ones),
                 lambda ones, ct: (jnp.zeros_like(ones), _ones_dot_impl(ones, ct, TN_DIMS)))


def _gla_tile(st, q, k, v, g, rev, q_scale):
    nc = q.shape[0] // CHUNK
    ii = lax.broadcasted_iota(jnp.int32, (CHUNK, CHUNK), 0)
    jj = lax.broadcasted_iota(jnp.int32, (CHUNK, CHUNK), 1)
    tri = (jj >= ii) if rev else (jj <= ii)
    ones = tri.astype(BF16)
    outs = [None] * nc
    for ci in (range(nc - 1, -1, -1) if rev else range(nc)):
        sl = slice(ci * CHUNK, (ci + 1) * CHUNK)
        qc, kc, vc, gc = q[sl] * q_scale, k[sl], v[sl], g[sl]
        b = _ones_dot(ones, gc)
        bl = b[0:1] if rev else b[CHUNK - 1:CHUNK]
        kd, qe, ke = kc * jnp.exp(bl - b), qc * jnp.exp(b), kc * jnp.exp(-b)
        att = jnp.where(tri, _bdot(qe, ke, NT_DIMS), 0.0)
        outs[ci] = _bdot(att, vc, NN_DIMS) + _bdot(qe, st, NT_DIMS)
        st = st * jnp.exp(bl) + _bdot(vc, kd, TN_DIMS)
    return st, jnp.concatenate(outs, axis=0)


def _gla_tiles(nt, nlat):
    return (lambda p: (p + nlat) % nt), (lambda p: nt - 1 - p)


def _gla_fwd(z1, g, dims, nt, nlat):
    DK, DV, q_off, k_off = dims
    KEY, TA = GLA_HEADS * DK, nt * TR
    in_specs, out_o = [], []
    for d, tile in enumerate(_gla_tiles(nt, nlat)):
        in_specs += [pl.BlockSpec((TR, DK), lambda h, p, t=tile: (t(p), q_off // DK + h)),
                     pl.BlockSpec((TR, DK), lambda h, p, t=tile: (t(p), k_off // DK + h)),
                     pl.BlockSpec((TR, DV), lambda h, p, t=tile: (t(p), h)),
                     pl.BlockSpec((TR, DK), lambda h, p, t=tile, d=d: (t(p), d * GLA_HEADS + h))]
        out_o.append(pl.BlockSpec((TR, DV), lambda h, p, t=tile: (t(p), h)))
    st_spec = pl.BlockSpec((None, None, DV, DK), lambda h, p: (h, p, 0, 0))

    def body(qf, kf, vf, gf, qb, kb, vb, gb, of_ref, ob_ref, sf_ref, sb_ref, st_f, st_b):
        @pl.when(pl.program_id(1) == 0)
        def _():
            st_f[...] = jnp.zeros_like(st_f)
            st_b[...] = jnp.zeros_like(st_b)

        sf_ref[...] = st_f[...]
        sb_ref[...] = st_b[...]
        new_f, out_f = _gla_tile(st_f[...], qf[...], kf[...], vf[...], gf[...], False, DK ** -0.5)
        new_b, out_b = _gla_tile(st_b[...], qb[...], kb[...], vb[...], gb[...], True, DK ** -0.5)
        st_f[...] = new_f
        st_b[...] = new_b
        of_ref[...] = out_f
        ob_ref[...] = out_b

    o_shape = jax.ShapeDtypeStruct((TA, GLA_HEADS * DV), F32)
    s_shape = jax.ShapeDtypeStruct((GLA_HEADS, nt, DV, DK), F32)
    return pl.pallas_call(
        body, name="gla_fwd", grid=(GLA_HEADS, nt), out_shape=(o_shape, o_shape, s_shape, s_shape),
        in_specs=in_specs, out_specs=(out_o[0], out_o[1], st_spec, st_spec),
        scratch_shapes=[pltpu.VMEM((DV, DK), F32)] * 2, compiler_params=_cparams(("parallel", "arbitrary")),
    )(z1, z1, z1, g, z1, z1, z1, g)


def _gla_bwd(z1, g, st_f, st_b, do, dims, nt, nlat):
    DK, DV, q_off, k_off = dims
    KEY, VAL, TA = GLA_HEADS * DK, GLA_HEADS * DV, nt * TR
    pos = lambda s: nt - 1 - s
    in_specs, out_specs = [], []
    for d, tile in enumerate(_gla_tiles(nt, nlat)):
        at = lambda w, f, t=tile: pl.BlockSpec((TR, w), lambda h, s: (t(pos(s)), f(h)))
        in_specs += [at(DK, lambda h: q_off // DK + h), at(DK, lambda h: k_off // DK + h), at(DV, lambda h: h),
                     at(DK, lambda h, d=d: d * GLA_HEADS + h),
                     pl.BlockSpec((None, None, DV, DK), lambda h, s: (h, pos(s), 0, 0)), at(DV, lambda h: h)]
        out_specs += [at(DK, lambda h: h), at(DK, lambda h: h), at(DV, lambda h: h), at(DK, lambda h: h)]

    def one(refs, outs, dst, rev):
        q_ref, k_ref, v_ref, g_ref, st_ref, do_ref = refs
        _, pull = jax.vjp(lambda st, q, k, v, gg: _gla_tile(st, q, k, v, gg, rev, DK ** -0.5),
                          st_ref[...], q_ref[...], k_ref[...], v_ref[...], g_ref[...])
        grads = pull((dst[...], do_ref[...]))
        dst[...] = grads[0]
        for o_ref, val in zip(outs, grads[1:]):
            o_ref[...] = val

    def body(*refs):
        dst_f, dst_b = refs[20:]

        @pl.when(pl.program_id(1) == 0)
        def _():
            dst_f[...] = jnp.zeros_like(dst_f)
            dst_b[...] = jnp.zeros_like(dst_b)

        one(refs[0:6], refs[12:16], dst_f, False)
        one(refs[6:12], refs[16:20], dst_b, True)

    shapes = [jax.ShapeDtypeStruct((TA, KEY), F32), jax.ShapeDtypeStruct((TA, KEY), F32),
              jax.ShapeDtypeStruct((TA, VAL), F32), jax.ShapeDtypeStruct((TA, KEY), F32)]
    res = pl.pallas_call(
        body, name="gla_bwd", grid=(GLA_HEADS, nt), out_shape=shapes * 2, in_specs=in_specs, out_specs=out_specs,
        scratch_shapes=[pltpu.VMEM((DV, DK), F32)] * 2, compiler_params=_cparams(("parallel", "arbitrary")),
    )(z1, z1, z1, g, st_f, do, z1, z1, z1, g, st_b, do)
    return res[:4], res[4:]


def _sum_slots(name, arr):
    S, R, C = arr.shape
    tr = _tile(R, max(16, (2 ** 19) // max(C, 1) // 16 * 16), 16)

    def body(a_ref, o_ref):
        acc = a_ref[0].astype(F32)
        for s in range(1, S):
            acc = acc + a_ref[s].astype(F32)
        o_ref[...] = acc

    return pl.pallas_call(
        body, name=name, grid=(R // tr,), out_shape=jax.ShapeDtypeStruct((R, C), F32),
        in_specs=[pl.BlockSpec((S, tr, C), lambda i: (0, i, 0))], out_specs=pl.BlockSpec((tr, C), lambda i: (i, 0)),
        compiler_params=_cparams(("parallel",)),
    )(arr)


def _sum_pair(name, a, b, out_dtype):
    shape, C = a.shape, a.shape[-1]
    a2, b2 = a.reshape(-1, C), b.reshape(-1, C)
    R = a2.shape[0]
    tr = _tile(R, max(16, (2 ** 19) // max(C, 1) // 16 * 16), 16)
    fn = lambda i, p, q: (p.astype(F32) + q.astype(F32),)
    (res,) = _rowwise(name, fn, R // tr, tr, [("row", a2, 0, C), ("row", b2, 0, C)], [("row", R, C, out_dtype)])
    return res.reshape(shape)


def _adam_math(w, g, m, v):
    m2 = ADAM_B1 * m + (1.0 - ADAM_B1) * g
    v2 = ADAM_B2 * v + (1.0 - ADAM_B2) * (g * g)
    m_hat = m2 / (1.0 - ADAM_B1 ** ADAM_STEP)
    v_hat = v2 / (1.0 - ADAM_B2 ** ADAM_STEP)
    return -ADAM_LR * (m_hat / (jnp.sqrt(v_hat) + ADAM_EPS) + ADAM_WD * w), m2, v2


def _adam(name, w, gs, m, v):
    R, C = w.shape
    tr = _tile(R, max(8, (2 ** 19) // max(C, 1) // 8 * 8), 8)
    ng = len(gs)

    def fn(i, w_, *rest):
        g = rest[0] if ng == 1 else rest[0] + rest[1]
        d, m2, v2 = _adam_math(w_, g, rest[ng], rest[ng + 1])
        return g, d, m2, v2

    ins = [("row", a, 0, C) for a in (w, *gs, m, v)]
    return _rowwise(name, fn, R // tr, tr, ins, [("row", R, C, F32)] * 4)


def _ffn_fwd(tag, h, w_up, cw, cb, w_down, n_rows, T, TA, F):
    tr = 128
    u = _matmul(tag + "_up", h, w_up, "nn", n_rows, 2 * F, h.shape[1])
    cwg, cwv, cbg, cbv = cw[:, :F], cw[:, F:], cb[:, :F], cb[:, F:]

    def fn(i, ug, uv, pg, pv, ng, nv, wg, wv, bg, bv):
        prev, nxt = _shifters(i, tr, T, TA)
        cg = wg[0:1] * prev(ug, pg) + wg[1:2] * ug + wg[2:3] * nxt(ug, ng) + bg
        cv = wv[0:1] * prev(uv, pv) + wv[1:2] * uv + wv[2:3] * nxt(uv, nv) + bv
        return (_silu(cg) * cv,)

    ins = [("row", u, 0, F), ("row", u, 1, F), ("prev", u, 0, F), ("prev", u, 1, F), ("next", u, 0, F), ("next", u, 1, F),
           ("full", cwg), ("full", cwv), ("full", cbg), ("full", cbv)]
    (act,) = _rowwise(tag + "_conv", fn, n_rows // tr, tr, ins, [("row", n_rows, F, BF16)])
    f = _matmul(tag + "_down", act, w_down, "nn", n_rows, w_down.shape[1], F)
    return u, act, f


def _ffn_bwd(tag, df, h, u, act, w_up, cw, cb, w_down, n_rows, T, TA, F):
    tr = 128
    D = w_down.shape[1]
    dact = _matmul(tag + "_ddown_x", df, w_down, "nt", n_rows, F, D, tm_cap=1408)
    g_down = _matmul(tag + "_ddown_w", act, df, "tn", F, D, n_rows, out_dtype=BF16)
    cwg, cwv, cbg, cbv = cw[:, :F], cw[:, F:], cb[:, :F], cb[:, F:]

    def fn_a(i, ug, uv, pg, pv, ng, nv, da, wg, wv, bg, bv):
        prev, nxt = _shifters(i, tr, T, TA)
        ugp, ugn, uvp, uvn = prev(ug, pg), nxt(ug, ng), prev(uv, pv), nxt(uv, nv)
        cg = wg[0:1] * ugp + wg[1:2] * ug + wg[2:3] * ugn + bg
        cv = wv[0:1] * uvp + wv[1:2] * uv + wv[2:3] * uvn + bv
        sg = _sigmoid(cg)
        dcv = da * (cg * sg)
        dcg = da * cv * (sg * (1.0 + cg * (1.0 - sg)))
        return (jnp.concatenate([dcg, dcv], axis=1),
                _colsum(ugp * dcg), _colsum(ug * dcg), _colsum(ugn * dcg),
                _colsum(uvp * dcv), _colsum(uv * dcv), _colsum(uvn * dcv), _colsum(dcg), _colsum(dcv))

    ins = [("row", u, 0, F), ("row", u, 1, F), ("prev", u, 0, F), ("prev", u, 1, F), ("next", u, 0, F), ("next", u, 1, F),
           ("row", dact, 0, F), ("full", cwg), ("full", cwv), ("full", cbg), ("full", cbv)]
    res = _rowwise(tag + "_dconv_a", fn_a, n_rows // tr, tr, ins, [("row", n_rows, 2 * F, F32)] + [("acc", 1, F)] * 8)
    duc = res[0]
    g_cw = jnp.concatenate([jnp.concatenate(res[1:4], axis=0), jnp.concatenate(res[4:7], axis=0)], axis=1)
    g_cb = jnp.concatenate([res[7], res[8]], axis=1)

    def fn_b(i, dg, dv, pg, pv, ng, nv, wg, wv):
        prev, nxt = _shifters(i, tr, T, TA)
        dug = wg[0:1] * nxt(dg, ng) + wg[1:2] * dg + wg[2:3] * prev(dg, pg)
        duv = wv[0:1] * nxt(dv, nv) + wv[1:2] * dv + wv[2:3] * prev(dv, pv)
        return (jnp.concatenate([dug, duv], axis=1),)

    ins = [("row", duc, 0, F), ("row", duc, 1, F), ("prev", duc, 0, F), ("prev", duc, 1, F), ("next", duc, 0, F),
           ("next", duc, 1, F), ("full", cwg), ("full", cwv)]
    (du,) = _rowwise(tag + "_dconv_b", fn_b, n_rows // tr, tr, ins, [("row", n_rows, 2 * F, BF16)])
    dh = _matmul(tag + "_dup_x", du, w_up, "nt", n_rows, D, 2 * F)
    g_up = _matmul(tag + "_dup_w", h, du, "tn", D, 2 * F, n_rows, out_dtype=BF16)
    return dh, g_up, g_cw, g_cb, g_down


def _cols_to_shards(g):
    r, c = g.shape
    return g.reshape(r, 4, c // 4).transpose(1, 0, 2)


def _shards_to_cols(s):
    return s.transpose(1, 0, 2).reshape(s.shape[1], 4 * s.shape[2])


def kernel(x, c, ctx, c_ctx, l0_ada_w, l0_ada_b, l0_norm1, l0_w_in, l0_conv_a, l0_q_norm, l0_w_qb, l0_kv_norm, l0_w_kvb, l0_w_out, l0_norm2, l0_ffn_up, l0_ffn_conv_w, l0_ffn_conv_b, l0_ffn_down, l1_ada_w, l1_ada_b, l1_norm1, l1_w_in, l1_gate_fw_w, l1_gate_fw_b, l1_gate_bw_w, l1_gate_bw_b, l1_o_norm, l1_w_out, l1_norm2, l1_ffn_up, l1_ffn_conv_w, l1_ffn_conv_b, l1_ffn_down, final_norm, loss_target, m_c_ctx, m_l0_ada_w, m_l0_ada_b, m_l0_norm1, m_l0_w_in, m_l0_conv_a, m_l0_q_norm, m_l0_w_qb, m_l0_kv_norm, m_l0_w_kvb, m_l0_w_out, m_l0_norm2, m_l0_ffn_up, m_l0_ffn_conv_w, m_l0_ffn_conv_b, m_l0_ffn_down, m_l1_ada_w, m_l1_ada_b, m_l1_norm1, m_l1_w_in, m_l1_gate_fw_w, m_l1_gate_fw_b, m_l1_gate_bw_w, m_l1_gate_bw_b, m_l1_o_norm, m_l1_w_out, m_l1_norm2, m_l1_ffn_up, m_l1_ffn_conv_w, m_l1_ffn_conv_b, m_l1_ffn_down, m_final_norm, v_c_ctx, v_l0_ada_w, v_l0_ada_b, v_l0_norm1, v_l0_w_in, v_l0_conv_a, v_l0_q_norm, v_l0_w_qb, v_l0_kv_norm, v_l0_w_kvb, v_l0_w_out, v_l0_norm2, v_l0_ffn_up, v_l0_ffn_conv_w, v_l0_ffn_conv_b, v_l0_ffn_down, v_l1_ada_w, v_l1_ada_b, v_l1_norm1, v_l1_w_in, v_l1_gate_fw_w, v_l1_gate_fw_b, v_l1_gate_bw_w, v_l1_gate_bw_b, v_l1_o_norm, v_l1_w_out, v_l1_norm2, v_l1_ffn_up, v_l1_ffn_conv_w, v_l1_ffn_conv_b, v_l1_ffn_down, v_final_norm):
    W = dict(c_ctx=c_ctx, l0_ada_w=l0_ada_w, l0_ada_b=l0_ada_b, l0_norm1=l0_norm1, l0_w_in=l0_w_in, l0_conv_a=l0_conv_a, l0_q_norm=l0_q_norm, l0_w_qb=l0_w_qb, l0_kv_norm=l0_kv_norm, l0_w_kvb=l0_w_kvb, l0_w_out=l0_w_out, l0_norm2=l0_norm2, l0_ffn_up=l0_ffn_up, l0_ffn_conv_w=l0_ffn_conv_w, l0_ffn_conv_b=l0_ffn_conv_b, l0_ffn_down=l0_ffn_down, l1_ada_w=l1_ada_w, l1_ada_b=l1_ada_b, l1_norm1=l1_norm1, l1_w_in=l1_w_in, l1_gate_fw_w=l1_gate_fw_w, l1_gate_fw_b=l1_gate_fw_b, l1_gate_bw_w=l1_gate_bw_w, l1_gate_bw_b=l1_gate_bw_b, l1_o_norm=l1_o_norm, l1_w_out=l1_w_out, l1_norm2=l1_norm2, l1_ffn_up=l1_ffn_up, l1_ffn_conv_w=l1_ffn_conv_w, l1_ffn_conv_b=l1_ffn_conv_b, l1_ffn_down=l1_ffn_down, final_norm=final_norm)
    MO = dict(c_ctx=m_c_ctx, l0_ada_w=m_l0_ada_w, l0_ada_b=m_l0_ada_b, l0_norm1=m_l0_norm1, l0_w_in=m_l0_w_in, l0_conv_a=m_l0_conv_a, l0_q_norm=m_l0_q_norm, l0_w_qb=m_l0_w_qb, l0_kv_norm=m_l0_kv_norm, l0_w_kvb=m_l0_w_kvb, l0_w_out=m_l0_w_out, l0_norm2=m_l0_norm2, l0_ffn_up=m_l0_ffn_up, l0_ffn_conv_w=m_l0_ffn_conv_w, l0_ffn_conv_b=m_l0_ffn_conv_b, l0_ffn_down=m_l0_ffn_down, l1_ada_w=m_l1_ada_w, l1_ada_b=m_l1_ada_b, l1_norm1=m_l1_norm1, l1_w_in=m_l1_w_in, l1_gate_fw_w=m_l1_gate_fw_w, l1_gate_fw_b=m_l1_gate_fw_b, l1_gate_bw_w=m_l1_gate_bw_w, l1_gate_bw_b=m_l1_gate_bw_b, l1_o_norm=m_l1_o_norm, l1_w_out=m_l1_w_out, l1_norm2=m_l1_norm2, l1_ffn_up=m_l1_ffn_up, l1_ffn_conv_w=m_l1_ffn_conv_w, l1_ffn_conv_b=m_l1_ffn_conv_b, l1_ffn_down=m_l1_ffn_down, final_norm=m_final_norm)
    VO = dict(c_ctx=v_c_ctx, l0_ada_w=v_l0_ada_w, l0_ada_b=v_l0_ada_b, l0_norm1=v_l0_norm1, l0_w_in=v_l0_w_in, l0_conv_a=v_l0_conv_a, l0_q_norm=v_l0_q_norm, l0_w_qb=v_l0_w_qb, l0_kv_norm=v_l0_kv_norm, l0_w_kvb=v_l0_w_kvb, l0_w_out=v_l0_w_out, l0_norm2=v_l0_norm2, l0_ffn_up=v_l0_ffn_up, l0_ffn_conv_w=v_l0_ffn_conv_w, l0_ffn_conv_b=v_l0_ffn_conv_b, l0_ffn_down=v_l0_ffn_down, l1_ada_w=v_l1_ada_w, l1_ada_b=v_l1_ada_b, l1_norm1=v_l1_norm1, l1_w_in=v_l1_w_in, l1_gate_fw_w=v_l1_gate_fw_w, l1_gate_fw_b=v_l1_gate_fw_b, l1_gate_bw_w=v_l1_gate_bw_w, l1_gate_bw_b=v_l1_gate_bw_b, l1_o_norm=v_l1_o_norm, l1_w_out=v_l1_w_out, l1_norm2=v_l1_norm2, l1_ffn_up=v_l1_ffn_up, l1_ffn_conv_w=v_l1_ffn_conv_w, l1_ffn_conv_b=v_l1_ffn_conv_b, l1_ffn_down=v_l1_ffn_down, final_norm=v_final_norm)
    names = list(W)

    T, D = x.shape[1], x.shape[2]
    TC = ctx.shape[1]
    assert TC == TR and T % TR == 0
    TA = T + TC
    nt, nlat = TA // TR, T // TR
    CC = D // 2
    F = l0_ffn_down.shape[0] * 4
    DK, DV = D // 2 // GLA_HEADS, D // GLA_HEADS
    KEY, VAL = GLA_HEADS * DK, GLA_HEADS * DV
    ADA_S = l0_ada_w.shape[1]
    H = MLA_HEADS
    my_x, my_y, my_c = lax.axis_index("x"), lax.axis_index("y"), lax.axis_index("c")
    chip, dev = 2 * my_x + my_y, 4 * my_x + 2 * my_y + my_c
    row = lambda a: a.reshape(1, -1)

    big = ["l0_w_in", "l0_w_qb", "l0_w_kvb", "l0_w_out", "l0_ffn_up", "l0_ffn_down",
           "l1_w_in", "l1_w_out", "l1_ffn_up", "l1_ffn_down"]
    small_sh = ["l0_conv_a", "l0_ffn_conv_w", "l1_gate_fw_w", "l1_gate_bw_w", "l1_ffn_conv_w"]
    own_half = lambda a, k: lax.dynamic_index_in_dim(a, k, 0, keepdims=False)
    set_own = lambda arr, val, k: lax.dynamic_update_index_in_dim(arr, val[None], k, 0)
    in_core_order = lambda mine, other: jnp.where(my_c == 0, jnp.stack([mine, other], 1), jnp.stack([other, mine], 1))
    early, late = big[:3], big[3:]
    w_half = {n: own_half(W[n].astype(BF16).reshape(2, W[n].shape[0] // 2, -1), my_c) for n in big}
    col_sh = {"l0_w_in", "l0_w_qb", "l0_w_kvb", "l0_ffn_up", "l1_w_in", "l1_ffn_up", "l0_conv_a", "l0_ffn_conv_w",
              "l1_gate_fw_w", "l1_gate_bw_w", "l1_ffn_conv_w"}
    whole_w = lambda n, s: _shards_to_cols(s) if n in col_sh else s.reshape(-1, s.shape[-1])

    def pair_up(tag, group, got):
        got = [set_own(g, w_half[n], chip) for n, g in zip(group, got)]
        other = _exchange("pair_weights_" + tag, got, "swap_c")
        return {n: whole_w(n, in_core_order(g, o).reshape(4, 2 * g.shape[1], g.shape[2])) for n, g, o in zip(group, got, other)}

    full = pair_up("a", early, _exchange("gather_weights", [w_half[n] for n in early], "ag_xy", fill_own=False))
    for n, s in zip(small_sh, _exchange("gather_small_weights", [W[n] for n in small_sh], "ag_xy")):
        full[n] = whole_w(n, s)
    (c_all,) = _exchange("gather_c", [c], "ag_all")

    o_kv, o_kr, o_ql, o_ax = 0, KV_LORA, KV_LORA + QK_ROPE, KV_LORA + QK_ROPE + Q_LORA
    n_sel = 3 * CC + Q_LORA + KV_LORA + QK_ROPE
    W0 = -(-(n_sel + QK_ROPE) // 256) * 256
    QOFF0, KVOFF0, KROFF0 = 3 * CC, 3 * CC + Q_LORA, 3 * CC + Q_LORA + KV_LORA
    w_in0 = full["l0_w_in"]
    w_kr = w_in0[:, o_kr:o_kr + QK_ROPE]
    w_in0p = jnp.concatenate([w_in0[:, o_ax:], w_in0[:, o_ql:o_ax], w_in0[:, :o_kr], w_kr, _rot_cols(w_kr),
                              jnp.zeros((D, W0 - n_sel - QK_ROPE), BF16)], axis=1)
    wq3 = full["l0_w_qb"].reshape(Q_LORA, H, QK_NOPE + QK_ROPE)
    w_qbp = jnp.concatenate([wq3, _rot_cols(wq3[:, :, QK_NOPE:])], axis=2).reshape(Q_LORA, H * QK_PAD)
    o_v1, o_lr1, o_q1, o_og1 = KEY, KEY + VAL, KEY + VAL + 2 * GATE_RANK, 2 * KEY + VAL + 2 * GATE_RANK
    W1 = 2 * VAL + 2 * KEY + LANES
    OGOFF, KOFF1, QOFF1, LROFF = VAL, 2 * VAL, 2 * VAL + KEY, 2 * VAL + 2 * KEY
    w_gate = jnp.zeros((LANES, 2 * KEY), F32)
    w_gate = w_gate.at[:GATE_RANK, :KEY].set(full["l1_gate_fw_w"]).at[GATE_RANK:2 * GATE_RANK, KEY:].set(full["l1_gate_bw_w"])
    b_gate = jnp.concatenate([l1_gate_fw_b, l1_gate_bw_b]).reshape(1, -1)

    pos = np.arange(T)
    inv = ROPE_THETA ** (-np.arange(0, QK_ROPE // 2, 2, dtype=np.float32) / (QK_ROPE // 2))
    ar, ac = (pos // GRID_W).astype(np.float32)[:, None] * inv, (pos % GRID_W).astype(np.float32)[:, None] * inv
    ang = jnp.asarray(np.concatenate([ar, ar, ac, ac], axis=-1).astype(np.float32))
    zpad = jnp.zeros((TA, LANES - QK_ROPE), F32)
    cos_t = jnp.concatenate([jnp.concatenate([jnp.cos(ang), jnp.ones((TC, QK_ROPE), F32)], axis=0), zpad], axis=1)
    sin_t = jnp.concatenate([jnp.concatenate([jnp.sin(ang), jnp.zeros((TC, QK_ROPE), F32)], axis=0), zpad], axis=1)

    c16 = jnp.concatenate([c_all.reshape(8, D), c_ctx.reshape(1, D), jnp.zeros((7, D), F32)], axis=0)
    (act16,) = _rowwise("silu_c", _f_silu, 1, 16, [("row", c16, 0, D)], [("row", 16, D, F32)])
    mod_parts = [_matmul("ada_fwd", act16, W["l%d_ada_w" % l], "nn", 16, ADA_S, D, tn_cap=512) for l in (0, 1)]
    mod_g = _exchange("gather_mod", mod_parts, "ag_xy")
    mods = []
    for l in (0, 1):
        mfull = _shards_to_cols(mod_g[l]) + W["l%d_ada_b" % l][None, :]
        mine = lax.dynamic_slice_in_dim(mfull, dev, 1, axis=0)
        mods.append(jnp.concatenate([mine, mfull[8:9]], axis=0).reshape(2, N_MOD, D))
    P = lambda l, k: mods[l][:, k:k + 1, :]

    X = jnp.concatenate([x[0], ctx[0]], axis=0)
    n1_0, n2_0, n1_1, n2_1 = row(l0_norm1), row(l0_norm2), row(l1_norm1), row(l1_norm2)
    (h0,) = _rowwise("l0_mod1", _f_mod, nt, TR, [("row", X, 0, D), ("full", n1_0), ("grp", P(0, 0)), ("grp", P(0, 1))],
                     [("row", TA, D, BF16)], nlat)
    z0 = _matmul("l0_in", h0, w_in0p, "nn", TA, W0, D)
    kvn_w, qn_w = row(l0_kv_norm), row(l0_q_norm)
    norm_ins = [("row", z0, KVOFF0 // KV_LORA, KV_LORA), ("row", z0, QOFF0 // Q_LORA, Q_LORA), ("full", kvn_w), ("full", qn_w)]
    kvn, qn = _rowwise("l0_latnorm", _f_norms, nt, TR, norm_ins, [("row", TA, KV_LORA, BF16), ("row", TA, Q_LORA, BF16)])
    kv = _matmul("l0_kvb", kvn, full["l0_w_kvb"], "nn", TA, H * QK_PAD, KV_LORA, out_dtype=BF16)
    qraw = _matmul("l0_qb", qn, w_qbp, "nn", TA, H * QK_PAD, Q_LORA)

    def f_rope(i, qr, kvv, krr, cs, sn):
        kr = _rope128(krr, cs, sn).astype(BF16)
        qs, ks = [], []
        for h in range(H):
            qs += [qr[:, h * QK_PAD:h * QK_PAD + QK_NOPE], _rope128(qr[:, h * QK_PAD + QK_NOPE:(h + 1) * QK_PAD], cs, sn)]
            ks += [kvv[:, h * QK_PAD:h * QK_PAD + QK_NOPE], kr]
        return jnp.concatenate(qs, axis=1) * MLA_QSCALE, jnp.concatenate(ks, axis=1)

    rope_ins = [("row", qraw, 0, H * QK_PAD), ("row", kv, 0, H * QK_PAD), ("row", z0, KROFF0 // LANES, LANES),
                ("row", cos_t, 0, LANES), ("row", sin_t, 0, LANES)]
    q, kf = _rowwise("l0_rope", f_rope, nt, TR, rope_ins, [("row", TA, H * QK_PAD, BF16), ("row", TA, H * QK_PAD, BF16)])
    o_lat, lse, got_late = _mla_fwd_lat(q, kf, kv, T, TA, _Hosted([w_half[n] for n in late], "ag_xy"))
    o_ctx = _mla_fwd_ctx(q, kf, kv, T)
    full.update(pair_up("b", late, got_late))
    w_in1 = full["l1_w_in"]
    w_in1p = jnp.concatenate([w_in1[:, o_v1:o_lr1], w_in1[:, o_og1:], w_in1[:, :o_v1], w_in1[:, o_q1:o_og1],
                              w_in1[:, o_lr1:o_q1], jnp.zeros((D, LANES - 2 * GATE_RANK), BF16)], axis=1)

    conv_a = full["l0_conv_a"]

    def f_conva(i, ax, ab, ac_, pax, pac, nax, nac, w):
        prev, nxt = _shifters(i, TR, T, TA)
        p = ac_ * ax
        return (ab * (w[0:1] * prev(p, pac * pax) + w[1:2] * p + w[2:3] * nxt(p, nac * nax)),)

    conva_ins = [("row", z0, 0, CC), ("row", z0, 1, CC), ("row", z0, 2, CC), ("prev", z0, 0, CC), ("prev", z0, 2, CC),
                 ("next", z0, 0, CC), ("next", z0, 2, CC), ("full", conv_a)]
    (y_a,) = _rowwise("l0_conva", f_conva, nt, TR, conva_ins, [("row", TA, CC, BF16)])
    cat0 = jnp.concatenate([y_a, jnp.concatenate([o_lat, o_ctx], axis=0)], axis=1)
    y0 = _matmul("l0_out", cat0, full["l0_w_out"], "nn", TA, D, CC + H * V_HEAD)
    resmod_outs = [("row", TA, D, F32), ("row", TA, D, BF16)]
    X1, h2_0 = _rowwise("l0_resmod2", _f_resmod, nt, TR,
                        [("row", X, 0, D), ("row", y0, 0, D), ("grp", P(0, 2)), ("full", n2_0), ("grp", P(0, 3)), ("grp", P(0, 4))],
                        resmod_outs, nlat)
    cw0, cb0 = full["l0_ffn_conv_w"], row(l0_ffn_conv_b)
    u0, act0, f0 = _ffn_fwd("l0_ffn", h2_0, full["l0_ffn_up"], cw0, cb0, full["l0_ffn_down"], TA, T, TA, F)
    X2, h1 = _rowwise("l1_resmod1", _f_resmod, nt, TR,
                      [("row", X1, 0, D), ("row", f0, 0, D), ("grp", P(0, 5)), ("full", n1_1), ("grp", P(1, 0)), ("grp", P(1, 1))],
                      resmod_outs, nlat)

    z1 = _matmul("l1_in", h1, w_in1p, "nn", TA, W1, D)

    def f_gates(i, lr, wg, bg):
        pre = _bdot(lr, wg, NN_DIMS) + bg
        return (_log_sigmoid(pre) / GATE_NORMALIZER,)

    gate_ins = [("row", z1, LROFF // LANES, LANES), ("full", w_gate), ("full", b_gate)]
    (gates,) = _rowwise("l1_gates", f_gates, nt, TR, gate_ins, [("row", TA, 2 * KEY, F32)])
    gla_dims = (DK, DV, QOFF1, KOFF1)
    o_f, o_b, st_f, st_b = _gla_fwd(z1, gates, gla_dims, nt, nlat)
    onw = row(l1_o_norm)

    def f_glaout(i, of, ob, og, w):
        o = of + ob
        on = jnp.concatenate([_rms(o[:, h * DV:(h + 1) * DV], w) for h in range(GLA_HEADS)], axis=1)
        return (on * _silu(og),)

    glaout_ins = [("row", o_f, 0, VAL), ("row", o_b, 0, VAL), ("row", z1, OGOFF // VAL, VAL), ("full", onw)]
    (go,) = _rowwise("l1_glaout", f_glaout, nlat, TR, glaout_ins, [("row", T, VAL, BF16)])
    y1 = _matmul("l1_out", go, full["l1_w_out"], "nn", T, D, VAL)
    X3, h2_1 = _rowwise("l1_resmod2", _f_resmod, nlat, TR,
                        [("row", X2, 0, D), ("row", y1, 0, D), ("grp", P(1, 2)), ("full", n2_1), ("grp", P(1, 3)), ("grp", P(1, 4))],
                        [("row", T, D, F32), ("row", T, D, BF16)])
    cw1, cb1 = full["l1_ffn_conv_w"], row(l1_ffn_conv_b)
    u1, act1, f1 = _ffn_fwd("l1_ffn", h2_1, full["l1_ffn_up"], cw1, cb1, full["l1_ffn_down"], T, T, T, F)

    fnw = row(final_norm)

    def f_head(i, x3, ff, g5, w, tgt):
        fin = lambda a, b, g_, w_: _rms(a + g_ * b, w_)
        y, pull = jax.vjp(fin, x3, ff, g5, w)
        err = y - tgt
        dx3, dff, dg5, dw = pull(err / D)
        loss = 0.5 * jnp.sum(jnp.mean(err * err, axis=-1, keepdims=True), axis=0, keepdims=True)
        return jnp.broadcast_to(loss, (1, LANES)), dx3, dff, dg5, dw

    head_ins = [("row", X3, 0, D), ("row", f1, 0, D), ("grp", P(1, 5)), ("full", fnw), ("row", loss_target[0], 0, D)]
    loss_acc, dX3, df1, dm5_1, g_final = _rowwise(
        "loss_head", f_head, nlat, TR, head_ins,
        [("acc", 1, LANES), ("row", T, D, F32), ("row", T, D, BF16), ("acc", 1, D), ("acc", 1, D)])
    loss = lax.psum(loss_acc[0, 0], ("x", "y", "c"))

    grads = {"final_norm": g_final}
    dh2_1, grads["l1_ffn_up"], grads["l1_ffn_conv_w"], grads["l1_ffn_conv_b"], grads["l1_ffn_down"] = _ffn_bwd(
        "l1_dffn", df1, h2_1, u1, act1, full["l1_ffn_up"], cw1, cb1, full["l1_ffn_down"], T, T, T, F)
    resmod_bwd = _vjp_of(_f_resmod, 6, (0, 1, 2, 3, 4, 5))
    dX2l, dy1, dm2_1, grads["l1_norm2"], dm3_1, dm4_1 = _rowwise(
        "l1_dresmod2", resmod_bwd, nlat, TR,
        [("row", X2, 0, D), ("row", y1, 0, D), ("grp", P(1, 2)), ("full", n2_1), ("grp", P(1, 3)), ("grp", P(1, 4)),
         ("row", dX3, 0, D), ("row", dh2_1, 0, D)],
        [("row", T, D, F32), ("row", T, D, BF16)] + [("acc", 1, D)] * 4)
    dgo = _matmul("l1_dout_x", dy1, full["l1_w_out"], "nt", T, VAL, D)
    grads["l1_w_out"] = _matmul("l1_dout_w", go, dy1, "tn", VAL, D, T, out_dtype=BF16)

    def f_glaout_bwd(i, of, ob, og, w, d):
        is_ctx = i >= nlat
        _, pull = jax.vjp(lambda a, b, c_, w_: f_glaout(i, a, b, c_, w_)[0], of, ob, og, w)
        dof, _, dog, dw = pull(jnp.where(is_ctx, 0.0, d))
        return dof, dog, dw

    glaout_b_ins = glaout_ins + [("rowc", dgo, 0, VAL, nlat - 1)]
    do_gla, dog, g_onorm = _rowwise("l1_dglaout", f_glaout_bwd, nt, TR, glaout_b_ins,
                                    [("row", TA, VAL, F32), ("row", TA, VAL, F32), ("acc", 1, DV)])
    grads["l1_o_norm"] = g_onorm
    (dq_f, dk_f, dv_f, dg_f), (dq_b, dk_b, dv_b, dg_b) = _gla_bwd(z1, gates, st_f, st_b, do_gla, gla_dims, nt, nlat)

    def f_dz1(i, dvf, dvb, dog_, dkf, dkb, dqf, dqb, dgf, dgb, lr, wg, bg):
        _, pull = jax.vjp(lambda a, b, c_: f_gates(i, a, b, c_)[0], lr, wg, bg)
        dlr, dwg, dbg = pull(jnp.concatenate([dgf, dgb], axis=1))
        return jnp.concatenate([dvf + dvb, dog_, dkf + dkb, dqf + dqb, dlr], axis=1), dwg, dbg

    dz1_ins = [("row", dv_f, 0, VAL), ("row", dv_b, 0, VAL), ("row", dog, 0, VAL), ("row", dk_f, 0, KEY), ("row", dk_b, 0, KEY),
               ("row", dq_f, 0, KEY), ("row", dq_b, 0, KEY), ("row", dg_f, 0, KEY), ("row", dg_b, 0, KEY)] + gate_ins
    dz1, g_wgate, g_bgate = _rowwise("l1_dz", f_dz1, nt, TR, dz1_ins, [("row", TA, W1, BF16), ("acc", LANES, 2 * KEY), ("acc", 1, 2 * KEY)])
    grads["l1_gate_fw_w"], grads["l1_gate_bw_w"] = g_wgate[:GATE_RANK, :KEY], g_wgate[GATE_RANK:2 * GATE_RANK, KEY:]
    grads["l1_gate_fw_b"], grads["l1_gate_bw_b"] = g_bgate[:, :KEY], g_bgate[:, KEY:]
    dh1 = _matmul("l1_din_x", dz1, w_in1p, "nt", TA, D, W1)
    g1p = _matmul("l1_din_w", h1, dz1, "tn", D, W1, TA, out_dtype=BF16)
    grads["l1_w_in"] = jnp.concatenate([g1p[:, KOFF1:QOFF1], g1p[:, :OGOFF], g1p[:, LROFF:LROFF + 2 * GATE_RANK],
                                        g1p[:, QOFF1:LROFF], g1p[:, OGOFF:KOFF1]], axis=1)

    def f_resmod1_bwd(i, x_, y_, g_, w_, sh, sc, dx2, dh):
        return resmod_bwd(i, x_, y_, g_, w_, sh, sc, jnp.where(i >= nlat, 0.0, dx2), dh)

    dX1, df0, dm5_0, grads["l1_norm1"], dm0_1, dm1_1 = _rowwise(
        "l1_dresmod1", f_resmod1_bwd, nt, TR,
        [("row", X1, 0, D), ("row", f0, 0, D), ("grp", P(0, 5)), ("full", n1_1), ("grp", P(1, 0)), ("grp", P(1, 1)),
         ("rowc", dX2l, 0, D, nlat - 1), ("row", dh1, 0, D)],
        [("row", TA, D, F32), ("row", TA, D, BF16), ("accg", 1, D), ("acc", 1, D), ("accg", 1, D), ("accg", 1, D)], nlat)

    dh2_0, grads["l0_ffn_up"], grads["l0_ffn_conv_w"], grads["l0_ffn_conv_b"], grads["l0_ffn_down"] = _ffn_bwd(
        "l0_dffn", df0, h2_0, u0, act0, full["l0_ffn_up"], cw0, cb0, full["l0_ffn_down"], TA, T, TA, F)
    dXd, dy0, dm2_0, grads["l0_norm2"], dm3_0, dm4_0 = _rowwise(
        "l0_dresmod2", resmod_bwd, nt, TR,
        [("row", X, 0, D), ("row", y0, 0, D), ("grp", P(0, 2)), ("full", n2_0), ("grp", P(0, 3)), ("grp", P(0, 4)),
         ("row", dX1, 0, D), ("row", dh2_0, 0, D)],
        [("row", TA, D, F32), ("row", TA, D, BF16), ("accg", 1, D), ("acc", 1, D), ("accg", 1, D), ("accg", 1, D)], nlat)
    dcat = _matmul("l0_dout_x", dy0, full["l0_w_out"], "nt", TA, CC + H * V_HEAD, D)
    grads["l0_w_out"] = _matmul("l0_dout_w", cat0, dy0, "tn", CC + H * V_HEAD, D, TA, out_dtype=BF16)

    def f_conva_bwd(i, ax, ab, ac_, pax, pab, pac, nax, nab, nac, dy, pdy, ndy, w):
        prev, nxt = _shifters(i, TR, T, TA)
        p = ac_ * ax
        pp, pn = prev(p, pac * pax), nxt(p, nac * nax)
        cv = w[0:1] * pp + w[1:2] * p + w[2:3] * pn
        dcv = dy * ab
        dp = w[0:1] * nxt(dcv, ndy * nab) + w[1:2] * dcv + w[2:3] * prev(dcv, pdy * pab)
        return (jnp.concatenate([dp * ac_, dy * cv, dp * ax], axis=1), _colsum(pp * dcv), _colsum(p * dcv), _colsum(pn * dcv))

    conva_b_ins = [("row", z0, 0, CC), ("row", z0, 1, CC), ("row", z0, 2, CC),
                   ("prev", z0, 0, CC), ("prev", z0, 1, CC), ("prev", z0, 2, CC),
                   ("next", z0, 0, CC), ("next", z0, 1, CC), ("next", z0, 2, CC),
                   ("row", dcat, 0, CC), ("prev", dcat, 0, CC), ("next", dcat, 0, CC), ("full", conv_a)]
    dz_a, ga0, ga1, ga2 = _rowwise("l0_dconva", f_conva_bwd, nt, TR, conva_b_ins, [("row", TA, 3 * CC, BF16)] + [("acc", 1, CC)] * 3)
    grads["l0_conv_a"] = jnp.concatenate([ga0, ga1, ga2], axis=0)
    do_cb0 = CC // V_HEAD
    row_sh = {"l0_w_out", "l1_w_out", "l0_ffn_down", "l1_ffn_down"}

    def pair_sums(tag, group):
        res = []
        for n in group:
            s = grads[n].reshape((4, -1) + grads[n].shape[1:]) if n in row_sh else _cols_to_shards(grads[n])
            res.append(s.reshape(4, 2, s.shape[1] // 2, s.shape[2]).transpose(1, 0, 2, 3))
        taken = _exchange("pair_split_grads_" + tag, [own_half(g, 1 - my_c) for g in res], "swap_c")
        return [_sum_pair("pairsum_" + n[3:], own_half(g, my_c), t, BF16) for n, g, t in zip(group, res, taken)]

    pair_late = pair_sums("a", late)
    dq_lat, dk_lat, dv_lat, got = _mla_bwd_lat(q, kf, kv, o_lat, dcat, lse, do_cb0, T, TA, _Hosted(pair_late, "a2a_xy"))
    landed_late = {n: set_own(l, own_half(p, chip), chip) for n, l, p in zip(late, got, pair_late)}
    dq_ctx, dk_ctx, dv_ctx = _mla_bwd_ctx(q, kf, kv, dcat, dk_lat, dv_lat, do_cb0, T)

    def f_rope_bwd(i, dql, dqc, dkl, dkc, dvl, dvc, cs, sn):
        is_ctx = i >= nlat
        dq_, dk_, dv_ = jnp.where(is_ctx, dqc, dql) * MLA_SCALE, jnp.where(is_ctx, dkc, dkl), jnp.where(is_ctx, dvc, dvl)
        dqs, dkvs, dkr = [], [], None
        for h in range(H):
            dqs += [dq_[:, h * QK_PAD:h * QK_PAD + QK_NOPE], _rope128_t(dq_[:, h * QK_PAD + QK_NOPE:(h + 1) * QK_PAD], cs, sn)]
            dkvs += [dk_[:, h * QK_PAD:h * QK_PAD + QK_NOPE].astype(BF16), dv_[:, h * V_HEAD:(h + 1) * V_HEAD]]
            part = dk_[:, h * QK_PAD + QK_NOPE:(h + 1) * QK_PAD]
            dkr = part if dkr is None else dkr + part
        return jnp.concatenate(dqs, axis=1), jnp.concatenate(dkvs, axis=1), _rope128_t(dkr, cs, sn)

    drope_ins = [("rowc", dq_lat, 0, H * QK_PAD, nlat - 1), ("full", dq_ctx), ("row", dk_lat, 0, H * QK_PAD), ("full", dk_ctx),
                 ("row", dv_lat, 0, H * V_HEAD), ("full", dv_ctx), ("row", cos_t, 0, LANES), ("row", sin_t, 0, LANES)]
    dqraw, dkv, dz_kr = _rowwise("l0_drope", f_rope_bwd, nt, TR, drope_ins,
                                 [("row", TA, H * QK_PAD, BF16), ("row", TA, H * QK_PAD, BF16), ("row", TA, LANES, BF16)])
    dqn = _matmul("l0_dqb_x", dqraw, w_qbp, "nt", TA, Q_LORA, H * QK_PAD)
    g_wqbp = _matmul("l0_dqb_w", qn, dqraw, "tn", Q_LORA, H * QK_PAD, TA).reshape(Q_LORA, H, QK_PAD)
    g_rope = g_wqbp[:, :, QK_NOPE:QK_NOPE + QK_ROPE] + _rot_cols_t(g_wqbp[:, :, QK_NOPE + QK_ROPE:])
    grads["l0_w_qb"] = jnp.concatenate([g_wqbp[:, :, :QK_NOPE], g_rope], axis=2).reshape(Q_LORA, H * (QK_NOPE + QK_ROPE)).astype(BF16)
    dkvn = _matmul("l0_dkvb_x", dkv, full["l0_w_kvb"], "nt", TA, KV_LORA, H * QK_PAD)
    grads["l0_w_kvb"] = _matmul("l0_dkvb_w", kvn, dkv, "tn", KV_LORA, H * QK_PAD, TA, out_dtype=BF16)
    norms_bwd = _vjp_of(_f_norms, 4, (0, 1, 2, 3))
    dz_kv, dz_q, grads["l0_kv_norm"], grads["l0_q_norm"] = _rowwise(
        "l0_dlatnorm", norms_bwd, nt, TR, norm_ins + [("row", dkvn, 0, KV_LORA), ("row", dqn, 0, Q_LORA)],
        [("row", TA, KV_LORA, BF16), ("row", TA, Q_LORA, BF16), ("acc", 1, KV_LORA), ("acc", 1, Q_LORA)])
    dz0 = jnp.concatenate([dz_a, dz_q, dz_kv, dz_kr, jnp.zeros((TA, W0 - KROFF0 - LANES), BF16)], axis=1)
    dh0 = _matmul("l0_din_x", dz0, w_in0p, "nt", TA, D, W0)
    g0p = _matmul("l0_din_w", h0, dz0, "tn", D, W0, TA)
    g_kr = g0p[:, KROFF0:KROFF0 + QK_ROPE] + _rot_cols_t(g0p[:, KROFF0 + QK_ROPE:KROFF0 + 2 * QK_ROPE])
    grads["l0_w_in"] = jnp.concatenate([g0p[:, KVOFF0:KROFF0], g_kr, g0p[:, QOFF0:KVOFF0], g0p[:, :QOFF0]], axis=1).astype(BF16)

    def f_mod_bwd(i, x_, w_, sh, sc, dh, dxd):
        _, pull = jax.vjp(lambda a, b, c_, d_: _modulate(a, b, c_, d_), x_, w_, sh, sc)
        dx, dw, dsh, dsc = pull(dh)
        return dx + dxd, dw, dsh, dsc

    dXf, grads["l0_norm1"], dm0_0, dm1_0 = _rowwise(
        "l0_dmod1", f_mod_bwd, nt, TR,
        [("row", X, 0, D), ("full", n1_0), ("grp", P(0, 0)), ("grp", P(0, 1)), ("row", dh0, 0, D), ("row", dXd, 0, D)],
        [("row", TA, D, F32), ("acc", 1, D), ("accg", 1, D), ("accg", 1, D)], nlat)
    grad_x = dXf[:T][None]

    zD = jnp.zeros((1, D), F32)
    lat = lambda a: a[0] if a.ndim == 3 else a
    cxt = lambda a: a[1] if a.ndim == 3 else zD
    dmods = []
    for parts in ((dm0_0, dm1_0, dm2_0, dm3_0, dm4_0, dm5_0), (dm0_1, dm1_1, dm2_1, dm3_1, dm4_1, dm5_1)):
        dmods.append((jnp.concatenate([lat(a) for a in parts], axis=1), jnp.concatenate([cxt(a) for a in parts], axis=1)))
    small_names = ["l0_norm1", "l0_norm2", "l0_kv_norm", "l0_q_norm", "l0_conv_a", "l0_ffn_conv_w", "l0_ffn_conv_b",
                   "l1_norm1", "l1_norm2", "l1_o_norm", "l1_gate_fw_w", "l1_gate_fw_b", "l1_gate_bw_w", "l1_gate_bw_b",
                   "l1_ffn_conv_w", "l1_ffn_conv_b", "final_norm"]
    pieces = [("dm0", dmods[0][0]), ("dmc0", dmods[0][1]), ("dm1", dmods[1][0]), ("dmc1", dmods[1][1])]
    pieces += [("l0_ada_b", dmods[0][0] + dmods[0][1]), ("l1_ada_b", dmods[1][0] + dmods[1][1])]
    pieces += [(n, grads[n]) for n in small_names]
    offs, cur = {}, 0
    for n, a in pieces:
        offs[n] = (cur, a.size, a.shape)
        cur += -(-a.size // LANES) * LANES
    n_pad = -(-cur // 1024) * 1024
    flat = jnp.concatenate([jnp.pad(a.reshape(-1), (0, -a.size % LANES)) for _, a in pieces] + [jnp.zeros((n_pad - cur,), F32)])
    (small_g,) = _exchange("gather_small", [flat.reshape(n_pad // LANES, LANES)], "ag_all")
    small_sum = _sum_slots("sum_small", small_g).reshape(-1)
    small_all = small_g.reshape(8, -1)
    take = lambda n: small_sum[offs[n][0]:offs[n][0] + offs[n][1]].reshape(offs[n][2])

    cc_parts = []
    for l in (0, 1):
        o_m, sz, _ = offs["dm%d" % l]
        d16 = jnp.concatenate([small_all[:, o_m:o_m + sz], take("dmc%d" % l).reshape(1, -1), jnp.zeros((7, sz), F32)], axis=0)
        d16 = lax.dynamic_slice_in_dim(d16.reshape(16, 4, ADA_S), chip, 1, axis=1).reshape(16, ADA_S)
        grads["l%d_ada_w" % l] = _matmul("ada_dw", act16, d16, "tn", D, ADA_S, 16, tn_cap=512)
        cc_parts.append(_matmul("ada_dx", d16, W["l%d_ada_w" % l], "nt", 16, D, ADA_S, tn_cap=512))
        grads["l%d_ada_b" % l] = take("l%d_ada_b" % l).reshape(-1)
    cc_g = _exchange("gather_cc", cc_parts, "ag_xy")

    def cc_call():
        def body(a_ref, b_ref, c_ref, o_ref):
            tot = a_ref[0, 8:9, :] + b_ref[0, 8:9, :]
            for j in range(1, 4):
                tot = tot + (a_ref[j, 8:9, :] + b_ref[j, 8:9, :])
            _, pull = jax.vjp(_silu, c_ref[...])
            o_ref[...] = pull(tot)[0]
        return pl.pallas_call(body, name="c_ctx_grad", out_shape=jax.ShapeDtypeStruct((1, D), F32))(cc_g[0], cc_g[1], row(c_ctx))

    grads["c_ctx"] = cc_call().reshape(-1)

    pair_early = pair_sums("b", early)
    landed = dict(landed_late)
    for n, l, p in zip(early, _exchange("scatter_grads", pair_early, "a2a_xy", fill_own=False), pair_early):
        landed[n] = set_own(l, own_half(p, chip), chip)
    halves = [_sum_slots("sum_" + n[3:], landed[n]) for n in big]
    others = _exchange("pair_grads", halves, "swap_c")

    out_g, out_d, out_m, out_v = {}, {}, {}, {}
    for n, mine, other in zip(big, halves, others):
        g2 = jnp.where(my_c == 0, jnp.concatenate([mine, other], 0), jnp.concatenate([other, mine], 0))
        out_g[n], out_d[n], out_m[n], out_v[n] = _adam("adam_" + n[3:], W[n], [g2], MO[n], VO[n])
    for l in (0, 1):
        n = "l%d_ada_w" % l
        out_g[n], out_d[n], out_m[n], out_v[n] = _adam("adam_ada_w", W[n], [grads[n]], MO[n], VO[n])
    rest = [n for n in names if n not in out_g]
    g_rest = {}
    for n in rest:
        if n == "c_ctx" or n.endswith("ada_b"):
            g_rest[n] = grads[n]
        elif n in small_sh:
            gfull = take(n)
            cs = gfull.shape[1] // 4
            g_rest[n] = lax.dynamic_slice_in_dim(gfull, chip * cs, cs, axis=1)
        else:
            g_rest[n] = take(n).reshape(W[n].shape)
    sizes = [W[n].size for n in rest]
    tot = sum(-(-s // LANES) * LANES for s in sizes)
    tot_pad = -(-tot // 1024) * 1024

    def pack(d):
        parts = [jnp.pad(d[n].reshape(-1), (0, -d[n].size % LANES)) for n in rest]
        return jnp.concatenate(parts + [jnp.zeros((tot_pad - tot,), F32)]).reshape(tot_pad // LANES, LANES)

    packed = _adam("adam_small", pack(W), [pack(g_rest)], pack(MO), pack(VO))
    cur = 0
    for n, s in zip(rest, sizes):
        for dst, arr in zip((out_g, out_d, out_m, out_v), packed):
            dst[n] = arr.reshape(-1)[cur:cur + s].reshape(W[n].shape)
        cur += -(-s // LANES) * LANES

    return (loss, grad_x, *[out_g[n] for n in names], *[out_d[n] for n in names],
            *[out_m[n] for n in names], *[out_v[n] for n in names])
```
